```python
import math
import jax, jax.numpy as jnp
from jax import lax
import numpy as np

D_MODEL = 1024
BATCH = 8
SEQ = 2048
DEPTH = 4
DEC_BATCH = 128
DEC_SEQ = 1
PAST_LEN = 8192
PAGE_SIZE = 128

D_MIX = 2 * D_MODEL
ATT_HEADS = 8
ATT_KV_HEADS = 2
ATT_GROUP = ATT_HEADS // ATT_KV_HEADS
ATT_HEAD_DIM = 64
ATT_WIDTH = ATT_HEADS * ATT_HEAD_DIM
ATT_KV_WIDTH = ATT_KV_HEADS * ATT_HEAD_DIM
ATT_SCALE = ATT_HEAD_DIM ** -0.5
WINDOW = 128
ROPE_THETA = 500000.0
ROPE_DIM = ATT_HEAD_DIM // 4
LRU_WIDTH = 3 * D_MIX // 8
LRU_BLOCKS = 8
LRU_BLOCK = LRU_WIDTH // LRU_BLOCKS
LRU_C = 8.0
CONV_W = 4
SSD_WIDTH = D_MIX - ATT_WIDTH - LRU_WIDTH
SSD_HEAD_DIM = 64
SSD_HEADS = SSD_WIDTH // SSD_HEAD_DIM
SSD_GROUPS = 2
SSD_HPG = SSD_HEADS // SSD_GROUPS
SSD_STATE = 128
SSD_CHUNK = 128
SSD_CONV_CH = SSD_WIDTH + 2 * SSD_GROUPS * SSD_STATE
SPLIT_SIZES = (ATT_WIDTH, ATT_KV_WIDTH, ATT_KV_WIDTH, ATT_WIDTH,
               LRU_WIDTH, LRU_WIDTH,
               SSD_WIDTH, SSD_CONV_CH, SSD_HEADS)
SPLIT_POINTS = tuple(int(s) for s in np.cumsum(SPLIT_SIZES)[:-1])
D_IN_PROJ = int(sum(SPLIT_SIZES))
DEEPNORM_ALPHA = (2.0 * DEPTH) ** 0.25
DEEPNORM_BETA = (8.0 * DEPTH) ** -0.25
NORM_EPS = 1e-5
F32 = jnp.float32

kernel_name = 'hybrid_swa_rglru_ssd_step'


def layer_norm(x, g, b):
    xf = x.astype(F32)
    mu = jnp.mean(xf, -1, keepdims=True)
    var = jnp.mean(jnp.square(xf - mu), -1, keepdims=True)
    return ((xf - mu) * lax.rsqrt(var + NORM_EPS) * g + b).astype(x.dtype)


def partial_rope(x, pos):
    half = ROPE_DIM // 2
    inv = ROPE_THETA ** (-jnp.arange(half, dtype=F32) / half)
    ang = pos.astype(F32)[:, None] * inv[None, :]
    cos = jnp.cos(ang)[None, :, None, :]
    sin = jnp.sin(ang)[None, :, None, :]
    xr = x[..., :ROPE_DIM].astype(F32)
    x1, x2 = xr[..., :half], xr[..., half:]
    rot = jnp.concatenate([x1 * cos - x2 * sin, x2 * cos + x1 * sin], -1).astype(x.dtype)
    return jnp.concatenate([rot, x[..., ROPE_DIM:]], -1)


def sink_softmax(s, mask, sink):
    s = jnp.where(mask, s, -jnp.inf)
    sink = sink.astype(F32)
    m = jnp.maximum(jnp.max(s, -1, keepdims=True), sink)
    e = jnp.exp(s - m)
    return e / (jnp.sum(e, -1, keepdims=True) + jnp.exp(sink - m))


def swa_prompt(q, k, v, sinks):
    B, S = q.shape[:2]
    nb = S // WINDOW
    qb = q.reshape(B, nb, WINDOW, ATT_KV_HEADS, ATT_GROUP, ATT_HEAD_DIM)

    def band_keys(t):
        tb = t.reshape(B, nb, WINDOW, ATT_KV_HEADS, ATT_HEAD_DIM)
        prev = jnp.pad(tb[:, :-1], ((0, 0), (1, 0), (0, 0), (0, 0), (0, 0)))
        return jnp.concatenate([prev, tb], axis=2)

    kk, vv = band_keys(k), band_keys(v)
    i = jnp.arange(WINDOW)[:, None]
    j = jnp.arange(2 * WINDOW)[None, :]
    band = (j >= i) & (j <= i + WINDOW)
    blk = jnp.arange(nb)[:, None, None]
    mask = band[None] & ((blk > 0) | (j[None] >= WINDOW))
    s = jnp.einsum('bnqhgd,bnkhd->bnhgqk', qb, kk, preferred_element_type=F32) * ATT_SCALE
    p = sink_softmax(s, mask[None, :, None, None],
                     sinks.reshape(ATT_KV_HEADS, ATT_GROUP)[None, None, :, :, None, None])
    o = jnp.einsum('bnhgqk,bnkhd->bnqhgd', p.astype(v.dtype), vv)
    return o.reshape(B, S, ATT_WIDTH)


def swa_sample(q, k, v, ck, cv, sinks):
    B, L = q.shape[:2]
    kk = jnp.concatenate([ck.astype(k.dtype), k], 1)
    vv = jnp.concatenate([cv.astype(v.dtype), v], 1)
    qpos = WINDOW + jnp.arange(L)[:, None]
    kpos = jnp.arange(WINDOW + L)[None, :]
    mask = (kpos <= qpos) & (kpos >= qpos - WINDOW)
    qg = q.reshape(B, L, ATT_KV_HEADS, ATT_GROUP, ATT_HEAD_DIM)
    s = jnp.einsum('bqhgd,bkhd->bhgqk', qg, kk, preferred_element_type=F32) * ATT_SCALE
    p = sink_softmax(s, mask[None, None, None],
                     sinks.reshape(ATT_KV_HEADS, ATT_GROUP)[None, :, :, None, None])
    o = jnp.einsum('bhgqk,bkhd->bqhgd', p.astype(vv.dtype), vv).reshape(B, L, ATT_WIDTH)
    return o, kk[:, -WINDOW:].astype(ck.dtype), vv[:, -WINDOW:].astype(cv.dtype)


def causal_conv(x, buf, w, b):
    L = x.shape[1]
    xp = jnp.concatenate([buf.astype(x.dtype), x], 1)
    y = xp[:, 0:L] * w[0]
    for t in range(1, CONV_W):
        y = y + xp[:, t:t + L] * w[t]
    return y + b, xp[:, -(CONV_W - 1):].astype(buf.dtype)


def rglru(x, h0, w_a, b_a, w_x, b_x, lam):
    B, L, W = x.shape
    xb = x.reshape(B, L, LRU_BLOCKS, LRU_BLOCK)
    r = jax.nn.sigmoid(jnp.einsum('blnc,ncd->blnd', xb, w_a).reshape(B, L, W).astype(F32) + b_a)
    ig = jax.nn.sigmoid(jnp.einsum('blnc,ncd->blnd', xb, w_x).reshape(B, L, W).astype(F32) + b_x)
    log_a = (-LRU_C * jax.nn.softplus(-lam.astype(F32))) * r
    a = jnp.exp(log_a)
    bterm = jnp.sqrt(-jnp.expm1(2.0 * log_a)) * (ig * x.astype(F32))
    bterm = bterm.at[:, 0].add(a[:, 0] * h0.astype(F32))

    def combine(left, right):
        a1, b1 = left
        a2, b2 = right
        return a1 * a2, a2 * b1 + b2

    _, h = lax.associative_scan(combine, (a, bterm), axis=1)
    return h.astype(x.dtype), h[:, -1].astype(h0.dtype)


def ssd_scan(xh, dt, a_head, bm, cm, h0):
    Bsz, L = xh.shape[:2]
    q = min(SSD_CHUNK, L)
    lp = -(-L // q) * q
    pad = lp - L
    if pad:
        xh = jnp.pad(xh, ((0, 0), (0, pad), (0, 0), (0, 0)))
        dt = jnp.pad(dt, ((0, 0), (0, pad), (0, 0)))
        bm = jnp.pad(bm, ((0, 0), (0, pad), (0, 0), (0, 0)))
        cm = jnp.pad(cm, ((0, 0), (0, pad), (0, 0), (0, 0)))
    nc = lp // q
    x = (xh.astype(F32) * dt[..., None]).reshape(Bsz, nc, q, SSD_GROUPS, SSD_HPG, SSD_HEAD_DIM)
    a = (dt * a_head).reshape(Bsz, nc, q, SSD_GROUPS, SSD_HPG)
    bc = bm.astype(F32).reshape(Bsz, nc, q, SSD_GROUPS, SSD_STATE)
    cc = cm.astype(F32).reshape(Bsz, nc, q, SSD_GROUPS, SSD_STATE)
    a_cs = jnp.cumsum(a, axis=2)
    diff = a_cs[:, :, :, None] - a_cs[:, :, None, :]
    causal = jnp.tril(jnp.ones((q, q), dtype=bool))[None, None, :, :, None, None]
    lmat = jnp.exp(jnp.where(causal, diff, -jnp.inf))
    cb = jnp.einsum('bcqgn,bcsgn->bcqsg', cc, bc)
    y_diag = jnp.einsum('bcqsg,bcqsge,bcsgep->bcqgep', cb, lmat, x)
    decay_to_end = jnp.exp(a_cs[:, :, -1:] - a_cs)
    states = jnp.einsum('bcqgn,bcqge,bcqgep->bcgepn', bc, decay_to_end, x)
    chunk_decay = jnp.exp(a_cs[:, :, -1])
    hg0 = h0.astype(F32).reshape(Bsz, SSD_GROUPS, SSD_HPG, SSD_HEAD_DIM, SSD_STATE)

    def step(h, inp):
        dec, st = inp
        return dec[..., None, None] * h + st, h

    h_last, h_in = lax.scan(step, hg0, (jnp.swapaxes(chunk_decay, 0, 1), jnp.swapaxes(states, 0, 1)))
    h_in = jnp.swapaxes(h_in, 0, 1)
    y_off = jnp.einsum('bcqgn,bcgepn,bcqge->bcqgep', cc, h_in, jnp.exp(a_cs))
    y = (y_diag + y_off).reshape(Bsz, lp, SSD_HEADS, SSD_HEAD_DIM)[:, :L]
    return y, h_last.reshape(Bsz, SSD_HEADS, SSD_HEAD_DIM, SSD_STATE).astype(h0.dtype)


def mixer_layer(x, pos, att_cache, lru_conv0, lru_h0, ssd_conv0, ssd_h0, p):
    B, L, _ = x.shape
    proj = jnp.einsum('bld,de->ble', x, p['w_in'])
    q, k, v, g_att, x_lru, g_lru, z_ssd, xbc, dt_raw = jnp.split(proj, SPLIT_POINTS, axis=-1)
    q = partial_rope(q.reshape(B, L, ATT_HEADS, ATT_HEAD_DIM), pos)
    k = partial_rope(k.reshape(B, L, ATT_KV_HEADS, ATT_HEAD_DIM), pos)
    v = v.reshape(B, L, ATT_KV_HEADS, ATT_HEAD_DIM)
    if att_cache is None:
        att = swa_prompt(q, k, v, p['att_sinks'])
        k_buf, v_buf = k[:, -WINDOW:], v[:, -WINDOW:]
    else:
        att, k_buf, v_buf = swa_sample(q, k, v, att_cache[0], att_cache[1], p['att_sinks'])
    br_a = att * jax.nn.silu(g_att)
    xl, lru_conv1 = causal_conv(x_lru, lru_conv0, p['lru_conv_w'], p['lru_conv_b'])
    hl, lru_h1 = rglru(xl, lru_h0, p['lru_wa'], p['lru_ba'], p['lru_wx'], p['lru_bx'], p['lru_lambda'])
    br_b = hl * jax.nn.silu(g_lru)
    xbc_c, ssd_conv1 = causal_conv(xbc, ssd_conv0, p['ssd_conv_w'], p['ssd_conv_b'])
    xbc_c = jax.nn.silu(xbc_c)
    xs, b_ssm, c_ssm = jnp.split(xbc_c, (SSD_WIDTH, SSD_WIDTH + SSD_GROUPS * SSD_STATE), axis=-1)
    dt = jax.nn.softplus(dt_raw.astype(F32) + p['ssd_dt_bias'].astype(F32))
    a_head = -jnp.exp(p['ssd_a_log'].astype(F32))
    xh = xs.reshape(B, L, SSD_HEADS, SSD_HEAD_DIM)
    y, ssd_h1 = ssd_scan(xh, dt, a_head,
                         b_ssm.reshape(B, L, SSD_GROUPS, SSD_STATE),
                         c_ssm.reshape(B, L, SSD_GROUPS, SSD_STATE), ssd_h0)
    y = y + p['ssd_d'].astype(F32)[:, None] * xh.astype(F32)
    y = y.reshape(B, L, SSD_WIDTH) * jax.nn.silu(z_ssd.astype(F32))
    y = y * lax.rsqrt(jnp.mean(jnp.square(y), -1, keepdims=True) + NORM_EPS) * p['ssd_norm_g']
    br_c = y.astype(x.dtype)
    mix = jnp.concatenate([br_a.astype(x.dtype), br_b.astype(x.dtype), br_c], -1)
    out = jnp.einsum('ble,ed->bld', mix, p['w_out'])
    y_out = layer_norm(DEEPNORM_ALPHA * x + out, p['ln_g'], p['ln_b'])
    return y_out, (k_buf, v_buf, lru_conv1, lru_h1, ssd_conv1, ssd_h1)


def setup_inputs(seed: int = 0) -> dict:
    key = jax.random.key(seed)
    ks = jax.random.split(key, 32)
    nrm = lambda k, shape, s: jax.random.normal(k, shape, F32) * s
    u = jax.random.uniform(ks[20], (DEPTH, LRU_WIDTH), F32, 0.9, 0.999)
    a_lru = u ** (1.0 / LRU_C)
    lru_lambda = jnp.log(a_lru) - jnp.log1p(-a_lru)
    dt0 = jnp.exp(jax.random.uniform(ks[21], (DEPTH, SSD_HEADS), F32, math.log(1e-3), math.log(1e-1)))
    ssd_dt_bias = dt0 + jnp.log(-jnp.expm1(-dt0))
    ssd_a_log = jnp.log(jax.random.uniform(ks[22], (DEPTH, SSD_HEADS), F32, 1.0, 16.0))
    return {
        'x_prompt': nrm(ks[0], (BATCH, SEQ, D_MODEL), 1.0),
        'x_sample': nrm(ks[1], (DEC_BATCH, DEC_SEQ, D_MODEL), 1.0),
        'cache_swa_k': nrm(ks[2], (DEPTH, DEC_BATCH, WINDOW, ATT_KV_HEADS, ATT_HEAD_DIM), 1.0),
        'cache_swa_v': nrm(ks[3], (DEPTH, DEC_BATCH, WINDOW, ATT_KV_HEADS, ATT_HEAD_DIM), 1.0),
        'state_lru_conv': nrm(ks[4], (DEPTH, DEC_BATCH, CONV_W - 1, LRU_WIDTH), 1.0),
        'state_lru_h': nrm(ks[5], (DEPTH, DEC_BATCH, LRU_WIDTH), 0.5),
        'state_ssd_conv': nrm(ks[6], (DEPTH, DEC_BATCH, CONV_W - 1, SSD_CONV_CH), 1.0),
        'state_ssd_h': nrm(ks[7], (DEPTH, DEC_BATCH, SSD_HEADS, SSD_HEAD_DIM, SSD_STATE), 0.1),
        'w_in': nrm(ks[8], (DEPTH, D_MODEL, D_IN_PROJ), D_MODEL ** -0.5),
        'w_out': nrm(ks[9], (DEPTH, D_MIX, D_MODEL), DEEPNORM_BETA * D_MIX ** -0.5),
        'att_sinks': nrm(ks[10], (DEPTH, ATT_HEADS), 0.5),
        'lru_conv_w': nrm(ks[11], (DEPTH, CONV_W, LRU_WIDTH), CONV_W ** -0.5),
        'lru_conv_b': nrm(ks[12], (DEPTH, LRU_WIDTH), 0.02),
        'lru_wa': nrm(ks[13], (DEPTH, LRU_BLOCKS, LRU_BLOCK, LRU_BLOCK), LRU_BLOCK ** -0.5),
        'lru_ba': nrm(ks[14], (DEPTH, LRU_WIDTH), 0.02),
        'lru_wx': nrm(ks[15], (DEPTH, LRU_BLOCKS, LRU_BLOCK, LRU_BLOCK), LRU_BLOCK ** -0.5),
        'lru_bx': nrm(ks[16], (DEPTH, LRU_WIDTH), 0.02),
        'lru_lambda': lru_lambda,
        'ssd_conv_w': nrm(ks[17], (DEPTH, CONV_W, SSD_CONV_CH), CONV_W ** -0.5),
        'ssd_conv_b': nrm(ks[18], (DEPTH, SSD_CONV_CH), 0.02),
        'ssd_dt_bias': ssd_dt_bias,
        'ssd_a_log': ssd_a_log,
        'ssd_d': 1.0 + nrm(ks[23], (DEPTH, SSD_HEADS), 0.05),
        'ssd_norm_g': 1.0 + nrm(ks[24], (DEPTH, SSD_WIDTH), 0.05),
        'ln_g': 1.0 + nrm(ks[25], (DEPTH, D_MODEL), 0.05),
        'ln_b': nrm(ks[26], (DEPTH, D_MODEL), 0.02),
    }


def reference(x_prompt, x_sample, cache_swa_k, cache_swa_v, state_lru_conv, state_lru_h,
              state_ssd_conv, state_ssd_h, w_in, w_out, att_sinks, lru_conv_w, lru_conv_b,
              lru_wa, lru_ba, lru_wx, lru_bx, lru_lambda, ssd_conv_w, ssd_conv_b,
              ssd_dt_bias, ssd_a_log, ssd_d, ssd_norm_g, ln_g, ln_b):
    bp = x_prompt.shape[0]
    pos_p = jnp.arange(x_prompt.shape[1], dtype=jnp.int32)
    pos_s = PAST_LEN + jnp.arange(x_sample.shape[1], dtype=jnp.int32)
    zero_lru_conv = jnp.zeros((bp, CONV_W - 1, LRU_WIDTH), x_prompt.dtype)
    zero_lru_h = jnp.zeros((bp, LRU_WIDTH), F32)
    zero_ssd_conv = jnp.zeros((bp, CONV_W - 1, SSD_CONV_CH), x_prompt.dtype)
    zero_ssd_h = jnp.zeros((bp, SSD_HEADS, SSD_HEAD_DIM, SSD_STATE), F32)
    xp, xs = x_prompt, x_sample
    new_p = [[] for _ in range(6)]
    new_s = [[] for _ in range(6)]
    for l in range(DEPTH):
        p = {'w_in': w_in[l], 'w_out': w_out[l], 'att_sinks': att_sinks[l],
             'lru_conv_w': lru_conv_w[l], 'lru_conv_b': lru_conv_b[l],
             'lru_wa': lru_wa[l], 'lru_ba': lru_ba[l], 'lru_wx': lru_wx[l], 'lru_bx': lru_bx[l],
             'lru_lambda': lru_lambda[l], 'ssd_conv_w': ssd_conv_w[l], 'ssd_conv_b': ssd_conv_b[l],
             'ssd_dt_bias': ssd_dt_bias[l], 'ssd_a_log': ssd_a_log[l], 'ssd_d': ssd_d[l],
             'ssd_norm_g': ssd_norm_g[l], 'ln_g': ln_g[l], 'ln_b': ln_b[l]}
        xp, st_p = mixer_layer(xp, pos_p, None, zero_lru_conv, zero_lru_h, zero_ssd_conv, zero_ssd_h, p)
        xs, st_s = mixer_layer(xs, pos_s, (cache_swa_k[l], cache_swa_v[l]), state_lru_conv[l],
                               state_lru_h[l], state_ssd_conv[l], state_ssd_h[l], p)
        for lst, t in zip(new_p, st_p):
            lst.append(t)
        for lst, t in zip(new_s, st_s):
            lst.append(t)
    p_swa_k, p_swa_v, p_lru_conv, p_lru_h, p_ssd_conv, p_ssd_h = [jnp.stack(t) for t in new_p]
    s_swa_k, s_swa_v, s_lru_conv, s_lru_h, s_ssd_conv, s_ssd_h = [jnp.stack(t) for t in new_s]
    return (xp, xs, p_swa_k, p_swa_v, p_lru_conv, p_lru_h, p_ssd_conv, p_ssd_h,
            s_swa_k, s_swa_v, s_lru_conv, s_lru_h, s_ssd_conv, s_ssd_h)
```

```python
import functools
import math

import jax
import jax.numpy as jnp
from jax import lax
from jax.experimental import pallas as pl
from jax.experimental.pallas import tpu as pltpu

F32 = jnp.float32
BF16 = jnp.bfloat16

D_MODEL = 1024
DEPTH = 4
PAST_LEN = 8192
D_MIX = 2 * D_MODEL
ATT_HEADS = 8
ATT_KV_HEADS = 2
ATT_HEAD_DIM = 64
ATT_WIDTH = ATT_HEADS * ATT_HEAD_DIM
ATT_KV_WIDTH = ATT_KV_HEADS * ATT_HEAD_DIM
ATT_SCALE = ATT_HEAD_DIM ** -0.5
WINDOW = 128
ROPE_THETA = 500000.0
ROPE_DIM = ATT_HEAD_DIM // 4
LRU_WIDTH = 3 * D_MIX // 8
LRU_BLOCKS = 8
LRU_BLOCK = LRU_WIDTH // LRU_BLOCKS
LRU_C = 8.0
CONV_W = 4
SSD_WIDTH = D_MIX - ATT_WIDTH - LRU_WIDTH
SSD_HEAD_DIM = 64
SSD_HEADS = SSD_WIDTH // SSD_HEAD_DIM
SSD_GROUPS = 2
SSD_STATE = 128
SSD_CONV_CH = SSD_WIDTH + 2 * SSD_GROUPS * SSD_STATE
DEEPNORM_ALPHA = (2.0 * DEPTH) ** 0.25
NORM_EPS = 1e-5

LANES = 128
SUBLANES = 8
HALF = LANES // 2
VMEM_LIMIT_BYTES = 60 * 1024 * 1024

Q_OFF = 0
K_OFF = Q_OFF + ATT_WIDTH
V_OFF = K_OFF + 2 * ATT_KV_WIDTH
GA_OFF = V_OFF + 2 * ATT_KV_WIDTH
XL_OFF = GA_OFF + ATT_WIDTH
GL_OFF = XL_OFF + LRU_WIDTH
Z_OFF = GL_OFF + LRU_WIDTH
XBC_OFF = Z_OFF + SSD_WIDTH
DT_OFF = XBC_OFF + SSD_CONV_CH
N_COLS = DT_OFF + LANES
B_OFF = XBC_OFF + SSD_WIDTH
C_OFF = B_OFF + SSD_GROUPS * SSD_STATE
MIX_LRU = ATT_WIDTH
MIX_SSD = ATT_WIDTH + LRU_WIDTH

N_LRU_TILES = LRU_WIDTH // LANES
N_SSD_TILES = SSD_WIDTH // LANES
TILES_PER_GROUP = N_SSD_TILES // SSD_GROUPS
GROUP_W = SSD_WIDTH // SSD_GROUPS

NEG_BIG = -1e30

PROMPT_TILE = 256
SAMPLE_BT = 8


def _silu(x):
    return x * jax.nn.sigmoid(x)


def _softplus(x):
    return jnp.maximum(x, 0.0) + jnp.log1p(jnp.exp(-jnp.abs(x)))


def _lane_lo(shape):
    return (lax.broadcasted_iota(jnp.int32, shape, len(shape) - 1) % LANES) < HALF


def _rope(t, tab):
    half = ROPE_DIM // 2
    return (t * tab[:, 0:LANES]
            + pltpu.roll(t, LANES - half, 1) * tab[:, LANES:2 * LANES]
            + pltpu.roll(t, half, 1) * tab[:, 2 * LANES:3 * LANES])


def _dot(a, b):
    return jnp.dot(a, b, preferred_element_type=F32)


def _dot_nt(a, b):
    return lax.dot_general(a, b, (((1,), (1,)), ((), ())), preferred_element_type=F32)


def _layer_norm(v, g, b):
    mu = jnp.mean(v, -1, keepdims=True)
    d = v - mu
    var = jnp.mean(d * d, -1, keepdims=True)
    return d * lax.rsqrt(var + NORM_EPS) * g + b


def _lru_coeffs(xl, gates, lvec_ref):
    r = jax.nn.sigmoid(gates[:, 0:LRU_WIDTH] + lvec_ref[1:2, :])
    ig = jax.nn.sigmoid(gates[:, LRU_WIDTH:2 * LRU_WIDTH] + lvec_ref[2:3, :])
    log_a = (-LRU_C * _softplus(-lvec_ref[3:4, :])) * r
    a = jnp.exp(log_a)
    bt = jnp.sqrt(-jnp.tanh(log_a) * (1.0 + a * a)) * (ig * xl)
    return a, bt


def _prompt_kernel(sinks_ref, x_ref, rope_ref, win_ref, wg_ref, wout_ref, lcw_ref, lvec_ref,
                   scw_ref, scb_ref, hvec_ref, svec_ref, ln_ref,
                   y_ref, ko_ref, vo_ref, lco_ref, lho_ref, sco_ref, sho_ref,
                   p_scr, kd_scr, vd_scr, lx_scr, sx_scr, sa_scr, sb_scr, hl_scr, ht_scr, mix_scr):
    T = PROMPT_TILE
    nsub = T // WINDOW
    c = pl.program_id(1)
    last = c == pl.num_programs(1) - 1

    @pl.when(c == 0)
    def _():
        kd_scr[:, 0:WINDOW, :] = jnp.zeros((2, WINDOW, LANES), BF16)
        vd_scr[:, 0:WINDOW, :] = jnp.zeros((2, WINDOW, LANES), BF16)
        lx_scr[0:SUBLANES, :] = jnp.zeros((SUBLANES, LRU_WIDTH), F32)
        sx_scr[0:SUBLANES, :] = jnp.zeros((SUBLANES, SSD_CONV_CH), F32)
        hl_scr[...] = jnp.zeros_like(hl_scr)
        ht_scr[...] = jnp.zeros_like(ht_scr)

    xb = x_ref[...].astype(BF16)
    for lo, hi in ((Q_OFF, XL_OFF), (XL_OFF, Z_OFF), (Z_OFF, N_COLS)):
        p_scr[:, lo:hi] = _dot(xb, win_ref[:, lo:hi])

    lo_sq = _lane_lo((WINDOW, LANES))

    for g in range(ATT_KV_HEADS):
        kd = _rope(p_scr[:, K_OFF + g * LANES:K_OFF + (g + 1) * LANES], rope_ref[...])
        kd_scr[g, WINDOW:WINDOW + T, :] = kd.astype(BF16)
        vd_scr[g, WINDOW:WINDOW + T, :] = p_scr[:, V_OFF + g * LANES:V_OFF + (g + 1) * LANES].astype(BF16)

    @pl.when(last)
    def _():
        tab = rope_ref[T - WINDOW:T, :]
        k0 = _rope(p_scr[T - WINDOW:T, K_OFF:K_OFF + LANES], tab)
        k1 = _rope(p_scr[T - WINDOW:T, K_OFF + LANES:K_OFF + 2 * LANES], tab)
        ko_ref[...] = jnp.where(lo_sq, k0, k1)
        vo_ref[...] = jnp.where(lo_sq, p_scr[T - WINDOW:T, V_OFF:V_OFF + LANES],
                                p_scr[T - WINDOW:T, V_OFF + LANES:V_OFF + 2 * LANES])

    row = lax.broadcasted_iota(jnp.int32, (WINDOW, 2 * WINDOW), 0)
    col = lax.broadcasted_iota(jnp.int32, (WINDOW, 2 * WINDOW), 1)
    band = (col >= row) & (col <= row + WINDOW)
    first_lo = jnp.where(c > 0, 0, WINDOW)
    for i in range(nsub):
        r0 = i * WINDOW
        msk = (band & (col >= first_lo)) if i == 0 else band
        tab = rope_ref[r0:r0 + WINDOW, :]
        for cc in range(ATT_HEADS // 2):
            g = cc // (ATT_HEADS // ATT_KV_HEADS // 2)
            kg = kd_scr[g, r0:r0 + 2 * WINDOW, :]
            vg = vd_scr[g, r0:r0 + 2 * WINDOW, :]
            qp = _rope(p_scr[r0:r0 + WINDOW, Q_OFF + cc * LANES:Q_OFF + (cc + 1) * LANES], tab)
            outs = []
            for half in range(2):
                sink = sinks_ref[2 * cc + half]
                qm = jnp.where(lo_sq if half == 0 else jnp.logical_not(lo_sq), qp, 0.0).astype(BF16)
                s = jnp.where(msk, _dot_nt(qm, kg), NEG_BIG)
                m = jnp.maximum(jnp.max(s, -1, keepdims=True), sink)
                e = jnp.exp(s - m)
                den = jnp.sum(e, -1, keepdims=True) + jnp.exp(sink - m)
                outs.append(_dot(e.astype(BF16), vg) * (1.0 / den))
            att = jnp.where(lo_sq, outs[0], outs[1])
            gate = p_scr[r0:r0 + WINDOW, GA_OFF + cc * LANES:GA_OFF + (cc + 1) * LANES]
            mix_scr[r0:r0 + WINDOW, cc * LANES:(cc + 1) * LANES] = (att * _silu(gate)).astype(BF16)
    for g in range(ATT_KV_HEADS):
        kd_scr[g, 0:WINDOW, :] = kd_scr[g, T:T + WINDOW, :]
        vd_scr[g, 0:WINDOW, :] = vd_scr[g, T:T + WINDOW, :]

    P = SUBLANES
    lx_scr[P:P + T, :] = p_scr[:, XL_OFF:XL_OFF + LRU_WIDTH]
    xl = lvec_ref[0:1, :] + lcw_ref[3:4, :] * lx_scr[P:P + T, :]
    for t in range(CONV_W - 1):
        xl = xl + lcw_ref[t:t + 1, :] * lx_scr[P - (CONV_W - 1) + t:P - (CONV_W - 1) + t + T, :]

    @pl.when(last)
    def _():
        lco_ref[...] = lx_scr[P + T - (CONV_W - 1):P + T, :]
    lx_scr[0:P, :] = lx_scr[T:T + P, :]

    gates = _dot(xl.astype(BF16), wg_ref[...])
    a, bt = _lru_coeffs(xl, gates, lvec_ref)
    G = T // SUBLANES
    row_g = lax.broadcasted_iota(jnp.int32, (G, LANES), 0)
    for j in range(N_LRU_TILES):
        sa_scr[j] = a[:, j * LANES:(j + 1) * LANES]
        sb_scr[j] = bt[:, j * LANES:(j + 1) * LANES]
    for j in range(N_LRU_TILES):
        ca = sa_scr[j, pl.ds(0, G, stride=SUBLANES), :]
        cb = sb_scr[j, pl.ds(0, G, stride=SUBLANES), :]
        for k in range(1, SUBLANES):
            ak = sa_scr[j, pl.ds(k, G, stride=SUBLANES), :]
            cb = ak * cb + sb_scr[j, pl.ds(k, G, stride=SUBLANES), :]
            ca = ak * ca
            sa_scr[j, pl.ds(k, G, stride=SUBLANES), :] = ca
            sb_scr[j, pl.ds(k, G, stride=SUBLANES), :] = cb
        hin = hl_scr[0:1, j * LANES:(j + 1) * LANES]
        cb = cb + jnp.where(row_g == 0, ca * hin, 0.0)
        d = 1
        while d < G:
            ok = row_g >= d
            cb = cb + ca * jnp.where(ok, pltpu.roll(cb, d, 0), 0.0)
            ca = ca * jnp.where(ok, pltpu.roll(ca, d, 0), 1.0)
            d *= 2
        hprev = jnp.where(row_g == 0, hin, pltpu.roll(cb, 1, 0))
        for k in range(SUBLANES):
            sb_scr[j, pl.ds(k, G, stride=SUBLANES), :] = (
                sb_scr[j, pl.ds(k, G, stride=SUBLANES), :] + sa_scr[j, pl.ds(k, G, stride=SUBLANES), :] * hprev)
        hl_scr[0:1, j * LANES:(j + 1) * LANES] = cb[G - 1:G, :]
        gl = p_scr[:, GL_OFF + j * LANES:GL_OFF + (j + 1) * LANES]
        mix_scr[:, MIX_LRU + j * LANES:MIX_LRU + (j + 1) * LANES] = (sb_scr[j] * _silu(gl)).astype(BF16)

    @pl.when(last)
    def _():
        lho_ref[...] = hl_scr[...]

    sx_scr[P:P + T, :] = p_scr[:, XBC_OFF:XBC_OFF + SSD_CONV_CH]
    xc = scb_ref[0:1, :] + scw_ref[3:4, :] * sx_scr[P:P + T, :]
    for t in range(CONV_W - 1):
        xc = xc + scw_ref[t:t + 1, :] * sx_scr[P - (CONV_W - 1) + t:P - (CONV_W - 1) + t + T, :]
    p_scr[:, XBC_OFF:XBC_OFF + SSD_CONV_CH] = _silu(xc)

    @pl.when(last)
    def _():
        sco_ref[...] = sx_scr[P + T - (CONV_W - 1):P + T, :]
    sx_scr[0:P, :] = sx_scr[T:T + P, :]

    a_head = -jnp.exp(hvec_ref[1:2, :])
    qi = lax.broadcasted_iota(jnp.int32, (WINDOW, WINDOW), 0)
    si = lax.broadcasted_iota(jnp.int32, (WINDOW, WINDOW), 1)
    causal = si <= qi
    tri = causal.astype(F32)
    for i in range(nsub):
        r0 = i * WINDOW
        dt_c = _softplus(p_scr[r0:r0 + WINDOW, DT_OFF:DT_OFF + LANES] + hvec_ref[0:1, :])
        cs = jnp.dot(tri, dt_c * a_head, precision=lax.Precision.HIGHEST, preferred_element_type=F32)
        cst = cs.T
        ys = [None] * N_SSD_TILES
        for g in range(SSD_GROUPS):
            bg_t = p_scr[r0:r0 + WINDOW, B_OFF + g * SSD_STATE:B_OFF + (g + 1) * SSD_STATE].T.astype(BF16)
            cg = p_scr[r0:r0 + WINDOW, C_OFF + g * SSD_STATE:C_OFF + (g + 1) * SSD_STATE].astype(BF16)
            cbm = _dot(cg, bg_t)
            yo = _dot(cg, ht_scr[g].astype(BF16))
            xdec, edec = [], []
            for pp in range(TILES_PER_GROUP):
                p = g * TILES_PER_GROUP + pp
                bcs, mm, dts = [], [], []
                for h in (2 * p, 2 * p + 1):
                    bc = jnp.broadcast_to(cs[:, h:h + 1], (WINDOW, WINDOW))
                    lmat = jnp.exp(jnp.where(causal, bc - cst[h:h + 1, :], NEG_BIG))
                    mm.append((cbm * lmat).astype(BF16))
                    bcs.append(bc)
                    dts.append(jnp.broadcast_to(dt_c[:, h:h + 1], (WINDOW, LANES)))
                csl = jnp.where(lo_sq, bcs[0], bcs[1])
                ecs = jnp.exp(csl)
                dec = jnp.exp(csl[WINDOW - 1:WINDOW, :] - csl)
                xs_p = p_scr[r0:r0 + WINDOW, XBC_OFF + p * LANES:XBC_OFF + (p + 1) * LANES]
                xdt = xs_p * jnp.where(lo_sq, dts[0], dts[1])
                xdt_b = xdt.astype(BF16)
                y = jnp.where(lo_sq, _dot(mm[0], xdt_b), _dot(mm[1], xdt_b))
                y = y + yo[:, pp * LANES:(pp + 1) * LANES] * ecs
                y = y + svec_ref[0:1, p * LANES:(p + 1) * LANES] * xs_p
                ys[p] = y * _silu(p_scr[r0:r0 + WINDOW, Z_OFF + p * LANES:Z_OFF + (p + 1) * LANES])
                xdec.append((xdt * dec).astype(BF16))
                edec.append(ecs[WINDOW - 1:WINDOW, :])
            ht_scr[g] = (ht_scr[g] * jnp.concatenate(edec, axis=1)
                         + _dot(bg_t, jnp.concatenate(xdec, axis=1)))
        ss = jnp.sum(ys[0] * ys[0], -1, keepdims=True)
        for p in range(1, N_SSD_TILES):
            ss = ss + jnp.sum(ys[p] * ys[p], -1, keepdims=True)
        rinv = lax.rsqrt(ss * (1.0 / SSD_WIDTH) + NORM_EPS)
        for p in range(N_SSD_TILES):
            mix_scr[r0:r0 + WINDOW, MIX_SSD + p * LANES:MIX_SSD + (p + 1) * LANES] = (
                ys[p] * rinv * svec_ref[1:2, p * LANES:(p + 1) * LANES]).astype(BF16)

    @pl.when(last)
    def _():
        for g in range(SSD_GROUPS):
            for pp in range(TILES_PER_GROUP):
                tile = ht_scr[g, :, pp * LANES:(pp + 1) * LANES].T
                h = 2 * (g * TILES_PER_GROUP + pp)
                sho_ref[h] = tile[0:SSD_HEAD_DIM, :]
                sho_ref[h + 1] = tile[SSD_HEAD_DIM:2 * SSD_HEAD_DIM, :]

    out = _dot(mix_scr[...], wout_ref[...])
    y_ref[...] = _layer_norm(DEEPNORM_ALPHA * x_ref[...] + out, ln_ref[0:1, :], ln_ref[1:2, :])


def _const_spec(shape, layer):
    nd = len(shape)
    return pl.BlockSpec((None,) + tuple(shape), lambda *_: (layer,) + (0,) * nd,
                        pipeline_mode=pl.Buffered(1))


def _prompt_layer(layer, x, rope_tab, sinks, prm):
    B, L, _ = x.shape
    T = PROMPT_TILE
    nc = L // T
    win, wg, wout, lcw, lvec, scw, scb, hvec, svec, ln = prm
    tile = lambda b, c: (b, c, 0)
    per_b = lambda b, c: (b, 0, 0)
    in_specs = [
        pl.BlockSpec(memory_space=pltpu.SMEM),
        pl.BlockSpec((None, T, D_MODEL), tile),
        pl.BlockSpec((T, 3 * LANES), lambda b, c: (c, 0)),
        _const_spec((D_MODEL, N_COLS), layer),
        _const_spec((LRU_WIDTH, 2 * LRU_WIDTH), layer),
        _const_spec((D_MIX, D_MODEL), layer),
        _const_spec((CONV_W, LRU_WIDTH), layer),
        _const_spec((4, LRU_WIDTH), layer),
        _const_spec((CONV_W, SSD_CONV_CH), layer),
        _const_spec((1, SSD_CONV_CH), layer),
        _const_spec((2, LANES), layer),
        _const_spec((2, SSD_WIDTH), layer),
        _const_spec((2, D_MODEL), layer),
    ]
    out_shape = (
        jax.ShapeDtypeStruct((B, L, D_MODEL), F32),
        jax.ShapeDtypeStruct((B, WINDOW, ATT_KV_WIDTH), F32),
        jax.ShapeDtypeStruct((B, WINDOW, ATT_KV_WIDTH), F32),
        jax.ShapeDtypeStruct((B, CONV_W - 1, LRU_WIDTH), F32),
        jax.ShapeDtypeStruct((B, 1, LRU_WIDTH), F32),
        jax.ShapeDtypeStruct((B, CONV_W - 1, SSD_CONV_CH), F32),
        jax.ShapeDtypeStruct((B, SSD_HEADS, SSD_HEAD_DIM, SSD_STATE), F32),
    )
    out_specs = (
        pl.BlockSpec((None, T, D_MODEL), tile),
        pl.BlockSpec((None, WINDOW, ATT_KV_WIDTH), per_b),
        pl.BlockSpec((None, WINDOW, ATT_KV_WIDTH), per_b),
        pl.BlockSpec((None, CONV_W - 1, LRU_WIDTH), per_b),
        pl.BlockSpec((None, 1, LRU_WIDTH), per_b),
        pl.BlockSpec((None, CONV_W - 1, SSD_CONV_CH), per_b),
        pl.BlockSpec((None, SSD_HEADS, SSD_HEAD_DIM, SSD_STATE), lambda b, c: (b, 0, 0, 0)),
    )
    scratch = [
        pltpu.VMEM((T, N_COLS), F32),
        pltpu.VMEM((ATT_KV_HEADS, WINDOW + T, LANES), BF16),
        pltpu.VMEM((ATT_KV_HEADS, WINDOW + T, LANES), BF16),
        pltpu.VMEM((SUBLANES + T, LRU_WIDTH), F32),
        pltpu.VMEM((SUBLANES + T, SSD_CONV_CH), F32),
        pltpu.VMEM((N_LRU_TILES, T, LANES), F32),
        pltpu.VMEM((N_LRU_TILES, T, LANES), F32),
        pltpu.VMEM((1, LRU_WIDTH), F32),
        pltpu.VMEM((SSD_GROUPS, SSD_STATE, GROUP_W), F32),
        pltpu.VMEM((T, D_MIX), BF16),
    ]
    return pl.pallas_call(
        _prompt_kernel,
        grid=(B, nc),
        in_specs=in_specs,
        out_specs=out_specs,
        out_shape=out_shape,
        scratch_shapes=scratch,
        compiler_params=pltpu.CompilerParams(
            dimension_semantics=("arbitrary", "arbitrary"),
            vmem_limit_bytes=VMEM_LIMIT_BYTES),
        name=f"prompt_layer{layer}",
    )(sinks, x, rope_tab, win, wg, wout, lcw, lvec, scw, scb, hvec, svec, ln)


def _conv_step(x_new, st, w_ref, bias):
    cw = x_new.shape[1]
    y = bias + w_ref[CONV_W - 1:CONV_W, :] * x_new
    for t in range(CONV_W - 1):
        y = y + w_ref[t:t + 1, :] * st[:, t * cw:(t + 1) * cw]
    return y


def _sample_kernel(sinks_ref, x_ref, rope_ref, win_ref, wg_ref, wout_ref, lcw_ref, lvec_ref,
                   scw_ref, scb_ref, hvec_ref, svec_ref, ln_ref,
                   ck_ref, cv_ref, lc_ref, lh_ref, sc_ref, sh_ref,
                   y_ref, cko_ref, cvo_ref, lco_ref, lho_ref, sco_ref, sho_ref,
                   p_scr, q_scr, o_scr, kn_scr, vn_scr, xdt_scr, da_scr, y_scr, mix_scr):
    NB = x_ref.shape[0]
    BT = SAMPLE_BT
    i = pl.program_id(0)
    lo_f = _lane_lo((NB, LANES))

    @pl.when(i == 0)
    def _():
        xb = x_ref[...].astype(BF16)
        for lo, hi in ((Q_OFF, XL_OFF), (XL_OFF, Z_OFF), (Z_OFF, N_COLS)):
            p_scr[:, lo:hi] = _dot(xb, win_ref[:, lo:hi])
        tab = rope_ref[...]
        for cc in range(ATT_HEADS // 2):
            g = cc // (ATT_HEADS // ATT_KV_HEADS // 2)
            qp = _rope(p_scr[:, Q_OFF + cc * LANES:Q_OFF + (cc + 1) * LANES], tab)
            for half in range(2):
                t = jnp.where(lo_f if half == 0 else jnp.logical_not(lo_f), qp, 0.0)
                if half != g:
                    t = pltpu.roll(t, HALF, 1)
                q_scr[pl.ds(2 * cc + half, NB, stride=ATT_HEADS), :] = t
        k0 = _rope(p_scr[:, K_OFF:K_OFF + LANES], tab)
        k1 = _rope(p_scr[:, K_OFF + LANES:K_OFF + 2 * LANES], tab)
        kn_scr[...] = jnp.where(lo_f, k0, k1)
        vn_scr[...] = jnp.where(lo_f, p_scr[:, V_OFF:V_OFF + LANES], p_scr[:, V_OFF + LANES:V_OFF + 2 * LANES])
        xin = p_scr[:, XL_OFF:XL_OFF + LRU_WIDTH]
        st = lc_ref[...]
        xl = _conv_step(xin, st, lcw_ref, lvec_ref[0:1, :])
        lco_ref[:, 0:2 * LRU_WIDTH] = st[:, LRU_WIDTH:3 * LRU_WIDTH]
        lco_ref[:, 2 * LRU_WIDTH:3 * LRU_WIDTH] = xin
        gates = _dot(xl.astype(BF16), wg_ref[...])
        a, bt = _lru_coeffs(xl, gates, lvec_ref)
        h1 = a * lh_ref[...] + bt
        lho_ref[...] = h1
        mix_scr[:, MIX_LRU:MIX_LRU + LRU_WIDTH] = (h1 * _silu(p_scr[:, GL_OFF:GL_OFF + LRU_WIDTH])).astype(BF16)
        xin2 = p_scr[:, XBC_OFF:XBC_OFF + SSD_CONV_CH]
        st2 = sc_ref[...]
        xc = _conv_step(xin2, st2, scw_ref, scb_ref[0:1, :])
        sco_ref[:, 0:2 * SSD_CONV_CH] = st2[:, SSD_CONV_CH:3 * SSD_CONV_CH]
        sco_ref[:, 2 * SSD_CONV_CH:3 * SSD_CONV_CH] = xin2
        p_scr[:, XBC_OFF:XBC_OFF + SSD_CONV_CH] = _silu(xc)
        dt = _softplus(p_scr[:, DT_OFF:DT_OFF + LANES] + hvec_ref[0:1, :])
        da_scr[...] = jnp.exp(dt * (-jnp.exp(hvec_ref[1:2, :])))
        for p in range(N_SSD_TILES):
            dte = jnp.where(lo_f, jnp.broadcast_to(dt[:, 2 * p:2 * p + 1], (NB, LANES)),
                            jnp.broadcast_to(dt[:, 2 * p + 1:2 * p + 2], (NB, LANES)))
            xdt_scr[:, p * LANES:(p + 1) * LANES] = p_scr[:, XBC_OFF + p * LANES:XBC_OFF + (p + 1) * LANES] * dte

    r8 = pl.multiple_of(i * BT, BT)
    kn_blk = kn_scr[pl.ds(r8, BT), :]
    vn_blk = vn_scr[pl.ds(r8, BT), :]
    zpad = jnp.zeros((LANES - BT, LANES), F32)
    da_t = jnp.concatenate([da_scr[pl.ds(r8, BT), :], zpad], axis=0).T
    x_t = [jnp.concatenate([xdt_scr[pl.ds(r8, BT), p * LANES:(p + 1) * LANES], zpad], axis=0).T
           for p in range(N_SSD_TILES)]
    b_blk = p_scr[pl.ds(r8, BT), B_OFF:B_OFF + SSD_GROUPS * SSD_STATE]
    c_blk = p_scr[pl.ds(r8, BT), C_OFF:C_OFF + SSD_GROUPS * SSD_STATE].astype(BF16)
    row8 = lax.broadcasted_iota(jnp.int32, (ATT_HEADS, 1), 0)
    sink = jnp.zeros((ATT_HEADS, 1), F32)
    for h in range(ATT_HEADS):
        sink = jnp.where(row8 == h, sinks_ref[h], sink)
    rowb = lax.broadcasted_iota(jnp.int32, (BT, GROUP_W), 0)
    y_acc = [jnp.zeros((BT, GROUP_W), F32) for _ in range(SSD_GROUPS)]
    for bb in range(BT):
        qr = q_scr[pl.ds(pl.multiple_of((r8 + bb) * ATT_HEADS, ATT_HEADS), ATT_HEADS), :]
        s = _dot_nt(qr.astype(BF16), ck_ref[bb].astype(BF16))
        s_new = jnp.sum(qr * kn_blk[bb:bb + 1, :], -1, keepdims=True)
        m = jnp.maximum(jnp.maximum(jnp.max(s, -1, keepdims=True), s_new), sink)
        e = jnp.exp(s - m)
        e_new = jnp.exp(s_new - m)
        den = jnp.sum(e, -1, keepdims=True) + e_new + jnp.exp(sink - m)
        o = _dot(e.astype(BF16), cv_ref[bb].astype(BF16)) + e_new * vn_blk[bb:bb + 1, :]
        o_scr[pl.ds(pl.multiple_of((r8 + bb) * ATT_HEADS, ATT_HEADS), ATT_HEADS), :] = o * (1.0 / den)
        cko_ref[bb, 0:WINDOW - 1, :] = ck_ref[bb, 1:WINDOW, :]
        cko_ref[bb, WINDOW - 1:WINDOW, :] = kn_blk[bb:bb + 1, :]
        cvo_ref[bb, 0:WINDOW - 1, :] = cv_ref[bb, 1:WINDOW, :]
        cvo_ref[bb, WINDOW - 1:WINDOW, :] = vn_blk[bb:bb + 1, :]
        for g in range(SSD_GROUPS):
            brow = b_blk[bb:bb + 1, g * SSD_STATE:(g + 1) * SSD_STATE]
            tiles = []
            for pp in range(TILES_PER_GROUP):
                p = g * TILES_PER_GROUP + pp
                for hh in range(2):
                    h = 2 * p + hh
                    dab = jnp.broadcast_to(da_t[h:h + 1, bb:bb + 1], (SSD_HEAD_DIM, SSD_STATE))
                    xcol = jnp.broadcast_to(x_t[p][hh * SSD_HEAD_DIM:(hh + 1) * SSD_HEAD_DIM, bb:bb + 1],
                                            (SSD_HEAD_DIM, SSD_STATE))
                    h1 = sh_ref[bb, h] * dab + xcol * brow
                    sho_ref[bb, h] = h1
                    tiles.append(h1.astype(BF16))
            res = _dot_nt(c_blk[:, g * SSD_STATE:(g + 1) * SSD_STATE], jnp.concatenate(tiles, axis=0))
            y_acc[g] = jnp.where(rowb == bb, res, y_acc[g])
    for g in range(SSD_GROUPS):
        y_scr[pl.ds(r8, BT), g * GROUP_W:(g + 1) * GROUP_W] = y_acc[g]

    @pl.when(i == pl.num_programs(0) - 1)
    def _():
        for cc in range(ATT_HEADS // 2):
            g = cc // (ATT_HEADS // ATT_KV_HEADS // 2)
            oe = o_scr[pl.ds(2 * cc, NB, stride=ATT_HEADS), :]
            oo = o_scr[pl.ds(2 * cc + 1, NB, stride=ATT_HEADS), :]
            if g == 0:
                oo = pltpu.roll(oo, HALF, 1)
            else:
                oe = pltpu.roll(oe, HALF, 1)
            gate = p_scr[:, GA_OFF + cc * LANES:GA_OFF + (cc + 1) * LANES]
            mix_scr[:, cc * LANES:(cc + 1) * LANES] = (jnp.where(lo_f, oe, oo) * _silu(gate)).astype(BF16)
        y = (y_scr[...] + svec_ref[0:1, :] * p_scr[:, XBC_OFF:XBC_OFF + SSD_WIDTH]) * _silu(p_scr[:, Z_OFF:Z_OFF + SSD_WIDTH])
        y = y * lax.rsqrt(jnp.mean(y * y, -1, keepdims=True) + NORM_EPS) * svec_ref[1:2, :]
        mix_scr[:, MIX_SSD:MIX_SSD + SSD_WIDTH] = y.astype(BF16)
        out = _dot(mix_scr[...], wout_ref[...])
        y_ref[...] = _layer_norm(DEEPNORM_ALPHA * x_ref[...] + out, ln_ref[0:1, :], ln_ref[1:2, :])


def _sample_layer(layer, x, rope_tab, sinks, prm, ck, cv, lc, lh, sc, sh):
    NB = x.shape[0]
    BT = SAMPLE_BT
    win, wg, wout, lcw, lvec, scw, scb, hvec, svec, ln = prm
    whole = lambda shape: pl.BlockSpec(shape, lambda i: (0,) * len(shape), pipeline_mode=pl.Buffered(1))
    lwhole = lambda shape: pl.BlockSpec((None,) + shape, lambda i: (layer,) + (0,) * len(shape),
                                        pipeline_mode=pl.Buffered(1))
    in_specs = [
        pl.BlockSpec(memory_space=pltpu.SMEM),
        whole((NB, D_MODEL)),
        whole((1, 3 * LANES)),
        _const_spec((D_MODEL, N_COLS), layer),
        _const_spec((LRU_WIDTH, 2 * LRU_WIDTH), layer),
        _const_spec((D_MIX, D_MODEL), layer),
        _const_spec((CONV_W, LRU_WIDTH), layer),
        _const_spec((4, LRU_WIDTH), layer),
        _const_spec((CONV_W, SSD_CONV_CH), layer),
        _const_spec((1, SSD_CONV_CH), layer),
        _const_spec((2, LANES), layer),
        _const_spec((2, SSD_WIDTH), layer),
        _const_spec((2, D_MODEL), layer),
        pl.BlockSpec((None, BT, WINDOW, ATT_KV_WIDTH), lambda i: (layer, i, 0, 0)),
        pl.BlockSpec((None, BT, WINDOW, ATT_KV_WIDTH), lambda i: (layer, i, 0, 0)),
        lwhole((NB, (CONV_W - 1) * LRU_WIDTH)),
        lwhole((NB, LRU_WIDTH)),
        lwhole((NB, (CONV_W - 1) * SSD_CONV_CH)),
        pl.BlockSpec((None, BT, SSD_HEADS, SSD_HEAD_DIM, SSD_STATE), lambda i: (layer, i, 0, 0, 0)),
    ]
    out_shape = (
        jax.ShapeDtypeStruct((NB, D_MODEL), F32),
        jax.ShapeDtypeStruct((NB, WINDOW, ATT_KV_WIDTH), F32),
        jax.ShapeDtypeStruct((NB, WINDOW, ATT_KV_WIDTH), F32),
        jax.ShapeDtypeStruct((NB, (CONV_W - 1) * LRU_WIDTH), F32),
        jax.ShapeDtypeStruct((NB, LRU_WIDTH), F32),
        jax.ShapeDtypeStruct((NB, (CONV_W - 1) * SSD_CONV_CH), F32),
        jax.ShapeDtypeStruct((NB, SSD_HEADS, SSD_HEAD_DIM, SSD_STATE), F32),
    )
    full = lambda shape: pl.BlockSpec(shape, lambda i: (0,) * len(shape))
    out_specs = (
        full((NB, D_MODEL)),
        pl.BlockSpec((BT, WINDOW, ATT_KV_WIDTH), lambda i: (i, 0, 0)),
        pl.BlockSpec((BT, WINDOW, ATT_KV_WIDTH), lambda i: (i, 0, 0)),
        full((NB, (CONV_W - 1) * LRU_WIDTH)),
        full((NB, LRU_WIDTH)),
        full((NB, (CONV_W - 1) * SSD_CONV_CH)),
        pl.BlockSpec((BT, SSD_HEADS, SSD_HEAD_DIM, SSD_STATE), lambda i: (i, 0, 0, 0)),
    )
    scratch = [
        pltpu.VMEM((NB, N_COLS), F32),
        pltpu.VMEM((NB * ATT_HEADS, LANES), F32),
        pltpu.VMEM((NB * ATT_HEADS, LANES), F32),
        pltpu.VMEM((NB, ATT_KV_WIDTH), F32),
        pltpu.VMEM((NB, ATT_KV_WIDTH), F32),
        pltpu.VMEM((NB, SSD_WIDTH), F32),
        pltpu.VMEM((NB, LANES), F32),
        pltpu.VMEM((NB, SSD_WIDTH), F32),
        pltpu.VMEM((NB, D_MIX), BF16),
    ]
    return pl.pallas_call(
        _sample_kernel,
        grid=(NB // BT,),
        in_specs=in_specs,
        out_specs=out_specs,
        out_shape=out_shape,
        scratch_shapes=scratch,
        compiler_params=pltpu.CompilerParams(
            dimension_semantics=("arbitrary",),
            vmem_limit_bytes=VMEM_LIMIT_BYTES),
        name=f"sample_layer{layer}",
    )(sinks, x, rope_tab, win, wg, wout, lcw, lvec, scw, scb, hvec, svec, ln, ck, cv, lc, lh, sc, sh)


def _rope_table(pos):
    half = ROPE_DIM // 2
    inv = ROPE_THETA ** (-jnp.arange(half, dtype=F32) / half)
    ang = pos.astype(F32)[:, None] * inv[None, :]
    cos, sin = jnp.cos(ang), jnp.sin(ang)
    d = jnp.arange(LANES) % ATT_HEAD_DIM
    fi = d % half
    c = jnp.where(d < ROPE_DIM, cos[:, fi], 1.0)
    s1 = jnp.where(d < half, -sin[:, fi], 0.0)
    s2 = jnp.where((d >= half) & (d < ROPE_DIM), sin[:, fi], 0.0)
    return jnp.concatenate([c, s1, s2], axis=1)


def _prep_params(w_in, w_out, lru_conv_w, lru_conv_b, lru_wa, lru_ba, lru_wx, lru_bx, lru_lambda,
                 ssd_conv_w, ssd_conv_b, ssd_dt_bias, ssd_a_log, ssd_d, ssd_norm_g, ln_g, ln_b):
    hd = ATT_HEAD_DIM
    q = w_in[..., 0:ATT_WIDTH] * ATT_SCALE
    k = w_in[..., ATT_WIDTH:ATT_WIDTH + ATT_KV_WIDTH]
    v = w_in[..., ATT_WIDTH + ATT_KV_WIDTH:ATT_WIDTH + 2 * ATT_KV_WIDTH]
    dup = lambda t: jnp.concatenate([t[..., :hd], t[..., :hd], t[..., hd:], t[..., hd:]], -1)
    rest = w_in[..., ATT_WIDTH + 2 * ATT_KV_WIDTH:]
    pad = jnp.zeros(w_in.shape[:-1] + (N_COLS - DT_OFF - SSD_HEADS,), w_in.dtype)
    win = jnp.concatenate([q, dup(k), dup(v), rest, pad], -1).astype(BF16)
    eye = jnp.eye(LRU_BLOCKS, dtype=F32)
    dense = lambda w: (w[:, :, :, None, :] * eye[None, :, None, :, None]).reshape(DEPTH, LRU_WIDTH, LRU_WIDTH)
    wg = jnp.concatenate([dense(lru_wa), dense(lru_wx)], -1).astype(BF16)
    wout = w_out.astype(BF16)
    lvec = jnp.stack([lru_conv_b, lru_ba, lru_bx, lru_lambda], 1)
    hpad = lambda t: jnp.pad(t, ((0, 0), (0, LANES - SSD_HEADS)))
    hvec = jnp.stack([hpad(ssd_dt_bias), hpad(ssd_a_log)], 1)
    svec = jnp.stack([jnp.repeat(ssd_d, SSD_HEAD_DIM, axis=1), ssd_norm_g], 1)
    ln = jnp.stack([ln_g, ln_b], 1)
    return (win, wg, wout, lru_conv_w, lvec, ssd_conv_w, ssd_conv_b[:, None, :], hvec, svec, ln)


def kernel(x_prompt, x_sample, cache_swa_k, cache_swa_v, state_lru_conv, state_lru_h, state_ssd_conv, state_ssd_h, w_in, w_out, att_sinks, lru_conv_w, lru_conv_b, lru_wa, lru_ba, lru_wx, lru_bx, lru_lambda, ssd_conv_w, ssd_conv_b, ssd_dt_bias, ssd_a_log, ssd_d, ssd_norm_g, ln_g, ln_b):
    prm = _prep_params(w_in, w_out, lru_conv_w, lru_conv_b, lru_wa, lru_ba, lru_wx, lru_bx, lru_lambda,
                       ssd_conv_w, ssd_conv_b, ssd_dt_bias, ssd_a_log, ssd_d, ssd_norm_g, ln_g, ln_b)
    bp, lp, _ = x_prompt.shape
    rope_p = _rope_table(jnp.arange(lp, dtype=jnp.int32))
    xp = x_prompt
    new_p = [[] for _ in range(6)]
    for l in range(DEPTH):
        xp, ko, vo, lco, lho, sco, sho = _prompt_layer(l, xp, rope_p, att_sinks[l], prm)
        st = (ko.reshape(bp, WINDOW, ATT_KV_HEADS, ATT_HEAD_DIM), vo.reshape(bp, WINDOW, ATT_KV_HEADS, ATT_HEAD_DIM),
              lco, lho.reshape(bp, LRU_WIDTH), sco, sho)
        for lst, t in zip(new_p, st):
            lst.append(t)
    outs_p = [jnp.stack(t) for t in new_p]

    nb = x_sample.shape[0]
    rope_s = _rope_table(PAST_LEN + jnp.arange(x_sample.shape[1], dtype=jnp.int32))
    ck = cache_swa_k.reshape(DEPTH, nb, WINDOW, ATT_KV_WIDTH)
    cv = cache_swa_v.reshape(DEPTH, nb, WINDOW, ATT_KV_WIDTH)
    lc = state_lru_conv.reshape(DEPTH, nb, (CONV_W - 1) * LRU_WIDTH)
    sc = state_ssd_conv.reshape(DEPTH, nb, (CONV_W - 1) * SSD_CONV_CH)
    xs = x_sample.reshape(nb, D_MODEL)
    new_s = [[] for _ in range(6)]
    for l in range(DEPTH):
        xs, cko, cvo, lco, lho, sco, sho = _sample_layer(l, xs, rope_s, att_sinks[l], prm, ck, cv, lc,
                                                         state_lru_h, sc, state_ssd_h)
        st = (cko.reshape(nb, WINDOW, ATT_KV_HEADS, ATT_HEAD_DIM), cvo.reshape(nb, WINDOW, ATT_KV_HEADS, ATT_HEAD_DIM),
              lco.reshape(nb, CONV_W - 1, LRU_WIDTH), lho, sco.reshape(nb, CONV_W - 1, SSD_CONV_CH), sho)
        for lst, t in zip(new_s, st):
            lst.append(t)
    outs_s = [jnp.stack(t) for t in new_s]
    return (xp, xs.reshape(x_sample.shape)) + tuple(outs_p) + tuple(outs_s)
```

```python
import functools
import math

import jax
import jax.numpy as jnp
from jax import lax
from jax.experimental import pallas as pl
from jax.experimental.pallas import tpu as pltpu

F32 = jnp.float32
BF16 = jnp.bfloat16

D_MODEL = 1024
DEPTH = 4
PAST_LEN = 8192
D_MIX = 2 * D_MODEL
ATT_HEADS = 8
ATT_KV_HEADS = 2
ATT_HEAD_DIM = 64
ATT_WIDTH = ATT_HEADS * ATT_HEAD_DIM
ATT_KV_WIDTH = ATT_KV_HEADS * ATT_HEAD_DIM
ATT_SCALE = ATT_HEAD_DIM ** -0.5
WINDOW = 128
ROPE_THETA = 500000.0
ROPE_DIM = ATT_HEAD_DIM // 4
LRU_WIDTH = 3 * D_MIX // 8
LRU_BLOCKS = 8
LRU_BLOCK = LRU_WIDTH // LRU_BLOCKS
LRU_C = 8.0
CONV_W = 4
SSD_WIDTH = D_MIX - ATT_WIDTH - LRU_WIDTH
SSD_HEAD_DIM = 64
SSD_HEADS = SSD_WIDTH // SSD_HEAD_DIM
SSD_GROUPS = 2
SSD_STATE = 128
SSD_CONV_CH = SSD_WIDTH + 2 * SSD_GROUPS * SSD_STATE
DEEPNORM_ALPHA = (2.0 * DEPTH) ** 0.25
NORM_EPS = 1e-5

LANES = 128
SUBLANES = 8
HALF = LANES // 2
VMEM_LIMIT_BYTES = 60 * 1024 * 1024

Q_OFF = 0
K_OFF = Q_OFF + ATT_WIDTH
V_OFF = K_OFF + 2 * ATT_KV_WIDTH
GA_OFF = V_OFF + 2 * ATT_KV_WIDTH
XL_OFF = GA_OFF + ATT_WIDTH
GL_OFF = XL_OFF + LRU_WIDTH
Z_OFF = GL_OFF + LRU_WIDTH
XBC_OFF = Z_OFF + SSD_WIDTH
DT_OFF = XBC_OFF + SSD_CONV_CH
N_COLS = DT_OFF + LANES
B_OFF = XBC_OFF + SSD_WIDTH
C_OFF = B_OFF + SSD_GROUPS * SSD_STATE
MIX_LRU = ATT_WIDTH
MIX_SSD = ATT_WIDTH + LRU_WIDTH

N_LRU_TILES = LRU_WIDTH // LANES
N_SSD_TILES = SSD_WIDTH // LANES
TILES_PER_GROUP = N_SSD_TILES // SSD_GROUPS
GROUP_W = SSD_WIDTH // SSD_GROUPS

NEG_BIG = -1e30

PROMPT_TILE = 256
SAMPLE_BT = 8


def _silu(x):
    return x * jax.nn.sigmoid(x)


def _softplus(x):
    return jnp.maximum(x, 0.0) + jnp.log1p(jnp.exp(-jnp.abs(x)))


def _lane_lo(shape):
    return (lax.broadcasted_iota(jnp.int32, shape, len(shape) - 1) % LANES) < HALF


def _rope(t, tab):
    half = ROPE_DIM // 2
    return (t * tab[:, 0:LANES]
            + pltpu.roll(t, LANES - half, 1) * tab[:, LANES:2 * LANES]
            + pltpu.roll(t, half, 1) * tab[:, 2 * LANES:3 * LANES])


def _dot(a, b):
    return jnp.dot(a, b, preferred_element_type=F32)


def _dot_nt(a, b):
    return lax.dot_general(a, b, (((1,), (1,)), ((), ())), preferred_element_type=F32)


def _layer_norm(v, g, b):
    mu = jnp.mean(v, -1, keepdims=True)
    d = v - mu
    var = jnp.mean(d * d, -1, keepdims=True)
    return d * lax.rsqrt(var + NORM_EPS) * g + b


def _lru_coeffs(xl, gates, lvec_ref):
    r = jax.nn.sigmoid(gates[:, 0:LRU_WIDTH] + lvec_ref[1:2, :])
    ig = jax.nn.sigmoid(gates[:, LRU_WIDTH:2 * LRU_WIDTH] + lvec_ref[2:3, :])
    log_a = (-LRU_C * _softplus(-lvec_ref[3:4, :])) * r
    a = jnp.exp(log_a)
    bt = jnp.sqrt(-jnp.tanh(log_a) * (1.0 + a * a)) * (ig * xl)
    return a, bt


def _prompt_kernel(sinks_ref, x_ref, rope_ref, win_ref, wg_ref, wout_ref, lcw_ref, lvec_ref,
                   scw_ref, scb_ref, hvec_ref, svec_ref, ln_ref,
                   y_ref, ko_ref, vo_ref, lco_ref, lho_ref, sco_ref, sho_ref,
                   p_scr, kd_scr, vd_scr, lx_scr, sx_scr, sa_scr, sb_scr, hl_scr, ht_scr, mix_scr):
    T = PROMPT_TILE
    nsub = T // WINDOW
    c = pl.program_id(1)
    last = c == pl.num_programs(1) - 1

    @pl.when(c == 0)
    def _():
        kd_scr[:, 0:WINDOW, :] = jnp.zeros((2, WINDOW, LANES), BF16)
        vd_scr[:, 0:WINDOW, :] = jnp.zeros((2, WINDOW, LANES), BF16)
        lx_scr[0:SUBLANES, :] = jnp.zeros((SUBLANES, LRU_WIDTH), F32)
        sx_scr[0:SUBLANES, :] = jnp.zeros((SUBLANES, SSD_CONV_CH), F32)
        hl_scr[...] = jnp.zeros_like(hl_scr)
        ht_scr[...] = jnp.zeros_like(ht_scr)

    xb = x_ref[...].astype(BF16)
    for lo, hi in ((Q_OFF, XL_OFF), (XL_OFF, Z_OFF), (Z_OFF, N_COLS)):
        p_scr[:, lo:hi] = _dot(xb, win_ref[:, lo:hi])

    lo_sq = _lane_lo((WINDOW, LANES))

    for g in range(ATT_KV_HEADS):
        kd = _rope(p_scr[:, K_OFF + g * LANES:K_OFF + (g + 1) * LANES], rope_ref[...])
        kd_scr[g, WINDOW:WINDOW + T, :] = kd.astype(BF16)
        vd_scr[g, WINDOW:WINDOW + T, :] = p_scr[:, V_OFF + g * LANES:V_OFF + (g + 1) * LANES].astype(BF16)

    row = lax.broadcasted_iota(jnp.int32, (WINDOW, 2 * WINDOW), 0)
    col = lax.broadcasted_iota(jnp.int32, (WINDOW, 2 * WINDOW), 1)
    band = (col >= row) & (col <= row + WINDOW)
    first_lo = jnp.where(c > 0, 0, WINDOW)
    for i in range(nsub):
        r0 = i * WINDOW
        msk = (band & (col >= first_lo)) if i == 0 else band
        tab = rope_ref[r0:r0 + WINDOW, :]
        for cc in range(ATT_HEADS // 2):
            g = cc // (ATT_HEADS // ATT_KV_HEADS // 2)
            kg = kd_scr[g, r0:r0 + 2 * WINDOW, :]
            vg = vd_scr[g, r0:r0 + 2 * WINDOW, :]
            qp = _rope(p_scr[r0:r0 + WINDOW, Q_OFF + cc * LANES:Q_OFF + (cc + 1) * LANES], tab)
            outs = []
            for half in range(2):
                sink = sinks_ref[2 * cc + half]
                qm = jnp.where(lo_sq if half == 0 else jnp.logical_not(lo_sq), qp, 0.0).astype(BF16)
                s = jnp.where(msk, _dot_nt(qm, kg), NEG_BIG)
                m = jnp.maximum(jnp.max(s, -1, keepdims=True), sink)
                e = jnp.exp(s - m)
                den = jnp.sum(e, -1, keepdims=True) + jnp.exp(sink - m)
                outs.append(_dot(e.astype(BF16), vg) * (1.0 / den))
            att = jnp.where(lo_sq, outs[0], outs[1])
            gate = p_scr[r0:r0 + WINDOW, GA_OFF + cc * LANES:GA_OFF + (cc + 1) * LANES]
            mix_scr[r0:r0 + WINDOW, cc * LANES:(cc + 1) * LANES] = (att * _silu(gate)).astype(BF16)
    for g in range(ATT_KV_HEADS):
        kd_scr[g, 0:WINDOW, :] = kd_scr[g, T:T + WINDOW, :]
        vd_scr[g, 0:WINDOW, :] = vd_scr[g, T:T + WINDOW, :]

    P = SUBLANES
    lx_scr[P:P + T, :] = p_scr[:, XL_OFF:XL_OFF + LRU_WIDTH]
    xl = lvec_ref[0:1, :] + lcw_ref[3:4, :] * lx_scr[P:P + T, :]
    for t in range(CONV_W - 1):
        xl = xl + lcw_ref[t:t + 1, :] * lx_scr[P - (CONV_W - 1) + t:P - (CONV_W - 1) + t + T, :]

    lx_scr[0:P, :] = lx_scr[T:T + P, :]

    gates = _dot(xl.astype(BF16), wg_ref[...])
    a, bt = _lru_coeffs(xl, gates, lvec_ref)
    G = T // SUBLANES
    row_g = lax.broadcasted_iota(jnp.int32, (G, LANES), 0)
    for j in range(N_LRU_TILES):
        sa_scr[j] = a[:, j * LANES:(j + 1) * LANES]
        sb_scr[j] = bt[:, j * LANES:(j + 1) * LANES]
    for j in range(N_LRU_TILES):
        ca = sa_scr[j, pl.ds(0, G, stride=SUBLANES), :]
        cb = sb_scr[j, pl.ds(0, G, stride=SUBLANES), :]
        for k in range(1, SUBLANES):
            ak = sa_scr[j, pl.ds(k, G, stride=SUBLANES), :]
            cb = ak * cb + sb_scr[j, pl.ds(k, G, stride=SUBLANES), :]
            ca = ak * ca
            sa_scr[j, pl.ds(k, G, stride=SUBLANES), :] = ca
            sb_scr[j, pl.ds(k, G, stride=SUBLANES), :] = cb
        hin = hl_scr[0:1, j * LANES:(j + 1) * LANES]
        cb = cb + jnp.where(row_g == 0, ca * hin, 0.0)
        d = 1
        while d < G:
            ok = row_g >= d
            cb = cb + ca * jnp.where(ok, pltpu.roll(cb, d, 0), 0.0)
            ca = ca * jnp.where(ok, pltpu.roll(ca, d, 0), 1.0)
            d *= 2
        hprev = jnp.where(row_g == 0, hin, pltpu.roll(cb, 1, 0))
        for k in range(SUBLANES):
            sb_scr[j, pl.ds(k, G, stride=SUBLANES), :] = (
                sb_scr[j, pl.ds(k, G, stride=SUBLANES), :] + sa_scr[j, pl.ds(k, G, stride=SUBLANES), :] * hprev)
        hl_scr[0:1, j * LANES:(j + 1) * LANES] = cb[G - 1:G, :]
        gl = p_scr[:, GL_OFF + j * LANES:GL_OFF + (j + 1) * LANES]
        mix_scr[:, MIX_LRU + j * LANES:MIX_LRU + (j + 1) * LANES] = (sb_scr[j] * _silu(gl)).astype(BF16)

    sx_scr[P:P + T, :] = p_scr[:, XBC_OFF:XBC_OFF + SSD_CONV_CH]
    xc = scb_ref[0:1, :] + scw_ref[3:4, :] * sx_scr[P:P + T, :]
    for t in range(CONV_W - 1):
        xc = xc + scw_ref[t:t + 1, :] * sx_scr[P - (CONV_W - 1) + t:P - (CONV_W - 1) + t + T, :]
    p_scr[:, XBC_OFF:XBC_OFF + SSD_CONV_CH] = _silu(xc)

    sx_scr[0:P, :] = sx_scr[T:T + P, :]

    a_head = -jnp.exp(hvec_ref[1:2, :])
    qi = lax.broadcasted_iota(jnp.int32, (WINDOW, WINDOW), 0)
    si = lax.broadcasted_iota(jnp.int32, (WINDOW, WINDOW), 1)
    causal = si <= qi
    tri = causal.astype(F32)
    for i in range(nsub):
        r0 = i * WINDOW
        dt_c = _softplus(p_scr[r0:r0 + WINDOW, DT_OFF:DT_OFF + LANES] + hvec_ref[0:1, :])
        cs = jnp.dot(tri, dt_c * a_head, precision=lax.Precision.HIGHEST, preferred_element_type=F32)
        cst = cs.T
        ys = [None] * N_SSD_TILES
        for g in range(SSD_GROUPS):
            bg_t = p_scr[r0:r0 + WINDOW, B_OFF + g * SSD_STATE:B_OFF + (g + 1) * SSD_STATE].T.astype(BF16)
            cg = p_scr[r0:r0 + WINDOW, C_OFF + g * SSD_STATE:C_OFF + (g + 1) * SSD_STATE].astype(BF16)
            cbm = _dot(cg, bg_t)
            yo = _dot(cg, ht_scr[g].astype(BF16))
            xdec, edec = [], []
            for pp in range(TILES_PER_GROUP):
                p = g * TILES_PER_GROUP + pp
                bcs, mm, dts = [], [], []
                for h in (2 * p, 2 * p + 1):
                    bc = jnp.broadcast_to(cs[:, h:h + 1], (WINDOW, WINDOW))
                    lmat = jnp.exp(jnp.where(causal, bc - cst[h:h + 1, :], NEG_BIG))
                    mm.append((cbm * lmat).astype(BF16))
                    bcs.append(bc)
                    dts.append(jnp.broadcast_to(dt_c[:, h:h + 1], (WINDOW, LANES)))
                csl = jnp.where(lo_sq, bcs[0], bcs[1])
                ecs = jnp.exp(csl)
                dec = jnp.exp(csl[WINDOW - 1:WINDOW, :] - csl)
                xs_p = p_scr[r0:r0 + WINDOW, XBC_OFF + p * LANES:XBC_OFF + (p + 1) * LANES]
                xdt = xs_p * jnp.where(lo_sq, dts[0], dts[1])
                xdt_b = xdt.astype(BF16)
                y = jnp.where(lo_sq, _dot(mm[0], xdt_b), _dot(mm[1], xdt_b))
                y = y + yo[:, pp * LANES:(pp + 1) * LANES] * ecs
                y = y + svec_ref[0:1, p * LANES:(p + 1) * LANES] * xs_p
                ys[p] = y * _silu(p_scr[r0:r0 + WINDOW, Z_OFF + p * LANES:Z_OFF + (p + 1) * LANES])
                xdec.append((xdt * dec).astype(BF16))
                edec.append(ecs[WINDOW - 1:WINDOW, :])
            ht_scr[g] = (ht_scr[g] * jnp.concatenate(edec, axis=1)
                         + _dot(bg_t, jnp.concatenate(xdec, axis=1)))
        ss = jnp.sum(ys[0] * ys[0], -1, keepdims=True)
        for p in range(1, N_SSD_TILES):
            ss = ss + jnp.sum(ys[p] * ys[p], -1, keepdims=True)
        rinv = lax.rsqrt(ss * (1.0 / SSD_WIDTH) + NORM_EPS)
        for p in range(N_SSD_TILES):
            mix_scr[r0:r0 + WINDOW, MIX_SSD + p * LANES:MIX_SSD + (p + 1) * LANES] = (
                ys[p] * rinv * svec_ref[1:2, p * LANES:(p + 1) * LANES]).astype(BF16)

    out = _dot(mix_scr[...], wout_ref[...])
    y_ref[...] = _layer_norm(DEEPNORM_ALPHA * x_ref[...] + out, ln_ref[0:1, :], ln_ref[1:2, :])

    @pl.when(last)
    def _():
        tab = rope_ref[T - WINDOW:T, :]
        k0 = _rope(p_scr[T - WINDOW:T, K_OFF:K_OFF + LANES], tab)
        k1 = _rope(p_scr[T - WINDOW:T, K_OFF + LANES:K_OFF + 2 * LANES], tab)
        ko_ref[...] = jnp.where(lo_sq, k0, k1)
        vo_ref[...] = jnp.where(lo_sq, p_scr[T - WINDOW:T, V_OFF:V_OFF + LANES],
                                p_scr[T - WINDOW:T, V_OFF + LANES:V_OFF + 2 * LANES])
        lco_ref[...] = lx_scr[P + T - (CONV_W - 1):P + T, :]
        lho_ref[...] = hl_scr[...]
        sco_ref[...] = sx_scr[P + T - (CONV_W - 1):P + T, :]
        for g in range(SSD_GROUPS):
            for pp in range(TILES_PER_GROUP):
                tile = ht_scr[g, :, pp * LANES:(pp + 1) * LANES].T
                h = 2 * (g * TILES_PER_GROUP + pp)
                sho_ref[h] = tile[0:SSD_HEAD_DIM, :]
                sho_ref[h + 1] = tile[SSD_HEAD_DIM:2 * SSD_HEAD_DIM, :]


def _const_spec(shape, layer):
    nd = len(shape)
    return pl.BlockSpec((None,) + tuple(shape), lambda *_: (layer,) + (0,) * nd,
                        pipeline_mode=pl.Buffered(1))


def _prompt_layer(layer, x, rope_tab, sinks, prm):
    B, L, _ = x.shape
    T = PROMPT_TILE
    nc = L // T
    win, wg, wout, lcw, lvec, scw, scb, hvec, svec, ln = prm
    tile = lambda b, c: (b, c, 0)
    per_b = lambda b, c: (b, 0, 0)
    in_specs = [
        pl.BlockSpec(memory_space=pltpu.SMEM),
        pl.BlockSpec((None, T, D_MODEL), tile),
        pl.BlockSpec((T, 3 * LANES), lambda b, c: (c, 0)),
        _const_spec((D_MODEL, N_COLS), layer),
        _const_spec((LRU_WIDTH, 2 * LRU_WIDTH), layer),
        _const_spec((D_MIX, D_MODEL), layer),
        _const_spec((CONV_W, LRU_WIDTH), layer),
        _const_spec((4, LRU_WIDTH), layer),
        _const_spec((CONV_W, SSD_CONV_CH), layer),
        _const_spec((1, SSD_CONV_CH), layer),
        _const_spec((2, LANES), layer),
        _const_spec((2, SSD_WIDTH), layer),
        _const_spec((2, D_MODEL), layer),
    ]
    out_shape = (
        jax.ShapeDtypeStruct((B, L, D_MODEL), F32),
        jax.ShapeDtypeStruct((B, WINDOW, ATT_KV_WIDTH), F32),
        jax.ShapeDtypeStruct((B, WINDOW, ATT_KV_WIDTH), F32),
        jax.ShapeDtypeStruct((B, CONV_W - 1, LRU_WIDTH), F32),
        jax.ShapeDtypeStruct((B, 1, LRU_WIDTH), F32),
        jax.ShapeDtypeStruct((B, CONV_W - 1, SSD_CONV_CH), F32),
        jax.ShapeDtypeStruct((B, SSD_HEADS, SSD_HEAD_DIM, SSD_STATE), F32),
    )
    out_specs = (
        pl.BlockSpec((None, T, D_MODEL), tile),
        pl.BlockSpec((None, WINDOW, ATT_KV_WIDTH), per_b),
        pl.BlockSpec((None, WINDOW, ATT_KV_WIDTH), per_b),
        pl.BlockSpec((None, CONV_W - 1, LRU_WIDTH), per_b),
        pl.BlockSpec((None, 1, LRU_WIDTH), per_b),
        pl.BlockSpec((None, CONV_W - 1, SSD_CONV_CH), per_b),
        pl.BlockSpec((None, SSD_HEADS, SSD_HEAD_DIM, SSD_STATE), lambda b, c: (b, 0, 0, 0)),
    )
    scratch = [
        pltpu.VMEM((T, N_COLS), F32),
        pltpu.VMEM((ATT_KV_HEADS, WINDOW + T, LANES), BF16),
        pltpu.VMEM((ATT_KV_HEADS, WINDOW + T, LANES), BF16),
        pltpu.VMEM((SUBLANES + T, LRU_WIDTH), F32),
        pltpu.VMEM((SUBLANES + T, SSD_CONV_CH), F32),
        pltpu.VMEM((N_LRU_TILES, T, LANES), F32),
        pltpu.VMEM((N_LRU_TILES, T, LANES), F32),
        pltpu.VMEM((1, LRU_WIDTH), F32),
        pltpu.VMEM((SSD_GROUPS, SSD_STATE, GROUP_W), F32),
        pltpu.VMEM((T, D_MIX), BF16),
    ]
    return pl.pallas_call(
        _prompt_kernel,
        grid=(B, nc),
        in_specs=in_specs,
        out_specs=out_specs,
        out_shape=out_shape,
        scratch_shapes=scratch,
        compiler_params=pltpu.CompilerParams(
            dimension_semantics=("arbitrary", "arbitrary"),
            vmem_limit_bytes=VMEM_LIMIT_BYTES),
        name=f"prompt_layer{layer}",
    )(sinks, x, rope_tab, win, wg, wout, lcw, lvec, scw, scb, hvec, svec, ln)


def _conv_step(x_new, st, w_ref, bias):
    cw = x_new.shape[1]
    y = bias + w_ref[CONV_W - 1:CONV_W, :] * x_new
    for t in range(CONV_W - 1):
        y = y + w_ref[t:t + 1, :] * st[:, t * cw:(t + 1) * cw]
    return y


def _sample_kernel(sinks_ref, x_ref, rope_ref, win_ref, wg_ref, wout_ref, lcw_ref, lvec_ref,
                   scw_ref, scb_ref, hvec_ref, svec_ref, ln_ref,
                   ck_ref, cv_ref, lc_ref, lh_ref, sc_ref, sh_ref,
                   y_ref, cko_ref, cvo_ref, lco_ref, lho_ref, sco_ref, sho_ref,
                   p_scr, q_scr, o_scr, kn_scr, vn_scr, xdt_scr, da_scr, y_scr, mix_scr):
    NB = x_ref.shape[0]
    BT = SAMPLE_BT
    i = pl.program_id(0)
    lo_f = _lane_lo((NB, LANES))

    @pl.when(i == 0)
    def _():
        xb = x_ref[...].astype(BF16)
        for lo, hi in ((Q_OFF, XL_OFF), (XL_OFF, Z_OFF), (Z_OFF, N_COLS)):
            p_scr[:, lo:hi] = _dot(xb, win_ref[:, lo:hi])
        tab = rope_ref[...]
        for cc in range(ATT_HEADS // 2):
            g = cc // (ATT_HEADS // ATT_KV_HEADS // 2)
            qp = _rope(p_scr[:, Q_OFF + cc * LANES:Q_OFF + (cc + 1) * LANES], tab)
            for half in range(2):
                t = jnp.where(lo_f if half == 0 else jnp.logical_not(lo_f), qp, 0.0)
                if half != g:
                    t = pltpu.roll(t, HALF, 1)
                q_scr[pl.ds(2 * cc + half, NB, stride=ATT_HEADS), :] = t
        k0 = _rope(p_scr[:, K_OFF:K_OFF + LANES], tab)
        k1 = _rope(p_scr[:, K_OFF + LANES:K_OFF + 2 * LANES], tab)
        kn_scr[...] = jnp.where(lo_f, k0, k1)
        vn_scr[...] = jnp.where(lo_f, p_scr[:, V_OFF:V_OFF + LANES], p_scr[:, V_OFF + LANES:V_OFF + 2 * LANES])
        xin = p_scr[:, XL_OFF:XL_OFF + LRU_WIDTH]
        st = lc_ref[...]
        xl = _conv_step(xin, st, lcw_ref, lvec_ref[0:1, :])
        lco_ref[:, 0:2 * LRU_WIDTH] = st[:, LRU_WIDTH:3 * LRU_WIDTH]
        lco_ref[:, 2 * LRU_WIDTH:3 * LRU_WIDTH] = xin
        gates = _dot(xl.astype(BF16), wg_ref[...])
        a, bt = _lru_coeffs(xl, gates, lvec_ref)
        h1 = a * lh_ref[...] + bt
        lho_ref[...] = h1
        mix_scr[:, MIX_LRU:MIX_LRU + LRU_WIDTH] = (h1 * _silu(p_scr[:, GL_OFF:GL_OFF + LRU_WIDTH])).astype(BF16)
        xin2 = p_scr[:, XBC_OFF:XBC_OFF + SSD_CONV_CH]
        st2 = sc_ref[...]
        xc = _conv_step(xin2, st2, scw_ref, scb_ref[0:1, :])
        sco_ref[:, 0:2 * SSD_CONV_CH] = st2[:, SSD_CONV_CH:3 * SSD_CONV_CH]
        sco_ref[:, 2 * SSD_CONV_CH:3 * SSD_CONV_CH] = xin2
        p_scr[:, XBC_OFF:XBC_OFF + SSD_CONV_CH] = _silu(xc)
        dt = _softplus(p_scr[:, DT_OFF:DT_OFF + LANES] + hvec_ref[0:1, :])
        da_scr[...] = jnp.exp(dt * (-jnp.exp(hvec_ref[1:2, :])))
        for p in range(N_SSD_TILES):
            dte = jnp.where(lo_f, jnp.broadcast_to(dt[:, 2 * p:2 * p + 1], (NB, LANES)),
                            jnp.broadcast_to(dt[:, 2 * p + 1:2 * p + 2], (NB, LANES)))
            xdt_scr[:, p * LANES:(p + 1) * LANES] = p_scr[:, XBC_OFF + p * LANES:XBC_OFF + (p + 1) * LANES] * dte

    r8 = pl.multiple_of(i * BT, BT)
    kn_blk = kn_scr[pl.ds(r8, BT), :]
    vn_blk = vn_scr[pl.ds(r8, BT), :]
    zpad = jnp.zeros((LANES - BT, LANES), F32)
    da_t = jnp.concatenate([da_scr[pl.ds(r8, BT), :], zpad], axis=0).T
    x_t = [jnp.concatenate([xdt_scr[pl.ds(r8, BT), p * LANES:(p + 1) * LANES], zpad], axis=0).T
           for p in range(N_SSD_TILES)]
    b_blk = p_scr[pl.ds(r8, BT), B_OFF:B_OFF + SSD_GROUPS * SSD_STATE]
    c_blk = p_scr[pl.ds(r8, BT), C_OFF:C_OFF + SSD_GROUPS * SSD_STATE].astype(BF16)
    row8 = lax.broadcasted_iota(jnp.int32, (ATT_HEADS, 1), 0)
    sink = jnp.zeros((ATT_HEADS, 1), F32)
    for h in range(ATT_HEADS):
        sink = jnp.where(row8 == h, sinks_ref[h], sink)
    rowb = lax.broadcasted_iota(jnp.int32, (BT, GROUP_W), 0)
    y_acc = [jnp.zeros((BT, GROUP_W), F32) for _ in range(SSD_GROUPS)]
    for bb in range(BT):
        qr = q_scr[pl.ds(pl.multiple_of((r8 + bb) * ATT_HEADS, ATT_HEADS), ATT_HEADS), :]
        s = _dot_nt(qr.astype(BF16), ck_ref[bb].astype(BF16))
        s_new = jnp.sum(qr * kn_blk[bb:bb + 1, :], -1, keepdims=True)
        m = jnp.maximum(jnp.maximum(jnp.max(s, -1, keepdims=True), s_new), sink)
        e = jnp.exp(s - m)
        e_new = jnp.exp(s_new - m)
        den = jnp.sum(e, -1, keepdims=True) + e_new + jnp.exp(sink - m)
        o = _dot(e.astype(BF16), cv_ref[bb].astype(BF16)) + e_new * vn_blk[bb:bb + 1, :]
        o_scr[pl.ds(pl.multiple_of((r8 + bb) * ATT_HEADS, ATT_HEADS), ATT_HEADS), :] = o * (1.0 / den)
        cko_ref[bb, 0:WINDOW - 1, :] = ck_ref[bb, 1:WINDOW, :]
        cko_ref[bb, WINDOW - 1:WINDOW, :] = kn_blk[bb:bb + 1, :]
        cvo_ref[bb, 0:WINDOW - 1, :] = cv_ref[bb, 1:WINDOW, :]
        cvo_ref[bb, WINDOW - 1:WINDOW, :] = vn_blk[bb:bb + 1, :]
        for g in range(SSD_GROUPS):
            brow = b_blk[bb:bb + 1, g * SSD_STATE:(g + 1) * SSD_STATE]
            tiles = []
            for pp in range(TILES_PER_GROUP):
                p = g * TILES_PER_GROUP + pp
                for hh in range(2):
                    h = 2 * p + hh
                    dab = jnp.broadcast_to(da_t[h:h + 1, bb:bb + 1], (SSD_HEAD_DIM, SSD_STATE))
                    xcol = jnp.broadcast_to(x_t[p][hh * SSD_HEAD_DIM:(hh + 1) * SSD_HEAD_DIM, bb:bb + 1],
                                            (SSD_HEAD_DIM, SSD_STATE))
                    h1 = sh_ref[bb, h] * dab + xcol * brow
                    sho_ref[bb, h] = h1
                    tiles.append(h1.astype(BF16))
            res = _dot_nt(c_blk[:, g * SSD_STATE:(g + 1) * SSD_STATE], jnp.concatenate(tiles, axis=0))
            y_acc[g] = jnp.where(rowb == bb, res, y_acc[g])
    for g in range(SSD_GROUPS):
        y_scr[pl.ds(r8, BT), g * GROUP_W:(g + 1) * GROUP_W] = y_acc[g]

    @pl.when(i == pl.num_programs(0) - 1)
    def _():
        for cc in range(ATT_HEADS // 2):
            g = cc // (ATT_HEADS // ATT_KV_HEADS // 2)
            oe = o_scr[pl.ds(2 * cc, NB, stride=ATT_HEADS), :]
            oo = o_scr[pl.ds(2 * cc + 1, NB, stride=ATT_HEADS), :]
            if g == 0:
                oo = pltpu.roll(oo, HALF, 1)
            else:
                oe = pltpu.roll(oe, HALF, 1)
            gate = p_scr[:, GA_OFF + cc * LANES:GA_OFF + (cc + 1) * LANES]
            mix_scr[:, cc * LANES:(cc + 1) * LANES] = (jnp.where(lo_f, oe, oo) * _silu(gate)).astype(BF16)
        y = (y_scr[...] + svec_ref[0:1, :] * p_scr[:, XBC_OFF:XBC_OFF + SSD_WIDTH]) * _silu(p_scr[:, Z_OFF:Z_OFF + SSD_WIDTH])
        y = y * lax.rsqrt(jnp.mean(y * y, -1, keepdims=True) + NORM_EPS) * svec_ref[1:2, :]
        mix_scr[:, MIX_SSD:MIX_SSD + SSD_WIDTH] = y.astype(BF16)
        out = _dot(mix_scr[...], wout_ref[...])
        y_ref[...] = _layer_norm(DEEPNORM_ALPHA * x_ref[...] + out, ln_ref[0:1, :], ln_ref[1:2, :])


def _sample_layer(layer, x, rope_tab, sinks, prm, ck, cv, lc, lh, sc, sh):
    NB = x.shape[0]
    BT = SAMPLE_BT
    win, wg, wout, lcw, lvec, scw, scb, hvec, svec, ln = prm
    whole = lambda shape: pl.BlockSpec(shape, lambda i: (0,) * len(shape), pipeline_mode=pl.Buffered(1))
    lwhole = lambda shape: pl.BlockSpec((None,) + shape, lambda i: (layer,) + (0,) * len(shape),
                                        pipeline_mode=pl.Buffered(1))
    in_specs = [
        pl.BlockSpec(memory_space=pltpu.SMEM),
        whole((NB, D_MODEL)),
        whole((1, 3 * LANES)),
        _const_spec((D_MODEL, N_COLS), layer),
        _const_spec((LRU_WIDTH, 2 * LRU_WIDTH), layer),
        _const_spec((D_MIX, D_MODEL), layer),
        _const_spec((CONV_W, LRU_WIDTH), layer),
        _const_spec((4, LRU_WIDTH), layer),
        _const_spec((CONV_W, SSD_CONV_CH), layer),
        _const_spec((1, SSD_CONV_CH), layer),
        _const_spec((2, LANES), layer),
        _const_spec((2, SSD_WIDTH), layer),
        _const_spec((2, D_MODEL), layer),
        pl.BlockSpec((None, BT, WINDOW, ATT_KV_WIDTH), lambda i: (layer, i, 0, 0)),
        pl.BlockSpec((None, BT, WINDOW, ATT_KV_WIDTH), lambda i: (layer, i, 0, 0)),
        lwhole((NB, (CONV_W - 1) * LRU_WIDTH)),
        lwhole((NB, LRU_WIDTH)),
        lwhole((NB, (CONV_W - 1) * SSD_CONV_CH)),
        pl.BlockSpec((None, BT, SSD_HEADS, SSD_HEAD_DIM, SSD_STATE), lambda i: (layer, i, 0, 0, 0)),
    ]
    out_shape = (
        jax.ShapeDtypeStruct((NB, D_MODEL), F32),
        jax.ShapeDtypeStruct((NB, WINDOW, ATT_KV_WIDTH), F32),
        jax.ShapeDtypeStruct((NB, WINDOW, ATT_KV_WIDTH), F32),
        jax.ShapeDtypeStruct((NB, (CONV_W - 1) * LRU_WIDTH), F32),
        jax.ShapeDtypeStruct((NB, LRU_WIDTH), F32),
        jax.ShapeDtypeStruct((NB, (CONV_W - 1) * SSD_CONV_CH), F32),
        jax.ShapeDtypeStruct((NB, SSD_HEADS, SSD_HEAD_DIM, SSD_STATE), F32),
    )
    full = lambda shape: pl.BlockSpec(shape, lambda i: (0,) * len(shape))
    out_specs = (
        full((NB, D_MODEL)),
        pl.BlockSpec((BT, WINDOW, ATT_KV_WIDTH), lambda i: (i, 0, 0)),
        pl.BlockSpec((BT, WINDOW, ATT_KV_WIDTH), lambda i: (i, 0, 0)),
        full((NB, (CONV_W - 1) * LRU_WIDTH)),
        full((NB, LRU_WIDTH)),
        full((NB, (CONV_W - 1) * SSD_CONV_CH)),
        pl.BlockSpec((BT, SSD_HEADS, SSD_HEAD_DIM, SSD_STATE), lambda i: (i, 0, 0, 0)),
    )
    scratch = [
        pltpu.VMEM((NB, N_COLS), F32),
        pltpu.VMEM((NB * ATT_HEADS, LANES), F32),
        pltpu.VMEM((NB * ATT_HEADS, LANES), F32),
        pltpu.VMEM((NB, ATT_KV_WIDTH), F32),
        pltpu.VMEM((NB, ATT_KV_WIDTH), F32),
        pltpu.VMEM((NB, SSD_WIDTH), F32),
        pltpu.VMEM((NB, LANES), F32),
        pltpu.VMEM((NB, SSD_WIDTH), F32),
        pltpu.VMEM((NB, D_MIX), BF16),
    ]
    return pl.pallas_call(
        _sample_kernel,
        grid=(NB // BT,),
        in_specs=in_specs,
        out_specs=out_specs,
        out_shape=out_shape,
        scratch_shapes=scratch,
        compiler_params=pltpu.CompilerParams(
            dimension_semantics=("arbitrary",),
            vmem_limit_bytes=VMEM_LIMIT_BYTES),
        name=f"sample_layer{layer}",
    )(sinks, x, rope_tab, win, wg, wout, lcw, lvec, scw, scb, hvec, svec, ln, ck, cv, lc, lh, sc, sh)


def _rope_table(pos):
    half = ROPE_DIM // 2
    inv = ROPE_THETA ** (-jnp.arange(half, dtype=F32) / half)
    ang = pos.astype(F32)[:, None] * inv[None, :]
    cos, sin = jnp.cos(ang), jnp.sin(ang)
    d = jnp.arange(LANES) % ATT_HEAD_DIM
    fi = d % half
    c = jnp.where(d < ROPE_DIM, cos[:, fi], 1.0)
    s1 = jnp.where(d < half, -sin[:, fi], 0.0)
    s2 = jnp.where((d >= half) & (d < ROPE_DIM), sin[:, fi], 0.0)
    return jnp.concatenate([c, s1, s2], axis=1)


def _prep_params(w_in, w_out, lru_conv_w, lru_conv_b, lru_wa, lru_ba, lru_wx, lru_bx, lru_lambda,
                 ssd_conv_w, ssd_conv_b, ssd_dt_bias, ssd_a_log, ssd_d, ssd_norm_g, ln_g, ln_b):
    hd = ATT_HEAD_DIM
    q = w_in[..., 0:ATT_WIDTH] * ATT_SCALE
    k = w_in[..., ATT_WIDTH:ATT_WIDTH + ATT_KV_WIDTH]
    v = w_in[..., ATT_WIDTH + ATT_KV_WIDTH:ATT_WIDTH + 2 * ATT_KV_WIDTH]
    dup = lambda t: jnp.concatenate([t[..., :hd], t[..., :hd], t[..., hd:], t[..., hd:]], -1)
    rest = w_in[..., ATT_WIDTH + 2 * ATT_KV_WIDTH:]
    pad = jnp.zeros(w_in.shape[:-1] + (N_COLS - DT_OFF - SSD_HEADS,), w_in.dtype)
    win = jnp.concatenate([q, dup(k), dup(v), rest, pad], -1).astype(BF16)
    eye = jnp.eye(LRU_BLOCKS, dtype=F32)
    dense = lambda w: (w[:, :, :, None, :] * eye[None, :, None, :, None]).reshape(DEPTH, LRU_WIDTH, LRU_WIDTH)
    wg = jnp.concatenate([dense(lru_wa), dense(lru_wx)], -1).astype(BF16)
    wout = w_out.astype(BF16)
    lvec = jnp.stack([lru_conv_b, lru_ba, lru_bx, lru_lambda], 1)
    hpad = lambda t: jnp.pad(t, ((0, 0), (0, LANES - SSD_HEADS)))
    hvec = jnp.stack([hpad(ssd_dt_bias), hpad(ssd_a_log)], 1)
    svec = jnp.stack([jnp.repeat(ssd_d, SSD_HEAD_DIM, axis=1), ssd_norm_g], 1)
    ln = jnp.stack([ln_g, ln_b], 1)
    return (win, wg, wout, lru_conv_w, lvec, ssd_conv_w, ssd_conv_b[:, None, :], hvec, svec, ln)


def kernel(x_prompt, x_sample, cache_swa_k, cache_swa_v, state_lru_conv, state_lru_h, state_ssd_conv, state_ssd_h, w_in, w_out, att_sinks, lru_conv_w, lru_conv_b, lru_wa, lru_ba, lru_wx, lru_bx, lru_lambda, ssd_conv_w, ssd_conv_b, ssd_dt_bias, ssd_a_log, ssd_d, ssd_norm_g, ln_g, ln_b):
    prm = _prep_params(w_in, w_out, lru_conv_w, lru_conv_b, lru_wa, lru_ba, lru_wx, lru_bx, lru_lambda,
                       ssd_conv_w, ssd_conv_b, ssd_dt_bias, ssd_a_log, ssd_d, ssd_norm_g, ln_g, ln_b)
    bp, lp, _ = x_prompt.shape
    rope_p = _rope_table(jnp.arange(lp, dtype=jnp.int32))
    xp = x_prompt
    new_p = [[] for _ in range(6)]
    for l in range(DEPTH):
        xp, ko, vo, lco, lho, sco, sho = _prompt_layer(l, xp, rope_p, att_sinks[l], prm)
        st = (ko.reshape(bp, WINDOW, ATT_KV_HEADS, ATT_HEAD_DIM), vo.reshape(bp, WINDOW, ATT_KV_HEADS, ATT_HEAD_DIM),
              lco, lho.reshape(bp, LRU_WIDTH), sco, sho)
        for lst, t in zip(new_p, st):
            lst.append(t)
    outs_p = [jnp.stack(t) for t in new_p]

    nb = x_sample.shape[0]
    rope_s = _rope_table(PAST_LEN + jnp.arange(x_sample.shape[1], dtype=jnp.int32))
    ck = cache_swa_k.reshape(DEPTH, nb, WINDOW, ATT_KV_WIDTH)
    cv = cache_swa_v.reshape(DEPTH, nb, WINDOW, ATT_KV_WIDTH)
    lc = state_lru_conv.reshape(DEPTH, nb, (CONV_W - 1) * LRU_WIDTH)
    sc = state_ssd_conv.reshape(DEPTH, nb, (CONV_W - 1) * SSD_CONV_CH)
    xs = x_sample.reshape(nb, D_MODEL)
    new_s = [[] for _ in range(6)]
    for l in range(DEPTH):
        xs, cko, cvo, lco, lho, sco, sho = _sample_layer(l, xs, rope_s, att_sinks[l], prm, ck, cv, lc,
                                                         state_lru_h, sc, state_ssd_h)
        st = (cko.reshape(nb, WINDOW, ATT_KV_HEADS, ATT_HEAD_DIM), cvo.reshape(nb, WINDOW, ATT_KV_HEADS, ATT_HEAD_DIM),
              lco.reshape(nb, CONV_W - 1, LRU_WIDTH), lho, sco.reshape(nb, CONV_W - 1, SSD_CONV_CH), sho)
        for lst, t in zip(new_s, st):
            lst.append(t)
    outs_s = [jnp.stack(t) for t in new_s]
    return (xp, xs.reshape(x_sample.shape)) + tuple(outs_p) + tuple(outs_s)
```

```python
import functools
import math

import jax
import jax.numpy as jnp
from jax import lax
from jax.experimental import pallas as pl
from jax.experimental.pallas import tpu as pltpu

F32 = jnp.float32
BF16 = jnp.bfloat16

D_MODEL = 1024
DEPTH = 4
PAST_LEN = 8192
D_MIX = 2 * D_MODEL
ATT_HEADS = 8
ATT_KV_HEADS = 2
ATT_HEAD_DIM = 64
ATT_WIDTH = ATT_HEADS * ATT_HEAD_DIM
ATT_KV_WIDTH = ATT_KV_HEADS * ATT_HEAD_DIM
ATT_SCALE = ATT_HEAD_DIM ** -0.5
WINDOW = 128
ROPE_THETA = 500000.0
ROPE_DIM = ATT_HEAD_DIM // 4
LRU_WIDTH = 3 * D_MIX // 8
LRU_BLOCKS = 8
LRU_BLOCK = LRU_WIDTH // LRU_BLOCKS
LRU_C = 8.0
CONV_W = 4
SSD_WIDTH = D_MIX - ATT_WIDTH - LRU_WIDTH
SSD_HEAD_DIM = 64
SSD_HEADS = SSD_WIDTH // SSD_HEAD_DIM
SSD_GROUPS = 2
SSD_STATE = 128
SSD_CONV_CH = SSD_WIDTH + 2 * SSD_GROUPS * SSD_STATE
DEEPNORM_ALPHA = (2.0 * DEPTH) ** 0.25
NORM_EPS = 1e-5

LANES = 128
SUBLANES = 8
HALF = LANES // 2
VMEM_LIMIT_BYTES = 60 * 1024 * 1024

Q_OFF = 0
K_OFF = Q_OFF + ATT_WIDTH
V_OFF = K_OFF + 2 * ATT_KV_WIDTH
GA_OFF = V_OFF + 2 * ATT_KV_WIDTH
XL_OFF = GA_OFF + ATT_WIDTH
GL_OFF = XL_OFF + LRU_WIDTH
Z_OFF = GL_OFF + LRU_WIDTH
XBC_OFF = Z_OFF + SSD_WIDTH
DT_OFF = XBC_OFF + SSD_CONV_CH
N_COLS = DT_OFF + LANES
B_OFF = XBC_OFF + SSD_WIDTH
C_OFF = B_OFF + SSD_GROUPS * SSD_STATE
MIX_LRU = ATT_WIDTH
MIX_SSD = ATT_WIDTH + LRU_WIDTH

N_LRU_TILES = LRU_WIDTH // LANES
N_SSD_TILES = SSD_WIDTH // LANES
TILES_PER_GROUP = N_SSD_TILES // SSD_GROUPS
GROUP_W = SSD_WIDTH // SSD_GROUPS

NEG_BIG = -1e30

PROMPT_TILE = 256
SAMPLE_BT = 8


def _silu(x):
    return x * jax.nn.sigmoid(x)


def _softplus(x):
    return jnp.maximum(x, 0.0) + jnp.log1p(jnp.exp(-jnp.abs(x)))


def _lane_lo(shape):
    return (lax.broadcasted_iota(jnp.int32, shape, len(shape) - 1) % LANES) < HALF


def _rope(t, tab):
    half = ROPE_DIM // 2
    return (t * tab[:, 0:LANES]
            + pltpu.roll(t, LANES - half, 1) * tab[:, LANES:2 * LANES]
            + pltpu.roll(t, half, 1) * tab[:, 2 * LANES:3 * LANES])


def _dot(a, b):
    return jnp.dot(a, b, preferred_element_type=F32)


def _dot_nt(a, b):
    return lax.dot_general(a, b, (((1,), (1,)), ((), ())), preferred_element_type=F32)


def _layer_norm(v, g, b):
    mu = jnp.mean(v, -1, keepdims=True)
    d = v - mu
    var = jnp.mean(d * d, -1, keepdims=True)
    return d * lax.rsqrt(var + NORM_EPS) * g + b


def _lru_coeffs(xl, gates, lvec_ref):
    r = jax.nn.sigmoid(gates[:, 0:LRU_WIDTH] + lvec_ref[1:2, :])
    ig = jax.nn.sigmoid(gates[:, LRU_WIDTH:2 * LRU_WIDTH] + lvec_ref[2:3, :])
    log_a = (-LRU_C * _softplus(-lvec_ref[3:4, :])) * r
    a = jnp.exp(log_a)
    bt = jnp.sqrt(-jnp.tanh(log_a) * (1.0 + a * a)) * (ig * xl)
    return a, bt


def _project(x_tile, win_ref, dst):
    xb = x_tile.astype(BF16)
    for lo, hi in ((Q_OFF, XL_OFF), (XL_OFF, Z_OFF), (Z_OFF, N_COLS)):
        dst[:, lo:hi] = _dot_nt(xb, win_ref[lo:hi, :])


def _prompt_kernel(sinks_ref, x_ref, rope_ref, win_ref, wg_ref, wout_ref, lcw_ref, lvec_ref,
                   scw_ref, scb_ref, hvec_ref, svec_ref, ln_ref,
                   y_ref, ko_ref, vo_ref, lco_ref, lho_ref, sco_ref, sho_ref,
                   p_scr, kd_scr, vd_scr, lx_scr, sx_scr, sa_scr, sb_scr, hl_scr, ht_scr, mix_scr, xlb_scr):
    T = PROMPT_TILE
    nsub = T // WINDOW
    c = pl.program_id(1)
    last = c == pl.num_programs(1) - 1

    @pl.when(c == 0)
    def _():
        kd_scr[:, 0:WINDOW, :] = jnp.zeros((2, WINDOW, LANES), BF16)
        vd_scr[:, 0:WINDOW, :] = jnp.zeros((2, WINDOW, LANES), BF16)
        lx_scr[0:SUBLANES, :] = jnp.zeros((SUBLANES, LRU_WIDTH), F32)
        sx_scr[0:SUBLANES, :] = jnp.zeros((SUBLANES, SSD_CONV_CH), F32)
        hl_scr[...] = jnp.zeros_like(hl_scr)
        ht_scr[...] = jnp.zeros_like(ht_scr)

    _project(x_ref[...], win_ref, p_scr)

    lo_sq = _lane_lo((WINDOW, LANES))

    for g in range(ATT_KV_HEADS):
        kd = _rope(p_scr[:, K_OFF + g * LANES:K_OFF + (g + 1) * LANES], rope_ref[...])
        kd_scr[g, WINDOW:WINDOW + T, :] = kd.astype(BF16)
        vd_scr[g, WINDOW:WINDOW + T, :] = p_scr[:, V_OFF + g * LANES:V_OFF + (g + 1) * LANES].astype(BF16)

    row = lax.broadcasted_iota(jnp.int32, (WINDOW, 2 * WINDOW), 0)
    col = lax.broadcasted_iota(jnp.int32, (WINDOW, 2 * WINDOW), 1)
    band = (col >= row) & (col <= row + WINDOW)
    first_lo = jnp.where(c > 0, 0, WINDOW)
    for i in range(nsub):
        r0 = i * WINDOW
        msk = (band & (col >= first_lo)) if i == 0 else band
        tab = rope_ref[r0:r0 + WINDOW, :]
        for cc in range(ATT_HEADS // 2):
            g = cc // (ATT_HEADS // ATT_KV_HEADS // 2)
            kg = kd_scr[g, r0:r0 + 2 * WINDOW, :]
            vg = vd_scr[g, r0:r0 + 2 * WINDOW, :]
            qp = _rope(p_scr[r0:r0 + WINDOW, Q_OFF + cc * LANES:Q_OFF + (cc + 1) * LANES], tab)
            outs = []
            for half in range(2):
                sink = sinks_ref[2 * cc + half]
                qm = jnp.where(lo_sq if half == 0 else jnp.logical_not(lo_sq), qp, 0.0).astype(BF16)
                s = jnp.where(msk, _dot_nt(qm, kg), NEG_BIG)
                m = jnp.maximum(jnp.max(s, -1, keepdims=True), sink)
                e = jnp.exp(s - m)
                den = jnp.sum(e, -1, keepdims=True) + jnp.exp(sink - m)
                outs.append(_dot(e.astype(BF16), vg) * (1.0 / den))
            att = jnp.where(lo_sq, outs[0], outs[1])
            gate = p_scr[r0:r0 + WINDOW, GA_OFF + cc * LANES:GA_OFF + (cc + 1) * LANES]
            mix_scr[r0:r0 + WINDOW, cc * LANES:(cc + 1) * LANES] = (att * _silu(gate)).astype(BF16)
    for g in range(ATT_KV_HEADS):
        kd_scr[g, 0:WINDOW, :] = kd_scr[g, T:T + WINDOW, :]
        vd_scr[g, 0:WINDOW, :] = vd_scr[g, T:T + WINDOW, :]

    P = SUBLANES
    H = CONV_W - 1
    for j in range(N_LRU_TILES):
        cs_ = slice(j * LANES, (j + 1) * LANES)
        lx_scr[P:P + T, cs_] = p_scr[:, XL_OFF + j * LANES:XL_OFF + (j + 1) * LANES]
        xl = lvec_ref[0:1, cs_] + lcw_ref[H:H + 1, cs_] * lx_scr[P:P + T, cs_]
        for t in range(H):
            xl = xl + lcw_ref[t:t + 1, cs_] * lx_scr[P - H + t:P - H + t + T, cs_]
        p_scr[:, XL_OFF + j * LANES:XL_OFF + (j + 1) * LANES] = xl
        xlb_scr[:, cs_] = xl.astype(BF16)
    lx_scr[0:P, :] = lx_scr[T:T + P, :]

    G = T // SUBLANES
    row_g = lax.broadcasted_iota(jnp.int32, (G, LANES), 0)
    cl = -LRU_C * _softplus(-lvec_ref[3:4, :])
    for jj in range(N_LRU_TILES // 2):
        c0 = jj * 2 * LANES
        k0 = (c0 // LRU_BLOCK) * LRU_BLOCK // LANES * LANES
        k1 = -(-(-(-(c0 + 2 * LANES) // LRU_BLOCK) * LRU_BLOCK) // LANES) * LANES
        xk = xlb_scr[:, k0:k1]
        gr = _dot(xk, wg_ref[k0:k1, c0:c0 + 2 * LANES])
        gi = _dot(xk, wg_ref[k0:k1, LRU_WIDTH + c0:LRU_WIDTH + c0 + 2 * LANES])
        for u in range(2):
            j = 2 * jj + u
            cs_ = slice(j * LANES, (j + 1) * LANES)
            us = slice(u * LANES, (u + 1) * LANES)
            r = jax.nn.sigmoid(gr[:, us] + lvec_ref[1:2, cs_])
            ig = jax.nn.sigmoid(gi[:, us] + lvec_ref[2:3, cs_])
            log_a = cl[:, cs_] * r
            a = jnp.exp(log_a)
            sa_scr[j] = a
            sb_scr[j] = (jnp.sqrt(-jnp.tanh(log_a) * (1.0 + a * a))
                         * (ig * p_scr[:, XL_OFF + j * LANES:XL_OFF + (j + 1) * LANES]))
    for j in range(N_LRU_TILES):
        ca = sa_scr[j, pl.ds(0, G, stride=SUBLANES), :]
        cb = sb_scr[j, pl.ds(0, G, stride=SUBLANES), :]
        for k in range(1, SUBLANES):
            ak = sa_scr[j, pl.ds(k, G, stride=SUBLANES), :]
            cb = ak * cb + sb_scr[j, pl.ds(k, G, stride=SUBLANES), :]
            ca = ak * ca
            sa_scr[j, pl.ds(k, G, stride=SUBLANES), :] = ca
            sb_scr[j, pl.ds(k, G, stride=SUBLANES), :] = cb
        hin = hl_scr[0:1, j * LANES:(j + 1) * LANES]
        cb = cb + jnp.where(row_g == 0, ca * hin, 0.0)
        d = 1
        while d < G:
            ok = row_g >= d
            cb = cb + ca * jnp.where(ok, pltpu.roll(cb, d, 0), 0.0)
            ca = ca * jnp.where(ok, pltpu.roll(ca, d, 0), 1.0)
            d *= 2
        hprev = jnp.where(row_g == 0, hin, pltpu.roll(cb, 1, 0))
        for k in range(SUBLANES):
            sb_scr[j, pl.ds(k, G, stride=SUBLANES), :] = (
                sb_scr[j, pl.ds(k, G, stride=SUBLANES), :] + sa_scr[j, pl.ds(k, G, stride=SUBLANES), :] * hprev)
        hl_scr[0:1, j * LANES:(j + 1) * LANES] = cb[G - 1:G, :]
        gl = p_scr[:, GL_OFF + j * LANES:GL_OFF + (j + 1) * LANES]
        mix_scr[:, MIX_LRU + j * LANES:MIX_LRU + (j + 1) * LANES] = (sb_scr[j] * _silu(gl)).astype(BF16)

    for j in range(SSD_CONV_CH // LANES):
        cs_ = slice(j * LANES, (j + 1) * LANES)
        sx_scr[P:P + T, cs_] = p_scr[:, XBC_OFF + j * LANES:XBC_OFF + (j + 1) * LANES]
        xc = scb_ref[0:1, cs_] + scw_ref[H:H + 1, cs_] * sx_scr[P:P + T, cs_]
        for t in range(H):
            xc = xc + scw_ref[t:t + 1, cs_] * sx_scr[P - H + t:P - H + t + T, cs_]
        p_scr[:, XBC_OFF + j * LANES:XBC_OFF + (j + 1) * LANES] = _silu(xc)
    sx_scr[0:P, :] = sx_scr[T:T + P, :]

    a_head = -jnp.exp(hvec_ref[1:2, :])
    qi = lax.broadcasted_iota(jnp.int32, (WINDOW, WINDOW), 0)
    si = lax.broadcasted_iota(jnp.int32, (WINDOW, WINDOW), 1)
    causal = si <= qi
    tri = causal.astype(F32)
    for i in range(nsub):
        r0 = i * WINDOW
        dt_c = _softplus(p_scr[r0:r0 + WINDOW, DT_OFF:DT_OFF + LANES] + hvec_ref[0:1, :])
        cs = jnp.dot(tri, dt_c * a_head, precision=lax.Precision.HIGHEST, preferred_element_type=F32)
        cst = cs.T
        ys = [None] * N_SSD_TILES
        for g in range(SSD_GROUPS):
            bg_t = p_scr[r0:r0 + WINDOW, B_OFF + g * SSD_STATE:B_OFF + (g + 1) * SSD_STATE].T.astype(BF16)
            cg = p_scr[r0:r0 + WINDOW, C_OFF + g * SSD_STATE:C_OFF + (g + 1) * SSD_STATE].astype(BF16)
            cbm = _dot(cg, bg_t)
            yo = _dot(cg, ht_scr[g].astype(BF16))
            xdec, edec = [], []
            for pp in range(TILES_PER_GROUP):
                p = g * TILES_PER_GROUP + pp
                bcs, mm, dts = [], [], []
                for h in (2 * p, 2 * p + 1):
                    bc = jnp.broadcast_to(cs[:, h:h + 1], (WINDOW, WINDOW))
                    lmat = jnp.exp(jnp.where(causal, bc - cst[h:h + 1, :], NEG_BIG))
                    mm.append((cbm * lmat).astype(BF16))
                    bcs.append(bc)
                    dts.append(jnp.broadcast_to(dt_c[:, h:h + 1], (WINDOW, LANES)))
                csl = jnp.where(lo_sq, bcs[0], bcs[1])
                ecs = jnp.exp(csl)
                dec = jnp.exp(csl[WINDOW - 1:WINDOW, :] - csl)
                xs_p = p_scr[r0:r0 + WINDOW, XBC_OFF + p * LANES:XBC_OFF + (p + 1) * LANES]
                xdt = xs_p * jnp.where(lo_sq, dts[0], dts[1])
                xdt_b = xdt.astype(BF16)
                y = jnp.where(lo_sq, _dot(mm[0], xdt_b), _dot(mm[1], xdt_b))
                y = y + yo[:, pp * LANES:(pp + 1) * LANES] * ecs
                y = y + svec_ref[0:1, p * LANES:(p + 1) * LANES] * xs_p
                ys[p] = y * _silu(p_scr[r0:r0 + WINDOW, Z_OFF + p * LANES:Z_OFF + (p + 1) * LANES])
                xdec.append((xdt * dec).astype(BF16))
                edec.append(ecs[WINDOW - 1:WINDOW, :])
            ht_scr[g] = (ht_scr[g] * jnp.concatenate(edec, axis=1)
                         + _dot(bg_t, jnp.concatenate(xdec, axis=1)))
        ss = jnp.sum(ys[0] * ys[0], -1, keepdims=True)
        for p in range(1, N_SSD_TILES):
            ss = ss + jnp.sum(ys[p] * ys[p], -1, keepdims=True)
        rinv = lax.rsqrt(ss * (1.0 / SSD_WIDTH) + NORM_EPS)
        for p in range(N_SSD_TILES):
            mix_scr[r0:r0 + WINDOW, MIX_SSD + p * LANES:MIX_SSD + (p + 1) * LANES] = (
                ys[p] * rinv * svec_ref[1:2, p * LANES:(p + 1) * LANES]).astype(BF16)

    out = _dot(mix_scr[...], wout_ref[...])
    y_ref[...] = _layer_norm(DEEPNORM_ALPHA * x_ref[...] + out, ln_ref[0:1, :], ln_ref[1:2, :])

    @pl.when(last)
    def _():
        tab = rope_ref[T - WINDOW:T, :]
        k0 = _rope(p_scr[T - WINDOW:T, K_OFF:K_OFF + LANES], tab)
        k1 = _rope(p_scr[T - WINDOW:T, K_OFF + LANES:K_OFF + 2 * LANES], tab)
        ko_ref[...] = jnp.where(lo_sq, k0, k1)
        vo_ref[...] = jnp.where(lo_sq, p_scr[T - WINDOW:T, V_OFF:V_OFF + LANES],
                                p_scr[T - WINDOW:T, V_OFF + LANES:V_OFF + 2 * LANES])
        lco_ref[...] = lx_scr[P + T - (CONV_W - 1):P + T, :]
        lho_ref[...] = hl_scr[...]
        sco_ref[...] = sx_scr[P + T - (CONV_W - 1):P + T, :]
        for g in range(SSD_GROUPS):
            for pp in range(TILES_PER_GROUP):
                tile = ht_scr[g, :, pp * LANES:(pp + 1) * LANES].T
                h = 2 * (g * TILES_PER_GROUP + pp)
                sho_ref[h] = tile[0:SSD_HEAD_DIM, :]
                sho_ref[h + 1] = tile[SSD_HEAD_DIM:2 * SSD_HEAD_DIM, :]


def _const_spec(shape, layer):
    nd = len(shape)
    return pl.BlockSpec((None,) + tuple(shape), lambda *_: (layer,) + (0,) * nd,
                        pipeline_mode=pl.Buffered(1))


def _prompt_layer(layer, x, rope_tab, sinks, prm):
    B, L, _ = x.shape
    T = PROMPT_TILE
    nc = L // T
    win, wg, wout, lcw, lvec, scw, scb, hvec, svec, ln = prm
    tile = lambda b, c: (b, c, 0)
    per_b = lambda b, c: (b, 0, 0)

    in_specs = [
        pl.BlockSpec(memory_space=pltpu.SMEM),
        pl.BlockSpec((None, T, D_MODEL), tile),
        pl.BlockSpec((T, 3 * LANES), lambda b, c: (c, 0)),
        _const_spec((N_COLS, D_MODEL), layer),
        _const_spec((LRU_WIDTH, 2 * LRU_WIDTH), layer),
        _const_spec((D_MIX, D_MODEL), layer),
        _const_spec((CONV_W, LRU_WIDTH), layer),
        _const_spec((4, LRU_WIDTH), layer),
        _const_spec((CONV_W, SSD_CONV_CH), layer),
        _const_spec((1, SSD_CONV_CH), layer),
        _const_spec((2, LANES), layer),
        _const_spec((2, SSD_WIDTH), layer),
        _const_spec((2, D_MODEL), layer),
    ]
    out_shape = (
        jax.ShapeDtypeStruct((B, L, D_MODEL), F32),
        jax.ShapeDtypeStruct((B, WINDOW, ATT_KV_WIDTH), F32),
        jax.ShapeDtypeStruct((B, WINDOW, ATT_KV_WIDTH), F32),
        jax.ShapeDtypeStruct((B, CONV_W - 1, LRU_WIDTH), F32),
        jax.ShapeDtypeStruct((B, 1, LRU_WIDTH), F32),
        jax.ShapeDtypeStruct((B, CONV_W - 1, SSD_CONV_CH), F32),
        jax.ShapeDtypeStruct((B, SSD_HEADS, SSD_HEAD_DIM, SSD_STATE), F32),
    )
    out_specs = (
        pl.BlockSpec((None, T, D_MODEL), tile),
        pl.BlockSpec((None, WINDOW, ATT_KV_WIDTH), per_b),
        pl.BlockSpec((None, WINDOW, ATT_KV_WIDTH), per_b),
        pl.BlockSpec((None, CONV_W - 1, LRU_WIDTH), per_b),
        pl.BlockSpec((None, 1, LRU_WIDTH), per_b),
        pl.BlockSpec((None, CONV_W - 1, SSD_CONV_CH), per_b),
        pl.BlockSpec((None, SSD_HEADS, SSD_HEAD_DIM, SSD_STATE), lambda b, c: (b, 0, 0, 0)),
    )
    scratch = [
        pltpu.VMEM((T, N_COLS), F32),
        pltpu.VMEM((ATT_KV_HEADS, WINDOW + T, LANES), BF16),
        pltpu.VMEM((ATT_KV_HEADS, WINDOW + T, LANES), BF16),
        pltpu.VMEM((SUBLANES + T, LRU_WIDTH), F32),
        pltpu.VMEM((SUBLANES + T, SSD_CONV_CH), F32),
        pltpu.VMEM((N_LRU_TILES, T, LANES), F32),
        pltpu.VMEM((N_LRU_TILES, T, LANES), F32),
        pltpu.VMEM((1, LRU_WIDTH), F32),
        pltpu.VMEM((SSD_GROUPS, SSD_STATE, GROUP_W), F32),
        pltpu.VMEM((T, D_MIX), BF16),
        pltpu.VMEM((T, LRU_WIDTH), BF16),
    ]
    return pl.pallas_call(
        _prompt_kernel,
        grid=(B, nc),
        in_specs=in_specs,
        out_specs=out_specs,
        out_shape=out_shape,
        scratch_shapes=scratch,
        compiler_params=pltpu.CompilerParams(
            dimension_semantics=("arbitrary", "arbitrary"),
            vmem_limit_bytes=VMEM_LIMIT_BYTES),
        name=f"prompt_layer{layer}",
    )(sinks, x, rope_tab, win, wg, wout, lcw, lvec, scw, scb, hvec, svec, ln)


def _conv_step(x_new, st_ref, sto_ref, w_ref, bias):
    y = bias + w_ref[CONV_W - 1:CONV_W, :] * x_new
    for t in range(CONV_W - 1):
        y = y + w_ref[t:t + 1, :] * st_ref[t]
    for t in range(CONV_W - 2):
        sto_ref[t] = st_ref[t + 1]
    sto_ref[CONV_W - 2] = x_new
    return y


def _sample_kernel(n_aliased, sinks_ref, x_ref, rope_ref, win_ref, wg_ref, wout_ref, lcw_ref, lvec_ref,
                   scw_ref, scb_ref, hvec_ref, svec_ref, ln_ref,
                   ck_ref, cv_ref, lc_ref, lh_ref, sc_ref, sh_ref, *refs):
    (y_ref, cko_ref, cvo_ref, lco_ref, lho_ref, sco_ref, sho_ref,
     p_scr, q_scr, o_scr, kn_scr, vn_scr, xdt_scr, da_scr, y_scr, mix_scr) = refs[n_aliased:]
    NB = x_ref.shape[0]
    BT = SAMPLE_BT
    i = pl.program_id(0)
    lo_f = _lane_lo((NB, LANES))

    @pl.when(i == 0)
    def _():
        _project(x_ref[...], win_ref, p_scr)
        tab = rope_ref[...]
        for cc in range(ATT_HEADS // 2):
            g = cc // (ATT_HEADS // ATT_KV_HEADS // 2)
            qp = _rope(p_scr[:, Q_OFF + cc * LANES:Q_OFF + (cc + 1) * LANES], tab)
            for half in range(2):
                t = jnp.where(lo_f if half == 0 else jnp.logical_not(lo_f), qp, 0.0)
                if half != g:
                    t = pltpu.roll(t, HALF, 1)
                q_scr[pl.ds(2 * cc + half, NB, stride=ATT_HEADS), :] = t
        k0 = _rope(p_scr[:, K_OFF:K_OFF + LANES], tab)
        k1 = _rope(p_scr[:, K_OFF + LANES:K_OFF + 2 * LANES], tab)
        kn_scr[...] = jnp.where(lo_f, k0, k1)
        vn_scr[...] = jnp.where(lo_f, p_scr[:, V_OFF:V_OFF + LANES], p_scr[:, V_OFF + LANES:V_OFF + 2 * LANES])
        xin = p_scr[:, XL_OFF:XL_OFF + LRU_WIDTH]
        xl = _conv_step(xin, lc_ref, lco_ref, lcw_ref, lvec_ref[0:1, :])
        gates = _dot(xl.astype(BF16), wg_ref[...])
        a, bt = _lru_coeffs(xl, gates, lvec_ref)
        h1 = a * lh_ref[...] + bt
        lho_ref[...] = h1
        mix_scr[:, MIX_LRU:MIX_LRU + LRU_WIDTH] = (h1 * _silu(p_scr[:, GL_OFF:GL_OFF + LRU_WIDTH])).astype(BF16)
        xin2 = p_scr[:, XBC_OFF:XBC_OFF + SSD_CONV_CH]
        xc = _conv_step(xin2, sc_ref, sco_ref, scw_ref, scb_ref[0:1, :])
        p_scr[:, XBC_OFF:XBC_OFF + SSD_CONV_CH] = _silu(xc)
        dt = _softplus(p_scr[:, DT_OFF:DT_OFF + LANES] + hvec_ref[0:1, :])
        da_scr[...] = jnp.exp(dt * (-jnp.exp(hvec_ref[1:2, :])))
        for p in range(N_SSD_TILES):
            dte = jnp.where(lo_f, jnp.broadcast_to(dt[:, 2 * p:2 * p + 1], (NB, LANES)),
                            jnp.broadcast_to(dt[:, 2 * p + 1:2 * p + 2], (NB, LANES)))
            xdt_scr[:, p * LANES:(p + 1) * LANES] = p_scr[:, XBC_OFF + p * LANES:XBC_OFF + (p + 1) * LANES] * dte

    r8 = pl.multiple_of(i * BT, BT)
    kn_blk = kn_scr[pl.ds(r8, BT), :]
    vn_blk = vn_scr[pl.ds(r8, BT), :]
    zpad = jnp.zeros((LANES - BT, LANES), F32)
    kn_t = jnp.concatenate([kn_blk, zpad], axis=0).T
    vn_t = jnp.concatenate([vn_blk, zpad], axis=0).T
    newest = lax.broadcasted_iota(jnp.int32, (ATT_KV_WIDTH, WINDOW), 1) == WINDOW - 1
    da_t =jnp.concatenate([da_scr[pl.ds(r8, BT), :], zpad], axis=0).T
    x_t = [jnp.concatenate([xdt_scr[pl.ds(r8, BT), p * LANES:(p + 1) * LANES], zpad], axis=0).T
           for p in range(N_SSD_TILES)]
    b_blk = p_scr[pl.ds(r8, BT), B_OFF:B_OFF + SSD_GROUPS * SSD_STATE]
    c_blk = p_scr[pl.ds(r8, BT), C_OFF:C_OFF + SSD_GROUPS * SSD_STATE].astype(BF16)
    row8 = lax.broadcasted_iota(jnp.int32, (ATT_HEADS, 1), 0)
    sink = jnp.zeros((ATT_HEADS, 1), F32)
    for h in range(ATT_HEADS):
        sink = jnp.where(row8 == h, sinks_ref[h], sink)
    rowb = lax.broadcasted_iota(jnp.int32, (BT, GROUP_W), 0)
    y_acc = [jnp.zeros((BT, GROUP_W), F32) for _ in range(SSD_GROUPS)]
    for bb in range(BT):
        qr = q_scr[pl.ds(pl.multiple_of((r8 + bb) * ATT_HEADS, ATT_HEADS), ATT_HEADS), :]
        kt = ck_ref[bb]
        vt = cv_ref[bb]
        s = _dot(qr.astype(BF16), kt.astype(BF16))
        s_new = jnp.sum(qr * kn_blk[bb:bb + 1, :], -1, keepdims=True)
        m = jnp.maximum(jnp.maximum(jnp.max(s, -1, keepdims=True), s_new), sink)
        e = jnp.exp(s - m)
        e_new = jnp.exp(s_new - m)
        den = jnp.sum(e, -1, keepdims=True) + e_new + jnp.exp(sink - m)
        o = _dot_nt(e.astype(BF16), vt.astype(BF16)) + e_new * vn_blk[bb:bb + 1, :]
        o_scr[pl.ds(pl.multiple_of((r8 + bb) * ATT_HEADS, ATT_HEADS), ATT_HEADS), :] = o * (1.0 / den)
        cko_ref[bb] = jnp.where(newest, kn_t[:, bb:bb + 1], pltpu.roll(kt, WINDOW - 1, 1))
        cvo_ref[bb] = jnp.where(newest, vn_t[:, bb:bb + 1], pltpu.roll(vt, WINDOW - 1, 1))
        for g in range(SSD_GROUPS):
            brow = b_blk[bb:bb + 1, g * SSD_STATE:(g + 1) * SSD_STATE]
            tiles = []
            for pp in range(TILES_PER_GROUP):
                p = g * TILES_PER_GROUP + pp
                for hh in range(2):
                    h = 2 * p + hh
                    dab = jnp.broadcast_to(da_t[h:h + 1, bb:bb + 1], (SSD_HEAD_DIM, SSD_STATE))
                    xcol = jnp.broadcast_to(x_t[p][hh * SSD_HEAD_DIM:(hh + 1) * SSD_HEAD_DIM, bb:bb + 1],
                                            (SSD_HEAD_DIM, SSD_STATE))
                    h1 = sh_ref[bb, h] * dab + xcol * brow
                    sho_ref[bb, h] = h1
                    tiles.append(h1.astype(BF16))
            res = _dot_nt(c_blk[:, g * SSD_STATE:(g + 1) * SSD_STATE], jnp.concatenate(tiles, axis=0))
            y_acc[g] = jnp.where(rowb == bb, res, y_acc[g])
    for g in range(SSD_GROUPS):
        y_scr[pl.ds(r8, BT), g * GROUP_W:(g + 1) * GROUP_W] = y_acc[g]

    @pl.when(i == pl.num_programs(0) - 1)
    def _():
        for cc in range(ATT_HEADS // 2):
            g = cc // (ATT_HEADS // ATT_KV_HEADS // 2)
            oe = o_scr[pl.ds(2 * cc, NB, stride=ATT_HEADS), :]
            oo = o_scr[pl.ds(2 * cc + 1, NB, stride=ATT_HEADS), :]
            if g == 0:
                oo = pltpu.roll(oo, HALF, 1)
            else:
                oe = pltpu.roll(oe, HALF, 1)
            gate = p_scr[:, GA_OFF + cc * LANES:GA_OFF + (cc + 1) * LANES]
            mix_scr[:, cc * LANES:(cc + 1) * LANES] = (jnp.where(lo_f, oe, oo) * _silu(gate)).astype(BF16)
        y = (y_scr[...] + svec_ref[0:1, :] * p_scr[:, XBC_OFF:XBC_OFF + SSD_WIDTH]) * _silu(p_scr[:, Z_OFF:Z_OFF + SSD_WIDTH])
        y = y * lax.rsqrt(jnp.mean(y * y, -1, keepdims=True) + NORM_EPS) * svec_ref[1:2, :]
        mix_scr[:, MIX_SSD:MIX_SSD + SSD_WIDTH] = y.astype(BF16)
        out = _dot(mix_scr[...], wout_ref[...])
        y_ref[...] = _layer_norm(DEEPNORM_ALPHA * x_ref[...] + out, ln_ref[0:1, :], ln_ref[1:2, :])


def _sample_layer(layer, x, rope_tab, sinks, prm, ck, cv, lc, lh, sc, sh, prev):
    NB = x.shape[0]
    BT = SAMPLE_BT
    H = CONV_W - 1
    win, wg, wout, lcw, lvec, scw, scb, hvec, svec, ln = prm
    whole = lambda shape: pl.BlockSpec(shape, lambda i: (0,) * len(shape), pipeline_mode=pl.Buffered(1))
    lwhole = lambda shape: pl.BlockSpec((None,) + shape, lambda i: (layer,) + (0,) * len(shape),
                                        pipeline_mode=pl.Buffered(1))
    rows = lambda shape: pl.BlockSpec((None, BT) + shape, lambda i: (layer, i) + (0,) * len(shape))
    state_specs = [
        rows((ATT_KV_WIDTH, WINDOW)),
        rows((ATT_KV_WIDTH, WINDOW)),
        lwhole((H, NB, LRU_WIDTH)),
        lwhole((NB, LRU_WIDTH)),
        lwhole((H, NB, SSD_CONV_CH)),
        rows((SSD_HEADS, SSD_HEAD_DIM, SSD_STATE)),
    ]
    in_specs = [
        pl.BlockSpec(memory_space=pltpu.SMEM),
        whole((NB, D_MODEL)),
        whole((1, 3 * LANES)),
        _const_spec((N_COLS, D_MODEL), layer),
        _const_spec((LRU_WIDTH, 2 * LRU_WIDTH), layer),
        _const_spec((D_MIX, D_MODEL), layer),
        _const_spec((CONV_W, LRU_WIDTH), layer),
        _const_spec((4, LRU_WIDTH), layer),
        _const_spec((CONV_W, SSD_CONV_CH), layer),
        _const_spec((1, SSD_CONV_CH), layer),
        _const_spec((2, LANES), layer),
        _const_spec((2, SSD_WIDTH), layer),
        _const_spec((2, D_MODEL), layer),
    ] + state_specs
    operands = [sinks, x, rope_tab, win, wg, wout, lcw, lvec, scw, scb, hvec, svec, ln, ck, cv, lc, lh, sc, sh]
    aliases = {}
    if prev is not None:
        for k, arr in enumerate(prev):
            aliases[len(operands)] = 1 + k
            operands.append(arr)
            in_specs.append(pl.BlockSpec(memory_space=pl.ANY))
    out_shape = (jax.ShapeDtypeStruct((NB, D_MODEL), F32),) + tuple(
        jax.ShapeDtypeStruct(a.shape, F32) for a in (ck, cv, lc, lh, sc, sh))
    full = lambda shape: pl.BlockSpec(shape, lambda i: (0,) * len(shape))
    lfull = lambda shape: pl.BlockSpec((None,) + shape, lambda i: (layer,) + (0,) * len(shape))
    out_specs = (
        full((NB, D_MODEL)),
        rows((ATT_KV_WIDTH, WINDOW)),
        rows((ATT_KV_WIDTH, WINDOW)),
        lfull((H, NB, LRU_WIDTH)),
        lfull((NB, LRU_WIDTH)),
        lfull((H, NB, SSD_CONV_CH)),
        rows((SSD_HEADS, SSD_HEAD_DIM, SSD_STATE)),
    )
    scratch = [
        pltpu.VMEM((NB, N_COLS), F32),
        pltpu.VMEM((NB * ATT_HEADS, LANES), F32),
        pltpu.VMEM((NB * ATT_HEADS, LANES), F32),
        pltpu.VMEM((NB, ATT_KV_WIDTH), F32),
        pltpu.VMEM((NB, ATT_KV_WIDTH), F32),
        pltpu.VMEM((NB, SSD_WIDTH), F32),
        pltpu.VMEM((NB, LANES), F32),
        pltpu.VMEM((NB, SSD_WIDTH), F32),
        pltpu.VMEM((NB, D_MIX), BF16),
    ]
    return pl.pallas_call(
        functools.partial(_sample_kernel, len(aliases)),
        grid=(NB // BT,),
        in_specs=in_specs,
        out_specs=out_specs,
        out_shape=out_shape,
        scratch_shapes=scratch,
        input_output_aliases=aliases,
        compiler_params=pltpu.CompilerParams(
            dimension_semantics=("arbitrary",),
            vmem_limit_bytes=VMEM_LIMIT_BYTES),
        name=f"sample_layer{layer}",
    )(*operands)


def _rope_table(pos):
    half = ROPE_DIM // 2
    inv = ROPE_THETA ** (-jnp.arange(half, dtype=F32) / half)
    ang = pos.astype(F32)[:, None] * inv[None, :]
    cos, sin = jnp.cos(ang), jnp.sin(ang)
    d = jnp.arange(LANES) % ATT_HEAD_DIM
    fi = d % half
    c = jnp.where(d < ROPE_DIM, cos[:, fi], 1.0)
    s1 = jnp.where(d < half, -sin[:, fi], 0.0)
    s2 = jnp.where((d >= half) & (d < ROPE_DIM), sin[:, fi], 0.0)
    return jnp.concatenate([c, s1, s2], axis=1)


def _prep_params(w_in, w_out, lru_conv_w, lru_conv_b, lru_wa, lru_ba, lru_wx, lru_bx, lru_lambda,
                 ssd_conv_w, ssd_conv_b, ssd_dt_bias, ssd_a_log, ssd_d, ssd_norm_g, ln_g, ln_b):
    hd = ATT_HEAD_DIM
    wt = jnp.transpose(w_in, (0, 2, 1)).astype(BF16)
    q = wt[:, 0:ATT_WIDTH] * jnp.asarray(ATT_SCALE, BF16)
    k = wt[:, ATT_WIDTH:ATT_WIDTH + ATT_KV_WIDTH]
    v = wt[:, ATT_WIDTH + ATT_KV_WIDTH:ATT_WIDTH + 2 * ATT_KV_WIDTH]
    dup = lambda t: jnp.concatenate([t[:, :hd], t[:, :hd], t[:, hd:], t[:, hd:]], 1)
    rest = wt[:, ATT_WIDTH + 2 * ATT_KV_WIDTH:]
    pad = jnp.zeros((DEPTH, N_COLS - DT_OFF - SSD_HEADS, D_MODEL), BF16)
    win = jnp.concatenate([q, dup(k), dup(v), rest, pad], 1)

    def dense(w):
        rows = [jnp.pad(w[:, n], ((0, 0), (0, 0), (n * LRU_BLOCK, LRU_WIDTH - (n + 1) * LRU_BLOCK)))
                for n in range(LRU_BLOCKS)]
        return jnp.concatenate(rows, 1)

    wg = jnp.concatenate([dense(lru_wa.astype(BF16)), dense(lru_wx.astype(BF16))], -1)
    wout = w_out.astype(BF16)
    lvec = jnp.stack([lru_conv_b, lru_ba, lru_bx, lru_lambda], 1)
    hpad = lambda t: jnp.pad(t, ((0, 0), (0, LANES - SSD_HEADS)))
    hvec = jnp.stack([hpad(ssd_dt_bias), hpad(ssd_a_log)], 1)
    svec = jnp.stack([jnp.repeat(ssd_d, SSD_HEAD_DIM, axis=1), ssd_norm_g], 1)
    ln = jnp.stack([ln_g, ln_b], 1)
    return (win, wg, wout, lru_conv_w, lvec, ssd_conv_w, ssd_conv_b[:, None, :], hvec, svec, ln)


def kernel(x_prompt, x_sample, cache_swa_k, cache_swa_v, state_lru_conv, state_lru_h, state_ssd_conv, state_ssd_h, w_in, w_out, att_sinks, lru_conv_w, lru_conv_b, lru_wa, lru_ba, lru_wx, lru_bx, lru_lambda, ssd_conv_w, ssd_conv_b, ssd_dt_bias, ssd_a_log, ssd_d, ssd_norm_g, ln_g, ln_b):
    prm = _prep_params(w_in, w_out, lru_conv_w, lru_conv_b, lru_wa, lru_ba, lru_wx, lru_bx, lru_lambda,
                       ssd_conv_w, ssd_conv_b, ssd_dt_bias, ssd_a_log, ssd_d, ssd_norm_g, ln_g, ln_b)
    bp, lp, _ = x_prompt.shape
    rope_p = _rope_table(jnp.arange(lp, dtype=jnp.int32))
    xp = x_prompt
    new_p = [[] for _ in range(6)]
    for l in range(DEPTH):
        xp, ko, vo, lco, lho, sco, sho = _prompt_layer(l, xp, rope_p, att_sinks[l], prm)
        st = (ko.reshape(bp, WINDOW, ATT_KV_HEADS, ATT_HEAD_DIM), vo.reshape(bp, WINDOW, ATT_KV_HEADS, ATT_HEAD_DIM),
              lco, lho.reshape(bp, LRU_WIDTH), sco, sho)
        for lst, t in zip(new_p, st):
            lst.append(t)
    outs_p = [jnp.stack(t) for t in new_p]

    nb = x_sample.shape[0]
    rope_s = _rope_table(PAST_LEN + jnp.arange(x_sample.shape[1], dtype=jnp.int32))
    to_kt = lambda t: jnp.transpose(t, (0, 1, 3, 4, 2)).reshape(DEPTH, nb, ATT_KV_WIDTH, WINDOW)
    from_kt = lambda t: jnp.transpose(t.reshape(DEPTH, nb, ATT_KV_HEADS, ATT_HEAD_DIM, WINDOW), (0, 1, 4, 2, 3))
    swap = lambda t: jnp.transpose(t, (0, 2, 1, 3))
    state = (to_kt(cache_swa_k), to_kt(cache_swa_v), swap(state_lru_conv), state_lru_h, swap(state_ssd_conv),
             state_ssd_h)
    xs = x_sample.reshape(nb, D_MODEL)
    new = None
    for l in range(DEPTH):
        xs, *new = _sample_layer(l, xs, rope_s, att_sinks[l], prm, *state, new)
    cko, cvo, lco, lho, sco, sho = new
    outs_s = (from_kt(cko), from_kt(cvo), swap(lco), lho, swap(sco), sho)
    return (xp, xs.reshape(x_sample.shape)) + tuple(outs_p) + tuple(outs_s)
```

```python
import functools
import math

import jax
import jax.numpy as jnp
from jax import lax
from jax.experimental import pallas as pl
from jax.experimental.pallas import tpu as pltpu

F32 = jnp.float32
BF16 = jnp.bfloat16

D_MODEL = 1024
DEPTH = 4
PAST_LEN = 8192
D_MIX = 2 * D_MODEL
ATT_HEADS = 8
ATT_KV_HEADS = 2
ATT_HEAD_DIM = 64
ATT_WIDTH = ATT_HEADS * ATT_HEAD_DIM
ATT_KV_WIDTH = ATT_KV_HEADS * ATT_HEAD_DIM
ATT_SCALE = ATT_HEAD_DIM ** -0.5
WINDOW = 128
ROPE_THETA = 500000.0
ROPE_DIM = ATT_HEAD_DIM // 4
LRU_WIDTH = 3 * D_MIX // 8
LRU_BLOCKS = 8
LRU_BLOCK = LRU_WIDTH // LRU_BLOCKS
LRU_C = 8.0
CONV_W = 4
SSD_WIDTH = D_MIX - ATT_WIDTH - LRU_WIDTH
SSD_HEAD_DIM = 64
SSD_HEADS = SSD_WIDTH // SSD_HEAD_DIM
SSD_GROUPS = 2
SSD_STATE = 128
SSD_CONV_CH = SSD_WIDTH + 2 * SSD_GROUPS * SSD_STATE
DEEPNORM_ALPHA = (2.0 * DEPTH) ** 0.25
NORM_EPS = 1e-5

LANES = 128
SUBLANES = 8
HALF = LANES // 2
VMEM_LIMIT_BYTES = 60 * 1024 * 1024

Q_OFF = 0
K_OFF = Q_OFF + ATT_WIDTH
V_OFF = K_OFF + 2 * ATT_KV_WIDTH
GA_OFF = V_OFF + 2 * ATT_KV_WIDTH
XL_OFF = GA_OFF + ATT_WIDTH
GL_OFF = XL_OFF + LRU_WIDTH
Z_OFF = GL_OFF + LRU_WIDTH
XBC_OFF = Z_OFF + SSD_WIDTH
DT_OFF = XBC_OFF + SSD_CONV_CH
N_COLS = DT_OFF + LANES
W_IN_COLS = 2 * ATT_WIDTH + 2 * ATT_KV_WIDTH + 2 * LRU_WIDTH + SSD_WIDTH + SSD_CONV_CH + SSD_HEADS
AUX_COLS = 4 * ATT_KV_WIDTH + LANES
B_OFF = XBC_OFF + SSD_WIDTH
C_OFF = B_OFF + SSD_GROUPS * SSD_STATE
MIX_LRU = ATT_WIDTH
MIX_SSD = ATT_WIDTH + LRU_WIDTH

N_LRU_TILES = LRU_WIDTH // LANES
N_SSD_TILES = SSD_WIDTH // LANES
TILES_PER_GROUP = N_SSD_TILES // SSD_GROUPS
GROUP_W = SSD_WIDTH // SSD_GROUPS

NEG_BIG = -1e30

PROMPT_TILE = 256
SAMPLE_BT = 8


def _silu(x):
    return x * jax.nn.sigmoid(x)


def _softplus(x):
    return jnp.maximum(x, 0.0) + jnp.log1p(jnp.exp(-jnp.abs(x)))


def _lane_lo(shape):
    return (lax.broadcasted_iota(jnp.int32, shape, len(shape) - 1) % LANES) < HALF


def _rope(t, tab):
    half = ROPE_DIM // 2
    return (t * tab[:, 0:LANES]
            + pltpu.roll(t, LANES - half, 1) * tab[:, LANES:2 * LANES]
            + pltpu.roll(t, half, 1) * tab[:, 2 * LANES:3 * LANES])


def _dot(a, b):
    return jnp.dot(a, b, preferred_element_type=F32)


def _dot_nt(a, b):
    return lax.dot_general(a, b, (((1,), (1,)), ((), ())), preferred_element_type=F32)


def _layer_norm(v, g, b):
    mu = jnp.mean(v, -1, keepdims=True)
    d = v - mu
    var = jnp.mean(d * d, -1, keepdims=True)
    return d * lax.rsqrt(var + NORM_EPS) * g + b


def _lru_coeffs(xl, gates, lvec_ref):
    r = jax.nn.sigmoid(gates[:, 0:LRU_WIDTH] + lvec_ref[1:2, :])
    ig = jax.nn.sigmoid(gates[:, LRU_WIDTH:2 * LRU_WIDTH] + lvec_ref[2:3, :])
    log_a = (-LRU_C * _softplus(-lvec_ref[3:4, :])) * r
    a = jnp.exp(log_a)
    bt = jnp.sqrt(-jnp.tanh(log_a) * (1.0 + a * a)) * (ig * xl)
    return a, bt


def _project(x_tile, win_ref, aux_ref, dst):
    xb = x_tile.astype(BF16)
    kv0 = ATT_WIDTH + 2 * ATT_KV_WIDTH
    dst[:, Q_OFF:K_OFF] = _dot(xb, win_ref[:, 0:ATT_WIDTH])
    dst[:, K_OFF:GA_OFF] = _dot(xb, aux_ref[:, 0:GA_OFF - K_OFF])
    dst[:, GA_OFF:Z_OFF] = _dot(xb, win_ref[:, kv0:kv0 + Z_OFF - GA_OFF])
    dst[:, Z_OFF:DT_OFF] = _dot(xb, win_ref[:, kv0 + Z_OFF - GA_OFF:kv0 + DT_OFF - GA_OFF])
    dst[:, DT_OFF:N_COLS] = _dot(xb, aux_ref[:, GA_OFF - K_OFF:GA_OFF - K_OFF + LANES])


def _prompt_kernel(sinks_ref, x_ref, rope_ref, win_ref, aux_ref, wg_ref, wout_ref, lcw_ref, lvec_ref,
                   scw_ref, scb_ref, hvec_ref, svec_ref, ln_ref,
                   y_ref, ko_ref, vo_ref, lco_ref, lho_ref, sco_ref, sho_ref,
                   p_scr, kd_scr, vd_scr, lx_scr, sx_scr, sa_scr, sb_scr, hl_scr, ht_scr, mix_scr, xlb_scr):
    T = PROMPT_TILE
    nsub = T // WINDOW
    c = pl.program_id(1)
    last = c == pl.num_programs(1) - 1

    @pl.when(c == 0)
    def _():
        kd_scr[:, 0:WINDOW, :] = jnp.zeros((2, WINDOW, LANES), BF16)
        vd_scr[:, 0:WINDOW, :] = jnp.zeros((2, WINDOW, LANES), BF16)
        lx_scr[0:SUBLANES, :] = jnp.zeros((SUBLANES, LRU_WIDTH), F32)
        sx_scr[0:SUBLANES, :] = jnp.zeros((SUBLANES, SSD_CONV_CH), F32)
        hl_scr[...] = jnp.zeros_like(hl_scr)
        ht_scr[...] = jnp.zeros_like(ht_scr)

    _project(x_ref[...], win_ref, aux_ref, p_scr)

    lo_sq = _lane_lo((WINDOW, LANES))

    for g in range(ATT_KV_HEADS):
        kd = _rope(p_scr[:, K_OFF + g * LANES:K_OFF + (g + 1) * LANES], rope_ref[...])
        kd_scr[g, WINDOW:WINDOW + T, :] = kd.astype(BF16)
        vd_scr[g, WINDOW:WINDOW + T, :] = p_scr[:, V_OFF + g * LANES:V_OFF + (g + 1) * LANES].astype(BF16)

    row = lax.broadcasted_iota(jnp.int32, (WINDOW, 2 * WINDOW), 0)
    col = lax.broadcasted_iota(jnp.int32, (WINDOW, 2 * WINDOW), 1)
    band = (col >= row) & (col <= row + WINDOW)
    first_lo = jnp.where(c > 0, 0, WINDOW)
    for i in range(nsub):
        r0 = i * WINDOW
        msk = (band & (col >= first_lo)) if i == 0 else band
        tab = rope_ref[r0:r0 + WINDOW, :]
        for cc in range(ATT_HEADS // 2):
            g = cc // (ATT_HEADS // ATT_KV_HEADS // 2)
            kg = kd_scr[g, r0:r0 + 2 * WINDOW, :]
            vg = vd_scr[g, r0:r0 + 2 * WINDOW, :]
            qp = _rope(p_scr[r0:r0 + WINDOW, Q_OFF + cc * LANES:Q_OFF + (cc + 1) * LANES], tab) * ATT_SCALE
            outs = []
            for half in range(2):
                sink = sinks_ref[2 * cc + half]
                qm = jnp.where(lo_sq if half == 0 else jnp.logical_not(lo_sq), qp, 0.0).astype(BF16)
                s = jnp.where(msk, _dot_nt(qm, kg), NEG_BIG)
                m = jnp.maximum(jnp.max(s, -1, keepdims=True), sink)
                e = jnp.exp(s - m)
                den = jnp.sum(e, -1, keepdims=True) + jnp.exp(sink - m)
                outs.append(_dot(e.astype(BF16), vg) * (1.0 / den))
            att = jnp.where(lo_sq, outs[0], outs[1])
            gate = p_scr[r0:r0 + WINDOW, GA_OFF + cc * LANES:GA_OFF + (cc + 1) * LANES]
            mix_scr[r0:r0 + WINDOW, cc * LANES:(cc + 1) * LANES] = (att * _silu(gate)).astype(BF16)
    for g in range(ATT_KV_HEADS):
        kd_scr[g, 0:WINDOW, :] = kd_scr[g, T:T + WINDOW, :]
        vd_scr[g, 0:WINDOW, :] = vd_scr[g, T:T + WINDOW, :]

    P = SUBLANES
    H = CONV_W - 1
    for j in range(N_LRU_TILES):
        cs_ = slice(j * LANES, (j + 1) * LANES)
        lx_scr[P:P + T, cs_] = p_scr[:, XL_OFF + j * LANES:XL_OFF + (j + 1) * LANES]
        xl = lvec_ref[0:1, cs_] + lcw_ref[H:H + 1, cs_] * lx_scr[P:P + T, cs_]
        for t in range(H):
            xl = xl + lcw_ref[t:t + 1, cs_] * lx_scr[P - H + t:P - H + t + T, cs_]
        p_scr[:, XL_OFF + j * LANES:XL_OFF + (j + 1) * LANES] = xl
        xlb_scr[:, cs_] = xl.astype(BF16)
    lx_scr[0:P, :] = lx_scr[T:T + P, :]

    G = T // SUBLANES
    row_g = lax.broadcasted_iota(jnp.int32, (G, LANES), 0)
    cl = -LRU_C * _softplus(-lvec_ref[3:4, :])
    for jj in range(N_LRU_TILES // 2):
        c0 = jj * 2 * LANES
        k0 = (c0 // LRU_BLOCK) * LRU_BLOCK // LANES * LANES
        k1 = -(-(-(-(c0 + 2 * LANES) // LRU_BLOCK) * LRU_BLOCK) // LANES) * LANES
        xk = xlb_scr[:, k0:k1]
        gr = _dot(xk, wg_ref[k0:k1, c0:c0 + 2 * LANES])
        gi = _dot(xk, wg_ref[k0:k1, LRU_WIDTH + c0:LRU_WIDTH + c0 + 2 * LANES])
        for u in range(2):
            j = 2 * jj + u
            cs_ = slice(j * LANES, (j + 1) * LANES)
            us = slice(u * LANES, (u + 1) * LANES)
            r = jax.nn.sigmoid(gr[:, us] + lvec_ref[1:2, cs_])
            ig = jax.nn.sigmoid(gi[:, us] + lvec_ref[2:3, cs_])
            log_a = cl[:, cs_] * r
            a = jnp.exp(log_a)
            sa_scr[j] = a
            sb_scr[j] = (jnp.sqrt(-jnp.tanh(log_a) * (1.0 + a * a))
                         * (ig * p_scr[:, XL_OFF + j * LANES:XL_OFF + (j + 1) * LANES]))
    for j in range(N_LRU_TILES):
        ca = sa_scr[j, pl.ds(0, G, stride=SUBLANES), :]
        cb = sb_scr[j, pl.ds(0, G, stride=SUBLANES), :]
        for k in range(1, SUBLANES):
            ak = sa_scr[j, pl.ds(k, G, stride=SUBLANES), :]
            cb = ak * cb + sb_scr[j, pl.ds(k, G, stride=SUBLANES), :]
            ca = ak * ca
            sa_scr[j, pl.ds(k, G, stride=SUBLANES), :] = ca
            sb_scr[j, pl.ds(k, G, stride=SUBLANES), :] = cb
        hin = hl_scr[0:1, j * LANES:(j + 1) * LANES]
        cb = cb + jnp.where(row_g == 0, ca * hin, 0.0)
        d = 1
        while d < G:
            ok = row_g >= d
            cb = cb + ca * jnp.where(ok, pltpu.roll(cb, d, 0), 0.0)
            ca = ca * jnp.where(ok, pltpu.roll(ca, d, 0), 1.0)
            d *= 2
        hprev = jnp.where(row_g == 0, hin, pltpu.roll(cb, 1, 0))
        for k in range(SUBLANES):
            sb_scr[j, pl.ds(k, G, stride=SUBLANES), :] = (
                sb_scr[j, pl.ds(k, G, stride=SUBLANES), :] + sa_scr[j, pl.ds(k, G, stride=SUBLANES), :] * hprev)
        hl_scr[0:1, j * LANES:(j + 1) * LANES] = cb[G - 1:G, :]
        gl = p_scr[:, GL_OFF + j * LANES:GL_OFF + (j + 1) * LANES]
        mix_scr[:, MIX_LRU + j * LANES:MIX_LRU + (j + 1) * LANES] = (sb_scr[j] * _silu(gl)).astype(BF16)

    for j in range(SSD_CONV_CH // LANES):
        cs_ = slice(j * LANES, (j + 1) * LANES)
        sx_scr[P:P + T, cs_] = p_scr[:, XBC_OFF + j * LANES:XBC_OFF + (j + 1) * LANES]
        xc = scb_ref[0:1, cs_] + scw_ref[H:H + 1, cs_] * sx_scr[P:P + T, cs_]
        for t in range(H):
            xc = xc + scw_ref[t:t + 1, cs_] * sx_scr[P - H + t:P - H + t + T, cs_]
        p_scr[:, XBC_OFF + j * LANES:XBC_OFF + (j + 1) * LANES] = _silu(xc)
    sx_scr[0:P, :] = sx_scr[T:T + P, :]

    a_head = -jnp.exp(hvec_ref[1:2, :])
    qi = lax.broadcasted_iota(jnp.int32, (WINDOW, WINDOW), 0)
    si = lax.broadcasted_iota(jnp.int32, (WINDOW, WINDOW), 1)
    causal = si <= qi
    tri = causal.astype(F32)
    for i in range(nsub):
        r0 = i * WINDOW
        dt_c = _softplus(p_scr[r0:r0 + WINDOW, DT_OFF:DT_OFF + LANES] + hvec_ref[0:1, :])
        cs = jnp.dot(tri, dt_c * a_head, precision=lax.Precision.HIGHEST, preferred_element_type=F32)
        cst = cs.T
        ys = [None] * N_SSD_TILES
        for g in range(SSD_GROUPS):
            bg_t = p_scr[r0:r0 + WINDOW, B_OFF + g * SSD_STATE:B_OFF + (g + 1) * SSD_STATE].T.astype(BF16)
            cg = p_scr[r0:r0 + WINDOW, C_OFF + g * SSD_STATE:C_OFF + (g + 1) * SSD_STATE].astype(BF16)
            cbm = _dot(cg, bg_t)
            yo = _dot(cg, ht_scr[g].astype(BF16))
            xdec, edec = [], []
            for pp in range(TILES_PER_GROUP):
                p = g * TILES_PER_GROUP + pp
                bcs, mm, dts = [], [], []
                for h in (2 * p, 2 * p + 1):
                    bc = jnp.broadcast_to(cs[:, h:h + 1], (WINDOW, WINDOW))
                    lmat = jnp.exp(jnp.where(causal, bc - cst[h:h + 1, :], NEG_BIG))
                    mm.append((cbm * lmat).astype(BF16))
                    bcs.append(bc)
                    dts.append(jnp.broadcast_to(dt_c[:, h:h + 1], (WINDOW, LANES)))
                csl = jnp.where(lo_sq, bcs[0], bcs[1])
                ecs = jnp.exp(csl)
                dec = jnp.exp(csl[WINDOW - 1:WINDOW, :] - csl)
                xs_p = p_scr[r0:r0 + WINDOW, XBC_OFF + p * LANES:XBC_OFF + (p + 1) * LANES]
                xdt = xs_p * jnp.where(lo_sq, dts[0], dts[1])
                xdt_b = xdt.astype(BF16)
                y = jnp.where(lo_sq, _dot(mm[0], xdt_b), _dot(mm[1], xdt_b))
                y = y + yo[:, pp * LANES:(pp + 1) * LANES] * ecs
                y = y + svec_ref[0:1, p * LANES:(p + 1) * LANES] * xs_p
                ys[p] = y * _silu(p_scr[r0:r0 + WINDOW, Z_OFF + p * LANES:Z_OFF + (p + 1) * LANES])
                xdec.append((xdt * dec).astype(BF16))
                edec.append(ecs[WINDOW - 1:WINDOW, :])
            ht_scr[g] = (ht_scr[g] * jnp.concatenate(edec, axis=1)
                         + _dot(bg_t, jnp.concatenate(xdec, axis=1)))
        ss = jnp.sum(ys[0] * ys[0], -1, keepdims=True)
        for p in range(1, N_SSD_TILES):
            ss = ss + jnp.sum(ys[p] * ys[p], -1, keepdims=True)
        rinv = lax.rsqrt(ss * (1.0 / SSD_WIDTH) + NORM_EPS)
        for p in range(N_SSD_TILES):
            mix_scr[r0:r0 + WINDOW, MIX_SSD + p * LANES:MIX_SSD + (p + 1) * LANES] = (
                ys[p] * rinv * svec_ref[1:2, p * LANES:(p + 1) * LANES]).astype(BF16)

    out = _dot(mix_scr[...], wout_ref[...])
    y_ref[...] = _layer_norm(DEEPNORM_ALPHA * x_ref[...] + out, ln_ref[0:1, :], ln_ref[1:2, :])

    @pl.when(last)
    def _():
        tab = rope_ref[T - WINDOW:T, :]
        k0 = _rope(p_scr[T - WINDOW:T, K_OFF:K_OFF + LANES], tab)
        k1 = _rope(p_scr[T - WINDOW:T, K_OFF + LANES:K_OFF + 2 * LANES], tab)
        ko_ref[...] = jnp.where(lo_sq, k0, k1)
        vo_ref[...] = jnp.where(lo_sq, p_scr[T - WINDOW:T, V_OFF:V_OFF + LANES],
                                p_scr[T - WINDOW:T, V_OFF + LANES:V_OFF + 2 * LANES])
        lco_ref[...] = lx_scr[P + T - (CONV_W - 1):P + T, :]
        lho_ref[...] = hl_scr[...]
        sco_ref[...] = sx_scr[P + T - (CONV_W - 1):P + T, :]
        for g in range(SSD_GROUPS):
            for pp in range(TILES_PER_GROUP):
                tile = ht_scr[g, :, pp * LANES:(pp + 1) * LANES].T
                h = 2 * (g * TILES_PER_GROUP + pp)
                sho_ref[h] = tile[0:SSD_HEAD_DIM, :]
                sho_ref[h + 1] = tile[SSD_HEAD_DIM:2 * SSD_HEAD_DIM, :]


def _const_spec(shape, layer):
    nd = len(shape)
    return pl.BlockSpec((None,) + tuple(shape), lambda *_: (layer,) + (0,) * nd,
                        pipeline_mode=pl.Buffered(1))


def _prompt_layer(layer, x, rope_tab, sinks, prm):
    B, L, _ = x.shape
    T = PROMPT_TILE
    nc = L // T
    win, aux, wg, wout, lcw, lvec, scw, scb, hvec, svec, ln = prm
    tile = lambda b, c: (b, c, 0)
    per_b = lambda b, c: (b, 0, 0)

    in_specs = [
        pl.BlockSpec(memory_space=pltpu.SMEM),
        pl.BlockSpec((None, T, D_MODEL), tile),
        pl.BlockSpec((T, 3 * LANES), lambda b, c: (c, 0)),
        _const_spec((D_MODEL, W_IN_COLS), layer),
        _const_spec((D_MODEL, AUX_COLS), layer),
        _const_spec((LRU_WIDTH, 2 * LRU_WIDTH), layer),
        _const_spec((D_MIX, D_MODEL), layer),
        _const_spec((CONV_W, LRU_WIDTH), layer),
        _const_spec((4, LRU_WIDTH), layer),
        _const_spec((CONV_W, SSD_CONV_CH), layer),
        _const_spec((1, SSD_CONV_CH), layer),
        _const_spec((2, LANES), layer),
        _const_spec((2, SSD_WIDTH), layer),
        _const_spec((2, D_MODEL), layer),
    ]
    out_shape = (
        jax.ShapeDtypeStruct((B, L, D_MODEL), F32),
        jax.ShapeDtypeStruct((B, WINDOW, ATT_KV_WIDTH), F32),
        jax.ShapeDtypeStruct((B, WINDOW, ATT_KV_WIDTH), F32),
        jax.ShapeDtypeStruct((B, CONV_W - 1, LRU_WIDTH), F32),
        jax.ShapeDtypeStruct((B, 1, LRU_WIDTH), F32),
        jax.ShapeDtypeStruct((B, CONV_W - 1, SSD_CONV_CH), F32),
        jax.ShapeDtypeStruct((B, SSD_HEADS, SSD_HEAD_DIM, SSD_STATE), F32),
    )
    out_specs = (
        pl.BlockSpec((None, T, D_MODEL), tile),
        pl.BlockSpec((None, WINDOW, ATT_KV_WIDTH), per_b),
        pl.BlockSpec((None, WINDOW, ATT_KV_WIDTH), per_b),
        pl.BlockSpec((None, CONV_W - 1, LRU_WIDTH), per_b),
        pl.BlockSpec((None, 1, LRU_WIDTH), per_b),
        pl.BlockSpec((None, CONV_W - 1, SSD_CONV_CH), per_b),
        pl.BlockSpec((None, SSD_HEADS, SSD_HEAD_DIM, SSD_STATE), lambda b, c: (b, 0, 0, 0)),
    )
    scratch = [
        pltpu.VMEM((T, N_COLS), F32),
        pltpu.VMEM((ATT_KV_HEADS, WINDOW + T, LANES), BF16),
        pltpu.VMEM((ATT_KV_HEADS, WINDOW + T, LANES), BF16),
        pltpu.VMEM((SUBLANES + T, LRU_WIDTH), F32),
        pltpu.VMEM((SUBLANES + T, SSD_CONV_CH), F32),
        pltpu.VMEM((N_LRU_TILES, T, LANES), F32),
        pltpu.VMEM((N_LRU_TILES, T, LANES), F32),
        pltpu.VMEM((1, LRU_WIDTH), F32),
        pltpu.VMEM((SSD_GROUPS, SSD_STATE, GROUP_W), F32),
        pltpu.VMEM((T, D_MIX), BF16),
        pltpu.VMEM((T, LRU_WIDTH), BF16),
    ]
    return pl.pallas_call(
        _prompt_kernel,
        grid=(B, nc),
        in_specs=in_specs,
        out_specs=out_specs,
        out_shape=out_shape,
        scratch_shapes=scratch,
        compiler_params=pltpu.CompilerParams(
            dimension_semantics=("arbitrary", "arbitrary"),
            vmem_limit_bytes=VMEM_LIMIT_BYTES),
        name=f"prompt_layer{layer}",
    )(sinks, x, rope_tab, win, aux, wg, wout, lcw, lvec, scw, scb, hvec, svec, ln)


def _conv_step(x_new, st_ref, sto_ref, w_ref, bias):
    y = bias + w_ref[CONV_W - 1:CONV_W, :] * x_new
    for t in range(CONV_W - 1):
        y = y + w_ref[t:t + 1, :] * st_ref[t]
    for t in range(CONV_W - 2):
        sto_ref[t] = st_ref[t + 1]
    sto_ref[CONV_W - 2] = x_new
    return y


def _sample_kernel(n_aliased, sinks_ref, x_ref, rope_ref, win_ref, aux_ref, wg_ref, wout_ref, lcw_ref, lvec_ref,
                   scw_ref, scb_ref, hvec_ref, svec_ref, ln_ref,
                   ck_ref, cv_ref, lc_ref, lh_ref, sc_ref, sh_ref, *refs):
    (y_ref, cko_ref, cvo_ref, lco_ref, lho_ref, sco_ref, sho_ref,
     p_scr, q_scr, o_scr, kn_scr, vn_scr, xdt_scr, da_scr, y_scr, mix_scr) = refs[n_aliased:]
    NB = x_ref.shape[0]
    BT = SAMPLE_BT
    i = pl.program_id(0)
    lo_f = _lane_lo((NB, LANES))

    @pl.when(i == 0)
    def _():
        _project(x_ref[...], win_ref, aux_ref, p_scr)
        tab = rope_ref[...]
        for cc in range(ATT_HEADS // 2):
            g = cc // (ATT_HEADS // ATT_KV_HEADS // 2)
            qp = _rope(p_scr[:, Q_OFF + cc * LANES:Q_OFF + (cc + 1) * LANES], tab) * ATT_SCALE
            for half in range(2):
                t = jnp.where(lo_f if half == 0 else jnp.logical_not(lo_f), qp, 0.0)
                if half != g:
                    t = pltpu.roll(t, HALF, 1)
                q_scr[pl.ds(2 * cc + half, NB, stride=ATT_HEADS), :] = t
        k0 = _rope(p_scr[:, K_OFF:K_OFF + LANES], tab)
        k1 = _rope(p_scr[:, K_OFF + LANES:K_OFF + 2 * LANES], tab)
        kn_scr[...] = jnp.where(lo_f, k0, k1)
        vn_scr[...] = jnp.where(lo_f, p_scr[:, V_OFF:V_OFF + LANES], p_scr[:, V_OFF + LANES:V_OFF + 2 * LANES])
        xin = p_scr[:, XL_OFF:XL_OFF + LRU_WIDTH]
        xl = _conv_step(xin, lc_ref, lco_ref, lcw_ref, lvec_ref[0:1, :])
        gates = _dot(xl.astype(BF16), wg_ref[...])
        a, bt = _lru_coeffs(xl, gates, lvec_ref)
        h1 = a * lh_ref[...] + bt
        lho_ref[...] = h1
        mix_scr[:, MIX_LRU:MIX_LRU + LRU_WIDTH] = (h1 * _silu(p_scr[:, GL_OFF:GL_OFF + LRU_WIDTH])).astype(BF16)
        xin2 = p_scr[:, XBC_OFF:XBC_OFF + SSD_CONV_CH]
        xc = _conv_step(xin2, sc_ref, sco_ref, scw_ref, scb_ref[0:1, :])
        p_scr[:, XBC_OFF:XBC_OFF + SSD_CONV_CH] = _silu(xc)
        dt = _softplus(p_scr[:, DT_OFF:DT_OFF + LANES] + hvec_ref[0:1, :])
        da_scr[...] = jnp.exp(dt * (-jnp.exp(hvec_ref[1:2, :])))
        for p in range(N_SSD_TILES):
            dte = jnp.where(lo_f, jnp.broadcast_to(dt[:, 2 * p:2 * p + 1], (NB, LANES)),
                            jnp.broadcast_to(dt[:, 2 * p + 1:2 * p + 2], (NB, LANES)))
            xdt_scr[:, p * LANES:(p + 1) * LANES] = p_scr[:, XBC_OFF + p * LANES:XBC_OFF + (p + 1) * LANES] * dte

    r8 = pl.multiple_of(i * BT, BT)
    kn_blk = kn_scr[pl.ds(r8, BT), :]
    vn_blk = vn_scr[pl.ds(r8, BT), :]
    zpad = jnp.zeros((LANES - BT, LANES), F32)
    kn_t = jnp.concatenate([kn_blk, zpad], axis=0).T
    vn_t = jnp.concatenate([vn_blk, zpad], axis=0).T
    newest = lax.broadcasted_iota(jnp.int32, (ATT_KV_WIDTH, WINDOW), 1) == WINDOW - 1
    da_t =jnp.concatenate([da_scr[pl.ds(r8, BT), :], zpad], axis=0).T
    x_t = [jnp.concatenate([xdt_scr[pl.ds(r8, BT), p * LANES:(p + 1) * LANES], zpad], axis=0).T
           for p in range(N_SSD_TILES)]
    b_blk = p_scr[pl.ds(r8, BT), B_OFF:B_OFF + SSD_GROUPS * SSD_STATE]
    c_blk = p_scr[pl.ds(r8, BT), C_OFF:C_OFF + SSD_GROUPS * SSD_STATE].astype(BF16)
    row8 = lax.broadcasted_iota(jnp.int32, (ATT_HEADS, 1), 0)
    sink = jnp.zeros((ATT_HEADS, 1), F32)
    for h in range(ATT_HEADS):
        sink = jnp.where(row8 == h, sinks_ref[h], sink)
    rowb = lax.broadcasted_iota(jnp.int32, (BT, GROUP_W), 0)
    y_acc = [jnp.zeros((BT, GROUP_W), F32) for _ in range(SSD_GROUPS)]
    for bb in range(BT):
        qr = q_scr[pl.ds(pl.multiple_of((r8 + bb) * ATT_HEADS, ATT_HEADS), ATT_HEADS), :]
        kt = ck_ref[bb]
        vt = cv_ref[bb]
        s = _dot(qr.astype(BF16), kt.astype(BF16))
        s_new = jnp.sum(qr * kn_blk[bb:bb + 1, :], -1, keepdims=True)
        m = jnp.maximum(jnp.maximum(jnp.max(s, -1, keepdims=True), s_new), sink)
        e = jnp.exp(s - m)
        e_new = jnp.exp(s_new - m)
        den = jnp.sum(e, -1, keepdims=True) + e_new + jnp.exp(sink - m)
        o = _dot_nt(e.astype(BF16), vt.astype(BF16)) + e_new * vn_blk[bb:bb + 1, :]
        o_scr[pl.ds(pl.multiple_of((r8 + bb) * ATT_HEADS, ATT_HEADS), ATT_HEADS), :] = o * (1.0 / den)
        cko_ref[bb] = jnp.where(newest, kn_t[:, bb:bb + 1], pltpu.roll(kt, WINDOW - 1, 1))
        cvo_ref[bb] = jnp.where(newest, vn_t[:, bb:bb + 1], pltpu.roll(vt, WINDOW - 1, 1))
        for g in range(SSD_GROUPS):
            brow = b_blk[bb:bb + 1, g * SSD_STATE:(g + 1) * SSD_STATE]
            tiles = []
            for pp in range(TILES_PER_GROUP):
                p = g * TILES_PER_GROUP + pp
                for hh in range(2):
                    h = 2 * p + hh
                    dab = jnp.broadcast_to(da_t[h:h + 1, bb:bb + 1], (SSD_HEAD_DIM, SSD_STATE))
                    xcol = jnp.broadcast_to(x_t[p][hh * SSD_HEAD_DIM:(hh + 1) * SSD_HEAD_DIM, bb:bb + 1],
                                            (SSD_HEAD_DIM, SSD_STATE))
                    h1 = sh_ref[bb, h] * dab + xcol * brow
                    sho_ref[bb, h] = h1
                    tiles.append(h1.astype(BF16))
            res = _dot_nt(c_blk[:, g * SSD_STATE:(g + 1) * SSD_STATE], jnp.concatenate(tiles, axis=0))
            y_acc[g] = jnp.where(rowb == bb, res, y_acc[g])
    for g in range(SSD_GROUPS):
        y_scr[pl.ds(r8, BT), g * GROUP_W:(g + 1) * GROUP_W] = y_acc[g]

    @pl.when(i == pl.num_programs(0) - 1)
    def _():
        for cc in range(ATT_HEADS // 2):
            g = cc // (ATT_HEADS // ATT_KV_HEADS // 2)
            oe = o_scr[pl.ds(2 * cc, NB, stride=ATT_HEADS), :]
            oo = o_scr[pl.ds(2 * cc + 1, NB, stride=ATT_HEADS), :]
            if g == 0:
                oo = pltpu.roll(oo, HALF, 1)
            else:
                oe = pltpu.roll(oe, HALF, 1)
            gate = p_scr[:, GA_OFF + cc * LANES:GA_OFF + (cc + 1) * LANES]
            mix_scr[:, cc * LANES:(cc + 1) * LANES] = (jnp.where(lo_f, oe, oo) * _silu(gate)).astype(BF16)
        y = (y_scr[...] + svec_ref[0:1, :] * p_scr[:, XBC_OFF:XBC_OFF + SSD_WIDTH]) * _silu(p_scr[:, Z_OFF:Z_OFF + SSD_WIDTH])
        y = y * lax.rsqrt(jnp.mean(y * y, -1, keepdims=True) + NORM_EPS) * svec_ref[1:2, :]
        mix_scr[:, MIX_SSD:MIX_SSD + SSD_WIDTH] = y.astype(BF16)
        out = _dot(mix_scr[...], wout_ref[...])
        y_ref[...] = _layer_norm(DEEPNORM_ALPHA * x_ref[...] + out, ln_ref[0:1, :], ln_ref[1:2, :])


def _sample_layer(layer, x, rope_tab, sinks, prm, ck, cv, lc, lh, sc, sh, prev):
    NB = x.shape[0]
    BT = SAMPLE_BT
    H = CONV_W - 1
    win, aux, wg, wout, lcw, lvec, scw, scb, hvec, svec, ln = prm
    whole = lambda shape: pl.BlockSpec(shape, lambda i: (0,) * len(shape), pipeline_mode=pl.Buffered(1))
    lwhole = lambda shape: pl.BlockSpec((None,) + shape, lambda i: (layer,) + (0,) * len(shape),
                                        pipeline_mode=pl.Buffered(1))
    rows = lambda shape: pl.BlockSpec((None, BT) + shape, lambda i: (layer, i) + (0,) * len(shape))
    state_specs = [
        rows((ATT_KV_WIDTH, WINDOW)),
        rows((ATT_KV_WIDTH, WINDOW)),
        lwhole((H, NB, LRU_WIDTH)),
        lwhole((NB, LRU_WIDTH)),
        lwhole((H, NB, SSD_CONV_CH)),
        rows((SSD_HEADS, SSD_HEAD_DIM, SSD_STATE)),
    ]
    in_specs = [
        pl.BlockSpec(memory_space=pltpu.SMEM),
        whole((NB, D_MODEL)),
        whole((1, 3 * LANES)),
        _const_spec((D_MODEL, W_IN_COLS), layer),
        _const_spec((D_MODEL, AUX_COLS), layer),
        _const_spec((LRU_WIDTH, 2 * LRU_WIDTH), layer),
        _const_spec((D_MIX, D_MODEL), layer),
        _const_spec((CONV_W, LRU_WIDTH), layer),
        _const_spec((4, LRU_WIDTH), layer),
        _const_spec((CONV_W, SSD_CONV_CH), layer),
        _const_spec((1, SSD_CONV_CH), layer),
        _const_spec((2, LANES), layer),
        _const_spec((2, SSD_WIDTH), layer),
        _const_spec((2, D_MODEL), layer),
    ] + state_specs
    operands = [sinks, x, rope_tab, win, aux, wg, wout, lcw, lvec, scw, scb, hvec, svec, ln, ck, cv, lc, lh, sc, sh]
    aliases = {}
    if prev is not None:
        for k, arr in enumerate(prev):
            aliases[len(operands)] = 1 + k
            operands.append(arr)
            in_specs.append(pl.BlockSpec(memory_space=pl.ANY))
    out_shape = (jax.ShapeDtypeStruct((NB, D_MODEL), F32),) + tuple(
        jax.ShapeDtypeStruct(a.shape, F32) for a in (ck, cv, lc, lh, sc, sh))
    full = lambda shape: pl.BlockSpec(shape, lambda i: (0,) * len(shape))
    lfull = lambda shape: pl.BlockSpec((None,) + shape, lambda i: (layer,) + (0,) * len(shape))
    out_specs = (
        full((NB, D_MODEL)),
        rows((ATT_KV_WIDTH, WINDOW)),
        rows((ATT_KV_WIDTH, WINDOW)),
        lfull((H, NB, LRU_WIDTH)),
        lfull((NB, LRU_WIDTH)),
        lfull((H, NB, SSD_CONV_CH)),
        rows((SSD_HEADS, SSD_HEAD_DIM, SSD_STATE)),
    )
    scratch = [
        pltpu.VMEM((NB, N_COLS), F32),
        pltpu.VMEM((NB * ATT_HEADS, LANES), F32),
        pltpu.VMEM((NB * ATT_HEADS, LANES), F32),
        pltpu.VMEM((NB, ATT_KV_WIDTH), F32),
        pltpu.VMEM((NB, ATT_KV_WIDTH), F32),
        pltpu.VMEM((NB, SSD_WIDTH), F32),
        pltpu.VMEM((NB, LANES), F32),
        pltpu.VMEM((NB, SSD_WIDTH), F32),
        pltpu.VMEM((NB, D_MIX), BF16),
    ]
    return pl.pallas_call(
        functools.partial(_sample_kernel, len(aliases)),
        grid=(NB // BT,),
        in_specs=in_specs,
        out_specs=out_specs,
        out_shape=out_shape,
        scratch_shapes=scratch,
        input_output_aliases=aliases,
        compiler_params=pltpu.CompilerParams(
            dimension_semantics=("arbitrary",),
            vmem_limit_bytes=VMEM_LIMIT_BYTES),
        name=f"sample_layer{layer}",
    )(*operands)


def _rope_table(pos):
    half = ROPE_DIM // 2
    inv = ROPE_THETA ** (-jnp.arange(half, dtype=F32) / half)
    ang = pos.astype(F32)[:, None] * inv[None, :]
    cos, sin = jnp.cos(ang), jnp.sin(ang)
    d = jnp.arange(LANES) % ATT_HEAD_DIM
    fi = d % half
    c = jnp.where(d < ROPE_DIM, cos[:, fi], 1.0)
    s1 = jnp.where(d < half, -sin[:, fi], 0.0)
    s2 = jnp.where((d >= half) & (d < ROPE_DIM), sin[:, fi], 0.0)
    return jnp.concatenate([c, s1, s2], axis=1)


def _prep_params(w_in, w_out, lru_conv_w, lru_conv_b, lru_wa, lru_ba, lru_wx, lru_bx, lru_lambda,
                 ssd_conv_w, ssd_conv_b, ssd_dt_bias, ssd_a_log, ssd_d, ssd_norm_g, ln_g, ln_b):
    hd = ATT_HEAD_DIM
    win = w_in.astype(BF16)
    k = w_in[..., ATT_WIDTH:ATT_WIDTH + ATT_KV_WIDTH]
    v = w_in[..., ATT_WIDTH + ATT_KV_WIDTH:ATT_WIDTH + 2 * ATT_KV_WIDTH]
    dup = lambda t: jnp.concatenate([t[..., :hd], t[..., :hd], t[..., hd:], t[..., hd:]], -1)
    dtw = jnp.pad(w_in[..., W_IN_COLS - SSD_HEADS:], ((0, 0), (0, 0), (0, LANES - SSD_HEADS)))
    aux = jnp.concatenate([dup(k), dup(v), dtw], -1).astype(BF16)

    def dense(w):
        rows = [jnp.pad(w[:, n], ((0, 0), (0, 0), (n * LRU_BLOCK, LRU_WIDTH - (n + 1) * LRU_BLOCK)))
                for n in range(LRU_BLOCKS)]
        return jnp.concatenate(rows, 1)

    wg = jnp.concatenate([dense(lru_wa.astype(BF16)), dense(lru_wx.astype(BF16))], -1)
    wout = w_out.astype(BF16)
    lvec = jnp.stack([lru_conv_b, lru_ba, lru_bx, lru_lambda], 1)
    hpad = lambda t: jnp.pad(t, ((0, 0), (0, LANES - SSD_HEADS)))
    hvec = jnp.stack([hpad(ssd_dt_bias), hpad(ssd_a_log)], 1)
    svec = jnp.stack([jnp.repeat(ssd_d, SSD_HEAD_DIM, axis=1), ssd_norm_g], 1)
    ln = jnp.stack([ln_g, ln_b], 1)
    return (win, aux, wg, wout, lru_conv_w, lvec, ssd_conv_w, ssd_conv_b[:, None, :], hvec, svec, ln)


def kernel(x_prompt, x_sample, cache_swa_k, cache_swa_v, state_lru_conv, state_lru_h, state_ssd_conv, state_ssd_h, w_in, w_out, att_sinks, lru_conv_w, lru_conv_b, lru_wa, lru_ba, lru_wx, lru_bx, lru_lambda, ssd_conv_w, ssd_conv_b, ssd_dt_bias, ssd_a_log, ssd_d, ssd_norm_g, ln_g, ln_b):
    prm = _prep_params(w_in, w_out, lru_conv_w, lru_conv_b, lru_wa, lru_ba, lru_wx, lru_bx, lru_lambda,
                       ssd_conv_w, ssd_conv_b, ssd_dt_bias, ssd_a_log, ssd_d, ssd_norm_g, ln_g, ln_b)
    bp, lp, _ = x_prompt.shape
    rope_p = _rope_table(jnp.arange(lp, dtype=jnp.int32))
    xp = x_prompt
    new_p = [[] for _ in range(6)]
    for l in range(DEPTH):
        xp, ko, vo, lco, lho, sco, sho = _prompt_layer(l, xp, rope_p, att_sinks[l], prm)
        st = (ko.reshape(bp, WINDOW, ATT_KV_HEADS, ATT_HEAD_DIM), vo.reshape(bp, WINDOW, ATT_KV_HEADS, ATT_HEAD_DIM),
              lco, lho.reshape(bp, LRU_WIDTH), sco, sho)
        for lst, t in zip(new_p, st):
            lst.append(t)
    outs_p = [jnp.stack(t) for t in new_p]

    nb = x_sample.shape[0]
    rope_s = _rope_table(PAST_LEN + jnp.arange(x_sample.shape[1], dtype=jnp.int32))
    to_kt = lambda t: jnp.transpose(t, (0, 1, 3, 4, 2)).reshape(DEPTH, nb, ATT_KV_WIDTH, WINDOW)
    from_kt = lambda t: jnp.transpose(t.reshape(DEPTH, nb, ATT_KV_HEADS, ATT_HEAD_DIM, WINDOW), (0, 1, 4, 2, 3))
    swap = lambda t: jnp.transpose(t, (0, 2, 1, 3))
    state = (to_kt(cache_swa_k), to_kt(cache_swa_v), swap(state_lru_conv), state_lru_h, swap(state_ssd_conv),
             state_ssd_h)
    xs = x_sample.reshape(nb, D_MODEL)
    new = None
    for l in range(DEPTH):
        xs, *new = _sample_layer(l, xs, rope_s, att_sinks[l], prm, *state, new)
    cko, cvo, lco, lho, sco, sho = new
    outs_s = (from_kt(cko), from_kt(cvo), swap(lco), lho, swap(sco), sho)
    return (xp, xs.reshape(x_sample.shape)) + tuple(outs_p) + tuple(outs_s)
```

```python
import functools
import math

import jax
import jax.numpy as jnp
from jax import lax
from jax.experimental import pallas as pl
from jax.experimental.pallas import tpu as pltpu

F32 = jnp.float32
BF16 = jnp.bfloat16

D_MODEL = 1024
DEPTH = 4
PAST_LEN = 8192
D_MIX = 2 * D_MODEL
ATT_HEADS = 8
ATT_KV_HEADS = 2
ATT_HEAD_DIM = 64
ATT_WIDTH = ATT_HEADS * ATT_HEAD_DIM
ATT_KV_WIDTH = ATT_KV_HEADS * ATT_HEAD_DIM
ATT_SCALE = ATT_HEAD_DIM ** -0.5
WINDOW = 128
ROPE_THETA = 500000.0
ROPE_DIM = ATT_HEAD_DIM // 4
LRU_WIDTH = 3 * D_MIX // 8
LRU_BLOCKS = 8
LRU_BLOCK = LRU_WIDTH // LRU_BLOCKS
LRU_C = 8.0
CONV_W = 4
SSD_WIDTH = D_MIX - ATT_WIDTH - LRU_WIDTH
SSD_HEAD_DIM = 64
SSD_HEADS = SSD_WIDTH // SSD_HEAD_DIM
SSD_GROUPS = 2
SSD_STATE = 128
SSD_CONV_CH = SSD_WIDTH + 2 * SSD_GROUPS * SSD_STATE
DEEPNORM_ALPHA = (2.0 * DEPTH) ** 0.25
NORM_EPS = 1e-5

LANES = 128
SUBLANES = 8
HALF = LANES // 2
VMEM_LIMIT_BYTES = 60 * 1024 * 1024

Q_OFF = 0
K_OFF = Q_OFF + ATT_WIDTH
V_OFF = K_OFF + 2 * ATT_KV_WIDTH
GA_OFF = V_OFF + 2 * ATT_KV_WIDTH
XL_OFF = GA_OFF + ATT_WIDTH
GL_OFF = XL_OFF + LRU_WIDTH
Z_OFF = GL_OFF + LRU_WIDTH
XBC_OFF = Z_OFF + SSD_WIDTH
DT_OFF = XBC_OFF + SSD_CONV_CH
N_COLS = DT_OFF + LANES
W_IN_COLS = 2 * ATT_WIDTH + 2 * ATT_KV_WIDTH + 2 * LRU_WIDTH + SSD_WIDTH + SSD_CONV_CH + SSD_HEADS
AUX_COLS = 4 * ATT_KV_WIDTH + LANES
B_OFF = XBC_OFF + SSD_WIDTH
C_OFF = B_OFF + SSD_GROUPS * SSD_STATE
MIX_LRU = ATT_WIDTH
MIX_SSD = ATT_WIDTH + LRU_WIDTH

N_LRU_TILES = LRU_WIDTH // LANES
N_SSD_TILES = SSD_WIDTH // LANES
N_CONV_TILES = SSD_CONV_CH // LANES
CONV_STRIDE = 4
TILES_PER_GROUP = N_SSD_TILES // SSD_GROUPS
GROUP_W = SSD_WIDTH // SSD_GROUPS

NEG_BIG = -1e30

PROMPT_TILE = 256
SAMPLE_BT = 8


def _silu(x):
    return x * jax.nn.sigmoid(x)


def _softplus(x):
    return jnp.maximum(x, 0.0) + jnp.log1p(jnp.exp(-jnp.abs(x)))


def _lane_lo(shape):
    return (lax.broadcasted_iota(jnp.int32, shape, len(shape) - 1) % LANES) < HALF


def _rope(t, tab):
    half = ROPE_DIM // 2
    return (t * tab[:, 0:LANES]
            + pltpu.roll(t, LANES - half, 1) * tab[:, LANES:2 * LANES]
            + pltpu.roll(t, half, 1) * tab[:, 2 * LANES:3 * LANES])


def _dot(a, b):
    return jnp.dot(a, b, preferred_element_type=F32)


def _dot_nt(a, b):
    return lax.dot_general(a, b, (((1,), (1,)), ((), ())), preferred_element_type=F32)


def _conv_classes(src_ref, j, rows, w_ref, bias, cs):
    n = CONV_STRIDE
    hist = CONV_W - 1
    taps = {s_: src_ref[j, pl.ds(SUBLANES + s_, rows // n, stride=n), :] for s_ in range(-hist, n)}
    outs = []
    for k in range(n):
        acc = bias + w_ref[hist:hist + 1, cs] * taps[k]
        for t in range(hist):
            acc = acc + w_ref[t:t + 1, cs] * taps[k - hist + t]
        outs.append(acc)
    return outs


def _layer_norm(v, g, b):
    mu = jnp.mean(v, -1, keepdims=True)
    d = v - mu
    var = jnp.mean(d * d, -1, keepdims=True)
    return d * lax.rsqrt(var + NORM_EPS) * g + b


def _lru_coeffs(xl, gates, lvec_ref):
    r = jax.nn.sigmoid(gates[:, 0:LRU_WIDTH] + lvec_ref[1:2, :])
    ig = jax.nn.sigmoid(gates[:, LRU_WIDTH:2 * LRU_WIDTH] + lvec_ref[2:3, :])
    log_a = (-LRU_C * _softplus(-lvec_ref[3:4, :])) * r
    a = jnp.exp(log_a)
    bt = jnp.sqrt(-jnp.tanh(log_a) * (1.0 + a * a)) * (ig * xl)
    return a, bt


def _project(x_tile, win_ref, aux_ref, dst):
    xb = x_tile.astype(BF16)
    kv0 = ATT_WIDTH + 2 * ATT_KV_WIDTH
    dst[:, Q_OFF:K_OFF] = _dot(xb, win_ref[:, 0:ATT_WIDTH])
    dst[:, K_OFF:GA_OFF] = _dot(xb, aux_ref[:, 0:GA_OFF - K_OFF])
    dst[:, GA_OFF:Z_OFF] = _dot(xb, win_ref[:, kv0:kv0 + Z_OFF - GA_OFF])
    dst[:, Z_OFF:DT_OFF] = _dot(xb, win_ref[:, kv0 + Z_OFF - GA_OFF:kv0 + DT_OFF - GA_OFF])
    dst[:, DT_OFF:N_COLS] = _dot(xb, aux_ref[:, GA_OFF - K_OFF:GA_OFF - K_OFF + LANES])


def _prompt_kernel(sinks_ref, x_ref, rope_ref, win_ref, aux_ref, wg_ref, wout_ref, lcw_ref, lvec_ref,
                   scw_ref, scb_ref, hvec_ref, svec_ref, ln_ref,
                   y_ref, ko_ref, vo_ref, lco_ref, lho_ref, sco_ref, sho_ref,
                   p_scr, kd_scr, vd_scr, lx_scr, sx_scr, sa_scr, sb_scr, hl_scr, ht_scr, mix_scr, xlb_scr, xc_scr):
    T = PROMPT_TILE
    nsub = T // WINDOW
    c = pl.program_id(1)
    last = c == pl.num_programs(1) - 1

    @pl.when(c == 0)
    def _():
        kd_scr[:, 0:WINDOW, :] = jnp.zeros((2, WINDOW, LANES), BF16)
        vd_scr[:, 0:WINDOW, :] = jnp.zeros((2, WINDOW, LANES), BF16)
        lx_scr[:, 0:SUBLANES, :] = jnp.zeros((N_LRU_TILES, SUBLANES, LANES), F32)
        sx_scr[:, 0:SUBLANES, :] = jnp.zeros((N_CONV_TILES, SUBLANES, LANES), F32)
        hl_scr[...] = jnp.zeros_like(hl_scr)
        ht_scr[...] = jnp.zeros_like(ht_scr)

    _project(x_ref[...], win_ref, aux_ref, p_scr)

    lo_sq = _lane_lo((WINDOW, LANES))

    for g in range(ATT_KV_HEADS):
        kd = _rope(p_scr[:, K_OFF + g * LANES:K_OFF + (g + 1) * LANES], rope_ref[...])
        kd_scr[g, WINDOW:WINDOW + T, :] = kd.astype(BF16)
        vd_scr[g, WINDOW:WINDOW + T, :] = p_scr[:, V_OFF + g * LANES:V_OFF + (g + 1) * LANES].astype(BF16)

    row = lax.broadcasted_iota(jnp.int32, (WINDOW, 2 * WINDOW), 0)
    col = lax.broadcasted_iota(jnp.int32, (WINDOW, 2 * WINDOW), 1)
    band = (col >= row) & (col <= row + WINDOW)
    first_lo = jnp.where(c > 0, 0, WINDOW)
    for i in range(nsub):
        r0 = i * WINDOW
        msk = (band & (col >= first_lo)) if i == 0 else band
        tab = rope_ref[r0:r0 + WINDOW, :]
        for cc in range(ATT_HEADS // 2):
            g = cc // (ATT_HEADS // ATT_KV_HEADS // 2)
            kg = kd_scr[g, r0:r0 + 2 * WINDOW, :]
            vg = vd_scr[g, r0:r0 + 2 * WINDOW, :]
            qp = _rope(p_scr[r0:r0 + WINDOW, Q_OFF + cc * LANES:Q_OFF + (cc + 1) * LANES], tab) * ATT_SCALE
            outs = []
            for half in range(2):
                sink = sinks_ref[2 * cc + half]
                qm = jnp.where(lo_sq if half == 0 else jnp.logical_not(lo_sq), qp, 0.0).astype(BF16)
                s = jnp.where(msk, _dot_nt(qm, kg), NEG_BIG)
                m = jnp.maximum(jnp.max(s, -1, keepdims=True), sink)
                e = jnp.exp(s - m)
                den = jnp.sum(e, -1, keepdims=True) + jnp.exp(sink - m)
                outs.append(_dot(e.astype(BF16), vg) * (1.0 / den))
            att = jnp.where(lo_sq, outs[0], outs[1])
            gate = p_scr[r0:r0 + WINDOW, GA_OFF + cc * LANES:GA_OFF + (cc + 1) * LANES]
            mix_scr[r0:r0 + WINDOW, cc * LANES:(cc + 1) * LANES] = (att * _silu(gate)).astype(BF16)
    for g in range(ATT_KV_HEADS):
        kd_scr[g, 0:WINDOW, :] = kd_scr[g, T:T + WINDOW, :]
        vd_scr[g, 0:WINDOW, :] = vd_scr[g, T:T + WINDOW, :]

    P = SUBLANES
    H = CONV_W - 1
    for j in range(N_LRU_TILES):
        cs_ = slice(j * LANES, (j + 1) * LANES)
        lx_scr[j, P:P + T, :] = p_scr[:, XL_OFF + j * LANES:XL_OFF + (j + 1) * LANES]
        for k, xl in enumerate(_conv_classes(lx_scr, j, T, lcw_ref, lvec_ref[0:1, cs_], cs_)):
            sb_scr[j, pl.ds(k, T // CONV_STRIDE, stride=CONV_STRIDE), :] = xl
        xlb_scr[:, cs_] = sb_scr[j].astype(BF16)
    lx_scr[:, 0:P, :] = lx_scr[:, T:T + P, :]

    G = T // SUBLANES
    row_g = lax.broadcasted_iota(jnp.int32, (G, LANES), 0)
    cl = -LRU_C * _softplus(-lvec_ref[3:4, :])
    for jj in range(N_LRU_TILES // 2):
        c0 = jj * 2 * LANES
        k0 = (c0 // LRU_BLOCK) * LRU_BLOCK // LANES * LANES
        k1 = -(-(-(-(c0 + 2 * LANES) // LRU_BLOCK) * LRU_BLOCK) // LANES) * LANES
        xk = xlb_scr[:, k0:k1]
        gr = _dot(xk, wg_ref[k0:k1, c0:c0 + 2 * LANES])
        gi = _dot(xk, wg_ref[k0:k1, LRU_WIDTH + c0:LRU_WIDTH + c0 + 2 * LANES])
        for u in range(2):
            j = 2 * jj + u
            cs_ = slice(j * LANES, (j + 1) * LANES)
            us = slice(u * LANES, (u + 1) * LANES)
            r = jax.nn.sigmoid(gr[:, us] + lvec_ref[1:2, cs_])
            ig = jax.nn.sigmoid(gi[:, us] + lvec_ref[2:3, cs_])
            log_a = cl[:, cs_] * r
            a = jnp.exp(log_a)
            sa_scr[j] = a
            sb_scr[j] = jnp.sqrt(-jnp.tanh(log_a) * (1.0 + a * a)) * (ig * sb_scr[j])
    for j in range(N_LRU_TILES):
        ca = sa_scr[j, pl.ds(0, G, stride=SUBLANES), :]
        cb = sb_scr[j, pl.ds(0, G, stride=SUBLANES), :]
        for k in range(1, SUBLANES):
            ak = sa_scr[j, pl.ds(k, G, stride=SUBLANES), :]
            cb = ak * cb + sb_scr[j, pl.ds(k, G, stride=SUBLANES), :]
            ca = ak * ca
            sa_scr[j, pl.ds(k, G, stride=SUBLANES), :] = ca
            sb_scr[j, pl.ds(k, G, stride=SUBLANES), :] = cb
        hin = hl_scr[0:1, j * LANES:(j + 1) * LANES]
        cb = cb + jnp.where(row_g == 0, ca * hin, 0.0)
        d = 1
        while d < G:
            ok = row_g >= d
            cb = cb + ca * jnp.where(ok, pltpu.roll(cb, d, 0), 0.0)
            ca = ca * jnp.where(ok, pltpu.roll(ca, d, 0), 1.0)
            d *= 2
        hprev = jnp.where(row_g == 0, hin, pltpu.roll(cb, 1, 0))
        for k in range(SUBLANES):
            sb_scr[j, pl.ds(k, G, stride=SUBLANES), :] = (
                sb_scr[j, pl.ds(k, G, stride=SUBLANES), :] + sa_scr[j, pl.ds(k, G, stride=SUBLANES), :] * hprev)
        hl_scr[0:1, j * LANES:(j + 1) * LANES] = cb[G - 1:G, :]
        gl = p_scr[:, GL_OFF + j * LANES:GL_OFF + (j + 1) * LANES]
        mix_scr[:, MIX_LRU + j * LANES:MIX_LRU + (j + 1) * LANES] = (sb_scr[j] * _silu(gl)).astype(BF16)

    for j in range(N_CONV_TILES):
        cs_ = slice(j * LANES, (j + 1) * LANES)
        sx_scr[j, P:P + T, :] = p_scr[:, XBC_OFF + j * LANES:XBC_OFF + (j + 1) * LANES]
        for k, xc in enumerate(_conv_classes(sx_scr, j, T, scw_ref, scb_ref[0:1, cs_], cs_)):
            xc_scr[j, pl.ds(k, T // CONV_STRIDE, stride=CONV_STRIDE), :] = _silu(xc)
    sx_scr[:, 0:P, :] = sx_scr[:, T:T + P, :]

    a_head = -jnp.exp(hvec_ref[1:2, :])
    qi = lax.broadcasted_iota(jnp.int32, (WINDOW, WINDOW), 0)
    si = lax.broadcasted_iota(jnp.int32, (WINDOW, WINDOW), 1)
    causal = si <= qi
    tri = causal.astype(F32)
    for i in range(nsub):
        r0 = i * WINDOW
        dt_c = _softplus(p_scr[r0:r0 + WINDOW, DT_OFF:DT_OFF + LANES] + hvec_ref[0:1, :])
        cs = jnp.dot(tri, dt_c * a_head, precision=lax.Precision.HIGHEST, preferred_element_type=F32)
        cst = cs.T
        ys = [None] * N_SSD_TILES
        for g in range(SSD_GROUPS):
            bg_t = xc_scr[N_SSD_TILES + g, r0:r0 + WINDOW, :].T.astype(BF16)
            cg = xc_scr[N_SSD_TILES + SSD_GROUPS + g, r0:r0 + WINDOW, :].astype(BF16)
            cbm = _dot(cg, bg_t)
            yo = _dot(cg, ht_scr[g].astype(BF16))
            xdec, edec = [], []
            for pp in range(TILES_PER_GROUP):
                p = g * TILES_PER_GROUP + pp
                bcs, mm, dts = [], [], []
                for h in (2 * p, 2 * p + 1):
                    bc = jnp.broadcast_to(cs[:, h:h + 1], (WINDOW, WINDOW))
                    lmat = jnp.exp(jnp.where(causal, bc - cst[h:h + 1, :], NEG_BIG))
                    mm.append((cbm * lmat).astype(BF16))
                    bcs.append(bc)
                    dts.append(jnp.broadcast_to(dt_c[:, h:h + 1], (WINDOW, LANES)))
                csl = jnp.where(lo_sq, bcs[0], bcs[1])
                ecs = jnp.exp(csl)
                dec = jnp.exp(csl[WINDOW - 1:WINDOW, :] - csl)
                xs_p = xc_scr[p, r0:r0 + WINDOW, :]
                xdt = xs_p * jnp.where(lo_sq, dts[0], dts[1])
                xdt_b = xdt.astype(BF16)
                y = jnp.where(lo_sq, _dot(mm[0], xdt_b), _dot(mm[1], xdt_b))
                y = y + yo[:, pp * LANES:(pp + 1) * LANES] * ecs
                y = y + svec_ref[0:1, p * LANES:(p + 1) * LANES] * xs_p
                ys[p] = y * _silu(p_scr[r0:r0 + WINDOW, Z_OFF + p * LANES:Z_OFF + (p + 1) * LANES])
                xdec.append((xdt * dec).astype(BF16))
                edec.append(ecs[WINDOW - 1:WINDOW, :])
            ht_scr[g] = (ht_scr[g] * jnp.concatenate(edec, axis=1)
                         + _dot(bg_t, jnp.concatenate(xdec, axis=1)))
        ss = jnp.sum(ys[0] * ys[0], -1, keepdims=True)
        for p in range(1, N_SSD_TILES):
            ss = ss + jnp.sum(ys[p] * ys[p], -1, keepdims=True)
        rinv = lax.rsqrt(ss * (1.0 / SSD_WIDTH) + NORM_EPS)
        for p in range(N_SSD_TILES):
            mix_scr[r0:r0 + WINDOW, MIX_SSD + p * LANES:MIX_SSD + (p + 1) * LANES] = (
                ys[p] * rinv * svec_ref[1:2, p * LANES:(p + 1) * LANES]).astype(BF16)

    out = _dot(mix_scr[...], wout_ref[...])
    y_ref[...] = _layer_norm(DEEPNORM_ALPHA * x_ref[...] + out, ln_ref[0:1, :], ln_ref[1:2, :])

    @pl.when(last)
    def _():
        tab = rope_ref[T - WINDOW:T, :]
        k0 = _rope(p_scr[T - WINDOW:T, K_OFF:K_OFF + LANES], tab)
        k1 = _rope(p_scr[T - WINDOW:T, K_OFF + LANES:K_OFF + 2 * LANES], tab)
        ko_ref[...] = jnp.where(lo_sq, k0, k1)
        vo_ref[...] = jnp.where(lo_sq, p_scr[T - WINDOW:T, V_OFF:V_OFF + LANES],
                                p_scr[T - WINDOW:T, V_OFF + LANES:V_OFF + 2 * LANES])
        for j in range(N_LRU_TILES):
            lco_ref[:, j * LANES:(j + 1) * LANES] = lx_scr[j, P + T - (CONV_W - 1):P + T, :]
        lho_ref[...] = hl_scr[...]
        for j in range(N_CONV_TILES):
            sco_ref[:, j * LANES:(j + 1) * LANES] = sx_scr[j, P + T - (CONV_W - 1):P + T, :]
        for g in range(SSD_GROUPS):
            for pp in range(TILES_PER_GROUP):
                tile = ht_scr[g, :, pp * LANES:(pp + 1) * LANES].T
                h = 2 * (g * TILES_PER_GROUP + pp)
                sho_ref[h] = tile[0:SSD_HEAD_DIM, :]
                sho_ref[h + 1] = tile[SSD_HEAD_DIM:2 * SSD_HEAD_DIM, :]


def _const_spec(shape, layer):
    nd = len(shape)
    return pl.BlockSpec((None,) + tuple(shape), lambda *_: (layer,) + (0,) * nd,
                        pipeline_mode=pl.Buffered(1))


def _prompt_layer(layer, x, rope_tab, sinks, prm):
    B, L, _ = x.shape
    T = PROMPT_TILE
    nc = L // T
    win, aux, wg, wout, lcw, lvec, scw, scb, hvec, svec, ln = prm
    tile = lambda b, c: (b, c, 0)
    per_b = lambda b, c: (b, 0, 0)

    in_specs = [
        pl.BlockSpec(memory_space=pltpu.SMEM),
        pl.BlockSpec((None, T, D_MODEL), tile),
        pl.BlockSpec((T, 3 * LANES), lambda b, c: (c, 0)),
        _const_spec((D_MODEL, W_IN_COLS), layer),
        _const_spec((D_MODEL, AUX_COLS), layer),
        _const_spec((LRU_WIDTH, 2 * LRU_WIDTH), layer),
        _const_spec((D_MIX, D_MODEL), layer),
        _const_spec((CONV_W, LRU_WIDTH), layer),
        _const_spec((4, LRU_WIDTH), layer),
        _const_spec((CONV_W, SSD_CONV_CH), layer),
        _const_spec((1, SSD_CONV_CH), layer),
        _const_spec((2, LANES), layer),
        _const_spec((2, SSD_WIDTH), layer),
        _const_spec((2, D_MODEL), layer),
    ]
    out_shape = (
        jax.ShapeDtypeStruct((B, L, D_MODEL), F32),
        jax.ShapeDtypeStruct((B, WINDOW, ATT_KV_WIDTH), F32),
        jax.ShapeDtypeStruct((B, WINDOW, ATT_KV_WIDTH), F32),
        jax.ShapeDtypeStruct((B, CONV_W - 1, LRU_WIDTH), F32),
        jax.ShapeDtypeStruct((B, 1, LRU_WIDTH), F32),
        jax.ShapeDtypeStruct((B, CONV_W - 1, SSD_CONV_CH), F32),
        jax.ShapeDtypeStruct((B, SSD_HEADS, SSD_HEAD_DIM, SSD_STATE), F32),
    )
    out_specs = (
        pl.BlockSpec((None, T, D_MODEL), tile),
        pl.BlockSpec((None, WINDOW, ATT_KV_WIDTH), per_b),
        pl.BlockSpec((None, WINDOW, ATT_KV_WIDTH), per_b),
        pl.BlockSpec((None, CONV_W - 1, LRU_WIDTH), per_b),
        pl.BlockSpec((None, 1, LRU_WIDTH), per_b),
        pl.BlockSpec((None, CONV_W - 1, SSD_CONV_CH), per_b),
        pl.BlockSpec((None, SSD_HEADS, SSD_HEAD_DIM, SSD_STATE), lambda b, c: (b, 0, 0, 0)),
    )
    scratch = [
        pltpu.VMEM((T, N_COLS), F32),
        pltpu.VMEM((ATT_KV_HEADS, WINDOW + T, LANES), BF16),
        pltpu.VMEM((ATT_KV_HEADS, WINDOW + T, LANES), BF16),
        pltpu.VMEM((N_LRU_TILES, SUBLANES + T, LANES), F32),
        pltpu.VMEM((N_CONV_TILES, SUBLANES + T, LANES), F32),
        pltpu.VMEM((N_LRU_TILES, T, LANES), F32),
        pltpu.VMEM((N_LRU_TILES, T, LANES), F32),
        pltpu.VMEM((1, LRU_WIDTH), F32),
        pltpu.VMEM((SSD_GROUPS, SSD_STATE, GROUP_W), F32),
        pltpu.VMEM((T, D_MIX), BF16),
        pltpu.VMEM((T, LRU_WIDTH), BF16),
        pltpu.VMEM((N_CONV_TILES, T, LANES), F32),
    ]
    return pl.pallas_call(
        _prompt_kernel,
        grid=(B, nc),
        in_specs=in_specs,
        out_specs=out_specs,
        out_shape=out_shape,
        scratch_shapes=scratch,
        compiler_params=pltpu.CompilerParams(
            dimension_semantics=("arbitrary", "arbitrary"),
            vmem_limit_bytes=VMEM_LIMIT_BYTES),
        name=f"prompt_layer{layer}",
    )(sinks, x, rope_tab, win, aux, wg, wout, lcw, lvec, scw, scb, hvec, svec, ln)


def _conv_step(x_new, st_ref, sto_ref, w_ref, bias):
    y = bias + w_ref[CONV_W - 1:CONV_W, :] * x_new
    for t in range(CONV_W - 1):
        y = y + w_ref[t:t + 1, :] * st_ref[t]
    for t in range(CONV_W - 2):
        sto_ref[t] = st_ref[t + 1]
    sto_ref[CONV_W - 2] = x_new
    return y


def _sample_kernel(n_aliased, sinks_ref, x_ref, rope_ref, win_ref, aux_ref, wg_ref, wout_ref, lcw_ref, lvec_ref,
                   scw_ref, scb_ref, hvec_ref, svec_ref, ln_ref,
                   ck_ref, cv_ref, lc_ref, lh_ref, sc_ref, sh_ref, *refs):
    (y_ref, cko_ref, cvo_ref, lco_ref, lho_ref, sco_ref, sho_ref,
     p_scr, q_scr, o_scr, kn_scr, vn_scr, xdt_scr, da_scr, y_scr, mix_scr) = refs[n_aliased:]
    NB = x_ref.shape[0]
    BT = SAMPLE_BT
    i = pl.program_id(0)
    lo_f = _lane_lo((NB, LANES))

    @pl.when(i == 0)
    def _():
        _project(x_ref[...], win_ref, aux_ref, p_scr)
        tab = rope_ref[...]
        for cc in range(ATT_HEADS // 2):
            g = cc // (ATT_HEADS // ATT_KV_HEADS // 2)
            qp = _rope(p_scr[:, Q_OFF + cc * LANES:Q_OFF + (cc + 1) * LANES], tab) * ATT_SCALE
            for half in range(2):
                t = jnp.where(lo_f if half == 0 else jnp.logical_not(lo_f), qp, 0.0)
                if half != g:
                    t = pltpu.roll(t, HALF, 1)
                q_scr[pl.ds(2 * cc + half, NB, stride=ATT_HEADS), :] = t
        k0 = _rope(p_scr[:, K_OFF:K_OFF + LANES], tab)
        k1 = _rope(p_scr[:, K_OFF + LANES:K_OFF + 2 * LANES], tab)
        kn_scr[...] = jnp.where(lo_f, k0, k1)
        vn_scr[...] = jnp.where(lo_f, p_scr[:, V_OFF:V_OFF + LANES], p_scr[:, V_OFF + LANES:V_OFF + 2 * LANES])
        xin = p_scr[:, XL_OFF:XL_OFF + LRU_WIDTH]
        xl = _conv_step(xin, lc_ref, lco_ref, lcw_ref, lvec_ref[0:1, :])
        gates = _dot(xl.astype(BF16), wg_ref[...])
        a, bt = _lru_coeffs(xl, gates, lvec_ref)
        h1 = a * lh_ref[...] + bt
        lho_ref[...] = h1
        mix_scr[:, MIX_LRU:MIX_LRU + LRU_WIDTH] = (h1 * _silu(p_scr[:, GL_OFF:GL_OFF + LRU_WIDTH])).astype(BF16)
        xin2 = p_scr[:, XBC_OFF:XBC_OFF + SSD_CONV_CH]
        xc = _conv_step(xin2, sc_ref, sco_ref, scw_ref, scb_ref[0:1, :])
        p_scr[:, XBC_OFF:XBC_OFF + SSD_CONV_CH] = _silu(xc)
        dt = _softplus(p_scr[:, DT_OFF:DT_OFF + LANES] + hvec_ref[0:1, :])
        da_scr[...] = jnp.exp(dt * (-jnp.exp(hvec_ref[1:2, :])))
        for p in range(N_SSD_TILES):
            dte = jnp.where(lo_f, jnp.broadcast_to(dt[:, 2 * p:2 * p + 1], (NB, LANES)),
                            jnp.broadcast_to(dt[:, 2 * p + 1:2 * p + 2], (NB, LANES)))
            xdt_scr[:, p * LANES:(p + 1) * LANES] = p_scr[:, XBC_OFF + p * LANES:XBC_OFF + (p + 1) * LANES] * dte

    r8 = pl.multiple_of(i * BT, BT)
    kn_blk = kn_scr[pl.ds(r8, BT), :]
    vn_blk = vn_scr[pl.ds(r8, BT), :]
    zpad = jnp.zeros((LANES - BT, LANES), F32)
    kn_t = jnp.concatenate([kn_blk, zpad], axis=0).T
    vn_t = jnp.concatenate([vn_blk, zpad], axis=0).T
    newest = lax.broadcasted_iota(jnp.int32, (ATT_KV_WIDTH, WINDOW), 1) == WINDOW - 1
    da_t =jnp.concatenate([da_scr[pl.ds(r8, BT), :], zpad], axis=0).T
    x_t = [jnp.concatenate([xdt_scr[pl.ds(r8, BT), p * LANES:(p + 1) * LANES], zpad], axis=0).T
           for p in range(N_SSD_TILES)]
    b_blk = p_scr[pl.ds(r8, BT), B_OFF:B_OFF + SSD_GROUPS * SSD_STATE]
    c_blk = p_scr[pl.ds(r8, BT), C_OFF:C_OFF + SSD_GROUPS * SSD_STATE].astype(BF16)
    row8 = lax.broadcasted_iota(jnp.int32, (ATT_HEADS, 1), 0)
    sink = jnp.zeros((ATT_HEADS, 1), F32)
    for h in range(ATT_HEADS):
        sink = jnp.where(row8 == h, sinks_ref[h], sink)
    rowb = lax.broadcasted_iota(jnp.int32, (BT, GROUP_W), 0)
    y_acc = [jnp.zeros((BT, GROUP_W), F32) for _ in range(SSD_GROUPS)]
    rowk = lax.broadcasted_iota(jnp.int32, (BT, SSD_STATE), 0)
    outer = []
    for g in range(SSD_GROUPS):
        bg = b_blk[:, g * SSD_STATE:(g + 1) * SSD_STATE]
        diag = jnp.concatenate([jnp.where(rowk == bb, bg, 0.0) for bb in range(BT)], axis=1)
        rhs = jnp.concatenate([diag, jnp.zeros((LANES - BT, BT * SSD_STATE), F32)], axis=0).astype(BF16)
        outer.append([_dot(x_t[g * TILES_PER_GROUP + pp].astype(BF16), rhs) for pp in range(TILES_PER_GROUP)])
    for bb in range(BT):
        qr = q_scr[pl.ds(pl.multiple_of((r8 + bb) * ATT_HEADS, ATT_HEADS), ATT_HEADS), :]
        kt = ck_ref[bb]
        vt = cv_ref[bb]
        s = _dot(qr.astype(BF16), kt.astype(BF16))
        s_new = jnp.sum(qr * kn_blk[bb:bb + 1, :], -1, keepdims=True)
        m = jnp.maximum(jnp.maximum(jnp.max(s, -1, keepdims=True), s_new), sink)
        e = jnp.exp(s - m)
        e_new = jnp.exp(s_new - m)
        den = jnp.sum(e, -1, keepdims=True) + e_new + jnp.exp(sink - m)
        o = _dot_nt(e.astype(BF16), vt.astype(BF16)) + e_new * vn_blk[bb:bb + 1, :]
        o_scr[pl.ds(pl.multiple_of((r8 + bb) * ATT_HEADS, ATT_HEADS), ATT_HEADS), :] = o * (1.0 / den)
        cko_ref[bb] = jnp.where(newest, kn_t[:, bb:bb + 1], pltpu.roll(kt, WINDOW - 1, 1))
        cvo_ref[bb] = jnp.where(newest, vn_t[:, bb:bb + 1], pltpu.roll(vt, WINDOW - 1, 1))
        for g in range(SSD_GROUPS):
            tiles = []
            for pp in range(TILES_PER_GROUP):
                p = g * TILES_PER_GROUP + pp
                for hh in range(2):
                    h = 2 * p + hh
                    dab = jnp.broadcast_to(da_t[h:h + 1, bb:bb + 1], (SSD_HEAD_DIM, SSD_STATE))
                    h1 = sh_ref[bb, h] * dab + outer[g][pp][hh * SSD_HEAD_DIM:(hh + 1) * SSD_HEAD_DIM,
                                                               bb * SSD_STATE:(bb + 1) * SSD_STATE]
                    sho_ref[bb, h] = h1
                    tiles.append(h1.astype(BF16))
            res = _dot_nt(c_blk[:, g * SSD_STATE:(g + 1) * SSD_STATE], jnp.concatenate(tiles, axis=0))
            y_acc[g] = jnp.where(rowb == bb, res, y_acc[g])
    for g in range(SSD_GROUPS):
        y_scr[pl.ds(r8, BT), g * GROUP_W:(g + 1) * GROUP_W] = y_acc[g]

    @pl.when(i == pl.num_programs(0) - 1)
    def _():
        for cc in range(ATT_HEADS // 2):
            g = cc // (ATT_HEADS // ATT_KV_HEADS // 2)
            oe = o_scr[pl.ds(2 * cc, NB, stride=ATT_HEADS), :]
            oo = o_scr[pl.ds(2 * cc + 1, NB, stride=ATT_HEADS), :]
            if g == 0:
                oo = pltpu.roll(oo, HALF, 1)
            else:
                oe = pltpu.roll(oe, HALF, 1)
            gate = p_scr[:, GA_OFF + cc * LANES:GA_OFF + (cc + 1) * LANES]
            mix_scr[:, cc * LANES:(cc + 1) * LANES] = (jnp.where(lo_f, oe, oo) * _silu(gate)).astype(BF16)
        y = (y_scr[...] + svec_ref[0:1, :] * p_scr[:, XBC_OFF:XBC_OFF + SSD_WIDTH]) * _silu(p_scr[:, Z_OFF:Z_OFF + SSD_WIDTH])
        y = y * lax.rsqrt(jnp.mean(y * y, -1, keepdims=True) + NORM_EPS) * svec_ref[1:2, :]
        mix_scr[:, MIX_SSD:MIX_SSD + SSD_WIDTH] = y.astype(BF16)
        out = _dot(mix_scr[...], wout_ref[...])
        y_ref[...] = _layer_norm(DEEPNORM_ALPHA * x_ref[...] + out, ln_ref[0:1, :], ln_ref[1:2, :])


def _sample_layer(layer, x, rope_tab, sinks, prm, ck, cv, lc, lh, sc, sh, prev):
    NB = x.shape[0]
    BT = SAMPLE_BT
    H = CONV_W - 1
    win, aux, wg, wout, lcw, lvec, scw, scb, hvec, svec, ln = prm
    whole = lambda shape: pl.BlockSpec(shape, lambda i: (0,) * len(shape), pipeline_mode=pl.Buffered(1))
    lwhole = lambda shape: pl.BlockSpec((None,) + shape, lambda i: (layer,) + (0,) * len(shape),
                                        pipeline_mode=pl.Buffered(1))
    rows = lambda shape: pl.BlockSpec((None, BT) + shape, lambda i: (layer, i) + (0,) * len(shape))
    state_specs = [
        rows((ATT_KV_WIDTH, WINDOW)),
        rows((ATT_KV_WIDTH, WINDOW)),
        lwhole((H, NB, LRU_WIDTH)),
        lwhole((NB, LRU_WIDTH)),
        lwhole((H, NB, SSD_CONV_CH)),
        rows((SSD_HEADS, SSD_HEAD_DIM, SSD_STATE)),
    ]
    in_specs = [
        pl.BlockSpec(memory_space=pltpu.SMEM),
        whole((NB, D_MODEL)),
        whole((1, 3 * LANES)),
        _const_spec((D_MODEL, W_IN_COLS), layer),
        _const_spec((D_MODEL, AUX_COLS), layer),
        _const_spec((LRU_WIDTH, 2 * LRU_WIDTH), layer),
        _const_spec((D_MIX, D_MODEL), layer),
        _const_spec((CONV_W, LRU_WIDTH), layer),
        _const_spec((4, LRU_WIDTH), layer),
        _const_spec((CONV_W, SSD_CONV_CH), layer),
        _const_spec((1, SSD_CONV_CH), layer),
        _const_spec((2, LANES), layer),
        _const_spec((2, SSD_WIDTH), layer),
        _const_spec((2, D_MODEL), layer),
    ] + state_specs
    operands = [sinks, x, rope_tab, win, aux, wg, wout, lcw, lvec, scw, scb, hvec, svec, ln, ck, cv, lc, lh, sc, sh]
    aliases = {}
    if prev is not None:
        for k, arr in enumerate(prev):
            aliases[len(operands)] = 1 + k
            operands.append(arr)
            in_specs.append(pl.BlockSpec(memory_space=pl.ANY))
    out_shape = (jax.ShapeDtypeStruct((NB, D_MODEL), F32),) + tuple(
        jax.ShapeDtypeStruct(a.shape, F32) for a in (ck, cv, lc, lh, sc, sh))
    full = lambda shape: pl.BlockSpec(shape, lambda i: (0,) * len(shape))
    lfull = lambda shape: pl.BlockSpec((None,) + shape, lambda i: (layer,) + (0,) * len(shape))
    out_specs = (
        full((NB, D_MODEL)),
        rows((ATT_KV_WIDTH, WINDOW)),
        rows((ATT_KV_WIDTH, WINDOW)),
        lfull((H, NB, LRU_WIDTH)),
        lfull((NB, LRU_WIDTH)),
        lfull((H, NB, SSD_CONV_CH)),
        rows((SSD_HEADS, SSD_HEAD_DIM, SSD_STATE)),
    )
    scratch = [
        pltpu.VMEM((NB, N_COLS), F32),
        pltpu.VMEM((NB * ATT_HEADS, LANES), F32),
        pltpu.VMEM((NB * ATT_HEADS, LANES), F32),
        pltpu.VMEM((NB, ATT_KV_WIDTH), F32),
        pltpu.VMEM((NB, ATT_KV_WIDTH), F32),
        pltpu.VMEM((NB, SSD_WIDTH), F32),
        pltpu.VMEM((NB, LANES), F32),
        pltpu.VMEM((NB, SSD_WIDTH), F32),
        pltpu.VMEM((NB, D_MIX), BF16),
    ]
    return pl.pallas_call(
        functools.partial(_sample_kernel, len(aliases)),
        grid=(NB // BT,),
        in_specs=in_specs,
        out_specs=out_specs,
        out_shape=out_shape,
        scratch_shapes=scratch,
        input_output_aliases=aliases,
        compiler_params=pltpu.CompilerParams(
            dimension_semantics=("arbitrary",),
            vmem_limit_bytes=VMEM_LIMIT_BYTES),
        name=f"sample_layer{layer}",
    )(*operands)


def _rope_table(pos):
    half = ROPE_DIM // 2
    inv = ROPE_THETA ** (-jnp.arange(half, dtype=F32) / half)
    ang = pos.astype(F32)[:, None] * inv[None, :]
    cos, sin = jnp.cos(ang), jnp.sin(ang)
    d = jnp.arange(LANES) % ATT_HEAD_DIM
    fi = d % half
    c = jnp.where(d < ROPE_DIM, cos[:, fi], 1.0)
    s1 = jnp.where(d < half, -sin[:, fi], 0.0)
    s2 = jnp.where((d >= half) & (d < ROPE_DIM), sin[:, fi], 0.0)
    return jnp.concatenate([c, s1, s2], axis=1)


def _prep_params(w_in, w_out, lru_conv_w, lru_conv_b, lru_wa, lru_ba, lru_wx, lru_bx, lru_lambda,
                 ssd_conv_w, ssd_conv_b, ssd_dt_bias, ssd_a_log, ssd_d, ssd_norm_g, ln_g, ln_b):
    hd = ATT_HEAD_DIM
    win = w_in.astype(BF16)
    k = w_in[..., ATT_WIDTH:ATT_WIDTH + ATT_KV_WIDTH]
    v = w_in[..., ATT_WIDTH + ATT_KV_WIDTH:ATT_WIDTH + 2 * ATT_KV_WIDTH]
    dup = lambda t: jnp.concatenate([t[..., :hd], t[..., :hd], t[..., hd:], t[..., hd:]], -1)
    dtw = jnp.pad(w_in[..., W_IN_COLS - SSD_HEADS:], ((0, 0), (0, 0), (0, LANES - SSD_HEADS)))
    aux = jnp.concatenate([dup(k), dup(v), dtw], -1).astype(BF16)

    def dense(w):
        rows = [jnp.pad(w[:, n], ((0, 0), (0, 0), (n * LRU_BLOCK, LRU_WIDTH - (n + 1) * LRU_BLOCK)))
                for n in range(LRU_BLOCKS)]
        return jnp.concatenate(rows, 1)

    wg = jnp.concatenate([dense(lru_wa.astype(BF16)), dense(lru_wx.astype(BF16))], -1)
    wout = w_out.astype(BF16)
    lvec = jnp.stack([lru_conv_b, lru_ba, lru_bx, lru_lambda], 1)
    hpad = lambda t: jnp.pad(t, ((0, 0), (0, LANES - SSD_HEADS)))
    hvec = jnp.stack([hpad(ssd_dt_bias), hpad(ssd_a_log)], 1)
    svec = jnp.stack([jnp.repeat(ssd_d, SSD_HEAD_DIM, axis=1), ssd_norm_g], 1)
    ln = jnp.stack([ln_g, ln_b], 1)
    return (win, aux, wg, wout, lru_conv_w, lvec, ssd_conv_w, ssd_conv_b[:, None, :], hvec, svec, ln)


def kernel(x_prompt, x_sample, cache_swa_k, cache_swa_v, state_lru_conv, state_lru_h, state_ssd_conv, state_ssd_h, w_in, w_out, att_sinks, lru_conv_w, lru_conv_b, lru_wa, lru_ba, lru_wx, lru_bx, lru_lambda, ssd_conv_w, ssd_conv_b, ssd_dt_bias, ssd_a_log, ssd_d, ssd_norm_g, ln_g, ln_b):
    prm = _prep_params(w_in, w_out, lru_conv_w, lru_conv_b, lru_wa, lru_ba, lru_wx, lru_bx, lru_lambda,
                       ssd_conv_w, ssd_conv_b, ssd_dt_bias, ssd_a_log, ssd_d, ssd_norm_g, ln_g, ln_b)
    bp, lp, _ = x_prompt.shape
    rope_p = _rope_table(jnp.arange(lp, dtype=jnp.int32))
    xp = x_prompt
    new_p = [[] for _ in range(6)]
    for l in range(DEPTH):
        xp, ko, vo, lco, lho, sco, sho = _prompt_layer(l, xp, rope_p, att_sinks[l], prm)
        st = (ko.reshape(bp, WINDOW, ATT_KV_HEADS, ATT_HEAD_DIM), vo.reshape(bp, WINDOW, ATT_KV_HEADS, ATT_HEAD_DIM),
              lco, lho.reshape(bp, LRU_WIDTH), sco, sho)
        for lst, t in zip(new_p, st):
            lst.append(t)
    outs_p = [jnp.stack(t) for t in new_p]

    nb = x_sample.shape[0]
    rope_s = _rope_table(PAST_LEN + jnp.arange(x_sample.shape[1], dtype=jnp.int32))
    to_kt = lambda t: jnp.transpose(t, (0, 1, 3, 4, 2)).reshape(DEPTH, nb, ATT_KV_WIDTH, WINDOW)
    from_kt = lambda t: jnp.transpose(t.reshape(DEPTH, nb, ATT_KV_HEADS, ATT_HEAD_DIM, WINDOW), (0, 1, 4, 2, 3))
    swap = lambda t: jnp.transpose(t, (0, 2, 1, 3))
    state = (to_kt(cache_swa_k), to_kt(cache_swa_v), swap(state_lru_conv), state_lru_h, swap(state_ssd_conv),
             state_ssd_h)
    xs = x_sample.reshape(nb, D_MODEL)
    new = None
    for l in range(DEPTH):
        xs, *new = _sample_layer(l, xs, rope_s, att_sinks[l], prm, *state, new)
    cko, cvo, lco, lho, sco, sho = new
    outs_s = (from_kt(cko), from_kt(cvo), swap(lco), lho, swap(sco), sho)
    return (xp, xs.reshape(x_sample.shape)) + tuple(outs_p) + tuple(outs_s)
```

```python
import functools
import math

import jax
import jax.numpy as jnp
from jax import lax
from jax.experimental import pallas as pl
from jax.experimental.pallas import tpu as pltpu

F32 = jnp.float32
BF16 = jnp.bfloat16

D_MODEL = 1024
DEPTH = 4
PAST_LEN = 8192
D_MIX = 2 * D_MODEL
ATT_HEADS = 8
ATT_KV_HEADS = 2
ATT_HEAD_DIM = 64
ATT_WIDTH = ATT_HEADS * ATT_HEAD_DIM
ATT_KV_WIDTH = ATT_KV_HEADS * ATT_HEAD_DIM
ATT_SCALE = ATT_HEAD_DIM ** -0.5
WINDOW = 128
ROPE_THETA = 500000.0
ROPE_DIM = ATT_HEAD_DIM // 4
LRU_WIDTH = 3 * D_MIX // 8
LRU_BLOCKS = 8
LRU_BLOCK = LRU_WIDTH // LRU_BLOCKS
LRU_C = 8.0
CONV_W = 4
SSD_WIDTH = D_MIX - ATT_WIDTH - LRU_WIDTH
SSD_HEAD_DIM = 64
SSD_HEADS = SSD_WIDTH // SSD_HEAD_DIM
SSD_GROUPS = 2
SSD_STATE = 128
SSD_CONV_CH = SSD_WIDTH + 2 * SSD_GROUPS * SSD_STATE
DEEPNORM_ALPHA = (2.0 * DEPTH) ** 0.25
NORM_EPS = 1e-5

LANES = 128
SUBLANES = 8
HALF = LANES // 2
VMEM_LIMIT_BYTES = 60 * 1024 * 1024

Q_OFF = 0
K_OFF = Q_OFF + ATT_WIDTH
V_OFF = K_OFF + 2 * ATT_KV_WIDTH
GA_OFF = V_OFF + 2 * ATT_KV_WIDTH
XL_OFF = GA_OFF + ATT_WIDTH
GL_OFF = XL_OFF + LRU_WIDTH
Z_OFF = GL_OFF + LRU_WIDTH
XBC_OFF = Z_OFF + SSD_WIDTH
DT_OFF = XBC_OFF + SSD_CONV_CH
N_COLS = DT_OFF + LANES
W_IN_COLS = 2 * ATT_WIDTH + 2 * ATT_KV_WIDTH + 2 * LRU_WIDTH + SSD_WIDTH + SSD_CONV_CH + SSD_HEADS
AUX_COLS = 4 * ATT_KV_WIDTH + LANES
B_OFF = XBC_OFF + SSD_WIDTH
C_OFF = B_OFF + SSD_GROUPS * SSD_STATE
MIX_LRU = ATT_WIDTH
MIX_SSD = ATT_WIDTH + LRU_WIDTH

N_LRU_TILES = LRU_WIDTH // LANES
N_SSD_TILES = SSD_WIDTH // LANES
N_CONV_TILES = SSD_CONV_CH // LANES
CONV_STRIDE = 4
TILES_PER_GROUP = N_SSD_TILES // SSD_GROUPS
GROUP_W = SSD_WIDTH // SSD_GROUPS

NEG_BIG = -1e30

PROMPT_TILE = 256
SAMPLE_BT = 8


def _sigmoid(x):
    return 0.5 * jnp.tanh(0.5 * x) + 0.5


def _silu(x):
    h = 0.5 * x
    return h + h * jnp.tanh(h)


def _softplus(x):
    return jnp.maximum(x, 0.0) + jnp.log1p(jnp.exp(-jnp.abs(x)))


def _lane_lo(shape):
    return (lax.broadcasted_iota(jnp.int32, shape, len(shape) - 1) % LANES) < HALF


def _rope(t, tab):
    half = ROPE_DIM // 2
    return (t * tab[:, 0:LANES]
            + pltpu.roll(t, LANES - half, 1) * tab[:, LANES:2 * LANES]
            + pltpu.roll(t, half, 1) * tab[:, 2 * LANES:3 * LANES])


def _dot(a, b):
    return jnp.dot(a, b, preferred_element_type=F32)


def _dot_nt(a, b):
    return lax.dot_general(a, b, (((1,), (1,)), ((), ())), preferred_element_type=F32)


def _conv_classes(src_ref, j, rows, w_ref, bias, cs):
    n = CONV_STRIDE
    hist = CONV_W - 1
    taps = {s_: src_ref[j, pl.ds(SUBLANES + s_, rows // n, stride=n), :] for s_ in range(-hist, n)}
    outs = []
    for k in range(n):
        acc = bias + w_ref[hist:hist + 1, cs] * taps[k]
        for t in range(hist):
            acc = acc + w_ref[t:t + 1, cs] * taps[k - hist + t]
        outs.append(acc)
    return outs


def _layer_norm(v, g, b):
    mu = jnp.mean(v, -1, keepdims=True)
    d = v - mu
    var = jnp.mean(d * d, -1, keepdims=True)
    return d * lax.rsqrt(var + NORM_EPS) * g + b


def _lru_coeffs(xl, gates, lvec_ref):
    r = _sigmoid(gates[:, 0:LRU_WIDTH] + lvec_ref[1:2, :])
    ig = _sigmoid(gates[:, LRU_WIDTH:2 * LRU_WIDTH] + lvec_ref[2:3, :])
    log_a = (-LRU_C * _softplus(-lvec_ref[3:4, :])) * r
    a = jnp.exp(log_a)
    bt = jnp.sqrt(-jnp.tanh(log_a) * (1.0 + a * a)) * (ig * xl)
    return a, bt


def _project(x_tile, win_ref, aux_ref, dst):
    xb = x_tile.astype(BF16)
    kv0 = ATT_WIDTH + 2 * ATT_KV_WIDTH
    dst[:, Q_OFF:K_OFF] = _dot(xb, win_ref[:, 0:ATT_WIDTH])
    dst[:, K_OFF:GA_OFF] = _dot(xb, aux_ref[:, 0:GA_OFF - K_OFF])
    dst[:, GA_OFF:Z_OFF] = _dot(xb, win_ref[:, kv0:kv0 + Z_OFF - GA_OFF])
    dst[:, Z_OFF:DT_OFF] = _dot(xb, win_ref[:, kv0 + Z_OFF - GA_OFF:kv0 + DT_OFF - GA_OFF])
    dst[:, DT_OFF:N_COLS] = _dot(xb, aux_ref[:, GA_OFF - K_OFF:GA_OFF - K_OFF + LANES])


def _prompt_kernel(sinks_ref, x_ref, rope_ref, win_ref, aux_ref, wg_ref, wout_ref, lcw_ref, lvec_ref,
                   scw_ref, scb_ref, hvec_ref, svec_ref, ln_ref,
                   y_ref, ko_ref, vo_ref, lco_ref, lho_ref, sco_ref, sho_ref,
                   p_scr, kd_scr, vd_scr, lx_scr, sx_scr, sa_scr, sb_scr, hl_scr, ht_scr, mix_scr, xlb_scr, xc_scr):
    T = PROMPT_TILE
    nsub = T // WINDOW
    c = pl.program_id(1)
    last = c == pl.num_programs(1) - 1

    @pl.when(c == 0)
    def _():
        kd_scr[:, 0:WINDOW, :] = jnp.zeros((2, WINDOW, LANES), BF16)
        vd_scr[:, 0:WINDOW, :] = jnp.zeros((2, WINDOW, LANES), BF16)
        lx_scr[:, 0:SUBLANES, :] = jnp.zeros((N_LRU_TILES, SUBLANES, LANES), F32)
        sx_scr[:, 0:SUBLANES, :] = jnp.zeros((N_CONV_TILES, SUBLANES, LANES), F32)
        hl_scr[...] = jnp.zeros_like(hl_scr)
        ht_scr[...] = jnp.zeros_like(ht_scr)

    _project(x_ref[...], win_ref, aux_ref, p_scr)

    lo_sq = _lane_lo((WINDOW, LANES))

    for g in range(ATT_KV_HEADS):
        kd = _rope(p_scr[:, K_OFF + g * LANES:K_OFF + (g + 1) * LANES], rope_ref[...])
        kd_scr[g, WINDOW:WINDOW + T, :] = kd.astype(BF16)
        vd_scr[g, WINDOW:WINDOW + T, :] = p_scr[:, V_OFF + g * LANES:V_OFF + (g + 1) * LANES].astype(BF16)

    row = lax.broadcasted_iota(jnp.int32, (WINDOW, 2 * WINDOW), 0)
    col = lax.broadcasted_iota(jnp.int32, (WINDOW, 2 * WINDOW), 1)
    band = (col >= row) & (col <= row + WINDOW)
    first_lo = jnp.where(c > 0, 0, WINDOW)
    for i in range(nsub):
        r0 = i * WINDOW
        msk = (band & (col >= first_lo)) if i == 0 else band
        tab = rope_ref[r0:r0 + WINDOW, :]
        for cc in range(ATT_HEADS // 2):
            g = cc // (ATT_HEADS // ATT_KV_HEADS // 2)
            kg = kd_scr[g, r0:r0 + 2 * WINDOW, :]
            vg = vd_scr[g, r0:r0 + 2 * WINDOW, :]
            qp = _rope(p_scr[r0:r0 + WINDOW, Q_OFF + cc * LANES:Q_OFF + (cc + 1) * LANES], tab) * ATT_SCALE
            outs = []
            for half in range(2):
                sink = sinks_ref[2 * cc + half]
                qm = jnp.where(lo_sq if half == 0 else jnp.logical_not(lo_sq), qp, 0.0).astype(BF16)
                s = jnp.where(msk, _dot_nt(qm, kg), NEG_BIG)
                m = jnp.maximum(jnp.max(s, -1, keepdims=True), sink)
                e = jnp.exp(s - m)
                den = jnp.sum(e, -1, keepdims=True) + jnp.exp(sink - m)
                outs.append(_dot(e.astype(BF16), vg) * (1.0 / den))
            att = jnp.where(lo_sq, outs[0], outs[1])
            gate = p_scr[r0:r0 + WINDOW, GA_OFF + cc * LANES:GA_OFF + (cc + 1) * LANES]
            mix_scr[r0:r0 + WINDOW, cc * LANES:(cc + 1) * LANES] = (att * _silu(gate)).astype(BF16)
    for g in range(ATT_KV_HEADS):
        kd_scr[g, 0:WINDOW, :] = kd_scr[g, T:T + WINDOW, :]
        vd_scr[g, 0:WINDOW, :] = vd_scr[g, T:T + WINDOW, :]

    P = SUBLANES
    H = CONV_W - 1
    for j in range(N_LRU_TILES):
        cs_ = slice(j * LANES, (j + 1) * LANES)
        lx_scr[j, P:P + T, :] = p_scr[:, XL_OFF + j * LANES:XL_OFF + (j + 1) * LANES]
        for k, xl in enumerate(_conv_classes(lx_scr, j, T, lcw_ref, lvec_ref[0:1, cs_], cs_)):
            sb_scr[j, pl.ds(k, T // CONV_STRIDE, stride=CONV_STRIDE), :] = xl
        xlb_scr[:, cs_] = sb_scr[j].astype(BF16)
    lx_scr[:, 0:P, :] = lx_scr[:, T:T + P, :]

    G = T // SUBLANES
    row_g = lax.broadcasted_iota(jnp.int32, (G, LANES), 0)
    cl = -LRU_C * _softplus(-lvec_ref[3:4, :])
    for jj in range(N_LRU_TILES // 2):
        c0 = jj * 2 * LANES
        k0 = (c0 // LRU_BLOCK) * LRU_BLOCK // LANES * LANES
        k1 = -(-(-(-(c0 + 2 * LANES) // LRU_BLOCK) * LRU_BLOCK) // LANES) * LANES
        xk = xlb_scr[:, k0:k1]
        gr = _dot(xk, wg_ref[k0:k1, c0:c0 + 2 * LANES])
        gi = _dot(xk, wg_ref[k0:k1, LRU_WIDTH + c0:LRU_WIDTH + c0 + 2 * LANES])
        for u in range(2):
            j = 2 * jj + u
            cs_ = slice(j * LANES, (j + 1) * LANES)
            us = slice(u * LANES, (u + 1) * LANES)
            r = _sigmoid(gr[:, us] + lvec_ref[1:2, cs_])
            ig = _sigmoid(gi[:, us] + lvec_ref[2:3, cs_])
            log_a = cl[:, cs_] * r
            a = jnp.exp(log_a)
            sa_scr[j] = a
            sb_scr[j] = jnp.sqrt(-jnp.tanh(log_a) * (1.0 + a * a)) * (ig * sb_scr[j])
    for j in range(N_LRU_TILES):
        ca = sa_scr[j, pl.ds(0, G, stride=SUBLANES), :]
        cb = sb_scr[j, pl.ds(0, G, stride=SUBLANES), :]
        for k in range(1, SUBLANES):
            ak = sa_scr[j, pl.ds(k, G, stride=SUBLANES), :]
            cb = ak * cb + sb_scr[j, pl.ds(k, G, stride=SUBLANES), :]
            ca = ak * ca
            sa_scr[j, pl.ds(k, G, stride=SUBLANES), :] = ca
            sb_scr[j, pl.ds(k, G, stride=SUBLANES), :] = cb
        hin = hl_scr[0:1, j * LANES:(j + 1) * LANES]
        cb = cb + jnp.where(row_g == 0, ca * hin, 0.0)
        d = 1
        while d < G:
            ok = row_g >= d
            cb = cb + ca * jnp.where(ok, pltpu.roll(cb, d, 0), 0.0)
            ca = ca * jnp.where(ok, pltpu.roll(ca, d, 0), 1.0)
            d *= 2
        hprev = jnp.where(row_g == 0, hin, pltpu.roll(cb, 1, 0))
        for k in range(SUBLANES):
            sb_scr[j, pl.ds(k, G, stride=SUBLANES), :] = (
                sb_scr[j, pl.ds(k, G, stride=SUBLANES), :] + sa_scr[j, pl.ds(k, G, stride=SUBLANES), :] * hprev)
        hl_scr[0:1, j * LANES:(j + 1) * LANES] = cb[G - 1:G, :]
        gl = p_scr[:, GL_OFF + j * LANES:GL_OFF + (j + 1) * LANES]
        mix_scr[:, MIX_LRU + j * LANES:MIX_LRU + (j + 1) * LANES] = (sb_scr[j] * _silu(gl)).astype(BF16)

    for j in range(N_CONV_TILES):
        cs_ = slice(j * LANES, (j + 1) * LANES)
        sx_scr[j, P:P + T, :] = p_scr[:, XBC_OFF + j * LANES:XBC_OFF + (j + 1) * LANES]
        for k, xc in enumerate(_conv_classes(sx_scr, j, T, scw_ref, scb_ref[0:1, cs_], cs_)):
            xc_scr[j, pl.ds(k, T // CONV_STRIDE, stride=CONV_STRIDE), :] = _silu(xc)
    sx_scr[:, 0:P, :] = sx_scr[:, T:T + P, :]

    a_head = -jnp.exp(hvec_ref[1:2, :])
    qi = lax.broadcasted_iota(jnp.int32, (WINDOW, WINDOW), 0)
    si = lax.broadcasted_iota(jnp.int32, (WINDOW, WINDOW), 1)
    causal = si <= qi
    tri = causal.astype(F32)
    for i in range(nsub):
        r0 = i * WINDOW
        dt_c = _softplus(p_scr[r0:r0 + WINDOW, DT_OFF:DT_OFF + LANES] + hvec_ref[0:1, :])
        cs = jnp.dot(tri, dt_c * a_head, precision=lax.Precision.HIGHEST, preferred_element_type=F32)
        cst = cs.T
        ys = [None] * N_SSD_TILES
        for g in range(SSD_GROUPS):
            bg_t = xc_scr[N_SSD_TILES + g, r0:r0 + WINDOW, :].T.astype(BF16)
            cg = xc_scr[N_SSD_TILES + SSD_GROUPS + g, r0:r0 + WINDOW, :].astype(BF16)
            cbm = _dot(cg, bg_t)
            yo = _dot(cg, ht_scr[g].astype(BF16))
            xdec, edec = [], []
            for pp in range(TILES_PER_GROUP):
                p = g * TILES_PER_GROUP + pp
                bcs, mm, dts = [], [], []
                for h in (2 * p, 2 * p + 1):
                    bc = jnp.broadcast_to(cs[:, h:h + 1], (WINDOW, WINDOW))
                    lmat = jnp.exp(jnp.where(causal, bc - cst[h:h + 1, :], NEG_BIG))
                    mm.append((cbm * lmat).astype(BF16))
                    bcs.append(bc)
                    dts.append(jnp.broadcast_to(dt_c[:, h:h + 1], (WINDOW, LANES)))
                csl = jnp.where(lo_sq, bcs[0], bcs[1])
                ecs = jnp.exp(csl)
                dec = jnp.exp(csl[WINDOW - 1:WINDOW, :] - csl)
                xs_p = xc_scr[p, r0:r0 + WINDOW, :]
                xdt = xs_p * jnp.where(lo_sq, dts[0], dts[1])
                xdt_b = xdt.astype(BF16)
                y = jnp.where(lo_sq, _dot(mm[0], xdt_b), _dot(mm[1], xdt_b))
                y = y + yo[:, pp * LANES:(pp + 1) * LANES] * ecs
                y = y + svec_ref[0:1, p * LANES:(p + 1) * LANES] * xs_p
                ys[p] = y * _silu(p_scr[r0:r0 + WINDOW, Z_OFF + p * LANES:Z_OFF + (p + 1) * LANES])
                xdec.append((xdt * dec).astype(BF16))
                edec.append(ecs[WINDOW - 1:WINDOW, :])
            ht_scr[g] = (ht_scr[g] * jnp.concatenate(edec, axis=1)
                         + _dot(bg_t, jnp.concatenate(xdec, axis=1)))
        ss = jnp.sum(ys[0] * ys[0], -1, keepdims=True)
        for p in range(1, N_SSD_TILES):
            ss = ss + jnp.sum(ys[p] * ys[p], -1, keepdims=True)
        rinv = lax.rsqrt(ss * (1.0 / SSD_WIDTH) + NORM_EPS)
        for p in range(N_SSD_TILES):
            mix_scr[r0:r0 + WINDOW, MIX_SSD + p * LANES:MIX_SSD + (p + 1) * LANES] = (
                ys[p] * rinv * svec_ref[1:2, p * LANES:(p + 1) * LANES]).astype(BF16)

    out = _dot(mix_scr[...], wout_ref[...])
    y_ref[...] = _layer_norm(DEEPNORM_ALPHA * x_ref[...] + out, ln_ref[0:1, :], ln_ref[1:2, :])

    @pl.when(last)
    def _():
        tab = rope_ref[T - WINDOW:T, :]
        k0 = _rope(p_scr[T - WINDOW:T, K_OFF:K_OFF + LANES], tab)
        k1 = _rope(p_scr[T - WINDOW:T, K_OFF + LANES:K_OFF + 2 * LANES], tab)
        ko_ref[...] = jnp.where(lo_sq, k0, k1)
        vo_ref[...] = jnp.where(lo_sq, p_scr[T - WINDOW:T, V_OFF:V_OFF + LANES],
                                p_scr[T - WINDOW:T, V_OFF + LANES:V_OFF + 2 * LANES])
        for j in range(N_LRU_TILES):
            lco_ref[:, j * LANES:(j + 1) * LANES] = lx_scr[j, P + T - (CONV_W - 1):P + T, :]
        lho_ref[...] = hl_scr[...]
        for j in range(N_CONV_TILES):
            sco_ref[:, j * LANES:(j + 1) * LANES] = sx_scr[j, P + T - (CONV_W - 1):P + T, :]
        for g in range(SSD_GROUPS):
            for pp in range(TILES_PER_GROUP):
                tile = ht_scr[g, :, pp * LANES:(pp + 1) * LANES].T
                h = 2 * (g * TILES_PER_GROUP + pp)
                sho_ref[h] = tile[0:SSD_HEAD_DIM, :]
                sho_ref[h + 1] = tile[SSD_HEAD_DIM:2 * SSD_HEAD_DIM, :]


def _const_spec(shape, layer):
    nd = len(shape)
    return pl.BlockSpec((None,) + tuple(shape), lambda *_: (layer,) + (0,) * nd,
                        pipeline_mode=pl.Buffered(1))


def _prompt_layer(layer, x, rope_tab, sinks, prm):
    B, L, _ = x.shape
    T = PROMPT_TILE
    nc = L // T
    win, aux, wg, wout, lcw, lvec, scw, scb, hvec, svec, ln = prm
    tile = lambda b, c: (b, c, 0)
    per_b = lambda b, c: (b, 0, 0)

    in_specs = [
        pl.BlockSpec(memory_space=pltpu.SMEM),
        pl.BlockSpec((None, T, D_MODEL), tile),
        pl.BlockSpec((T, 3 * LANES), lambda b, c: (c, 0)),
        _const_spec((D_MODEL, W_IN_COLS), layer),
        _const_spec((D_MODEL, AUX_COLS), layer),
        _const_spec((LRU_WIDTH, 2 * LRU_WIDTH), layer),
        _const_spec((D_MIX, D_MODEL), layer),
        _const_spec((CONV_W, LRU_WIDTH), layer),
        _const_spec((4, LRU_WIDTH), layer),
        _const_spec((CONV_W, SSD_CONV_CH), layer),
        _const_spec((1, SSD_CONV_CH), layer),
        _const_spec((2, LANES), layer),
        _const_spec((2, SSD_WIDTH), layer),
        _const_spec((2, D_MODEL), layer),
    ]
    out_shape = (
        jax.ShapeDtypeStruct((B, L, D_MODEL), F32),
        jax.ShapeDtypeStruct((B, WINDOW, ATT_KV_WIDTH), F32),
        jax.ShapeDtypeStruct((B, WINDOW, ATT_KV_WIDTH), F32),
        jax.ShapeDtypeStruct((B, CONV_W - 1, LRU_WIDTH), F32),
        jax.ShapeDtypeStruct((B, 1, LRU_WIDTH), F32),
        jax.ShapeDtypeStruct((B, CONV_W - 1, SSD_CONV_CH), F32),
        jax.ShapeDtypeStruct((B, SSD_HEADS, SSD_HEAD_DIM, SSD_STATE), F32),
    )
    out_specs = (
        pl.BlockSpec((None, T, D_MODEL), tile),
        pl.BlockSpec((None, WINDOW, ATT_KV_WIDTH), per_b),
        pl.BlockSpec((None, WINDOW, ATT_KV_WIDTH), per_b),
        pl.BlockSpec((None, CONV_W - 1, LRU_WIDTH), per_b),
        pl.BlockSpec((None, 1, LRU_WIDTH), per_b),
        pl.BlockSpec((None, CONV_W - 1, SSD_CONV_CH), per_b),
        pl.BlockSpec((None, SSD_HEADS, SSD_HEAD_DIM, SSD_STATE), lambda b, c: (b, 0, 0, 0)),
    )
    scratch = [
        pltpu.VMEM((T, N_COLS), F32),
        pltpu.VMEM((ATT_KV_HEADS, WINDOW + T, LANES), BF16),
        pltpu.VMEM((ATT_KV_HEADS, WINDOW + T, LANES), BF16),
        pltpu.VMEM((N_LRU_TILES, SUBLANES + T, LANES), F32),
        pltpu.VMEM((N_CONV_TILES, SUBLANES + T, LANES), F32),
        pltpu.VMEM((N_LRU_TILES, T, LANES), F32),
        pltpu.VMEM((N_LRU_TILES, T, LANES), F32),
        pltpu.VMEM((1, LRU_WIDTH), F32),
        pltpu.VMEM((SSD_GROUPS, SSD_STATE, GROUP_W), F32),
        pltpu.VMEM((T, D_MIX), BF16),
        pltpu.VMEM((T, LRU_WIDTH), BF16),
        pltpu.VMEM((N_CONV_TILES, T, LANES), F32),
    ]
    return pl.pallas_call(
        _prompt_kernel,
        grid=(B, nc),
        in_specs=in_specs,
        out_specs=out_specs,
        out_shape=out_shape,
        scratch_shapes=scratch,
        compiler_params=pltpu.CompilerParams(
            dimension_semantics=("arbitrary", "arbitrary"),
            vmem_limit_bytes=VMEM_LIMIT_BYTES),
        name=f"prompt_layer{layer}",
    )(sinks, x, rope_tab, win, aux, wg, wout, lcw, lvec, scw, scb, hvec, svec, ln)


def _conv_step(x_new, st_ref, sto_ref, w_ref, bias):
    y = bias + w_ref[CONV_W - 1:CONV_W, :] * x_new
    for t in range(CONV_W - 1):
        y = y + w_ref[t:t + 1, :] * st_ref[t]
    for t in range(CONV_W - 2):
        sto_ref[t] = st_ref[t + 1]
    sto_ref[CONV_W - 2] = x_new
    return y


def _sample_kernel(n_aliased, sinks_ref, x_ref, rope_ref, win_ref, aux_ref, wg_ref, wout_ref, lcw_ref, lvec_ref,
                   scw_ref, scb_ref, hvec_ref, svec_ref, ln_ref,
                   ck_ref, cv_ref, lc_ref, lh_ref, sc_ref, sh_ref, *refs):
    (y_ref, cko_ref, cvo_ref, lco_ref, lho_ref, sco_ref, sho_ref,
     p_scr, q_scr, o_scr, kn_scr, vn_scr, xdt_scr, da_scr, y_scr, mix_scr) = refs[n_aliased:]
    NB = x_ref.shape[0]
    BT = SAMPLE_BT
    i = pl.program_id(0)
    lo_f = _lane_lo((NB, LANES))

    @pl.when(i == 0)
    def _():
        _project(x_ref[...], win_ref, aux_ref, p_scr)
        tab = rope_ref[...]
        for cc in range(ATT_HEADS // 2):
            g = cc // (ATT_HEADS // ATT_KV_HEADS // 2)
            qp = _rope(p_scr[:, Q_OFF + cc * LANES:Q_OFF + (cc + 1) * LANES], tab) * ATT_SCALE
            for half in range(2):
                t = jnp.where(lo_f if half == 0 else jnp.logical_not(lo_f), qp, 0.0)
                if half != g:
                    t = pltpu.roll(t, HALF, 1)
                q_scr[pl.ds(2 * cc + half, NB, stride=ATT_HEADS), :] = t
        k0 = _rope(p_scr[:, K_OFF:K_OFF + LANES], tab)
        k1 = _rope(p_scr[:, K_OFF + LANES:K_OFF + 2 * LANES], tab)
        kn_scr[...] = jnp.where(lo_f, k0, k1)
        vn_scr[...] = jnp.where(lo_f, p_scr[:, V_OFF:V_OFF + LANES], p_scr[:, V_OFF + LANES:V_OFF + 2 * LANES])
        xin = p_scr[:, XL_OFF:XL_OFF + LRU_WIDTH]
        xl = _conv_step(xin, lc_ref, lco_ref, lcw_ref, lvec_ref[0:1, :])
        gates = _dot(xl.astype(BF16), wg_ref[...])
        a, bt = _lru_coeffs(xl, gates, lvec_ref)
        h1 = a * lh_ref[...] + bt
        lho_ref[...] = h1
        mix_scr[:, MIX_LRU:MIX_LRU + LRU_WIDTH] = (h1 * _silu(p_scr[:, GL_OFF:GL_OFF + LRU_WIDTH])).astype(BF16)
        xin2 = p_scr[:, XBC_OFF:XBC_OFF + SSD_CONV_CH]
        xc = _conv_step(xin2, sc_ref, sco_ref, scw_ref, scb_ref[0:1, :])
        p_scr[:, XBC_OFF:XBC_OFF + SSD_CONV_CH] = _silu(xc)
        dt = _softplus(p_scr[:, DT_OFF:DT_OFF + LANES] + hvec_ref[0:1, :])
        da_scr[...] = jnp.exp(dt * (-jnp.exp(hvec_ref[1:2, :])))
        for p in range(N_SSD_TILES):
            dte = jnp.where(lo_f, jnp.broadcast_to(dt[:, 2 * p:2 * p + 1], (NB, LANES)),
                            jnp.broadcast_to(dt[:, 2 * p + 1:2 * p + 2], (NB, LANES)))
            xdt_scr[:, p * LANES:(p + 1) * LANES] = p_scr[:, XBC_OFF + p * LANES:XBC_OFF + (p + 1) * LANES] * dte

    r8 = pl.multiple_of(i * BT, BT)
    kn_blk = kn_scr[pl.ds(r8, BT), :]
    vn_blk = vn_scr[pl.ds(r8, BT), :]
    zpad = jnp.zeros((LANES - BT, LANES), F32)
    kn_t = jnp.concatenate([kn_blk, zpad], axis=0).T
    vn_t = jnp.concatenate([vn_blk, zpad], axis=0).T
    newest = lax.broadcasted_iota(jnp.int32, (ATT_KV_WIDTH, WINDOW), 1) == WINDOW - 1
    da_t =jnp.concatenate([da_scr[pl.ds(r8, BT), :], zpad], axis=0).T
    x_t = [jnp.concatenate([xdt_scr[pl.ds(r8, BT), p * LANES:(p + 1) * LANES], zpad], axis=0).T
           for p in range(N_SSD_TILES)]
    b_blk = p_scr[pl.ds(r8, BT), B_OFF:B_OFF + SSD_GROUPS * SSD_STATE]
    c_blk = p_scr[pl.ds(r8, BT), C_OFF:C_OFF + SSD_GROUPS * SSD_STATE].astype(BF16)
    row8 = lax.broadcasted_iota(jnp.int32, (ATT_HEADS, 1), 0)
    sink = jnp.zeros((ATT_HEADS, 1), F32)
    for h in range(ATT_HEADS):
        sink = jnp.where(row8 == h, sinks_ref[h], sink)
    rowb = lax.broadcasted_iota(jnp.int32, (BT, GROUP_W), 0)
    y_acc = [jnp.zeros((BT, GROUP_W), F32) for _ in range(SSD_GROUPS)]
    rowk = lax.broadcasted_iota(jnp.int32, (BT, SSD_STATE), 0)
    outer = []
    for g in range(SSD_GROUPS):
        bg = b_blk[:, g * SSD_STATE:(g + 1) * SSD_STATE]
        diag = jnp.concatenate([jnp.where(rowk == bb, bg, 0.0) for bb in range(BT)], axis=1)
        rhs = jnp.concatenate([diag, jnp.zeros((LANES - BT, BT * SSD_STATE), F32)], axis=0).astype(BF16)
        outer.append([_dot(x_t[g * TILES_PER_GROUP + pp].astype(BF16), rhs) for pp in range(TILES_PER_GROUP)])
    for bb in range(BT):
        qr = q_scr[pl.ds(pl.multiple_of((r8 + bb) * ATT_HEADS, ATT_HEADS), ATT_HEADS), :]
        kt = ck_ref[bb]
        vt = cv_ref[bb]
        s = _dot(qr.astype(BF16), kt.astype(BF16))
        s_new = jnp.sum(qr * kn_blk[bb:bb + 1, :], -1, keepdims=True)
        m = jnp.maximum(jnp.maximum(jnp.max(s, -1, keepdims=True), s_new), sink)
        e = jnp.exp(s - m)
        e_new = jnp.exp(s_new - m)
        den = jnp.sum(e, -1, keepdims=True) + e_new + jnp.exp(sink - m)
        o = _dot_nt(e.astype(BF16), vt.astype(BF16)) + e_new * vn_blk[bb:bb + 1, :]
        o_scr[pl.ds(pl.multiple_of((r8 + bb) * ATT_HEADS, ATT_HEADS), ATT_HEADS), :] = o * (1.0 / den)
        cko_ref[bb] = jnp.where(newest, kn_t[:, bb:bb + 1], pltpu.roll(kt, WINDOW - 1, 1))
        cvo_ref[bb] = jnp.where(newest, vn_t[:, bb:bb + 1], pltpu.roll(vt, WINDOW - 1, 1))
        for g in range(SSD_GROUPS):
            tiles = []
            for pp in range(TILES_PER_GROUP):
                p = g * TILES_PER_GROUP + pp
                for hh in range(2):
                    h = 2 * p + hh
                    dab = jnp.broadcast_to(da_t[h:h + 1, bb:bb + 1], (SSD_HEAD_DIM, SSD_STATE))
                    h1 = sh_ref[bb, h] * dab + outer[g][pp][hh * SSD_HEAD_DIM:(hh + 1) * SSD_HEAD_DIM,
                                                               bb * SSD_STATE:(bb + 1) * SSD_STATE]
                    sho_ref[bb, h] = h1
                    tiles.append(h1.astype(BF16))
            res = _dot_nt(c_blk[:, g * SSD_STATE:(g + 1) * SSD_STATE], jnp.concatenate(tiles, axis=0))
            y_acc[g] = jnp.where(rowb == bb, res, y_acc[g])
    for g in range(SSD_GROUPS):
        y_scr[pl.ds(r8, BT), g * GROUP_W:(g + 1) * GROUP_W] = y_acc[g]

    @pl.when(i == pl.num_programs(0) - 1)
    def _():
        for cc in range(ATT_HEADS // 2):
            g = cc // (ATT_HEADS // ATT_KV_HEADS // 2)
            oe = o_scr[pl.ds(2 * cc, NB, stride=ATT_HEADS), :]
            oo = o_scr[pl.ds(2 * cc + 1, NB, stride=ATT_HEADS), :]
            if g == 0:
                oo = pltpu.roll(oo, HALF, 1)
            else:
                oe = pltpu.roll(oe, HALF, 1)
            gate = p_scr[:, GA_OFF + cc * LANES:GA_OFF + (cc + 1) * LANES]
            mix_scr[:, cc * LANES:(cc + 1) * LANES] = (jnp.where(lo_f, oe, oo) * _silu(gate)).astype(BF16)
        y = (y_scr[...] + svec_ref[0:1, :] * p_scr[:, XBC_OFF:XBC_OFF + SSD_WIDTH]) * _silu(p_scr[:, Z_OFF:Z_OFF + SSD_WIDTH])
        y = y * lax.rsqrt(jnp.mean(y * y, -1, keepdims=True) + NORM_EPS) * svec_ref[1:2, :]
        mix_scr[:, MIX_SSD:MIX_SSD + SSD_WIDTH] = y.astype(BF16)
        out = _dot(mix_scr[...], wout_ref[...])
        y_ref[...] = _layer_norm(DEEPNORM_ALPHA * x_ref[...] + out, ln_ref[0:1, :], ln_ref[1:2, :])


def _sample_layer(layer, x, rope_tab, sinks, prm, ck, cv, lc, lh, sc, sh, prev):
    NB = x.shape[0]
    BT = SAMPLE_BT
    H = CONV_W - 1
    win, aux, wg, wout, lcw, lvec, scw, scb, hvec, svec, ln = prm
    whole = lambda shape: pl.BlockSpec(shape, lambda i: (0,) * len(shape), pipeline_mode=pl.Buffered(1))
    lwhole = lambda shape: pl.BlockSpec((None,) + shape, lambda i: (layer,) + (0,) * len(shape),
                                        pipeline_mode=pl.Buffered(1))
    rows = lambda shape: pl.BlockSpec((None, BT) + shape, lambda i: (layer, i) + (0,) * len(shape))
    state_specs = [
        rows((ATT_KV_WIDTH, WINDOW)),
        rows((ATT_KV_WIDTH, WINDOW)),
        lwhole((H, NB, LRU_WIDTH)),
        lwhole((NB, LRU_WIDTH)),
        lwhole((H, NB, SSD_CONV_CH)),
        rows((SSD_HEADS, SSD_HEAD_DIM, SSD_STATE)),
    ]
    in_specs = [
        pl.BlockSpec(memory_space=pltpu.SMEM),
        whole((NB, D_MODEL)),
        whole((1, 3 * LANES)),
        _const_spec((D_MODEL, W_IN_COLS), layer),
        _const_spec((D_MODEL, AUX_COLS), layer),
        _const_spec((LRU_WIDTH, 2 * LRU_WIDTH), layer),
        _const_spec((D_MIX, D_MODEL), layer),
        _const_spec((CONV_W, LRU_WIDTH), layer),
        _const_spec((4, LRU_WIDTH), layer),
        _const_spec((CONV_W, SSD_CONV_CH), layer),
        _const_spec((1, SSD_CONV_CH), layer),
        _const_spec((2, LANES), layer),
        _const_spec((2, SSD_WIDTH), layer),
        _const_spec((2, D_MODEL), layer),
    ] + state_specs
    operands = [sinks, x, rope_tab, win, aux, wg, wout, lcw, lvec, scw, scb, hvec, svec, ln, ck, cv, lc, lh, sc, sh]
    aliases = {}
    if prev is not None:
        for k, arr in enumerate(prev):
            aliases[len(operands)] = 1 + k
            operands.append(arr)
            in_specs.append(pl.BlockSpec(memory_space=pl.ANY))
    out_shape = (jax.ShapeDtypeStruct((NB, D_MODEL), F32),) + tuple(
        jax.ShapeDtypeStruct(a.shape, F32) for a in (ck, cv, lc, lh, sc, sh))
    full = lambda shape: pl.BlockSpec(shape, lambda i: (0,) * len(shape))
    lfull = lambda shape: pl.BlockSpec((None,) + shape, lambda i: (layer,) + (0,) * len(shape))
    out_specs = (
        full((NB, D_MODEL)),
        rows((ATT_KV_WIDTH, WINDOW)),
        rows((ATT_KV_WIDTH, WINDOW)),
        lfull((H, NB, LRU_WIDTH)),
        lfull((NB, LRU_WIDTH)),
        lfull((H, NB, SSD_CONV_CH)),
        rows((SSD_HEADS, SSD_HEAD_DIM, SSD_STATE)),
    )
    scratch = [
        pltpu.VMEM((NB, N_COLS), F32),
        pltpu.VMEM((NB * ATT_HEADS, LANES), F32),
        pltpu.VMEM((NB * ATT_HEADS, LANES), F32),
        pltpu.VMEM((NB, ATT_KV_WIDTH), F32),
        pltpu.VMEM((NB, ATT_KV_WIDTH), F32),
        pltpu.VMEM((NB, SSD_WIDTH), F32),
        pltpu.VMEM((NB, LANES), F32),
        pltpu.VMEM((NB, SSD_WIDTH), F32),
        pltpu.VMEM((NB, D_MIX), BF16),
    ]
    return pl.pallas_call(
        functools.partial(_sample_kernel, len(aliases)),
        grid=(NB // BT,),
        in_specs=in_specs,
        out_specs=out_specs,
        out_shape=out_shape,
        scratch_shapes=scratch,
        input_output_aliases=aliases,
        compiler_params=pltpu.CompilerParams(
            dimension_semantics=("arbitrary",),
            vmem_limit_bytes=VMEM_LIMIT_BYTES),
        name=f"sample_layer{layer}",
    )(*operands)


def _rope_table(pos):
    half = ROPE_DIM // 2
    inv = ROPE_THETA ** (-jnp.arange(half, dtype=F32) / half)
    ang = pos.astype(F32)[:, None] * inv[None, :]
    cos, sin = jnp.cos(ang), jnp.sin(ang)
    d = jnp.arange(LANES) % ATT_HEAD_DIM
    fi = d % half
    c = jnp.where(d < ROPE_DIM, cos[:, fi], 1.0)
    s1 = jnp.where(d < half, -sin[:, fi], 0.0)
    s2 = jnp.where((d >= half) & (d < ROPE_DIM), sin[:, fi], 0.0)
    return jnp.concatenate([c, s1, s2], axis=1)


def _prep_params(w_in, w_out, lru_conv_w, lru_conv_b, lru_wa, lru_ba, lru_wx, lru_bx, lru_lambda,
                 ssd_conv_w, ssd_conv_b, ssd_dt_bias, ssd_a_log, ssd_d, ssd_norm_g, ln_g, ln_b):
    hd = ATT_HEAD_DIM
    win = w_in.astype(BF16)
    k = w_in[..., ATT_WIDTH:ATT_WIDTH + ATT_KV_WIDTH]
    v = w_in[..., ATT_WIDTH + ATT_KV_WIDTH:ATT_WIDTH + 2 * ATT_KV_WIDTH]
    dup = lambda t: jnp.concatenate([t[..., :hd], t[..., :hd], t[..., hd:], t[..., hd:]], -1)
    dtw = jnp.pad(w_in[..., W_IN_COLS - SSD_HEADS:], ((0, 0), (0, 0), (0, LANES - SSD_HEADS)))
    aux = jnp.concatenate([dup(k), dup(v), dtw], -1).astype(BF16)

    def dense(w):
        rows = [jnp.pad(w[:, n], ((0, 0), (0, 0), (n * LRU_BLOCK, LRU_WIDTH - (n + 1) * LRU_BLOCK)))
                for n in range(LRU_BLOCKS)]
        return jnp.concatenate(rows, 1)

    wg = jnp.concatenate([dense(lru_wa.astype(BF16)), dense(lru_wx.astype(BF16))], -1)
    wout = w_out.astype(BF16)
    lvec = jnp.stack([lru_conv_b, lru_ba, lru_bx, lru_lambda], 1)
    hpad = lambda t: jnp.pad(t, ((0, 0), (0, LANES - SSD_HEADS)))
    hvec = jnp.stack([hpad(ssd_dt_bias), hpad(ssd_a_log)], 1)
    svec = jnp.stack([jnp.repeat(ssd_d, SSD_HEAD_DIM, axis=1), ssd_norm_g], 1)
    ln = jnp.stack([ln_g, ln_b], 1)
    return (win, aux, wg, wout, lru_conv_w, lvec, ssd_conv_w, ssd_conv_b[:, None, :], hvec, svec, ln)


def kernel(x_prompt, x_sample, cache_swa_k, cache_swa_v, state_lru_conv, state_lru_h, state_ssd_conv, state_ssd_h, w_in, w_out, att_sinks, lru_conv_w, lru_conv_b, lru_wa, lru_ba, lru_wx, lru_bx, lru_lambda, ssd_conv_w, ssd_conv_b, ssd_dt_bias, ssd_a_log, ssd_d, ssd_norm_g, ln_g, ln_b):
    prm = _prep_params(w_in, w_out, lru_conv_w, lru_conv_b, lru_wa, lru_ba, lru_wx, lru_bx, lru_lambda,
                       ssd_conv_w, ssd_conv_b, ssd_dt_bias, ssd_a_log, ssd_d, ssd_norm_g, ln_g, ln_b)
    bp, lp, _ = x_prompt.shape
    rope_p = _rope_table(jnp.arange(lp, dtype=jnp.int32))
    xp = x_prompt
    new_p = [[] for _ in range(6)]
    for l in range(DEPTH):
        xp, ko, vo, lco, lho, sco, sho = _prompt_layer(l, xp, rope_p, att_sinks[l], prm)
        st = (ko.reshape(bp, WINDOW, ATT_KV_HEADS, ATT_HEAD_DIM), vo.reshape(bp, WINDOW, ATT_KV_HEADS, ATT_HEAD_DIM),
              lco, lho.reshape(bp, LRU_WIDTH), sco, sho)
        for lst, t in zip(new_p, st):
            lst.append(t)
    outs_p = [jnp.stack(t) for t in new_p]

    nb = x_sample.shape[0]
    rope_s = _rope_table(PAST_LEN + jnp.arange(x_sample.shape[1], dtype=jnp.int32))
    to_kt = lambda t: jnp.transpose(t, (0, 1, 3, 4, 2)).reshape(DEPTH, nb, ATT_KV_WIDTH, WINDOW)
    from_kt = lambda t: jnp.transpose(t.reshape(DEPTH, nb, ATT_KV_HEADS, ATT_HEAD_DIM, WINDOW), (0, 1, 4, 2, 3))
    swap = lambda t: jnp.transpose(t, (0, 2, 1, 3))
    state = (to_kt(cache_swa_k), to_kt(cache_swa_v), swap(state_lru_conv), state_lru_h, swap(state_ssd_conv),
             state_ssd_h)
    xs = x_sample.reshape(nb, D_MODEL)
    new = None
    for l in range(DEPTH):
        xs, *new = _sample_layer(l, xs, rope_s, att_sinks[l], prm, *state, new)
    cko, cvo, lco, lho, sco, sho = new
    outs_s = (from_kt(cko), from_kt(cvo), swap(lco), lho, swap(sco), sho)
    return (xp, xs.reshape(x_sample.shape)) + tuple(outs_p) + tuple(outs_s)
```

```python
import functools
import math

import jax
import jax.numpy as jnp
from jax import lax
from jax.experimental import pallas as pl
from jax.experimental.pallas import tpu as pltpu

F32 = jnp.float32
BF16 = jnp.bfloat16

D_MODEL = 1024
DEPTH = 4
PAST_LEN = 8192
D_MIX = 2 * D_MODEL
ATT_HEADS = 8
ATT_KV_HEADS = 2
ATT_HEAD_DIM = 64
ATT_WIDTH = ATT_HEADS * ATT_HEAD_DIM
ATT_KV_WIDTH = ATT_KV_HEADS * ATT_HEAD_DIM
ATT_SCALE = ATT_HEAD_DIM ** -0.5
WINDOW = 128
ROPE_THETA = 500000.0
ROPE_DIM = ATT_HEAD_DIM // 4
LRU_WIDTH = 3 * D_MIX // 8
LRU_BLOCKS = 8
LRU_BLOCK = LRU_WIDTH // LRU_BLOCKS
LRU_C = 8.0
CONV_W = 4
SSD_WIDTH = D_MIX - ATT_WIDTH - LRU_WIDTH
SSD_HEAD_DIM = 64
SSD_HEADS = SSD_WIDTH // SSD_HEAD_DIM
SSD_GROUPS = 2
SSD_STATE = 128
SSD_CONV_CH = SSD_WIDTH + 2 * SSD_GROUPS * SSD_STATE
DEEPNORM_ALPHA = (2.0 * DEPTH) ** 0.25
NORM_EPS = 1e-5

LANES = 128
SUBLANES = 8
HALF = LANES // 2
VMEM_LIMIT_BYTES = 62 * 1024 * 1024

Q_OFF = 0
K_OFF = Q_OFF + ATT_WIDTH
V_OFF = K_OFF + 2 * ATT_KV_WIDTH
GA_OFF = V_OFF + 2 * ATT_KV_WIDTH
XL_OFF = GA_OFF + ATT_WIDTH
GL_OFF = XL_OFF + LRU_WIDTH
Z_OFF = GL_OFF + LRU_WIDTH
XBC_OFF = Z_OFF + SSD_WIDTH
DT_OFF = XBC_OFF + SSD_CONV_CH
N_COLS = DT_OFF + LANES
PGL_OFF = XL_OFF
PZ_OFF = PGL_OFF + LRU_WIDTH
PDT_OFF = PZ_OFF + SSD_WIDTH
P_COLS = PDT_OFF + LANES
W_IN_COLS = 2 * ATT_WIDTH + 2 * ATT_KV_WIDTH + 2 * LRU_WIDTH + SSD_WIDTH + SSD_CONV_CH + SSD_HEADS
AUX_COLS = 4 * ATT_KV_WIDTH + LANES
B_OFF = XBC_OFF + SSD_WIDTH
C_OFF = B_OFF + SSD_GROUPS * SSD_STATE
MIX_LRU = ATT_WIDTH
MIX_SSD = ATT_WIDTH + LRU_WIDTH

N_LRU_TILES = LRU_WIDTH // LANES
N_SSD_TILES = SSD_WIDTH // LANES
N_CONV_TILES = SSD_CONV_CH // LANES
CONV_STRIDE = 4
TILES_PER_GROUP = N_SSD_TILES // SSD_GROUPS
GROUP_W = SSD_WIDTH // SSD_GROUPS

NEG_BIG = -1e30

PROMPT_TILE = 512
SAMPLE_BT = 8


def _sigmoid(x):
    return 0.5 * jnp.tanh(0.5 * x) + 0.5


def _silu(x):
    h = 0.5 * x
    return h + h * jnp.tanh(h)


def _softplus(x):
    return jnp.maximum(x, 0.0) + jnp.log1p(jnp.exp(-jnp.abs(x)))


def _lane_lo(shape):
    return (lax.broadcasted_iota(jnp.int32, shape, len(shape) - 1) % LANES) < HALF


def _rope(t, tab):
    half = ROPE_DIM // 2
    return (t * tab[:, 0:LANES]
            + pltpu.roll(t, LANES - half, 1) * tab[:, LANES:2 * LANES]
            + pltpu.roll(t, half, 1) * tab[:, 2 * LANES:3 * LANES])


def _dot(a, b):
    return jnp.dot(a, b, preferred_element_type=F32)


def _dot_nt(a, b):
    return lax.dot_general(a, b, (((1,), (1,)), ((), ())), preferred_element_type=F32)


def _conv_classes(src_ref, j, rows, w_ref, bias, cs):
    n = CONV_STRIDE
    hist = CONV_W - 1
    taps = {s_: src_ref[j, pl.ds(SUBLANES + s_, rows // n, stride=n), :] for s_ in range(-hist, n)}
    outs = []
    for k in range(n):
        acc = bias + w_ref[hist:hist + 1, cs] * taps[k]
        for t in range(hist):
            acc = acc + w_ref[t:t + 1, cs] * taps[k - hist + t]
        outs.append(acc)
    return outs


def _layer_norm(v, g, b):
    mu = jnp.mean(v, -1, keepdims=True)
    d = v - mu
    var = jnp.mean(d * d, -1, keepdims=True)
    return d * lax.rsqrt(var + NORM_EPS) * g + b


def _lru_coeffs(xl, gates, lvec_ref):
    r = _sigmoid(gates[:, 0:LRU_WIDTH] + lvec_ref[1:2, :])
    ig = _sigmoid(gates[:, LRU_WIDTH:2 * LRU_WIDTH] + lvec_ref[2:3, :])
    log_a = (-LRU_C * _softplus(-lvec_ref[3:4, :])) * r
    a = jnp.exp(log_a)
    bt = jnp.sqrt(-jnp.tanh(log_a) * (1.0 + a * a)) * (ig * xl)
    return a, bt


def _project(x_tile, win_ref, aux_ref, dst, lru_in=None, ssd_in=None):
    assert (lru_in is None) == (ssd_in is None)
    rows = x_tile.shape[0]
    xb = x_tile.astype(BF16)
    kv0 = ATT_WIDTH + 2 * ATT_KV_WIDTH
    dst[:, Q_OFF:K_OFF] = _dot(xb, win_ref[:, 0:ATT_WIDTH])
    dst[:, K_OFF:GA_OFF] = _dot(xb, aux_ref[:, 0:GA_OFF - K_OFF])
    res = _dot(xb, win_ref[:, kv0:kv0 + Z_OFF - GA_OFF])
    if lru_in is None:
        dst[:, GA_OFF:Z_OFF] = res
    else:
        dst[:, GA_OFF:XL_OFF] = res[:, 0:XL_OFF - GA_OFF]
        for j in range(N_LRU_TILES):
            lru_in[j, SUBLANES:SUBLANES + rows, :] = res[:, XL_OFF - GA_OFF + j * LANES:XL_OFF - GA_OFF + (j + 1) * LANES]
        dst[:, PGL_OFF:PZ_OFF] = res[:, GL_OFF - GA_OFF:Z_OFF - GA_OFF]
    res = _dot(xb, win_ref[:, kv0 + Z_OFF - GA_OFF:kv0 + DT_OFF - GA_OFF])
    if ssd_in is None:
        dst[:, Z_OFF:DT_OFF] = res
    else:
        dst[:, PZ_OFF:PDT_OFF] = res[:, 0:XBC_OFF - Z_OFF]
        for j in range(N_CONV_TILES):
            ssd_in[j, SUBLANES:SUBLANES + rows, :] = res[:, XBC_OFF - Z_OFF + j * LANES:XBC_OFF - Z_OFF + (j + 1) * LANES]
    dt_lo = DT_OFF if ssd_in is None else PDT_OFF
    dst[:, dt_lo:dt_lo + LANES] = _dot(xb, aux_ref[:, GA_OFF - K_OFF:GA_OFF - K_OFF + LANES])


def _prompt_kernel(sinks_ref, x_ref, rope_ref, win_ref, aux_ref, wg_ref, wout_ref, lcw_ref, lvec_ref,
                   scw_ref, scb_ref, hvec_ref, svec_ref, ln_ref,
                   y_ref, ko_ref, vo_ref, lco_ref, lho_ref, sco_ref, sho_ref,
                   p_scr, kd_scr, vd_scr, lx_scr, sx_scr, sa_scr, sb_scr, hl_scr, ht_scr, mix_scr, xlb_scr):
    T = PROMPT_TILE
    nsub = T // WINDOW
    c = pl.program_id(1)
    last = c == pl.num_programs(1) - 1

    @pl.when(c == 0)
    def _():
        kd_scr[:, 0:WINDOW, :] = jnp.zeros((2, WINDOW, LANES), BF16)
        vd_scr[:, 0:WINDOW, :] = jnp.zeros((2, WINDOW, LANES), BF16)
        lx_scr[:, 0:SUBLANES, :] = jnp.zeros((N_LRU_TILES, SUBLANES, LANES), F32)
        sx_scr[:, 0:SUBLANES, :] = jnp.zeros((N_CONV_TILES, SUBLANES, LANES), F32)
        hl_scr[...] = jnp.zeros_like(hl_scr)
        ht_scr[...] = jnp.zeros_like(ht_scr)

    _project(x_ref[...], win_ref, aux_ref, p_scr, lx_scr, sx_scr)

    lo_sq = _lane_lo((WINDOW, LANES))

    for g in range(ATT_KV_HEADS):
        kd = _rope(p_scr[:, K_OFF + g * LANES:K_OFF + (g + 1) * LANES], rope_ref[...])
        kd_scr[g, WINDOW:WINDOW + T, :] = kd.astype(BF16)
        vd_scr[g, WINDOW:WINDOW + T, :] = p_scr[:, V_OFF + g * LANES:V_OFF + (g + 1) * LANES].astype(BF16)

    row = lax.broadcasted_iota(jnp.int32, (WINDOW, 2 * WINDOW), 0)
    col = lax.broadcasted_iota(jnp.int32, (WINDOW, 2 * WINDOW), 1)
    band = (col >= row) & (col <= row + WINDOW)
    first_lo = jnp.where(c > 0, 0, WINDOW)
    for i in range(nsub):
        r0 = i * WINDOW
        msk = (band & (col >= first_lo)) if i == 0 else band
        tab = rope_ref[r0:r0 + WINDOW, :]
        for cc in range(ATT_HEADS // 2):
            g = cc // (ATT_HEADS // ATT_KV_HEADS // 2)
            kg = kd_scr[g, r0:r0 + 2 * WINDOW, :]
            vg = vd_scr[g, r0:r0 + 2 * WINDOW, :]
            qp = _rope(p_scr[r0:r0 + WINDOW, Q_OFF + cc * LANES:Q_OFF + (cc + 1) * LANES], tab) * ATT_SCALE
            outs = []
            for half in range(2):
                sink = sinks_ref[2 * cc + half]
                qm = jnp.where(lo_sq if half == 0 else jnp.logical_not(lo_sq), qp, 0.0).astype(BF16)
                s = jnp.where(msk, _dot_nt(qm, kg), NEG_BIG)
                m = jnp.maximum(jnp.max(s, -1, keepdims=True), sink)
                e = jnp.exp(s - m)
                den = jnp.sum(e, -1, keepdims=True) + jnp.exp(sink - m)
                outs.append(_dot(e.astype(BF16), vg) * (1.0 / den))
            att = jnp.where(lo_sq, outs[0], outs[1])
            gate = p_scr[r0:r0 + WINDOW, GA_OFF + cc * LANES:GA_OFF + (cc + 1) * LANES]
            mix_scr[r0:r0 + WINDOW, cc * LANES:(cc + 1) * LANES] = (att * _silu(gate)).astype(BF16)
    for g in range(ATT_KV_HEADS):
        kd_scr[g, 0:WINDOW, :] = kd_scr[g, T:T + WINDOW, :]
        vd_scr[g, 0:WINDOW, :] = vd_scr[g, T:T + WINDOW, :]

    P = SUBLANES
    H = CONV_W - 1
    for j in range(N_LRU_TILES):
        cs_ = slice(j * LANES, (j + 1) * LANES)
        for k, xl in enumerate(_conv_classes(lx_scr, j, T, lcw_ref, lvec_ref[0:1, cs_], cs_)):
            sb_scr[j, pl.ds(k, T // CONV_STRIDE, stride=CONV_STRIDE), :] = xl
        xlb_scr[:, cs_] = sb_scr[j].astype(BF16)
    lx_scr[:, 0:P, :] = lx_scr[:, T:T + P, :]

    G = T // SUBLANES
    row_g = lax.broadcasted_iota(jnp.int32, (G, LANES), 0)
    cl = -LRU_C * _softplus(-lvec_ref[3:4, :])
    for jj in range(N_LRU_TILES // 2):
        c0 = jj * 2 * LANES
        k0 = (c0 // LRU_BLOCK) * LRU_BLOCK // LANES * LANES
        k1 = -(-(-(-(c0 + 2 * LANES) // LRU_BLOCK) * LRU_BLOCK) // LANES) * LANES
        xk = xlb_scr[:, k0:k1]
        gr = _dot(xk, wg_ref[k0:k1, c0:c0 + 2 * LANES])
        gi = _dot(xk, wg_ref[k0:k1, LRU_WIDTH + c0:LRU_WIDTH + c0 + 2 * LANES])
        for u in range(2):
            j = 2 * jj + u
            cs_ = slice(j * LANES, (j + 1) * LANES)
            us = slice(u * LANES, (u + 1) * LANES)
            r = _sigmoid(gr[:, us] + lvec_ref[1:2, cs_])
            ig = _sigmoid(gi[:, us] + lvec_ref[2:3, cs_])
            log_a = cl[:, cs_] * r
            a = jnp.exp(log_a)
            sa_scr[j] = a
            sb_scr[j] = jnp.sqrt(-jnp.tanh(log_a) * (1.0 + a * a)) * (ig * sb_scr[j])
    for j in range(N_LRU_TILES):
        ca = sa_scr[j, pl.ds(0, G, stride=SUBLANES), :]
        cb = sb_scr[j, pl.ds(0, G, stride=SUBLANES), :]
        for k in range(1, SUBLANES):
            ak = sa_scr[j, pl.ds(k, G, stride=SUBLANES), :]
            cb = ak * cb + sb_scr[j, pl.ds(k, G, stride=SUBLANES), :]
            ca = ak * ca
            sa_scr[j, pl.ds(k, G, stride=SUBLANES), :] = ca
            sb_scr[j, pl.ds(k, G, stride=SUBLANES), :] = cb
        hin = hl_scr[0:1, j * LANES:(j + 1) * LANES]
        cb = cb + jnp.where(row_g == 0, ca * hin, 0.0)
        d = 1
        while d < G:
            ok = row_g >= d
            cb = cb + ca * jnp.where(ok, pltpu.roll(cb, d, 0), 0.0)
            ca = ca * jnp.where(ok, pltpu.roll(ca, d, 0), 1.0)
            d *= 2
        hprev = jnp.where(row_g == 0, hin, pltpu.roll(cb, 1, 0))
        for k in range(SUBLANES):
            sb_scr[j, pl.ds(k, G, stride=SUBLANES), :] = (
                sb_scr[j, pl.ds(k, G, stride=SUBLANES), :] + sa_scr[j, pl.ds(k, G, stride=SUBLANES), :] * hprev)
        hl_scr[0:1, j * LANES:(j + 1) * LANES] = cb[G - 1:G, :]
        gl = p_scr[:, PGL_OFF + j * LANES:PGL_OFF + (j + 1) * LANES]
        mix_scr[:, MIX_LRU + j * LANES:MIX_LRU + (j + 1) * LANES] = (sb_scr[j] * _silu(gl)).astype(BF16)

    for j in range(N_CONV_TILES):
        cs_ = slice(j * LANES, (j + 1) * LANES)
        classes = _conv_classes(sx_scr, j, T, scw_ref, scb_ref[0:1, cs_], cs_)
        sx_scr[j, 0:P, :] = sx_scr[j, T:T + P, :]
        for k, xc in enumerate(classes):
            sx_scr[j, pl.ds(P + k, T // CONV_STRIDE, stride=CONV_STRIDE), :] = _silu(xc)

    a_head = -jnp.exp(hvec_ref[1:2, :])
    qi = lax.broadcasted_iota(jnp.int32, (WINDOW, WINDOW), 0)
    si = lax.broadcasted_iota(jnp.int32, (WINDOW, WINDOW), 1)
    causal = si <= qi
    tri = causal.astype(F32)
    for i in range(nsub):
        r0 = i * WINDOW
        dt_c = _softplus(p_scr[r0:r0 + WINDOW, PDT_OFF:PDT_OFF + LANES] + hvec_ref[0:1, :])
        cs = jnp.dot(tri, dt_c * a_head, precision=lax.Precision.HIGHEST, preferred_element_type=F32)
        cst = cs.T
        ys = [None] * N_SSD_TILES
        for g in range(SSD_GROUPS):
            bg_t = sx_scr[N_SSD_TILES + g, P + r0:P + r0 + WINDOW, :].T.astype(BF16)
            cg = sx_scr[N_SSD_TILES + SSD_GROUPS + g, P + r0:P + r0 + WINDOW, :].astype(BF16)
            cbm = _dot(cg, bg_t)
            yo = _dot(cg, ht_scr[g].astype(BF16))
            xdec, edec = [], []
            for pp in range(TILES_PER_GROUP):
                p = g * TILES_PER_GROUP + pp
                bcs, mm, dts = [], [], []
                for h in (2 * p, 2 * p + 1):
                    bc = jnp.broadcast_to(cs[:, h:h + 1], (WINDOW, WINDOW))
                    lmat = jnp.exp(jnp.where(causal, bc - cst[h:h + 1, :], NEG_BIG))
                    mm.append((cbm * lmat).astype(BF16))
                    bcs.append(bc)
                    dts.append(jnp.broadcast_to(dt_c[:, h:h + 1], (WINDOW, LANES)))
                csl = jnp.where(lo_sq, bcs[0], bcs[1])
                ecs = jnp.exp(csl)
                dec = jnp.exp(csl[WINDOW - 1:WINDOW, :] - csl)
                xs_p = sx_scr[p, P + r0:P + r0 + WINDOW, :]
                xdt = xs_p * jnp.where(lo_sq, dts[0], dts[1])
                xdt_b = xdt.astype(BF16)
                y = jnp.where(lo_sq, _dot(mm[0], xdt_b), _dot(mm[1], xdt_b))
                y = y + yo[:, pp * LANES:(pp + 1) * LANES] * ecs
                y = y + svec_ref[0:1, p * LANES:(p + 1) * LANES] * xs_p
                ys[p] = y * _silu(p_scr[r0:r0 + WINDOW, PZ_OFF + p * LANES:PZ_OFF + (p + 1) * LANES])
                xdec.append((xdt * dec).astype(BF16))
                edec.append(ecs[WINDOW - 1:WINDOW, :])
            ht_scr[g] = (ht_scr[g] * jnp.concatenate(edec, axis=1)
                         + _dot(bg_t, jnp.concatenate(xdec, axis=1)))
        ss = jnp.sum(ys[0] * ys[0], -1, keepdims=True)
        for p in range(1, N_SSD_TILES):
            ss = ss + jnp.sum(ys[p] * ys[p], -1, keepdims=True)
        rinv = lax.rsqrt(ss * (1.0 / SSD_WIDTH) + NORM_EPS)
        for p in range(N_SSD_TILES):
            mix_scr[r0:r0 + WINDOW, MIX_SSD + p * LANES:MIX_SSD + (p + 1) * LANES] = (
                ys[p] * rinv * svec_ref[1:2, p * LANES:(p + 1) * LANES]).astype(BF16)

    out = _dot(mix_scr[...], wout_ref[...])
    y_ref[...] = _layer_norm(DEEPNORM_ALPHA * x_ref[...] + out, ln_ref[0:1, :], ln_ref[1:2, :])

    @pl.when(last)
    def _():
        tab = rope_ref[T - WINDOW:T, :]
        k0 = _rope(p_scr[T - WINDOW:T, K_OFF:K_OFF + LANES], tab)
        k1 = _rope(p_scr[T - WINDOW:T, K_OFF + LANES:K_OFF + 2 * LANES], tab)
        ko_ref[...] = jnp.where(lo_sq, k0, k1)
        vo_ref[...] = jnp.where(lo_sq, p_scr[T - WINDOW:T, V_OFF:V_OFF + LANES],
                                p_scr[T - WINDOW:T, V_OFF + LANES:V_OFF + 2 * LANES])
        for j in range(N_LRU_TILES):
            lco_ref[:, j * LANES:(j + 1) * LANES] = lx_scr[j, P + T - (CONV_W - 1):P + T, :]
        lho_ref[...] = hl_scr[...]
        for j in range(N_CONV_TILES):
            sco_ref[:, j * LANES:(j + 1) * LANES] = sx_scr[j, P - (CONV_W - 1):P, :]
        for g in range(SSD_GROUPS):
            for pp in range(TILES_PER_GROUP):
                tile = ht_scr[g, :, pp * LANES:(pp + 1) * LANES].T
                h = 2 * (g * TILES_PER_GROUP + pp)
                sho_ref[h] = tile[0:SSD_HEAD_DIM, :]
                sho_ref[h + 1] = tile[SSD_HEAD_DIM:2 * SSD_HEAD_DIM, :]


def _const_spec(shape, layer):
    nd = len(shape)
    return pl.BlockSpec((None,) + tuple(shape), lambda *_: (layer,) + (0,) * nd,
                        pipeline_mode=pl.Buffered(1))


def _prompt_layer(layer, x, rope_tab, sinks, prm):
    B, L, _ = x.shape
    T = PROMPT_TILE
    nc = L // T
    win, aux, wg, wout, lcw, lvec, scw, scb, hvec, svec, ln = prm
    tile = lambda b, c: (b, c, 0)
    per_b = lambda b, c: (b, 0, 0)

    in_specs = [
        pl.BlockSpec(memory_space=pltpu.SMEM),
        pl.BlockSpec((None, T, D_MODEL), tile),
        pl.BlockSpec((T, 3 * LANES), lambda b, c: (c, 0)),
        _const_spec((D_MODEL, W_IN_COLS), layer),
        _const_spec((D_MODEL, AUX_COLS), layer),
        _const_spec((LRU_WIDTH, 2 * LRU_WIDTH), layer),
        _const_spec((D_MIX, D_MODEL), layer),
        _const_spec((CONV_W, LRU_WIDTH), layer),
        _const_spec((4, LRU_WIDTH), layer),
        _const_spec((CONV_W, SSD_CONV_CH), layer),
        _const_spec((1, SSD_CONV_CH), layer),
        _const_spec((2, LANES), layer),
        _const_spec((2, SSD_WIDTH), layer),
        _const_spec((2, D_MODEL), layer),
    ]
    out_shape = (
        jax.ShapeDtypeStruct((B, L, D_MODEL), F32),
        jax.ShapeDtypeStruct((B, WINDOW, ATT_KV_WIDTH), F32),
        jax.ShapeDtypeStruct((B, WINDOW, ATT_KV_WIDTH), F32),
        jax.ShapeDtypeStruct((B, CONV_W - 1, LRU_WIDTH), F32),
        jax.ShapeDtypeStruct((B, 1, LRU_WIDTH), F32),
        jax.ShapeDtypeStruct((B, CONV_W - 1, SSD_CONV_CH), F32),
        jax.ShapeDtypeStruct((B, SSD_HEADS, SSD_HEAD_DIM, SSD_STATE), F32),
    )
    out_specs = (
        pl.BlockSpec((None, T, D_MODEL), tile),
        pl.BlockSpec((None, WINDOW, ATT_KV_WIDTH), per_b),
        pl.BlockSpec((None, WINDOW, ATT_KV_WIDTH), per_b),
        pl.BlockSpec((None, CONV_W - 1, LRU_WIDTH), per_b),
        pl.BlockSpec((None, 1, LRU_WIDTH), per_b),
        pl.BlockSpec((None, CONV_W - 1, SSD_CONV_CH), per_b),
        pl.BlockSpec((None, SSD_HEADS, SSD_HEAD_DIM, SSD_STATE), lambda b, c: (b, 0, 0, 0)),
    )
    scratch = [
        pltpu.VMEM((T, P_COLS), F32),
        pltpu.VMEM((ATT_KV_HEADS, WINDOW + T, LANES), BF16),
        pltpu.VMEM((ATT_KV_HEADS, WINDOW + T, LANES), BF16),
        pltpu.VMEM((N_LRU_TILES, SUBLANES + T, LANES), F32),
        pltpu.VMEM((N_CONV_TILES, SUBLANES + T, LANES), F32),
        pltpu.VMEM((N_LRU_TILES, T, LANES), F32),
        pltpu.VMEM((N_LRU_TILES, T, LANES), F32),
        pltpu.VMEM((1, LRU_WIDTH), F32),
        pltpu.VMEM((SSD_GROUPS, SSD_STATE, GROUP_W), F32),
        pltpu.VMEM((T, D_MIX), BF16),
        pltpu.VMEM((T, LRU_WIDTH), BF16),
    ]
    return pl.pallas_call(
        _prompt_kernel,
        grid=(B, nc),
        in_specs=in_specs,
        out_specs=out_specs,
        out_shape=out_shape,
        scratch_shapes=scratch,
        compiler_params=pltpu.CompilerParams(
            dimension_semantics=("arbitrary", "arbitrary"),
            vmem_limit_bytes=VMEM_LIMIT_BYTES),
        name=f"prompt_layer{layer}",
    )(sinks, x, rope_tab, win, aux, wg, wout, lcw, lvec, scw, scb, hvec, svec, ln)


def _conv_step(x_new, st_ref, sto_ref, w_ref, bias):
    y = bias + w_ref[CONV_W - 1:CONV_W, :] * x_new
    for t in range(CONV_W - 1):
        y = y + w_ref[t:t + 1, :] * st_ref[t]
    for t in range(CONV_W - 2):
        sto_ref[t] = st_ref[t + 1]
    sto_ref[CONV_W - 2] = x_new
    return y


def _sample_kernel(n_aliased, sinks_ref, x_ref, rope_ref, win_ref, aux_ref, wg_ref, wout_ref, lcw_ref, lvec_ref,
                   scw_ref, scb_ref, hvec_ref, svec_ref, ln_ref,
                   ck_ref, cv_ref, lc_ref, lh_ref, sc_ref, sh_ref, *refs):
    (y_ref, cko_ref, cvo_ref, lco_ref, lho_ref, sco_ref, sho_ref,
     p_scr, q_scr, o_scr, kn_scr, vn_scr, xdt_scr, da_scr, y_scr, mix_scr) = refs[n_aliased:]
    NB = x_ref.shape[0]
    BT = SAMPLE_BT
    i = pl.program_id(0)
    lo_f = _lane_lo((NB, LANES))

    @pl.when(i == 0)
    def _():
        _project(x_ref[...], win_ref, aux_ref, p_scr)
        tab = rope_ref[...]
        for cc in range(ATT_HEADS // 2):
            g = cc // (ATT_HEADS // ATT_KV_HEADS // 2)
            qp = _rope(p_scr[:, Q_OFF + cc * LANES:Q_OFF + (cc + 1) * LANES], tab) * ATT_SCALE
            for half in range(2):
                t = jnp.where(lo_f if half == 0 else jnp.logical_not(lo_f), qp, 0.0)
                if half != g:
                    t = pltpu.roll(t, HALF, 1)
                q_scr[pl.ds(2 * cc + half, NB, stride=ATT_HEADS), :] = t
        k0 = _rope(p_scr[:, K_OFF:K_OFF + LANES], tab)
        k1 = _rope(p_scr[:, K_OFF + LANES:K_OFF + 2 * LANES], tab)
        kn_scr[...] = jnp.where(lo_f, k0, k1)
        vn_scr[...] = jnp.where(lo_f, p_scr[:, V_OFF:V_OFF + LANES], p_scr[:, V_OFF + LANES:V_OFF + 2 * LANES])
        xin = p_scr[:, XL_OFF:XL_OFF + LRU_WIDTH]
        xl = _conv_step(xin, lc_ref, lco_ref, lcw_ref, lvec_ref[0:1, :])
        gates = _dot(xl.astype(BF16), wg_ref[...])
        a, bt = _lru_coeffs(xl, gates, lvec_ref)
        h1 = a * lh_ref[...] + bt
        lho_ref[...] = h1
        mix_scr[:, MIX_LRU:MIX_LRU + LRU_WIDTH] = (h1 * _silu(p_scr[:, GL_OFF:GL_OFF + LRU_WIDTH])).astype(BF16)
        xin2 = p_scr[:, XBC_OFF:XBC_OFF + SSD_CONV_CH]
        xc = _conv_step(xin2, sc_ref, sco_ref, scw_ref, scb_ref[0:1, :])
        p_scr[:, XBC_OFF:XBC_OFF + SSD_CONV_CH] = _silu(xc)
        dt = _softplus(p_scr[:, DT_OFF:DT_OFF + LANES] + hvec_ref[0:1, :])
        da_scr[...] = jnp.exp(dt * (-jnp.exp(hvec_ref[1:2, :])))
        for p in range(N_SSD_TILES):
            dte = jnp.where(lo_f, jnp.broadcast_to(dt[:, 2 * p:2 * p + 1], (NB, LANES)),
                            jnp.broadcast_to(dt[:, 2 * p + 1:2 * p + 2], (NB, LANES)))
            xdt_scr[:, p * LANES:(p + 1) * LANES] = p_scr[:, XBC_OFF + p * LANES:XBC_OFF + (p + 1) * LANES] * dte

    r8 = pl.multiple_of(i * BT, BT)
    kn_blk = kn_scr[pl.ds(r8, BT), :]
    vn_blk = vn_scr[pl.ds(r8, BT), :]
    zpad = jnp.zeros((LANES - BT, LANES), F32)
    kn_t = jnp.concatenate([kn_blk, zpad], axis=0).T
    vn_t = jnp.concatenate([vn_blk, zpad], axis=0).T
    newest = lax.broadcasted_iota(jnp.int32, (ATT_KV_WIDTH, WINDOW), 1) == WINDOW - 1
    da_t =jnp.concatenate([da_scr[pl.ds(r8, BT), :], zpad], axis=0).T
    x_t = [jnp.concatenate([xdt_scr[pl.ds(r8, BT), p * LANES:(p + 1) * LANES], zpad], axis=0).T
           for p in range(N_SSD_TILES)]
    b_blk = p_scr[pl.ds(r8, BT), B_OFF:B_OFF + SSD_GROUPS * SSD_STATE]
    c_blk = p_scr[pl.ds(r8, BT), C_OFF:C_OFF + SSD_GROUPS * SSD_STATE].astype(BF16)
    row8 = lax.broadcasted_iota(jnp.int32, (ATT_HEADS, 1), 0)
    sink = jnp.zeros((ATT_HEADS, 1), F32)
    for h in range(ATT_HEADS):
        sink = jnp.where(row8 == h, sinks_ref[h], sink)
    rowb = lax.broadcasted_iota(jnp.int32, (BT, GROUP_W), 0)
    y_acc = [jnp.zeros((BT, GROUP_W), F32) for _ in range(SSD_GROUPS)]
    rowk = lax.broadcasted_iota(jnp.int32, (BT, SSD_STATE), 0)
    outer = []
    for g in range(SSD_GROUPS):
        bg = b_blk[:, g * SSD_STATE:(g + 1) * SSD_STATE]
        diag = jnp.concatenate([jnp.where(rowk == bb, bg, 0.0) for bb in range(BT)], axis=1)
        rhs = jnp.concatenate([diag, jnp.zeros((LANES - BT, BT * SSD_STATE), F32)], axis=0).astype(BF16)
        outer.append([_dot(x_t[g * TILES_PER_GROUP + pp].astype(BF16), rhs) for pp in range(TILES_PER_GROUP)])
    for bb in range(BT):
        qr = q_scr[pl.ds(pl.multiple_of((r8 + bb) * ATT_HEADS, ATT_HEADS), ATT_HEADS), :]
        kt = ck_ref[bb]
        vt = cv_ref[bb]
        s = _dot(qr.astype(BF16), kt.astype(BF16))
        s_new = jnp.sum(qr * kn_blk[bb:bb + 1, :], -1, keepdims=True)
        m = jnp.maximum(jnp.maximum(jnp.max(s, -1, keepdims=True), s_new), sink)
        e = jnp.exp(s - m)
        e_new = jnp.exp(s_new - m)
        den = jnp.sum(e, -1, keepdims=True) + e_new + jnp.exp(sink - m)
        o = _dot_nt(e.astype(BF16), vt.astype(BF16)) + e_new * vn_blk[bb:bb + 1, :]
        o_scr[pl.ds(pl.multiple_of((r8 + bb) * ATT_HEADS, ATT_HEADS), ATT_HEADS), :] = o * (1.0 / den)
        cko_ref[bb] = jnp.where(newest, kn_t[:, bb:bb + 1], pltpu.roll(kt, WINDOW - 1, 1))
        cvo_ref[bb] = jnp.where(newest, vn_t[:, bb:bb + 1], pltpu.roll(vt, WINDOW - 1, 1))
        for g in range(SSD_GROUPS):
            tiles = []
            for pp in range(TILES_PER_GROUP):
                p = g * TILES_PER_GROUP + pp
                for hh in range(2):
                    h = 2 * p + hh
                    dab = jnp.broadcast_to(da_t[h:h + 1, bb:bb + 1], (SSD_HEAD_DIM, SSD_STATE))
                    h1 = sh_ref[bb, h] * dab + outer[g][pp][hh * SSD_HEAD_DIM:(hh + 1) * SSD_HEAD_DIM,
                                                               bb * SSD_STATE:(bb + 1) * SSD_STATE]
                    sho_ref[bb, h] = h1
                    tiles.append(h1.astype(BF16))
            res = _dot_nt(c_blk[:, g * SSD_STATE:(g + 1) * SSD_STATE], jnp.concatenate(tiles, axis=0))
            y_acc[g] = jnp.where(rowb == bb, res, y_acc[g])
    for g in range(SSD_GROUPS):
        y_scr[pl.ds(r8, BT), g * GROUP_W:(g + 1) * GROUP_W] = y_acc[g]

    @pl.when(i == pl.num_programs(0) - 1)
    def _():
        for cc in range(ATT_HEADS // 2):
            g = cc // (ATT_HEADS // ATT_KV_HEADS // 2)
            oe = o_scr[pl.ds(2 * cc, NB, stride=ATT_HEADS), :]
            oo = o_scr[pl.ds(2 * cc + 1, NB, stride=ATT_HEADS), :]
            if g == 0:
                oo = pltpu.roll(oo, HALF, 1)
            else:
                oe = pltpu.roll(oe, HALF, 1)
            gate = p_scr[:, GA_OFF + cc * LANES:GA_OFF + (cc + 1) * LANES]
            mix_scr[:, cc * LANES:(cc + 1) * LANES] = (jnp.where(lo_f, oe, oo) * _silu(gate)).astype(BF16)
        y = (y_scr[...] + svec_ref[0:1, :] * p_scr[:, XBC_OFF:XBC_OFF + SSD_WIDTH]) * _silu(p_scr[:, Z_OFF:Z_OFF + SSD_WIDTH])
        y = y * lax.rsqrt(jnp.mean(y * y, -1, keepdims=True) + NORM_EPS) * svec_ref[1:2, :]
        mix_scr[:, MIX_SSD:MIX_SSD + SSD_WIDTH] = y.astype(BF16)
        out = _dot(mix_scr[...], wout_ref[...])
        y_ref[...] = _layer_norm(DEEPNORM_ALPHA * x_ref[...] + out, ln_ref[0:1, :], ln_ref[1:2, :])


def _sample_layer(layer, x, rope_tab, sinks, prm, ck, cv, lc, lh, sc, sh, prev):
    NB = x.shape[0]
    BT = SAMPLE_BT
    H = CONV_W - 1
    win, aux, wg, wout, lcw, lvec, scw, scb, hvec, svec, ln = prm
    whole = lambda shape: pl.BlockSpec(shape, lambda i: (0,) * len(shape), pipeline_mode=pl.Buffered(1))
    lwhole = lambda shape: pl.BlockSpec((None,) + shape, lambda i: (layer,) + (0,) * len(shape),
                                        pipeline_mode=pl.Buffered(1))
    rows = lambda shape: pl.BlockSpec((None, BT) + shape, lambda i: (layer, i) + (0,) * len(shape))
    state_specs = [
        rows((ATT_KV_WIDTH, WINDOW)),
        rows((ATT_KV_WIDTH, WINDOW)),
        lwhole((H, NB, LRU_WIDTH)),
        lwhole((NB, LRU_WIDTH)),
        lwhole((H, NB, SSD_CONV_CH)),
        rows((SSD_HEADS, SSD_HEAD_DIM, SSD_STATE)),
    ]
    in_specs = [
        pl.BlockSpec(memory_space=pltpu.SMEM),
        whole((NB, D_MODEL)),
        whole((1, 3 * LANES)),
        _const_spec((D_MODEL, W_IN_COLS), layer),
        _const_spec((D_MODEL, AUX_COLS), layer),
        _const_spec((LRU_WIDTH, 2 * LRU_WIDTH), layer),
        _const_spec((D_MIX, D_MODEL), layer),
        _const_spec((CONV_W, LRU_WIDTH), layer),
        _const_spec((4, LRU_WIDTH), layer),
        _const_spec((CONV_W, SSD_CONV_CH), layer),
        _const_spec((1, SSD_CONV_CH), layer),
        _const_spec((2, LANES), layer),
        _const_spec((2, SSD_WIDTH), layer),
        _const_spec((2, D_MODEL), layer),
    ] + state_specs
    operands = [sinks, x, rope_tab, win, aux, wg, wout, lcw, lvec, scw, scb, hvec, svec, ln, ck, cv, lc, lh, sc, sh]
    aliases = {}
    if prev is not None:
        for k, arr in enumerate(prev):
            aliases[len(operands)] = 1 + k
            operands.append(arr)
            in_specs.append(pl.BlockSpec(memory_space=pl.ANY))
    out_shape = (jax.ShapeDtypeStruct((NB, D_MODEL), F32),) + tuple(
        jax.ShapeDtypeStruct(a.shape, F32) for a in (ck, cv, lc, lh, sc, sh))
    full = lambda shape: pl.BlockSpec(shape, lambda i: (0,) * len(shape))
    lfull = lambda shape: pl.BlockSpec((None,) + shape, lambda i: (layer,) + (0,) * len(shape))
    out_specs = (
        full((NB, D_MODEL)),
        rows((ATT_KV_WIDTH, WINDOW)),
        rows((ATT_KV_WIDTH, WINDOW)),
        lfull((H, NB, LRU_WIDTH)),
        lfull((NB, LRU_WIDTH)),
        lfull((H, NB, SSD_CONV_CH)),
        rows((SSD_HEADS, SSD_HEAD_DIM, SSD_STATE)),
    )
    scratch = [
        pltpu.VMEM((NB, N_COLS), F32),
        pltpu.VMEM((NB * ATT_HEADS, LANES), F32),
        pltpu.VMEM((NB * ATT_HEADS, LANES), F32),
        pltpu.VMEM((NB, ATT_KV_WIDTH), F32),
        pltpu.VMEM((NB, ATT_KV_WIDTH), F32),
        pltpu.VMEM((NB, SSD_WIDTH), F32),
        pltpu.VMEM((NB, LANES), F32),
        pltpu.VMEM((NB, SSD_WIDTH), F32),
        pltpu.VMEM((NB, D_MIX), BF16),
    ]
    return pl.pallas_call(
        functools.partial(_sample_kernel, len(aliases)),
        grid=(NB // BT,),
        in_specs=in_specs,
        out_specs=out_specs,
        out_shape=out_shape,
        scratch_shapes=scratch,
        input_output_aliases=aliases,
        compiler_params=pltpu.CompilerParams(
            dimension_semantics=("arbitrary",),
            vmem_limit_bytes=VMEM_LIMIT_BYTES),
        name=f"sample_layer{layer}",
    )(*operands)


def _rope_table(pos):
    half = ROPE_DIM // 2
    inv = ROPE_THETA ** (-jnp.arange(half, dtype=F32) / half)
    ang = pos.astype(F32)[:, None] * inv[None, :]
    cos, sin = jnp.cos(ang), jnp.sin(ang)
    d = jnp.arange(LANES) % ATT_HEAD_DIM
    fi = d % half
    c = jnp.where(d < ROPE_DIM, cos[:, fi], 1.0)
    s1 = jnp.where(d < half, -sin[:, fi], 0.0)
    s2 = jnp.where((d >= half) & (d < ROPE_DIM), sin[:, fi], 0.0)
    return jnp.concatenate([c, s1, s2], axis=1)


def _prep_params(w_in, w_out, lru_conv_w, lru_conv_b, lru_wa, lru_ba, lru_wx, lru_bx, lru_lambda,
                 ssd_conv_w, ssd_conv_b, ssd_dt_bias, ssd_a_log, ssd_d, ssd_norm_g, ln_g, ln_b):
    hd = ATT_HEAD_DIM
    win = w_in.astype(BF16)
    k = w_in[..., ATT_WIDTH:ATT_WIDTH + ATT_KV_WIDTH]
    v = w_in[..., ATT_WIDTH + ATT_KV_WIDTH:ATT_WIDTH + 2 * ATT_KV_WIDTH]
    dup = lambda t: jnp.concatenate([t[..., :hd], t[..., :hd], t[..., hd:], t[..., hd:]], -1)
    dtw = jnp.pad(w_in[..., W_IN_COLS - SSD_HEADS:], ((0, 0), (0, 0), (0, LANES - SSD_HEADS)))
    aux = jnp.concatenate([dup(k), dup(v), dtw], -1).astype(BF16)

    def dense(w):
        rows = [jnp.pad(w[:, n], ((0, 0), (0, 0), (n * LRU_BLOCK, LRU_WIDTH - (n + 1) * LRU_BLOCK)))
                for n in range(LRU_BLOCKS)]
        return jnp.concatenate(rows, 1)

    wg = jnp.concatenate([dense(lru_wa.astype(BF16)), dense(lru_wx.astype(BF16))], -1)
    wout = w_out.astype(BF16)
    lvec = jnp.stack([lru_conv_b, lru_ba, lru_bx, lru_lambda], 1)
    hpad = lambda t: jnp.pad(t, ((0, 0), (0, LANES - SSD_HEADS)))
    hvec = jnp.stack([hpad(ssd_dt_bias), hpad(ssd_a_log)], 1)
    svec = jnp.stack([jnp.repeat(ssd_d, SSD_HEAD_DIM, axis=1), ssd_norm_g], 1)
    ln = jnp.stack([ln_g, ln_b], 1)
    return (win, aux, wg, wout, lru_conv_w, lvec, ssd_conv_w, ssd_conv_b[:, None, :], hvec, svec, ln)


def kernel(x_prompt, x_sample, cache_swa_k, cache_swa_v, state_lru_conv, state_lru_h, state_ssd_conv, state_ssd_h, w_in, w_out, att_sinks, lru_conv_w, lru_conv_b, lru_wa, lru_ba, lru_wx, lru_bx, lru_lambda, ssd_conv_w, ssd_conv_b, ssd_dt_bias, ssd_a_log, ssd_d, ssd_norm_g, ln_g, ln_b):
    prm = _prep_params(w_in, w_out, lru_conv_w, lru_conv_b, lru_wa, lru_ba, lru_wx, lru_bx, lru_lambda,
                       ssd_conv_w, ssd_conv_b, ssd_dt_bias, ssd_a_log, ssd_d, ssd_norm_g, ln_g, ln_b)
    bp, lp, _ = x_prompt.shape
    rope_p = _rope_table(jnp.arange(lp, dtype=jnp.int32))
    xp = x_prompt
    new_p = [[] for _ in range(6)]
    for l in range(DEPTH):
        xp, ko, vo, lco, lho, sco, sho = _prompt_layer(l, xp, rope_p, att_sinks[l], prm)
        st = (ko.reshape(bp, WINDOW, ATT_KV_HEADS, ATT_HEAD_DIM), vo.reshape(bp, WINDOW, ATT_KV_HEADS, ATT_HEAD_DIM),
              lco, lho.reshape(bp, LRU_WIDTH), sco, sho)
        for lst, t in zip(new_p, st):
            lst.append(t)
    outs_p = [jnp.stack(t) for t in new_p]

    nb = x_sample.shape[0]
    rope_s = _rope_table(PAST_LEN + jnp.arange(x_sample.shape[1], dtype=jnp.int32))
    to_kt = lambda t: jnp.transpose(t, (0, 1, 3, 4, 2)).reshape(DEPTH, nb, ATT_KV_WIDTH, WINDOW)
    from_kt = lambda t: jnp.transpose(t.reshape(DEPTH, nb, ATT_KV_HEADS, ATT_HEAD_DIM, WINDOW), (0, 1, 4, 2, 3))
    swap = lambda t: jnp.transpose(t, (0, 2, 1, 3))
    state = (to_kt(cache_swa_k), to_kt(cache_swa_v), swap(state_lru_conv), state_lru_h, swap(state_ssd_conv),
             state_ssd_h)
    xs = x_sample.reshape(nb, D_MODEL)
    new = None
    for l in range(DEPTH):
        xs, *new = _sample_layer(l, xs, rope_s, att_sinks[l], prm, *state, new)
    cko, cvo, lco, lho, sco, sho = new
    outs_s = (from_kt(cko), from_kt(cvo), swap(lco), lho, swap(sco), sho)
    return (xp, xs.reshape(x_sample.shape)) + tuple(outs_p) + tuple(outs_s)
```

```python
import functools

import jax
import jax.numpy as jnp
from jax import lax
from jax.experimental import pallas as pl
from jax.experimental.pallas import tpu as pltpu

F32 = jnp.float32
BF16 = jnp.bfloat16

D_MODEL = 1024
DEPTH = 4
PAST_LEN = 8192
D_MIX = 2 * D_MODEL
ATT_HEADS = 8
ATT_KV_HEADS = 2
ATT_HEAD_DIM = 64
ATT_WIDTH = ATT_HEADS * ATT_HEAD_DIM
ATT_KV_WIDTH = ATT_KV_HEADS * ATT_HEAD_DIM
ATT_SCALE = ATT_HEAD_DIM ** -0.5
WINDOW = 128
ROPE_THETA = 500000.0
ROPE_DIM = ATT_HEAD_DIM // 4
LRU_WIDTH = 3 * D_MIX // 8
LRU_BLOCKS = 8
LRU_BLOCK = LRU_WIDTH // LRU_BLOCKS
LRU_C = 8.0
CONV_W = 4
SSD_WIDTH = D_MIX - ATT_WIDTH - LRU_WIDTH
SSD_HEAD_DIM = 64
SSD_HEADS = SSD_WIDTH // SSD_HEAD_DIM
SSD_GROUPS = 2
SSD_STATE = 128
SSD_CONV_CH = SSD_WIDTH + 2 * SSD_GROUPS * SSD_STATE
DEEPNORM_ALPHA = (2.0 * DEPTH) ** 0.25
NORM_EPS = 1e-5

LANES = 128
SUBLANES = 8
HALF = LANES // 2
VMEM_LIMIT_BYTES = 60 * 1024 * 1024

Q_OFF = 0
K_OFF = Q_OFF + ATT_WIDTH
V_OFF = K_OFF + 2 * ATT_KV_WIDTH
GA_OFF = V_OFF + 2 * ATT_KV_WIDTH
XL_OFF = GA_OFF + ATT_WIDTH
GL_OFF = XL_OFF + LRU_WIDTH
Z_OFF = GL_OFF + LRU_WIDTH
XBC_OFF = Z_OFF + SSD_WIDTH
DT_OFF = XBC_OFF + SSD_CONV_CH
N_COLS = DT_OFF + LANES
W_IN_COLS = 2 * ATT_WIDTH + 2 * ATT_KV_WIDTH + 2 * LRU_WIDTH + SSD_WIDTH + SSD_CONV_CH + SSD_HEADS
AUX_COLS = 4 * ATT_KV_WIDTH + LANES
B_OFF = XBC_OFF + SSD_WIDTH
C_OFF = B_OFF + SSD_GROUPS * SSD_STATE
MIX_LRU = ATT_WIDTH
MIX_SSD = ATT_WIDTH + LRU_WIDTH

N_LRU_TILES = LRU_WIDTH // LANES
N_SSD_TILES = SSD_WIDTH // LANES
N_CONV_TILES = SSD_CONV_CH // LANES
CONV_STRIDE = 4
TILES_PER_GROUP = N_SSD_TILES // SSD_GROUPS
GROUP_W = SSD_WIDTH // SSD_GROUPS

NEG_BIG = -1e30

PROMPT_TILE = 256
SAMPLE_BT = 8


def _sigmoid(x):
    return 0.5 * jnp.tanh(0.5 * x) + 0.5


def _silu(x):
    h = 0.5 * x
    return h + h * jnp.tanh(h)


def _softplus(x):
    return jnp.maximum(x, 0.0) + jnp.log1p(jnp.exp(-jnp.abs(x)))


def _lane_lo(shape):
    return (lax.broadcasted_iota(jnp.int32, shape, len(shape) - 1) % LANES) < HALF


def _rope(t, tab):
    half = ROPE_DIM // 2
    return (t * tab[:, 0:LANES]
            + pltpu.roll(t, LANES - half, 1) * tab[:, LANES:2 * LANES]
            + pltpu.roll(t, half, 1) * tab[:, 2 * LANES:3 * LANES])


def _dot(a, b):
    return jnp.dot(a, b, preferred_element_type=F32)


def _dot_nt(a, b):
    return lax.dot_general(a, b, (((1,), (1,)), ((), ())), preferred_element_type=F32)


def _conv_classes(src_ref, j, rows, w_ref, bias, cs):
    n = CONV_STRIDE
    hist = CONV_W - 1
    taps = {s_: src_ref[j, pl.ds(SUBLANES + s_, rows // n, stride=n), :] for s_ in range(-hist, n)}
    outs = []
    for k in range(n):
        acc = bias + w_ref[hist:hist + 1, cs] * taps[k]
        for t in range(hist):
            acc = acc + w_ref[t:t + 1, cs] * taps[k - hist + t]
        outs.append(acc)
    return outs


def _layer_norm(v, g, b):
    mu = jnp.mean(v, -1, keepdims=True)
    d = v - mu
    var = jnp.mean(d * d, -1, keepdims=True)
    return d * lax.rsqrt(var + NORM_EPS) * g + b


def _lru_coeffs(xl, gates, lvec_ref):
    r = _sigmoid(gates[:, 0:LRU_WIDTH] + lvec_ref[1:2, :])
    ig = _sigmoid(gates[:, LRU_WIDTH:2 * LRU_WIDTH] + lvec_ref[2:3, :])
    log_a = (-LRU_C * _softplus(-lvec_ref[3:4, :])) * r
    a = jnp.exp(log_a)
    bt = jnp.sqrt(-jnp.tanh(log_a) * (1.0 + a * a)) * (ig * xl)
    return a, bt


def _project(x_tile, win_ref, aux_ref, dst):
    xb = x_tile.astype(BF16)
    kv0 = ATT_WIDTH + 2 * ATT_KV_WIDTH
    dst[:, Q_OFF:K_OFF] = _dot(xb, win_ref[:, 0:ATT_WIDTH])
    dst[:, K_OFF:GA_OFF] = _dot(xb, aux_ref[:, 0:GA_OFF - K_OFF])
    dst[:, GA_OFF:Z_OFF] = _dot(xb, win_ref[:, kv0:kv0 + Z_OFF - GA_OFF])
    dst[:, Z_OFF:DT_OFF] = _dot(xb, win_ref[:, kv0 + Z_OFF - GA_OFF:kv0 + DT_OFF - GA_OFF])
    dst[:, DT_OFF:N_COLS] = _dot(xb, aux_ref[:, GA_OFF - K_OFF:GA_OFF - K_OFF + LANES])


def _prompt_kernel(sinks_ref, x_ref, rope_ref, win_ref, aux_ref, wg_ref, wout_ref, lcw_ref, lvec_ref,
                   scw_ref, scb_ref, hvec_ref, svec_ref, ln_ref,
                   y_ref, ko_ref, vo_ref, lco_ref, lho_ref, sco_ref, sho_ref,
                   p_scr, kd_scr, vd_scr, lx_scr, sx_scr, sa_scr, sb_scr, hl_scr, ht_scr, mix_scr, xlb_scr, xc_scr):
    T = PROMPT_TILE
    nsub = T // WINDOW
    c = pl.program_id(1)
    last = c == pl.num_programs(1) - 1

    @pl.when(c == 0)
    def _():
        kd_scr[:, 0:WINDOW, :] = jnp.zeros((2, WINDOW, LANES), BF16)
        vd_scr[:, 0:WINDOW, :] = jnp.zeros((2, WINDOW, LANES), BF16)
        lx_scr[:, 0:SUBLANES, :] = jnp.zeros((N_LRU_TILES, SUBLANES, LANES), F32)
        sx_scr[:, 0:SUBLANES, :] = jnp.zeros((N_CONV_TILES, SUBLANES, LANES), F32)
        hl_scr[...] = jnp.zeros_like(hl_scr)
        ht_scr[...] = jnp.zeros_like(ht_scr)

    _project(x_ref[...], win_ref, aux_ref, p_scr)

    lo_sq = _lane_lo((WINDOW, LANES))

    for g in range(ATT_KV_HEADS):
        kd = _rope(p_scr[:, K_OFF + g * LANES:K_OFF + (g + 1) * LANES], rope_ref[...])
        kd_scr[g, WINDOW:WINDOW + T, :] = kd.astype(BF16)
        vd_scr[g, WINDOW:WINDOW + T, :] = p_scr[:, V_OFF + g * LANES:V_OFF + (g + 1) * LANES].astype(BF16)

    row = lax.broadcasted_iota(jnp.int32, (WINDOW, 2 * WINDOW), 0)
    col = lax.broadcasted_iota(jnp.int32, (WINDOW, 2 * WINDOW), 1)
    band = (col >= row) & (col <= row + WINDOW)
    first_lo = jnp.where(c > 0, 0, WINDOW)
    for i in range(nsub):
        r0 = i * WINDOW
        msk = (band & (col >= first_lo)) if i == 0 else band
        tab = rope_ref[r0:r0 + WINDOW, :]
        for cc in range(ATT_HEADS // 2):
            g = cc // (ATT_HEADS // ATT_KV_HEADS // 2)
            kg = kd_scr[g, r0:r0 + 2 * WINDOW, :]
            vg = vd_scr[g, r0:r0 + 2 * WINDOW, :]
            qp = _rope(p_scr[r0:r0 + WINDOW, Q_OFF + cc * LANES:Q_OFF + (cc + 1) * LANES], tab) * ATT_SCALE
            outs = []
            for half in range(2):
                sink = sinks_ref[2 * cc + half]
                qm = jnp.where(lo_sq if half == 0 else jnp.logical_not(lo_sq), qp, 0.0).astype(BF16)
                s = jnp.where(msk, _dot_nt(qm, kg), NEG_BIG)
                m = jnp.maximum(jnp.max(s, -1, keepdims=True), sink)
                e = jnp.exp(s - m)
                den = jnp.sum(e, -1, keepdims=True) + jnp.exp(sink - m)
                outs.append(_dot(e.astype(BF16), vg) * (1.0 / den))
            att = jnp.where(lo_sq, outs[0], outs[1])
            gate = p_scr[r0:r0 + WINDOW, GA_OFF + cc * LANES:GA_OFF + (cc + 1) * LANES]
            mix_scr[r0:r0 + WINDOW, cc * LANES:(cc + 1) * LANES] = (att * _silu(gate)).astype(BF16)
    for g in range(ATT_KV_HEADS):
        kd_scr[g, 0:WINDOW, :] = kd_scr[g, T:T + WINDOW, :]
        vd_scr[g, 0:WINDOW, :] = vd_scr[g, T:T + WINDOW, :]

    P = SUBLANES
    H = CONV_W - 1
    for j in range(N_LRU_TILES):
        cs_ = slice(j * LANES, (j + 1) * LANES)
        lx_scr[j, P:P + T, :] = p_scr[:, XL_OFF + j * LANES:XL_OFF + (j + 1) * LANES]
        for k, xl in enumerate(_conv_classes(lx_scr, j, T, lcw_ref, lvec_ref[0:1, cs_], cs_)):
            sb_scr[j, pl.ds(k, T // CONV_STRIDE, stride=CONV_STRIDE), :] = xl
        xlb_scr[:, cs_] = sb_scr[j].astype(BF16)
    lx_scr[:, 0:P, :] = lx_scr[:, T:T + P, :]

    G = T // SUBLANES
    row_g = lax.broadcasted_iota(jnp.int32, (G, LANES), 0)
    cl = -LRU_C * _softplus(-lvec_ref[3:4, :])
    for jj in range(N_LRU_TILES // 2):
        c0 = jj * 2 * LANES
        k0 = (c0 // LRU_BLOCK) * LRU_BLOCK // LANES * LANES
        k1 = -(-(-(-(c0 + 2 * LANES) // LRU_BLOCK) * LRU_BLOCK) // LANES) * LANES
        xk = xlb_scr[:, k0:k1]
        gr = _dot(xk, wg_ref[k0:k1, c0:c0 + 2 * LANES])
        gi = _dot(xk, wg_ref[k0:k1, LRU_WIDTH + c0:LRU_WIDTH + c0 + 2 * LANES])
        for u in range(2):
            j = 2 * jj + u
            cs_ = slice(j * LANES, (j + 1) * LANES)
            us = slice(u * LANES, (u + 1) * LANES)
            r = _sigmoid(gr[:, us] + lvec_ref[1:2, cs_])
            ig = _sigmoid(gi[:, us] + lvec_ref[2:3, cs_])
            log_a = cl[:, cs_] * r
            a = jnp.exp(log_a)
            sa_scr[j] = a
            sb_scr[j] = jnp.sqrt(-jnp.tanh(log_a) * (1.0 + a * a)) * (ig * sb_scr[j])
    for j in range(N_LRU_TILES):
        ca = sa_scr[j, pl.ds(0, G, stride=SUBLANES), :]
        cb = sb_scr[j, pl.ds(0, G, stride=SUBLANES), :]
        for k in range(1, SUBLANES):
            ak = sa_scr[j, pl.ds(k, G, stride=SUBLANES), :]
            cb = ak * cb + sb_scr[j, pl.ds(k, G, stride=SUBLANES), :]
            ca = ak * ca
            sa_scr[j, pl.ds(k, G, stride=SUBLANES), :] = ca
            sb_scr[j, pl.ds(k, G, stride=SUBLANES), :] = cb
        hin = hl_scr[0:1, j * LANES:(j + 1) * LANES]
        cb = cb + jnp.where(row_g == 0, ca * hin, 0.0)
        d = 1
        while d < G:
            ok = row_g >= d
            cb = cb + ca * jnp.where(ok, pltpu.roll(cb, d, 0), 0.0)
            ca = ca * jnp.where(ok, pltpu.roll(ca, d, 0), 1.0)
            d *= 2
        hprev = jnp.where(row_g == 0, hin, pltpu.roll(cb, 1, 0))
        for k in range(SUBLANES):
            sb_scr[j, pl.ds(k, G, stride=SUBLANES), :] = (
                sb_scr[j, pl.ds(k, G, stride=SUBLANES), :] + sa_scr[j, pl.ds(k, G, stride=SUBLANES), :] * hprev)
        hl_scr[0:1, j * LANES:(j + 1) * LANES] = cb[G - 1:G, :]
        gl = p_scr[:, GL_OFF + j * LANES:GL_OFF + (j + 1) * LANES]
        mix_scr[:, MIX_LRU + j * LANES:MIX_LRU + (j + 1) * LANES] = (sb_scr[j] * _silu(gl)).astype(BF16)

    for j in range(N_CONV_TILES):
        cs_ = slice(j * LANES, (j + 1) * LANES)
        sx_scr[j, P:P + T, :] = p_scr[:, XBC_OFF + j * LANES:XBC_OFF + (j + 1) * LANES]
        for k, xc in enumerate(_conv_classes(sx_scr, j, T, scw_ref, scb_ref[0:1, cs_], cs_)):
            xc_scr[j, pl.ds(k, T // CONV_STRIDE, stride=CONV_STRIDE), :] = _silu(xc)
    sx_scr[:, 0:P, :] = sx_scr[:, T:T + P, :]

    a_head = -jnp.exp(hvec_ref[1:2, :])
    qi = lax.broadcasted_iota(jnp.int32, (WINDOW, WINDOW), 0)
    si = lax.broadcasted_iota(jnp.int32, (WINDOW, WINDOW), 1)
    causal = si <= qi
    tri = causal.astype(F32)
    for i in range(nsub):
        r0 = i * WINDOW
        dt_c = _softplus(p_scr[r0:r0 + WINDOW, DT_OFF:DT_OFF + LANES] + hvec_ref[0:1, :])
        cs = jnp.dot(tri, dt_c * a_head, precision=lax.Precision.HIGHEST, preferred_element_type=F32)
        cst = cs.T
        ys = [None] * N_SSD_TILES
        for g in range(SSD_GROUPS):
            bg_t = xc_scr[N_SSD_TILES + g, r0:r0 + WINDOW, :].T.astype(BF16)
            cg = xc_scr[N_SSD_TILES + SSD_GROUPS + g, r0:r0 + WINDOW, :].astype(BF16)
            cbm = _dot(cg, bg_t)
            yo = _dot(cg, ht_scr[g].astype(BF16))
            xdec, edec = [], []
            for pp in range(TILES_PER_GROUP):
                p = g * TILES_PER_GROUP + pp
                bcs, mm, dts = [], [], []
                for h in (2 * p, 2 * p + 1):
                    bc = jnp.broadcast_to(cs[:, h:h + 1], (WINDOW, WINDOW))
                    lmat = jnp.exp(jnp.where(causal, bc - cst[h:h + 1, :], NEG_BIG))
                    mm.append((cbm * lmat).astype(BF16))
                    bcs.append(bc)
                    dts.append(jnp.broadcast_to(dt_c[:, h:h + 1], (WINDOW, LANES)))
                csl = jnp.where(lo_sq, bcs[0], bcs[1])
                ecs = jnp.exp(csl)
                dec = jnp.exp(csl[WINDOW - 1:WINDOW, :] - csl)
                xs_p = xc_scr[p, r0:r0 + WINDOW, :]
                xdt = xs_p * jnp.where(lo_sq, dts[0], dts[1])
                xdt_b = xdt.astype(BF16)
                y = jnp.where(lo_sq, _dot(mm[0], xdt_b), _dot(mm[1], xdt_b))
                y = y + yo[:, pp * LANES:(pp + 1) * LANES] * ecs
                y = y + svec_ref[0:1, p * LANES:(p + 1) * LANES] * xs_p
                ys[p] = y * _silu(p_scr[r0:r0 + WINDOW, Z_OFF + p * LANES:Z_OFF + (p + 1) * LANES])
                xdec.append((xdt * dec).astype(BF16))
                edec.append(ecs[WINDOW - 1:WINDOW, :])
            ht_scr[g] = (ht_scr[g] * jnp.concatenate(edec, axis=1)
                         + _dot(bg_t, jnp.concatenate(xdec, axis=1)))
        ss = jnp.sum(ys[0] * ys[0], -1, keepdims=True)
        for p in range(1, N_SSD_TILES):
            ss = ss + jnp.sum(ys[p] * ys[p], -1, keepdims=True)
        rinv = lax.rsqrt(ss * (1.0 / SSD_WIDTH) + NORM_EPS)
        for p in range(N_SSD_TILES):
            mix_scr[r0:r0 + WINDOW, MIX_SSD + p * LANES:MIX_SSD + (p + 1) * LANES] = (
                ys[p] * rinv * svec_ref[1:2, p * LANES:(p + 1) * LANES]).astype(BF16)

    out = (_dot(mix_scr[:, 0:MIX_SSD], wout_ref[0:MIX_SSD, :])
           + _dot(mix_scr[:, MIX_SSD:D_MIX], wout_ref[MIX_SSD:D_MIX, :]))
    y_ref[...] = _layer_norm(DEEPNORM_ALPHA * x_ref[...] + out, ln_ref[0:1, :], ln_ref[1:2, :])

    @pl.when(last)
    def _():
        tab = rope_ref[T - WINDOW:T, :]
        k0 = _rope(p_scr[T - WINDOW:T, K_OFF:K_OFF + LANES], tab)
        k1 = _rope(p_scr[T - WINDOW:T, K_OFF + LANES:K_OFF + 2 * LANES], tab)
        ko_ref[...] = jnp.where(lo_sq, k0, k1)
        vo_ref[...] = jnp.where(lo_sq, p_scr[T - WINDOW:T, V_OFF:V_OFF + LANES],
                                p_scr[T - WINDOW:T, V_OFF + LANES:V_OFF + 2 * LANES])
        for j in range(N_LRU_TILES):
            lco_ref[:, j * LANES:(j + 1) * LANES] = lx_scr[j, P + T - (CONV_W - 1):P + T, :]
        lho_ref[...] = hl_scr[...]
        for j in range(N_CONV_TILES):
            sco_ref[:, j * LANES:(j + 1) * LANES] = sx_scr[j, P + T - (CONV_W - 1):P + T, :]
        for g in range(SSD_GROUPS):
            for pp in range(TILES_PER_GROUP):
                tile = ht_scr[g, :, pp * LANES:(pp + 1) * LANES].T
                h = 2 * (g * TILES_PER_GROUP + pp)
                sho_ref[h] = tile[0:SSD_HEAD_DIM, :]
                sho_ref[h + 1] = tile[SSD_HEAD_DIM:2 * SSD_HEAD_DIM, :]


def _const_spec(shape, layer):
    nd = len(shape)
    return pl.BlockSpec((None,) + tuple(shape), lambda *_: (layer,) + (0,) * nd,
                        pipeline_mode=pl.Buffered(1))


def _prompt_layer(layer, x, rope_tab, sinks, prm):
    B, L, _ = x.shape
    T = PROMPT_TILE
    nc = L // T
    win, aux, wg, wout, lcw, lvec, scw, scb, hvec, svec, ln = prm
    tile = lambda b, c: (b, c, 0)
    per_b = lambda b, c: (b, 0, 0)

    in_specs = [
        pl.BlockSpec(memory_space=pltpu.SMEM),
        pl.BlockSpec((None, T, D_MODEL), tile),
        pl.BlockSpec((T, 3 * LANES), lambda b, c: (c, 0)),
        _const_spec((D_MODEL, W_IN_COLS), layer),
        _const_spec((D_MODEL, AUX_COLS), layer),
        _const_spec((LRU_WIDTH, 2 * LRU_WIDTH), layer),
        _const_spec((D_MIX, D_MODEL), layer),
        _const_spec((CONV_W, LRU_WIDTH), layer),
        _const_spec((4, LRU_WIDTH), layer),
        _const_spec((CONV_W, SSD_CONV_CH), layer),
        _const_spec((1, SSD_CONV_CH), layer),
        _const_spec((2, LANES), layer),
        _const_spec((2, SSD_WIDTH), layer),
        _const_spec((2, D_MODEL), layer),
    ]
    out_shape = (
        jax.ShapeDtypeStruct((B, L, D_MODEL), F32),
        jax.ShapeDtypeStruct((B, WINDOW, ATT_KV_WIDTH), F32),
        jax.ShapeDtypeStruct((B, WINDOW, ATT_KV_WIDTH), F32),
        jax.ShapeDtypeStruct((B, CONV_W - 1, LRU_WIDTH), F32),
        jax.ShapeDtypeStruct((B, 1, LRU_WIDTH), F32),
        jax.ShapeDtypeStruct((B, CONV_W - 1, SSD_CONV_CH), F32),
        jax.ShapeDtypeStruct((B, SSD_HEADS, SSD_HEAD_DIM, SSD_STATE), F32),
    )
    out_specs = (
        pl.BlockSpec((None, T, D_MODEL), tile),
        pl.BlockSpec((None, WINDOW, ATT_KV_WIDTH), per_b),
        pl.BlockSpec((None, WINDOW, ATT_KV_WIDTH), per_b),
        pl.BlockSpec((None, CONV_W - 1, LRU_WIDTH), per_b),
        pl.BlockSpec((None, 1, LRU_WIDTH), per_b),
        pl.BlockSpec((None, CONV_W - 1, SSD_CONV_CH), per_b),
        pl.BlockSpec((None, SSD_HEADS, SSD_HEAD_DIM, SSD_STATE), lambda b, c: (b, 0, 0, 0)),
    )
    scratch = [
        pltpu.VMEM((T, N_COLS), F32),
        pltpu.VMEM((ATT_KV_HEADS, WINDOW + T, LANES), BF16),
        pltpu.VMEM((ATT_KV_HEADS, WINDOW + T, LANES), BF16),
        pltpu.VMEM((N_LRU_TILES, SUBLANES + T, LANES), F32),
        pltpu.VMEM((N_CONV_TILES, SUBLANES + T, LANES), F32),
        pltpu.VMEM((N_LRU_TILES, T, LANES), F32),
        pltpu.VMEM((N_LRU_TILES, T, LANES), F32),
        pltpu.VMEM((1, LRU_WIDTH), F32),
        pltpu.VMEM((SSD_GROUPS, SSD_STATE, GROUP_W), F32),
        pltpu.VMEM((T, D_MIX), BF16),
        pltpu.VMEM((T, LRU_WIDTH), BF16),
        pltpu.VMEM((N_CONV_TILES, T, LANES), F32),
    ]
    return pl.pallas_call(
        _prompt_kernel,
        grid=(B, nc),
        in_specs=in_specs,
        out_specs=out_specs,
        out_shape=out_shape,
        scratch_shapes=scratch,
        compiler_params=pltpu.CompilerParams(
            dimension_semantics=("arbitrary", "arbitrary"),
            vmem_limit_bytes=VMEM_LIMIT_BYTES),
        name=f"prompt_layer{layer}",
    )(sinks, x, rope_tab, win, aux, wg, wout, lcw, lvec, scw, scb, hvec, svec, ln)


def _conv_step(x_new, st_ref, sto_ref, w_ref, bias):
    y = bias + w_ref[CONV_W - 1:CONV_W, :] * x_new
    for t in range(CONV_W - 1):
        y = y + w_ref[t:t + 1, :] * st_ref[t]
    for t in range(CONV_W - 2):
        sto_ref[t] = st_ref[t + 1]
    sto_ref[CONV_W - 2] = x_new
    return y


def _sample_kernel(n_aliased, sinks_ref, x_ref, rope_ref, win_ref, aux_ref, wg_ref, wout_ref, lcw_ref, lvec_ref,
                   scw_ref, scb_ref, hvec_ref, svec_ref, ln_ref,
                   ck_ref, cv_ref, lc_ref, lh_ref, sc_ref, sh_ref, *refs):
    (y_ref, cko_ref, cvo_ref, lco_ref, lho_ref, sco_ref, sho_ref,
     p_scr, q_scr, o_scr, kn_scr, vn_scr, xdt_scr, da_scr, y_scr, mix_scr) = refs[n_aliased:]
    NB = x_ref.shape[0]
    BT = SAMPLE_BT
    i = pl.program_id(0)
    lo_f = _lane_lo((NB, LANES))

    @pl.when(i == 0)
    def _():
        _project(x_ref[...], win_ref, aux_ref, p_scr)
        tab = rope_ref[...]
        for cc in range(ATT_HEADS // 2):
            g = cc // (ATT_HEADS // ATT_KV_HEADS // 2)
            qp = _rope(p_scr[:, Q_OFF + cc * LANES:Q_OFF + (cc + 1) * LANES], tab) * ATT_SCALE
            for half in range(2):
                t = jnp.where(lo_f if half == 0 else jnp.logical_not(lo_f), qp, 0.0)
                if half != g:
                    t = pltpu.roll(t, HALF, 1)
                q_scr[pl.ds(2 * cc + half, NB, stride=ATT_HEADS), :] = t
        k0 = _rope(p_scr[:, K_OFF:K_OFF + LANES], tab)
        k1 = _rope(p_scr[:, K_OFF + LANES:K_OFF + 2 * LANES], tab)
        kn_scr[...] = jnp.where(lo_f, k0, k1)
        vn_scr[...] = jnp.where(lo_f, p_scr[:, V_OFF:V_OFF + LANES], p_scr[:, V_OFF + LANES:V_OFF + 2 * LANES])
        xin = p_scr[:, XL_OFF:XL_OFF + LRU_WIDTH]
        xl = _conv_step(xin, lc_ref, lco_ref, lcw_ref, lvec_ref[0:1, :])
        gates = _dot(xl.astype(BF16), wg_ref[...])
        a, bt = _lru_coeffs(xl, gates, lvec_ref)
        h1 = a * lh_ref[...] + bt
        lho_ref[...] = h1
        mix_scr[:, MIX_LRU:MIX_LRU + LRU_WIDTH] = (h1 * _silu(p_scr[:, GL_OFF:GL_OFF + LRU_WIDTH])).astype(BF16)
        xin2 = p_scr[:, XBC_OFF:XBC_OFF + SSD_CONV_CH]
        xc = _conv_step(xin2, sc_ref, sco_ref, scw_ref, scb_ref[0:1, :])
        p_scr[:, XBC_OFF:XBC_OFF + SSD_CONV_CH] = _silu(xc)
        dt = _softplus(p_scr[:, DT_OFF:DT_OFF + LANES] + hvec_ref[0:1, :])
        da_scr[...] = jnp.exp(dt * (-jnp.exp(hvec_ref[1:2, :])))
        for p in range(N_SSD_TILES):
            dte = jnp.where(lo_f, jnp.broadcast_to(dt[:, 2 * p:2 * p + 1], (NB, LANES)),
                            jnp.broadcast_to(dt[:, 2 * p + 1:2 * p + 2], (NB, LANES)))
            xdt_scr[:, p * LANES:(p + 1) * LANES] = p_scr[:, XBC_OFF + p * LANES:XBC_OFF + (p + 1) * LANES] * dte

    r8 = pl.multiple_of(i * BT, BT)
    kn_blk = kn_scr[pl.ds(r8, BT), :]
    vn_blk = vn_scr[pl.ds(r8, BT), :]
    zpad = jnp.zeros((LANES - BT, LANES), F32)
    kn_t = jnp.concatenate([kn_blk, zpad], axis=0).T
    vn_t = jnp.concatenate([vn_blk, zpad], axis=0).T
    newest = lax.broadcasted_iota(jnp.int32, (ATT_KV_WIDTH, WINDOW), 1) == WINDOW - 1
    da_t =jnp.concatenate([da_scr[pl.ds(r8, BT), :], zpad], axis=0).T
    x_t = [jnp.concatenate([xdt_scr[pl.ds(r8, BT), p * LANES:(p + 1) * LANES], zpad], axis=0).T
           for p in range(N_SSD_TILES)]
    b_blk = p_scr[pl.ds(r8, BT), B_OFF:B_OFF + SSD_GROUPS * SSD_STATE]
    c_blk = p_scr[pl.ds(r8, BT), C_OFF:C_OFF + SSD_GROUPS * SSD_STATE].astype(BF16)
    row8 = lax.broadcasted_iota(jnp.int32, (ATT_HEADS, 1), 0)
    sink = jnp.zeros((ATT_HEADS, 1), F32)
    for h in range(ATT_HEADS):
        sink = jnp.where(row8 == h, sinks_ref[h], sink)
    rowb = lax.broadcasted_iota(jnp.int32, (BT, GROUP_W), 0)
    y_acc = [jnp.zeros((BT, GROUP_W), F32) for _ in range(SSD_GROUPS)]
    rowk = lax.broadcasted_iota(jnp.int32, (BT, SSD_STATE), 0)
    outer = []
    for g in range(SSD_GROUPS):
        bg = b_blk[:, g * SSD_STATE:(g + 1) * SSD_STATE]
        diag = jnp.concatenate([jnp.where(rowk == bb, bg, 0.0) for bb in range(BT)], axis=1)
        rhs = jnp.concatenate([diag, jnp.zeros((LANES - BT, BT * SSD_STATE), F32)], axis=0).astype(BF16)
        outer.append([_dot(x_t[g * TILES_PER_GROUP + pp].astype(BF16), rhs) for pp in range(TILES_PER_GROUP)])
    for bb in range(BT):
        qr = q_scr[pl.ds(pl.multiple_of((r8 + bb) * ATT_HEADS, ATT_HEADS), ATT_HEADS), :]
        kt = ck_ref[bb]
        vt = cv_ref[bb]
        s = _dot(qr.astype(BF16), kt.astype(BF16))
        s_new = jnp.sum(qr * kn_blk[bb:bb + 1, :], -1, keepdims=True)
        m = jnp.maximum(jnp.maximum(jnp.max(s, -1, keepdims=True), s_new), sink)
        e = jnp.exp(s - m)
        e_new = jnp.exp(s_new - m)
        den = jnp.sum(e, -1, keepdims=True) + e_new + jnp.exp(sink - m)
        o = _dot_nt(e.astype(BF16), vt.astype(BF16)) + e_new * vn_blk[bb:bb + 1, :]
        o_scr[pl.ds(pl.multiple_of((r8 + bb) * ATT_HEADS, ATT_HEADS), ATT_HEADS), :] = o * (1.0 / den)
        cko_ref[bb] = jnp.where(newest, kn_t[:, bb:bb + 1], pltpu.roll(kt, WINDOW - 1, 1))
        cvo_ref[bb] = jnp.where(newest, vn_t[:, bb:bb + 1], pltpu.roll(vt, WINDOW - 1, 1))
        for g in range(SSD_GROUPS):
            tiles = []
            for pp in range(TILES_PER_GROUP):
                p = g * TILES_PER_GROUP + pp
                for hh in range(2):
                    h = 2 * p + hh
                    dab = jnp.broadcast_to(da_t[h:h + 1, bb:bb + 1], (SSD_HEAD_DIM, SSD_STATE))
                    h1 = sh_ref[bb, h] * dab + outer[g][pp][hh * SSD_HEAD_DIM:(hh + 1) * SSD_HEAD_DIM,
                                                               bb * SSD_STATE:(bb + 1) * SSD_STATE]
                    sho_ref[bb, h] = h1
                    tiles.append(h1.astype(BF16))
            res = _dot_nt(c_blk[:, g * SSD_STATE:(g + 1) * SSD_STATE], jnp.concatenate(tiles, axis=0))
            y_acc[g] = jnp.where(rowb == bb, res, y_acc[g])
    for g in range(SSD_GROUPS):
        y_scr[pl.ds(r8, BT), g * GROUP_W:(g + 1) * GROUP_W] = y_acc[g]

    @pl.when(i == pl.num_programs(0) - 1)
    def _():
        for cc in range(ATT_HEADS // 2):
            g = cc // (ATT_HEADS // ATT_KV_HEADS // 2)
            oe = o_scr[pl.ds(2 * cc, NB, stride=ATT_HEADS), :]
            oo = o_scr[pl.ds(2 * cc + 1, NB, stride=ATT_HEADS), :]
            if g == 0:
                oo = pltpu.roll(oo, HALF, 1)
            else:
                oe = pltpu.roll(oe, HALF, 1)
            gate = p_scr[:, GA_OFF + cc * LANES:GA_OFF + (cc + 1) * LANES]
            mix_scr[:, cc * LANES:(cc + 1) * LANES] = (jnp.where(lo_f, oe, oo) * _silu(gate)).astype(BF16)
        y = (y_scr[...] + svec_ref[0:1, :] * p_scr[:, XBC_OFF:XBC_OFF + SSD_WIDTH]) * _silu(p_scr[:, Z_OFF:Z_OFF + SSD_WIDTH])
        y = y * lax.rsqrt(jnp.mean(y * y, -1, keepdims=True) + NORM_EPS) * svec_ref[1:2, :]
        mix_scr[:, MIX_SSD:MIX_SSD + SSD_WIDTH] = y.astype(BF16)
        out = _dot(mix_scr[...], wout_ref[...])
        y_ref[...] = _layer_norm(DEEPNORM_ALPHA * x_ref[...] + out, ln_ref[0:1, :], ln_ref[1:2, :])


def _sample_layer(layer, x, rope_tab, sinks, prm, ck, cv, lc, lh, sc, sh, prev):
    NB = x.shape[0]
    BT = SAMPLE_BT
    H = CONV_W - 1
    win, aux, wg, wout, lcw, lvec, scw, scb, hvec, svec, ln = prm
    whole = lambda shape: pl.BlockSpec(shape, lambda i: (0,) * len(shape), pipeline_mode=pl.Buffered(1))
    lwhole = lambda shape: pl.BlockSpec((None,) + shape, lambda i: (layer,) + (0,) * len(shape),
                                        pipeline_mode=pl.Buffered(1))
    rows = lambda shape: pl.BlockSpec((None, BT) + shape, lambda i: (layer, i) + (0,) * len(shape))
    state_specs = [
        rows((ATT_KV_WIDTH, WINDOW)),
        rows((ATT_KV_WIDTH, WINDOW)),
        lwhole((H, NB, LRU_WIDTH)),
        lwhole((NB, LRU_WIDTH)),
        lwhole((H, NB, SSD_CONV_CH)),
        rows((SSD_HEADS, SSD_HEAD_DIM, SSD_STATE)),
    ]
    in_specs = [
        pl.BlockSpec(memory_space=pltpu.SMEM),
        whole((NB, D_MODEL)),
        whole((1, 3 * LANES)),
        _const_spec((D_MODEL, W_IN_COLS), layer),
        _const_spec((D_MODEL, AUX_COLS), layer),
        _const_spec((LRU_WIDTH, 2 * LRU_WIDTH), layer),
        _const_spec((D_MIX, D_MODEL), layer),
        _const_spec((CONV_W, LRU_WIDTH), layer),
        _const_spec((4, LRU_WIDTH), layer),
        _const_spec((CONV_W, SSD_CONV_CH), layer),
        _const_spec((1, SSD_CONV_CH), layer),
        _const_spec((2, LANES), layer),
        _const_spec((2, SSD_WIDTH), layer),
        _const_spec((2, D_MODEL), layer),
    ] + state_specs
    operands = [sinks, x, rope_tab, win, aux, wg, wout, lcw, lvec, scw, scb, hvec, svec, ln, ck, cv, lc, lh, sc, sh]
    aliases = {}
    if prev is not None:
        for k, arr in enumerate(prev):
            aliases[len(operands)] = 1 + k
            operands.append(arr)
            in_specs.append(pl.BlockSpec(memory_space=pl.ANY))
    out_shape = (jax.ShapeDtypeStruct((NB, D_MODEL), F32),) + tuple(
        jax.ShapeDtypeStruct(a.shape, F32) for a in (ck, cv, lc, lh, sc, sh))
    full = lambda shape: pl.BlockSpec(shape, lambda i: (0,) * len(shape))
    lfull = lambda shape: pl.BlockSpec((None,) + shape, lambda i: (layer,) + (0,) * len(shape))
    out_specs = (
        full((NB, D_MODEL)),
        rows((ATT_KV_WIDTH, WINDOW)),
        rows((ATT_KV_WIDTH, WINDOW)),
        lfull((H, NB, LRU_WIDTH)),
        lfull((NB, LRU_WIDTH)),
        lfull((H, NB, SSD_CONV_CH)),
        rows((SSD_HEADS, SSD_HEAD_DIM, SSD_STATE)),
    )
    scratch = [
        pltpu.VMEM((NB, N_COLS), F32),
        pltpu.VMEM((NB * ATT_HEADS, LANES), F32),
        pltpu.VMEM((NB * ATT_HEADS, LANES), F32),
        pltpu.VMEM((NB, ATT_KV_WIDTH), F32),
        pltpu.VMEM((NB, ATT_KV_WIDTH), F32),
        pltpu.VMEM((NB, SSD_WIDTH), F32),
        pltpu.VMEM((NB, LANES), F32),
        pltpu.VMEM((NB, SSD_WIDTH), F32),
        pltpu.VMEM((NB, D_MIX), BF16),
    ]
    return pl.pallas_call(
        functools.partial(_sample_kernel, len(aliases)),
        grid=(NB // BT,),
        in_specs=in_specs,
        out_specs=out_specs,
        out_shape=out_shape,
        scratch_shapes=scratch,
        input_output_aliases=aliases,
        compiler_params=pltpu.CompilerParams(
            dimension_semantics=("arbitrary",),
            vmem_limit_bytes=VMEM_LIMIT_BYTES),
        name=f"sample_layer{layer}",
    )(*operands)


def _rope_table(pos):
    half = ROPE_DIM // 2
    inv = ROPE_THETA ** (-jnp.arange(half, dtype=F32) / half)
    ang = pos.astype(F32)[:, None] * inv[None, :]
    cos, sin = jnp.cos(ang), jnp.sin(ang)
    d = jnp.arange(LANES) % ATT_HEAD_DIM
    fi = d % half
    c = jnp.where(d < ROPE_DIM, cos[:, fi], 1.0)
    s1 = jnp.where(d < half, -sin[:, fi], 0.0)
    s2 = jnp.where((d >= half) & (d < ROPE_DIM), sin[:, fi], 0.0)
    return jnp.concatenate([c, s1, s2], axis=1)


def _prep_params(w_in, w_out, lru_conv_w, lru_conv_b, lru_wa, lru_ba, lru_wx, lru_bx, lru_lambda,
                 ssd_conv_w, ssd_conv_b, ssd_dt_bias, ssd_a_log, ssd_d, ssd_norm_g, ln_g, ln_b):
    hd = ATT_HEAD_DIM
    win = w_in.astype(BF16)
    k = w_in[..., ATT_WIDTH:ATT_WIDTH + ATT_KV_WIDTH]
    v = w_in[..., ATT_WIDTH + ATT_KV_WIDTH:ATT_WIDTH + 2 * ATT_KV_WIDTH]
    dup = lambda t: jnp.concatenate([t[..., :hd], t[..., :hd], t[..., hd:], t[..., hd:]], -1)
    dtw = jnp.pad(w_in[..., W_IN_COLS - SSD_HEADS:], ((0, 0), (0, 0), (0, LANES - SSD_HEADS)))
    aux = jnp.concatenate([dup(k), dup(v), dtw], -1).astype(BF16)

    def dense(w):
        rows = [jnp.pad(w[:, n], ((0, 0), (0, 0), (n * LRU_BLOCK, LRU_WIDTH - (n + 1) * LRU_BLOCK)))
                for n in range(LRU_BLOCKS)]
        return jnp.concatenate(rows, 1)

    wg = jnp.concatenate([dense(lru_wa.astype(BF16)), dense(lru_wx.astype(BF16))], -1)
    wout = w_out.astype(BF16)
    lvec = jnp.stack([lru_conv_b, lru_ba, lru_bx, lru_lambda], 1)
    hpad = lambda t: jnp.pad(t, ((0, 0), (0, LANES - SSD_HEADS)))
    hvec = jnp.stack([hpad(ssd_dt_bias), hpad(ssd_a_log)], 1)
    svec = jnp.stack([jnp.repeat(ssd_d, SSD_HEAD_DIM, axis=1), ssd_norm_g], 1)
    ln = jnp.stack([ln_g, ln_b], 1)
    return (win, aux, wg, wout, lru_conv_w, lvec, ssd_conv_w, ssd_conv_b[:, None, :], hvec, svec, ln)


def kernel(x_prompt, x_sample, cache_swa_k, cache_swa_v, state_lru_conv, state_lru_h, state_ssd_conv, state_ssd_h, w_in, w_out, att_sinks, lru_conv_w, lru_conv_b, lru_wa, lru_ba, lru_wx, lru_bx, lru_lambda, ssd_conv_w, ssd_conv_b, ssd_dt_bias, ssd_a_log, ssd_d, ssd_norm_g, ln_g, ln_b):
    prm = _prep_params(w_in, w_out, lru_conv_w, lru_conv_b, lru_wa, lru_ba, lru_wx, lru_bx, lru_lambda,
                       ssd_conv_w, ssd_conv_b, ssd_dt_bias, ssd_a_log, ssd_d, ssd_norm_g, ln_g, ln_b)
    bp, lp, _ = x_prompt.shape
    rope_p = _rope_table(jnp.arange(lp, dtype=jnp.int32))
    xp = x_prompt
    new_p = [[] for _ in range(6)]
    for l in range(DEPTH):
        xp, ko, vo, lco, lho, sco, sho = _prompt_layer(l, xp, rope_p, att_sinks[l], prm)
        st = (ko.reshape(bp, WINDOW, ATT_KV_HEADS, ATT_HEAD_DIM), vo.reshape(bp, WINDOW, ATT_KV_HEADS, ATT_HEAD_DIM),
              lco, lho.reshape(bp, LRU_WIDTH), sco, sho)
        for lst, t in zip(new_p, st):
            lst.append(t)
    outs_p = [jnp.stack(t) for t in new_p]

    nb = x_sample.shape[0]
    rope_s = _rope_table(PAST_LEN + jnp.arange(x_sample.shape[1], dtype=jnp.int32))
    to_kt = lambda t: jnp.transpose(t, (0, 1, 3, 4, 2)).reshape(DEPTH, nb, ATT_KV_WIDTH, WINDOW)
    from_kt = lambda t: jnp.transpose(t.reshape(DEPTH, nb, ATT_KV_HEADS, ATT_HEAD_DIM, WINDOW), (0, 1, 4, 2, 3))
    swap = lambda t: jnp.transpose(t, (0, 2, 1, 3))
    state = (to_kt(cache_swa_k), to_kt(cache_swa_v), swap(state_lru_conv), state_lru_h, swap(state_ssd_conv),
             state_ssd_h)
    xs = x_sample.reshape(nb, D_MODEL)
    new = None
    for l in range(DEPTH):
        xs, *new = _sample_layer(l, xs, rope_s, att_sinks[l], prm, *state, new)
    cko, cvo, lco, lho, sco, sho = new
    outs_s = (from_kt(cko), from_kt(cvo), swap(lco), lho, swap(sco), sho)
    return (xp, xs.reshape(x_sample.shape)) + tuple(outs_p) + tuple(outs_s)
```

```python
import functools
import math

import jax
import jax.numpy as jnp
import numpy as np
from jax import lax
from jax.experimental import pallas as pl
from jax.experimental.pallas import tpu as pltpu

F32 = jnp.float32
BF16 = jnp.bfloat16

D_MODEL = 1024
DEPTH = 4
PAST_LEN = 8192
D_MIX = 2 * D_MODEL
ATT_HEADS = 8
ATT_KV_HEADS = 2
ATT_HEAD_DIM = 64
ATT_WIDTH = ATT_HEADS * ATT_HEAD_DIM
ATT_KV_WIDTH = ATT_KV_HEADS * ATT_HEAD_DIM
ATT_SCALE = ATT_HEAD_DIM ** -0.5
WINDOW = 128
ROPE_THETA = 500000.0
ROPE_DIM = ATT_HEAD_DIM // 4
LRU_WIDTH = 3 * D_MIX // 8
LRU_BLOCKS = 8
LRU_BLOCK = LRU_WIDTH // LRU_BLOCKS
LRU_C = 8.0
CONV_W = 4
SSD_WIDTH = D_MIX - ATT_WIDTH - LRU_WIDTH
SSD_HEAD_DIM = 64
SSD_HEADS = SSD_WIDTH // SSD_HEAD_DIM
SSD_GROUPS = 2
SSD_STATE = 128
SSD_CONV_CH = SSD_WIDTH + 2 * SSD_GROUPS * SSD_STATE
DEEPNORM_ALPHA = (2.0 * DEPTH) ** 0.25
NORM_EPS = 1e-5

LANES = 128
SUBLANES = 8
HALF = LANES // 2
VMEM_LIMIT_BYTES = 60 * 1024 * 1024

Q_OFF = 0
K_OFF = Q_OFF + ATT_WIDTH
V_OFF = K_OFF + 2 * ATT_KV_WIDTH
GA_OFF = V_OFF + 2 * ATT_KV_WIDTH
XL_OFF = GA_OFF + ATT_WIDTH
GL_OFF = XL_OFF + LRU_WIDTH
Z_OFF = GL_OFF + LRU_WIDTH
XBC_OFF = Z_OFF + SSD_WIDTH
DT_OFF = XBC_OFF + SSD_CONV_CH
N_COLS = DT_OFF + LANES
W_IN_COLS = 2 * ATT_WIDTH + 2 * ATT_KV_WIDTH + 2 * LRU_WIDTH + SSD_WIDTH + SSD_CONV_CH + SSD_HEADS
AUX_COLS = 4 * ATT_KV_WIDTH + LANES
B_OFF = XBC_OFF + SSD_WIDTH
C_OFF = B_OFF + SSD_GROUPS * SSD_STATE
MIX_LRU = ATT_WIDTH
MIX_SSD = ATT_WIDTH + LRU_WIDTH

N_LRU_TILES = LRU_WIDTH // LANES
N_SSD_TILES = SSD_WIDTH // LANES
N_CONV_TILES = SSD_CONV_CH // LANES
CONV_STRIDE = 4
TILES_PER_GROUP = N_SSD_TILES // SSD_GROUPS
GROUP_W = SSD_WIDTH // SSD_GROUPS
HEAD_ROWS = 16

NEG_BIG = -1e30

PROMPT_TILE = 256
SAMPLE_BT = 8


def _sigmoid(x):
    return 0.5 * jnp.tanh(0.5 * x) + 0.5


def _silu(x):
    h = 0.5 * x
    return h + h * jnp.tanh(h)


def _softplus(x):
    return jnp.maximum(x, 0.0) + jnp.log1p(jnp.exp(-jnp.abs(x)))


def _lane_lo(shape):
    return (lax.broadcasted_iota(jnp.int32, shape, len(shape) - 1) % LANES) < HALF


def _rope(t, tab):
    half = ROPE_DIM // 2
    return (t * tab[:, 0:LANES]
            + pltpu.roll(t, LANES - half, 1) * tab[:, LANES:2 * LANES]
            + pltpu.roll(t, half, 1) * tab[:, 2 * LANES:3 * LANES])


def _dot(a, b):
    return jnp.dot(a, b, preferred_element_type=F32)


def _dot_nt(a, b):
    return lax.dot_general(a, b, (((1,), (1,)), ((), ())), preferred_element_type=F32)


def _conv_classes(src_ref, j, rows, w_ref, bias, cs):
    n = CONV_STRIDE
    hist = CONV_W - 1
    taps = {s_: src_ref[j, pl.ds(SUBLANES + s_, rows // n, stride=n), :] for s_ in range(-hist, n)}
    outs = []
    for k in range(n):
        acc = bias + w_ref[hist:hist + 1, cs] * taps[k]
        for t in range(hist):
            acc = acc + w_ref[t:t + 1, cs] * taps[k - hist + t]
        outs.append(acc)
    return outs


def _layer_norm(v, g, b):
    mu = jnp.mean(v, -1, keepdims=True)
    d = v - mu
    var = jnp.mean(d * d, -1, keepdims=True)
    return d * lax.rsqrt(var + NORM_EPS) * g + b


def _lru_coeffs(xl, gates, lvec_ref):
    r = _sigmoid(gates[:, 0:LRU_WIDTH] + lvec_ref[1:2, :])
    ig = _sigmoid(gates[:, LRU_WIDTH:2 * LRU_WIDTH] + lvec_ref[2:3, :])
    log_a = (-LRU_C * _softplus(-lvec_ref[3:4, :])) * r
    a = jnp.exp(log_a)
    bt = jnp.sqrt(-jnp.tanh(log_a) * (1.0 + a * a)) * (ig * xl)
    return a, bt


def _project(x_tile, win_ref, aux_ref, dst):
    xb = x_tile.astype(BF16)
    kv0 = ATT_WIDTH + 2 * ATT_KV_WIDTH
    dst[:, Q_OFF:K_OFF] = _dot(xb, win_ref[:, 0:ATT_WIDTH])
    dst[:, K_OFF:GA_OFF] = _dot(xb, aux_ref[:, 0:GA_OFF - K_OFF])
    dst[:, GA_OFF:Z_OFF] = _dot(xb, win_ref[:, kv0:kv0 + Z_OFF - GA_OFF])
    dst[:, Z_OFF:DT_OFF] = _dot(xb, win_ref[:, kv0 + Z_OFF - GA_OFF:kv0 + DT_OFF - GA_OFF])
    dst[:, DT_OFF:N_COLS] = _dot(xb, aux_ref[:, GA_OFF - K_OFF:GA_OFF - K_OFF + LANES])


def _prompt_kernel(sinks_ref, x_ref, rope_ref, win_ref, aux_ref, wg_ref, wout_ref, lcw_ref, lvec_ref,
                   scw_ref, scb_ref, hcol_ref, svec_ref, ln_ref,
                   y_ref, ko_ref, vo_ref, lco_ref, lho_ref, sco_ref, sho_ref,
                   p_scr, kd_scr, vd_scr, lx_scr, sx_scr, sa_scr, sb_scr, hl_scr, ht_scr, mix_scr, xlb_scr, xc_scr):
    T = PROMPT_TILE
    nsub = T // WINDOW
    c = pl.program_id(1)
    last = c == pl.num_programs(1) - 1

    @pl.when(c == 0)
    def _():
        kd_scr[:, 0:WINDOW, :] = jnp.zeros((2, WINDOW, LANES), BF16)
        vd_scr[:, 0:WINDOW, :] = jnp.zeros((2, WINDOW, LANES), BF16)
        lx_scr[:, 0:SUBLANES, :] = jnp.zeros((N_LRU_TILES, SUBLANES, LANES), F32)
        sx_scr[:, 0:SUBLANES, :] = jnp.zeros((N_CONV_TILES, SUBLANES, LANES), F32)
        hl_scr[...] = jnp.zeros_like(hl_scr)
        ht_scr[...] = jnp.zeros_like(ht_scr)

    _project(x_ref[...], win_ref, aux_ref, p_scr)

    lo_sq = _lane_lo((WINDOW, LANES))

    for g in range(ATT_KV_HEADS):
        kd = _rope(p_scr[:, K_OFF + g * LANES:K_OFF + (g + 1) * LANES], rope_ref[...])
        kd_scr[g, WINDOW:WINDOW + T, :] = kd.astype(BF16)
        vd_scr[g, WINDOW:WINDOW + T, :] = p_scr[:, V_OFF + g * LANES:V_OFF + (g + 1) * LANES].astype(BF16)

    row = lax.broadcasted_iota(jnp.int32, (WINDOW, 2 * WINDOW), 0)
    col = lax.broadcasted_iota(jnp.int32, (WINDOW, 2 * WINDOW), 1)
    band = (col >= row) & (col <= row + WINDOW)
    first_lo = jnp.where(c > 0, 0, WINDOW)
    for i in range(nsub):
        r0 = i * WINDOW
        msk = (band & (col >= first_lo)) if i == 0 else band
        tab = rope_ref[r0:r0 + WINDOW, :]
        for cc in range(ATT_HEADS // 2):
            g = cc // (ATT_HEADS // ATT_KV_HEADS // 2)
            kg = kd_scr[g, r0:r0 + 2 * WINDOW, :]
            vg = vd_scr[g, r0:r0 + 2 * WINDOW, :]
            qp = _rope(p_scr[r0:r0 + WINDOW, Q_OFF + cc * LANES:Q_OFF + (cc + 1) * LANES], tab) * ATT_SCALE
            outs = []
            for half in range(2):
                sink = sinks_ref[2 * cc + half]
                qm = jnp.where(lo_sq if half == 0 else jnp.logical_not(lo_sq), qp, 0.0).astype(BF16)
                s = jnp.where(msk, _dot_nt(qm, kg), NEG_BIG)
                m = jnp.maximum(jnp.max(s, -1, keepdims=True), sink)
                e = jnp.exp(s - m)
                den = jnp.sum(e, -1, keepdims=True) + jnp.exp(sink - m)
                outs.append(_dot(e.astype(BF16), vg) * (1.0 / den))
            att = jnp.where(lo_sq, outs[0], outs[1])
            gate = p_scr[r0:r0 + WINDOW, GA_OFF + cc * LANES:GA_OFF + (cc + 1) * LANES]
            mix_scr[r0:r0 + WINDOW, cc * LANES:(cc + 1) * LANES] = (att * _silu(gate)).astype(BF16)
    for g in range(ATT_KV_HEADS):
        kd_scr[g, 0:WINDOW, :] = kd_scr[g, T:T + WINDOW, :]
        vd_scr[g, 0:WINDOW, :] = vd_scr[g, T:T + WINDOW, :]

    P = SUBLANES
    H = CONV_W - 1
    for j in range(N_LRU_TILES):
        cs_ = slice(j * LANES, (j + 1) * LANES)
        lx_scr[j, P:P + T, :] = p_scr[:, XL_OFF + j * LANES:XL_OFF + (j + 1) * LANES]
        for k, xl in enumerate(_conv_classes(lx_scr, j, T, lcw_ref, lvec_ref[0:1, cs_], cs_)):
            sb_scr[j, pl.ds(k, T // CONV_STRIDE, stride=CONV_STRIDE), :] = xl
        xlb_scr[:, cs_] = sb_scr[j].astype(BF16)
    lx_scr[:, 0:P, :] = lx_scr[:, T:T + P, :]

    G = T // SUBLANES
    row_g = lax.broadcasted_iota(jnp.int32, (G, LANES), 0)
    cl = -LRU_C * _softplus(-lvec_ref[3:4, :])
    for jj in range(N_LRU_TILES // 2):
        c0 = jj * 2 * LANES
        k0 = (c0 // LRU_BLOCK) * LRU_BLOCK // LANES * LANES
        k1 = -(-(-(-(c0 + 2 * LANES) // LRU_BLOCK) * LRU_BLOCK) // LANES) * LANES
        xk = xlb_scr[:, k0:k1]
        gr = _dot(xk, wg_ref[k0:k1, c0:c0 + 2 * LANES])
        gi = _dot(xk, wg_ref[k0:k1, LRU_WIDTH + c0:LRU_WIDTH + c0 + 2 * LANES])
        for u in range(2):
            j = 2 * jj + u
            cs_ = slice(j * LANES, (j + 1) * LANES)
            us = slice(u * LANES, (u + 1) * LANES)
            r = _sigmoid(gr[:, us] + lvec_ref[1:2, cs_])
            ig = _sigmoid(gi[:, us] + lvec_ref[2:3, cs_])
            log_a = cl[:, cs_] * r
            a = jnp.exp(log_a)
            sa_scr[j] = a
            sb_scr[j] = jnp.sqrt(-jnp.tanh(log_a) * (1.0 + a * a)) * (ig * sb_scr[j])
    for j in range(N_LRU_TILES):
        ca = sa_scr[j, pl.ds(0, G, stride=SUBLANES), :]
        cb = sb_scr[j, pl.ds(0, G, stride=SUBLANES), :]
        for k in range(1, SUBLANES):
            ak = sa_scr[j, pl.ds(k, G, stride=SUBLANES), :]
            cb = ak * cb + sb_scr[j, pl.ds(k, G, stride=SUBLANES), :]
            ca = ak * ca
            sa_scr[j, pl.ds(k, G, stride=SUBLANES), :] = ca
            sb_scr[j, pl.ds(k, G, stride=SUBLANES), :] = cb
        hin = hl_scr[0:1, j * LANES:(j + 1) * LANES]
        cb = cb + jnp.where(row_g == 0, ca * hin, 0.0)
        d = 1
        while d < G:
            ok = row_g >= d
            cb = cb + ca * jnp.where(ok, pltpu.roll(cb, d, 0), 0.0)
            ca = ca * jnp.where(ok, pltpu.roll(ca, d, 0), 1.0)
            d *= 2
        hprev = jnp.where(row_g == 0, hin, pltpu.roll(cb, 1, 0))
        for k in range(SUBLANES):
            sb_scr[j, pl.ds(k, G, stride=SUBLANES), :] = (
                sb_scr[j, pl.ds(k, G, stride=SUBLANES), :] + sa_scr[j, pl.ds(k, G, stride=SUBLANES), :] * hprev)
        hl_scr[0:1, j * LANES:(j + 1) * LANES] = cb[G - 1:G, :]
        gl = p_scr[:, GL_OFF + j * LANES:GL_OFF + (j + 1) * LANES]
        mix_scr[:, MIX_LRU + j * LANES:MIX_LRU + (j + 1) * LANES] = (sb_scr[j] * _silu(gl)).astype(BF16)

    for j in range(N_CONV_TILES):
        cs_ = slice(j * LANES, (j + 1) * LANES)
        sx_scr[j, P:P + T, :] = p_scr[:, XBC_OFF + j * LANES:XBC_OFF + (j + 1) * LANES]
        for k, xc in enumerate(_conv_classes(sx_scr, j, T, scw_ref, scb_ref[0:1, cs_], cs_)):
            xc_scr[j, pl.ds(k, T // CONV_STRIDE, stride=CONV_STRIDE), :] = _silu(xc)
    sx_scr[:, 0:P, :] = sx_scr[:, T:T + P, :]

    qi = lax.broadcasted_iota(jnp.int32, (WINDOW, WINDOW), 0)
    si = lax.broadcasted_iota(jnp.int32, (WINDOW, WINDOW), 1)
    causal = si <= qi
    upper = (qi <= si).astype(F32)
    hpad = jnp.zeros((WINDOW - HEAD_ROWS, WINDOW), F32)
    a_col = -jnp.exp(hcol_ref[1])
    for i in range(nsub):
        r0 = i * WINDOW
        dt_t = _softplus(p_scr[r0:r0 + WINDOW, DT_OFF:DT_OFF + LANES].T[0:HEAD_ROWS, :] + hcol_ref[0])
        cst = jnp.dot(dt_t * a_col, upper, precision=lax.Precision.HIGHEST, preferred_element_type=F32)
        dt_c = jnp.concatenate([dt_t, hpad], axis=0).T
        cs = jnp.concatenate([cst, hpad], axis=0).T
        ys = [None] * N_SSD_TILES
        for g in range(SSD_GROUPS):
            bg_t = xc_scr[N_SSD_TILES + g, r0:r0 + WINDOW, :].T.astype(BF16)
            cg = xc_scr[N_SSD_TILES + SSD_GROUPS + g, r0:r0 + WINDOW, :].astype(BF16)
            cbm = _dot(cg, bg_t)
            yo = _dot(cg, ht_scr[g].astype(BF16))
            xdec, edec = [], []
            for pp in range(TILES_PER_GROUP):
                p = g * TILES_PER_GROUP + pp
                bcs, mm, dts = [], [], []
                for h in (2 * p, 2 * p + 1):
                    bc = jnp.broadcast_to(cs[:, h:h + 1], (WINDOW, WINDOW))
                    lmat = jnp.exp(jnp.where(causal, bc - cst[h:h + 1, :], NEG_BIG))
                    mm.append((cbm * lmat).astype(BF16))
                    bcs.append(bc)
                    dts.append(jnp.broadcast_to(dt_c[:, h:h + 1], (WINDOW, LANES)))
                csl = jnp.where(lo_sq, bcs[0], bcs[1])
                ecs = jnp.exp(csl)
                dec = jnp.exp(csl[WINDOW - 1:WINDOW, :] - csl)
                xs_p = xc_scr[p, r0:r0 + WINDOW, :]
                xdt = xs_p * jnp.where(lo_sq, dts[0], dts[1])
                xdt_b = xdt.astype(BF16)
                y = jnp.where(lo_sq, _dot(mm[0], xdt_b), _dot(mm[1], xdt_b))
                y = y + yo[:, pp * LANES:(pp + 1) * LANES] * ecs
                y = y + svec_ref[0:1, p * LANES:(p + 1) * LANES] * xs_p
                ys[p] = y * _silu(p_scr[r0:r0 + WINDOW, Z_OFF + p * LANES:Z_OFF + (p + 1) * LANES])
                xdec.append((xdt * dec).astype(BF16))
                edec.append(ecs[WINDOW - 1:WINDOW, :])
            ht_scr[g] = (ht_scr[g] * jnp.concatenate(edec, axis=1)
                         + _dot(bg_t, jnp.concatenate(xdec, axis=1)))
        ss = jnp.sum(ys[0] * ys[0], -1, keepdims=True)
        for p in range(1, N_SSD_TILES):
            ss = ss + jnp.sum(ys[p] * ys[p], -1, keepdims=True)
        rinv = lax.rsqrt(ss * (1.0 / SSD_WIDTH) + NORM_EPS)
        for p in range(N_SSD_TILES):
            mix_scr[r0:r0 + WINDOW, MIX_SSD + p * LANES:MIX_SSD + (p + 1) * LANES] = (
                ys[p] * rinv * svec_ref[1:2, p * LANES:(p + 1) * LANES]).astype(BF16)

    out = _dot(mix_scr[...], wout_ref[...])
    y_ref[...] = _layer_norm(DEEPNORM_ALPHA * x_ref[...] + out, ln_ref[0:1, :], ln_ref[1:2, :])

    @pl.when(last)
    def _():
        tab = rope_ref[T - WINDOW:T, :]
        k0 = _rope(p_scr[T - WINDOW:T, K_OFF:K_OFF + LANES], tab)
        k1 = _rope(p_scr[T - WINDOW:T, K_OFF + LANES:K_OFF + 2 * LANES], tab)
        ko_ref[...] = jnp.where(lo_sq, k0, k1)
        vo_ref[...] = jnp.where(lo_sq, p_scr[T - WINDOW:T, V_OFF:V_OFF + LANES],
                                p_scr[T - WINDOW:T, V_OFF + LANES:V_OFF + 2 * LANES])
        for j in range(N_LRU_TILES):
            lco_ref[:, j * LANES:(j + 1) * LANES] = lx_scr[j, P + T - (CONV_W - 1):P + T, :]
        lho_ref[...] = hl_scr[...]
        for j in range(N_CONV_TILES):
            sco_ref[:, j * LANES:(j + 1) * LANES] = sx_scr[j, P + T - (CONV_W - 1):P + T, :]
        for g in range(SSD_GROUPS):
            for pp in range(TILES_PER_GROUP):
                tile = ht_scr[g, :, pp * LANES:(pp + 1) * LANES].T
                h = 2 * (g * TILES_PER_GROUP + pp)
                sho_ref[h] = tile[0:SSD_HEAD_DIM, :]
                sho_ref[h + 1] = tile[SSD_HEAD_DIM:2 * SSD_HEAD_DIM, :]


def _const_spec(shape, layer):
    nd = len(shape)
    return pl.BlockSpec((None,) + tuple(shape), lambda *_: (layer,) + (0,) * nd,
                        pipeline_mode=pl.Buffered(1))


def _prompt_layer(layer, x, rope_tab, sinks, prm):
    B, L, _ = x.shape
    T = PROMPT_TILE
    nc = L // T
    win, aux, wg, wout, lcw, lvec, scw, scb, hvec, svec, ln, hcol = prm
    tile = lambda b, c: (b, c, 0)
    per_b = lambda b, c: (b, 0, 0)

    in_specs = [
        pl.BlockSpec(memory_space=pltpu.SMEM),
        pl.BlockSpec((None, T, D_MODEL), tile),
        pl.BlockSpec((T, 3 * LANES), lambda b, c: (c, 0)),
        _const_spec((D_MODEL, W_IN_COLS), layer),
        _const_spec((D_MODEL, AUX_COLS), layer),
        _const_spec((LRU_WIDTH, 2 * LRU_WIDTH), layer),
        _const_spec((D_MIX, D_MODEL), layer),
        _const_spec((CONV_W, LRU_WIDTH), layer),
        _const_spec((4, LRU_WIDTH), layer),
        _const_spec((CONV_W, SSD_CONV_CH), layer),
        _const_spec((1, SSD_CONV_CH), layer),
        _const_spec((2, HEAD_ROWS, LANES), layer),
        _const_spec((2, SSD_WIDTH), layer),
        _const_spec((2, D_MODEL), layer),
    ]
    out_shape = (
        jax.ShapeDtypeStruct((B, L, D_MODEL), F32),
        jax.ShapeDtypeStruct((B, WINDOW, ATT_KV_WIDTH), F32),
        jax.ShapeDtypeStruct((B, WINDOW, ATT_KV_WIDTH), F32),
        jax.ShapeDtypeStruct((B, CONV_W - 1, LRU_WIDTH), F32),
        jax.ShapeDtypeStruct((B, 1, LRU_WIDTH), F32),
        jax.ShapeDtypeStruct((B, CONV_W - 1, SSD_CONV_CH), F32),
        jax.ShapeDtypeStruct((B, SSD_HEADS, SSD_HEAD_DIM, SSD_STATE), F32),
    )
    out_specs = (
        pl.BlockSpec((None, T, D_MODEL), tile),
        pl.BlockSpec((None, WINDOW, ATT_KV_WIDTH), per_b),
        pl.BlockSpec((None, WINDOW, ATT_KV_WIDTH), per_b),
        pl.BlockSpec((None, CONV_W - 1, LRU_WIDTH), per_b),
        pl.BlockSpec((None, 1, LRU_WIDTH), per_b),
        pl.BlockSpec((None, CONV_W - 1, SSD_CONV_CH), per_b),
        pl.BlockSpec((None, SSD_HEADS, SSD_HEAD_DIM, SSD_STATE), lambda b, c: (b, 0, 0, 0)),
    )
    scratch = [
        pltpu.VMEM((T, N_COLS), F32),
        pltpu.VMEM((ATT_KV_HEADS, WINDOW + T, LANES), BF16),
        pltpu.VMEM((ATT_KV_HEADS, WINDOW + T, LANES), BF16),
        pltpu.VMEM((N_LRU_TILES, SUBLANES + T, LANES), F32),
        pltpu.VMEM((N_CONV_TILES, SUBLANES + T, LANES), F32),
        pltpu.VMEM((N_LRU_TILES, T, LANES), F32),
        pltpu.VMEM((N_LRU_TILES, T, LANES), F32),
        pltpu.VMEM((1, LRU_WIDTH), F32),
        pltpu.VMEM((SSD_GROUPS, SSD_STATE, GROUP_W), F32),
        pltpu.VMEM((T, D_MIX), BF16),
        pltpu.VMEM((T, LRU_WIDTH), BF16),
        pltpu.VMEM((N_CONV_TILES, T, LANES), F32),
    ]
    return pl.pallas_call(
        _prompt_kernel,
        grid=(B, nc),
        in_specs=in_specs,
        out_specs=out_specs,
        out_shape=out_shape,
        scratch_shapes=scratch,
        compiler_params=pltpu.CompilerParams(
            dimension_semantics=("arbitrary", "arbitrary"),
            vmem_limit_bytes=VMEM_LIMIT_BYTES),
        name=f"prompt_layer{layer}",
    )(sinks, x, rope_tab, win, aux, wg, wout, lcw, lvec, scw, scb, hcol, svec, ln)


def _conv_step(x_new, st_ref, sto_ref, w_ref, bias):
    y = bias + w_ref[CONV_W - 1:CONV_W, :] * x_new
    for t in range(CONV_W - 1):
        y = y + w_ref[t:t + 1, :] * st_ref[t]
    for t in range(CONV_W - 2):
        sto_ref[t] = st_ref[t + 1]
    sto_ref[CONV_W - 2] = x_new
    return y


def _sample_kernel(n_aliased, sinks_ref, x_ref, rope_ref, win_ref, aux_ref, wg_ref, wout_ref, lcw_ref, lvec_ref,
                   scw_ref, scb_ref, hvec_ref, svec_ref, ln_ref,
                   ck_ref, cv_ref, lc_ref, lh_ref, sc_ref, sh_ref, *refs):
    (y_ref, cko_ref, cvo_ref, lco_ref, lho_ref, sco_ref, sho_ref,
     p_scr, q_scr, o_scr, kn_scr, vn_scr, xdt_scr, da_scr, y_scr, mix_scr) = refs[n_aliased:]
    NB = x_ref.shape[0]
    BT = SAMPLE_BT
    i = pl.program_id(0)
    lo_f = _lane_lo((NB, LANES))

    @pl.when(i == 0)
    def _():
        _project(x_ref[...], win_ref, aux_ref, p_scr)
        tab = rope_ref[...]
        for cc in range(ATT_HEADS // 2):
            g = cc // (ATT_HEADS // ATT_KV_HEADS // 2)
            qp = _rope(p_scr[:, Q_OFF + cc * LANES:Q_OFF + (cc + 1) * LANES], tab) * ATT_SCALE
            for half in range(2):
                t = jnp.where(lo_f if half == 0 else jnp.logical_not(lo_f), qp, 0.0)
                if half != g:
                    t = pltpu.roll(t, HALF, 1)
                q_scr[pl.ds(2 * cc + half, NB, stride=ATT_HEADS), :] = t
        k0 = _rope(p_scr[:, K_OFF:K_OFF + LANES], tab)
        k1 = _rope(p_scr[:, K_OFF + LANES:K_OFF + 2 * LANES], tab)
        kn_scr[...] = jnp.where(lo_f, k0, k1)
        vn_scr[...] = jnp.where(lo_f, p_scr[:, V_OFF:V_OFF + LANES], p_scr[:, V_OFF + LANES:V_OFF + 2 * LANES])
        xin = p_scr[:, XL_OFF:XL_OFF + LRU_WIDTH]
        xl = _conv_step(xin, lc_ref, lco_ref, lcw_ref, lvec_ref[0:1, :])
        gates = _dot(xl.astype(BF16), wg_ref[...])
        a, bt = _lru_coeffs(xl, gates, lvec_ref)
        h1 = a * lh_ref[...] + bt
        lho_ref[...] = h1
        mix_scr[:, MIX_LRU:MIX_LRU + LRU_WIDTH] = (h1 * _silu(p_scr[:, GL_OFF:GL_OFF + LRU_WIDTH])).astype(BF16)
        xin2 = p_scr[:, XBC_OFF:XBC_OFF + SSD_CONV_CH]
        xc = _conv_step(xin2, sc_ref, sco_ref, scw_ref, scb_ref[0:1, :])
        p_scr[:, XBC_OFF:XBC_OFF + SSD_CONV_CH] = _silu(xc)
        dt = _softplus(p_scr[:, DT_OFF:DT_OFF + LANES] + hvec_ref[0:1, :])
        da_scr[...] = jnp.exp(dt * (-jnp.exp(hvec_ref[1:2, :])))
        for p in range(N_SSD_TILES):
            dte = jnp.where(lo_f, jnp.broadcast_to(dt[:, 2 * p:2 * p + 1], (NB, LANES)),
                            jnp.broadcast_to(dt[:, 2 * p + 1:2 * p + 2], (NB, LANES)))
            xdt_scr[:, p * LANES:(p + 1) * LANES] = p_scr[:, XBC_OFF + p * LANES:XBC_OFF + (p + 1) * LANES] * dte

    r8 = pl.multiple_of(i * BT, BT)
    kn_blk = kn_scr[pl.ds(r8, BT), :]
    vn_blk = vn_scr[pl.ds(r8, BT), :]
    zpad = jnp.zeros((LANES - BT, LANES), F32)
    kn_t = jnp.concatenate([kn_blk, zpad], axis=0).T
    vn_t = jnp.concatenate([vn_blk, zpad], axis=0).T
    newest = lax.broadcasted_iota(jnp.int32, (ATT_KV_WIDTH, WINDOW), 1) == WINDOW - 1
    da_t =jnp.concatenate([da_scr[pl.ds(r8, BT), :], zpad], axis=0).T
    x_t = [jnp.concatenate([xdt_scr[pl.ds(r8, BT), p * LANES:(p + 1) * LANES], zpad], axis=0).T
           for p in range(N_SSD_TILES)]
    b_blk = p_scr[pl.ds(r8, BT), B_OFF:B_OFF + SSD_GROUPS * SSD_STATE]
    c_blk = p_scr[pl.ds(r8, BT), C_OFF:C_OFF + SSD_GROUPS * SSD_STATE].astype(BF16)
    row8 = lax.broadcasted_iota(jnp.int32, (ATT_HEADS, 1), 0)
    sink = jnp.zeros((ATT_HEADS, 1), F32)
    for h in range(ATT_HEADS):
        sink = jnp.where(row8 == h, sinks_ref[h], sink)
    rowb = lax.broadcasted_iota(jnp.int32, (BT, GROUP_W), 0)
    y_acc = [jnp.zeros((BT, GROUP_W), F32) for _ in range(SSD_GROUPS)]
    rowk = lax.broadcasted_iota(jnp.int32, (BT, SSD_STATE), 0)
    outer = []
    for g in range(SSD_GROUPS):
        bg = b_blk[:, g * SSD_STATE:(g + 1) * SSD_STATE]
        diag = jnp.concatenate([jnp.where(rowk == bb, bg, 0.0) for bb in range(BT)], axis=1)
        rhs = jnp.concatenate([diag, jnp.zeros((LANES - BT, BT * SSD_STATE), F32)], axis=0).astype(BF16)
        outer.append([_dot(x_t[g * TILES_PER_GROUP + pp].astype(BF16), rhs) for pp in range(TILES_PER_GROUP)])
    for bb in range(BT):
        qr = q_scr[pl.ds(pl.multiple_of((r8 + bb) * ATT_HEADS, ATT_HEADS), ATT_HEADS), :]
        kt = ck_ref[bb]
        vt = cv_ref[bb]
        s = _dot(qr.astype(BF16), kt.astype(BF16))
        s_new = jnp.sum(qr * kn_blk[bb:bb + 1, :], -1, keepdims=True)
        m = jnp.maximum(jnp.maximum(jnp.max(s, -1, keepdims=True), s_new), sink)
        e = jnp.exp(s - m)
        e_new = jnp.exp(s_new - m)
        den = jnp.sum(e, -1, keepdims=True) + e_new + jnp.exp(sink - m)
        o = _dot_nt(e.astype(BF16), vt.astype(BF16)) + e_new * vn_blk[bb:bb + 1, :]
        o_scr[pl.ds(pl.multiple_of((r8 + bb) * ATT_HEADS, ATT_HEADS), ATT_HEADS), :] = o * (1.0 / den)
        cko_ref[bb] = jnp.where(newest, kn_t[:, bb:bb + 1], pltpu.roll(kt, WINDOW - 1, 1))
        cvo_ref[bb] = jnp.where(newest, vn_t[:, bb:bb + 1], pltpu.roll(vt, WINDOW - 1, 1))
        for g in range(SSD_GROUPS):
            tiles = []
            for pp in range(TILES_PER_GROUP):
                p = g * TILES_PER_GROUP + pp
                for hh in range(2):
                    h = 2 * p + hh
                    dab = jnp.broadcast_to(da_t[h:h + 1, bb:bb + 1], (SSD_HEAD_DIM, SSD_STATE))
                    h1 = sh_ref[bb, h] * dab + outer[g][pp][hh * SSD_HEAD_DIM:(hh + 1) * SSD_HEAD_DIM,
                                                               bb * SSD_STATE:(bb + 1) * SSD_STATE]
                    sho_ref[bb, h] = h1
                    tiles.append(h1.astype(BF16))
            res = _dot_nt(c_blk[:, g * SSD_STATE:(g + 1) * SSD_STATE], jnp.concatenate(tiles, axis=0))
            y_acc[g] = jnp.where(rowb == bb, res, y_acc[g])
    for g in range(SSD_GROUPS):
        y_scr[pl.ds(r8, BT), g * GROUP_W:(g + 1) * GROUP_W] = y_acc[g]

    @pl.when(i == pl.num_programs(0) - 1)
    def _():
        for cc in range(ATT_HEADS // 2):
            g = cc // (ATT_HEADS // ATT_KV_HEADS // 2)
            oe = o_scr[pl.ds(2 * cc, NB, stride=ATT_HEADS), :]
            oo = o_scr[pl.ds(2 * cc + 1, NB, stride=ATT_HEADS), :]
            if g == 0:
                oo = pltpu.roll(oo, HALF, 1)
            else:
                oe = pltpu.roll(oe, HALF, 1)
            gate = p_scr[:, GA_OFF + cc * LANES:GA_OFF + (cc + 1) * LANES]
            mix_scr[:, cc * LANES:(cc + 1) * LANES] = (jnp.where(lo_f, oe, oo) * _silu(gate)).astype(BF16)
        y = (y_scr[...] + svec_ref[0:1, :] * p_scr[:, XBC_OFF:XBC_OFF + SSD_WIDTH]) * _silu(p_scr[:, Z_OFF:Z_OFF + SSD_WIDTH])
        y = y * lax.rsqrt(jnp.mean(y * y, -1, keepdims=True) + NORM_EPS) * svec_ref[1:2, :]
        mix_scr[:, MIX_SSD:MIX_SSD + SSD_WIDTH] = y.astype(BF16)
        out = _dot(mix_scr[...], wout_ref[...])
        y_ref[...] = _layer_norm(DEEPNORM_ALPHA * x_ref[...] + out, ln_ref[0:1, :], ln_ref[1:2, :])


def _sample_layer(layer, x, rope_tab, sinks, prm, ck, cv, lc, lh, sc, sh, prev):
    NB = x.shape[0]
    BT = SAMPLE_BT
    H = CONV_W - 1
    win, aux, wg, wout, lcw, lvec, scw, scb, hvec, svec, ln, _ = prm
    whole = lambda shape: pl.BlockSpec(shape, lambda i: (0,) * len(shape), pipeline_mode=pl.Buffered(1))
    lwhole = lambda shape: pl.BlockSpec((None,) + shape, lambda i: (layer,) + (0,) * len(shape),
                                        pipeline_mode=pl.Buffered(1))
    rows = lambda shape: pl.BlockSpec((None, BT) + shape, lambda i: (layer, i) + (0,) * len(shape))
    state_specs = [
        rows((ATT_KV_WIDTH, WINDOW)),
        rows((ATT_KV_WIDTH, WINDOW)),
        lwhole((H, NB, LRU_WIDTH)),
        lwhole((NB, LRU_WIDTH)),
        lwhole((H, NB, SSD_CONV_CH)),
        rows((SSD_HEADS, SSD_HEAD_DIM, SSD_STATE)),
    ]
    in_specs = [
        pl.BlockSpec(memory_space=pltpu.SMEM),
        whole((NB, D_MODEL)),
        whole((1, 3 * LANES)),
        _const_spec((D_MODEL, W_IN_COLS), layer),
        _const_spec((D_MODEL, AUX_COLS), layer),
        _const_spec((LRU_WIDTH, 2 * LRU_WIDTH), layer),
        _const_spec((D_MIX, D_MODEL), layer),
        _const_spec((CONV_W, LRU_WIDTH), layer),
        _const_spec((4, LRU_WIDTH), layer),
        _const_spec((CONV_W, SSD_CONV_CH), layer),
        _const_spec((1, SSD_CONV_CH), layer),
        _const_spec((2, LANES), layer),
        _const_spec((2, SSD_WIDTH), layer),
        _const_spec((2, D_MODEL), layer),
    ] + state_specs
    operands = [sinks, x, rope_tab, win, aux, wg, wout, lcw, lvec, scw, scb, hvec, svec, ln, ck, cv, lc, lh, sc, sh]
    aliases = {}
    if prev is not None:
        for k, arr in enumerate(prev):
            aliases[len(operands)] = 1 + k
            operands.append(arr)
            in_specs.append(pl.BlockSpec(memory_space=pl.ANY))
    out_shape = (jax.ShapeDtypeStruct((NB, D_MODEL), F32),) + tuple(
        jax.ShapeDtypeStruct(a.shape, F32) for a in (ck, cv, lc, lh, sc, sh))
    full = lambda shape: pl.BlockSpec(shape, lambda i: (0,) * len(shape))
    lfull = lambda shape: pl.BlockSpec((None,) + shape, lambda i: (layer,) + (0,) * len(shape))
    out_specs = (
        full((NB, D_MODEL)),
        rows((ATT_KV_WIDTH, WINDOW)),
        rows((ATT_KV_WIDTH, WINDOW)),
        lfull((H, NB, LRU_WIDTH)),
        lfull((NB, LRU_WIDTH)),
        lfull((H, NB, SSD_CONV_CH)),
        rows((SSD_HEADS, SSD_HEAD_DIM, SSD_STATE)),
    )
    scratch = [
        pltpu.VMEM((NB, N_COLS), F32),
        pltpu.VMEM((NB * ATT_HEADS, LANES), F32),
        pltpu.VMEM((NB * ATT_HEADS, LANES), F32),
        pltpu.VMEM((NB, ATT_KV_WIDTH), F32),
        pltpu.VMEM((NB, ATT_KV_WIDTH), F32),
        pltpu.VMEM((NB, SSD_WIDTH), F32),
        pltpu.VMEM((NB, LANES), F32),
        pltpu.VMEM((NB, SSD_WIDTH), F32),
        pltpu.VMEM((NB, D_MIX), BF16),
    ]
    return pl.pallas_call(
        functools.partial(_sample_kernel, len(aliases)),
        grid=(NB // BT,),
        in_specs=in_specs,
        out_specs=out_specs,
        out_shape=out_shape,
        scratch_shapes=scratch,
        input_output_aliases=aliases,
        compiler_params=pltpu.CompilerParams(
            dimension_semantics=("arbitrary",),
            vmem_limit_bytes=VMEM_LIMIT_BYTES),
        name=f"sample_layer{layer}",
    )(*operands)


def _rope_table(start, length):
    half = ROPE_DIM // 2
    inv = (np.float32(ROPE_THETA) ** (-np.arange(half, dtype=np.float32) / np.float32(half))).astype(np.float32)
    ang = (np.arange(start, start + length).astype(np.float32)[:, None] * inv[None, :]).astype(np.float32)
    cos, sin = np.cos(ang), np.sin(ang)
    d = np.arange(LANES) % ATT_HEAD_DIM
    fi = d % half
    c = np.where(d < ROPE_DIM, cos[:, fi], np.float32(1.0))
    s1 = np.where(d < half, -sin[:, fi], np.float32(0.0))
    s2 = np.where((d >= half) & (d < ROPE_DIM), sin[:, fi], np.float32(0.0))
    return jnp.asarray(np.concatenate([c, s1, s2], axis=1).astype(np.float32))


def _prep_params(w_in, w_out, lru_conv_w, lru_conv_b, lru_wa, lru_ba, lru_wx, lru_bx, lru_lambda,
                 ssd_conv_w, ssd_conv_b, ssd_dt_bias, ssd_a_log, ssd_d, ssd_norm_g, ln_g, ln_b):
    hd = ATT_HEAD_DIM
    win = w_in.astype(BF16)
    k = w_in[..., ATT_WIDTH:ATT_WIDTH + ATT_KV_WIDTH]
    v = w_in[..., ATT_WIDTH + ATT_KV_WIDTH:ATT_WIDTH + 2 * ATT_KV_WIDTH]
    dup = lambda t: jnp.concatenate([t[..., :hd], t[..., :hd], t[..., hd:], t[..., hd:]], -1)
    dtw = jnp.pad(w_in[..., W_IN_COLS - SSD_HEADS:], ((0, 0), (0, 0), (0, LANES - SSD_HEADS)))
    aux = jnp.concatenate([dup(k), dup(v), dtw], -1).astype(BF16)

    def dense(w):
        rows = [jnp.pad(w[:, n], ((0, 0), (0, 0), (n * LRU_BLOCK, LRU_WIDTH - (n + 1) * LRU_BLOCK)))
                for n in range(LRU_BLOCKS)]
        return jnp.concatenate(rows, 1)

    wg = jnp.concatenate([dense(lru_wa.astype(BF16)), dense(lru_wx.astype(BF16))], -1)
    wout = w_out.astype(BF16)
    lvec = jnp.stack([lru_conv_b, lru_ba, lru_bx, lru_lambda], 1)
    hpad = lambda t: jnp.pad(t, ((0, 0), (0, LANES - SSD_HEADS)))
    hvec = jnp.stack([hpad(ssd_dt_bias), hpad(ssd_a_log)], 1)
    rpad = lambda t: jnp.broadcast_to(jnp.pad(t, ((0, 0), (0, HEAD_ROWS - SSD_HEADS)))[:, :, None],
                                      (DEPTH, HEAD_ROWS, LANES))
    hcol = jnp.stack([rpad(ssd_dt_bias), rpad(ssd_a_log)], 1)
    svec = jnp.stack([jnp.repeat(ssd_d, SSD_HEAD_DIM, axis=1), ssd_norm_g], 1)
    ln = jnp.stack([ln_g, ln_b], 1)
    return (win, aux, wg, wout, lru_conv_w, lvec, ssd_conv_w, ssd_conv_b[:, None, :], hvec, svec, ln, hcol)


def kernel(x_prompt, x_sample, cache_swa_k, cache_swa_v, state_lru_conv, state_lru_h, state_ssd_conv, state_ssd_h, w_in, w_out, att_sinks, lru_conv_w, lru_conv_b, lru_wa, lru_ba, lru_wx, lru_bx, lru_lambda, ssd_conv_w, ssd_conv_b, ssd_dt_bias, ssd_a_log, ssd_d, ssd_norm_g, ln_g, ln_b):
    prm = _prep_params(w_in, w_out, lru_conv_w, lru_conv_b, lru_wa, lru_ba, lru_wx, lru_bx, lru_lambda,
                       ssd_conv_w, ssd_conv_b, ssd_dt_bias, ssd_a_log, ssd_d, ssd_norm_g, ln_g, ln_b)
    bp, lp, _ = x_prompt.shape
    rope_p = _rope_table(0, lp)
    xp = x_prompt
    new_p = [[] for _ in range(6)]
    for l in range(DEPTH):
        xp, ko, vo, lco, lho, sco, sho = _prompt_layer(l, xp, rope_p, att_sinks[l], prm)
        st = (ko.reshape(bp, WINDOW, ATT_KV_HEADS, ATT_HEAD_DIM), vo.reshape(bp, WINDOW, ATT_KV_HEADS, ATT_HEAD_DIM),
              lco, lho.reshape(bp, LRU_WIDTH), sco, sho)
        for lst, t in zip(new_p, st):
            lst.append(t)
    outs_p = [jnp.stack(t) for t in new_p]

    nb = x_sample.shape[0]
    rope_s = _rope_table(PAST_LEN, x_sample.shape[1])
    to_kt = lambda t: jnp.transpose(t, (0, 1, 3, 4, 2)).reshape(DEPTH, nb, ATT_KV_WIDTH, WINDOW)
    from_kt = lambda t: jnp.transpose(t.reshape(DEPTH, nb, ATT_KV_HEADS, ATT_HEAD_DIM, WINDOW), (0, 1, 4, 2, 3))
    swap = lambda t: jnp.transpose(t, (0, 2, 1, 3))
    state = (to_kt(cache_swa_k), to_kt(cache_swa_v), swap(state_lru_conv), state_lru_h, swap(state_ssd_conv),
             state_ssd_h)
    xs = x_sample.reshape(nb, D_MODEL)
    new = None
    for l in range(DEPTH):
        xs, *new = _sample_layer(l, xs, rope_s, att_sinks[l], prm, *state, new)
    cko, cvo, lco, lho, sco, sho = new
    outs_s = (from_kt(cko), from_kt(cvo), swap(lco), lho, swap(sco), sho)
    return (xp, xs.reshape(x_sample.shape)) + tuple(outs_p) + tuple(outs_s)
```

```python
import functools

import jax
import jax.numpy as jnp
import numpy as np
from jax import lax
from jax.experimental import pallas as pl
from jax.experimental.pallas import tpu as pltpu

F32 = jnp.float32
BF16 = jnp.bfloat16

D_MODEL = 1024
DEPTH = 4
PAST_LEN = 8192
D_MIX = 2 * D_MODEL
ATT_HEADS = 8
ATT_KV_HEADS = 2
ATT_HEAD_DIM = 64
ATT_WIDTH = ATT_HEADS * ATT_HEAD_DIM
ATT_KV_WIDTH = ATT_KV_HEADS * ATT_HEAD_DIM
ATT_SCALE = ATT_HEAD_DIM ** -0.5
WINDOW = 128
ROPE_THETA = 500000.0
ROPE_DIM = ATT_HEAD_DIM // 4
LRU_WIDTH = 3 * D_MIX // 8
LRU_BLOCKS = 8
LRU_BLOCK = LRU_WIDTH // LRU_BLOCKS
LRU_C = 8.0
CONV_W = 4
SSD_WIDTH = D_MIX - ATT_WIDTH - LRU_WIDTH
SSD_HEAD_DIM = 64
SSD_HEADS = SSD_WIDTH // SSD_HEAD_DIM
SSD_GROUPS = 2
SSD_STATE = 128
SSD_CONV_CH = SSD_WIDTH + 2 * SSD_GROUPS * SSD_STATE
DEEPNORM_ALPHA = (2.0 * DEPTH) ** 0.25
NORM_EPS = 1e-5

LANES = 128
SUBLANES = 8
HALF = LANES // 2
VMEM_LIMIT_BYTES = 60 * 1024 * 1024

Q_OFF = 0
K_OFF = Q_OFF + ATT_WIDTH
V_OFF = K_OFF + 2 * ATT_KV_WIDTH
GA_OFF = V_OFF + 2 * ATT_KV_WIDTH
XL_OFF = GA_OFF + ATT_WIDTH
GL_OFF = XL_OFF + LRU_WIDTH
Z_OFF = GL_OFF + LRU_WIDTH
XBC_OFF = Z_OFF + SSD_WIDTH
DT_OFF = XBC_OFF + SSD_CONV_CH
N_COLS = DT_OFF + LANES
W_IN_COLS = 2 * ATT_WIDTH + 2 * ATT_KV_WIDTH + 2 * LRU_WIDTH + SSD_WIDTH + SSD_CONV_CH + SSD_HEADS
AUX_COLS = 4 * ATT_KV_WIDTH + LANES
B_OFF = XBC_OFF + SSD_WIDTH
C_OFF = B_OFF + SSD_GROUPS * SSD_STATE
MIX_LRU = ATT_WIDTH
MIX_SSD = ATT_WIDTH + LRU_WIDTH

N_LRU_TILES = LRU_WIDTH // LANES
N_SSD_TILES = SSD_WIDTH // LANES
N_CONV_TILES = SSD_CONV_CH // LANES
CONV_STRIDE = 4
TILES_PER_GROUP = N_SSD_TILES // SSD_GROUPS
GROUP_W = SSD_WIDTH // SSD_GROUPS
HEAD_ROWS = 16

NEG_BIG = -1e30

PROMPT_TILE = 256
SAMPLE_BT = 8


def _sigmoid(x):
    return 0.5 * jnp.tanh(0.5 * x) + 0.5


def _silu(x):
    h = 0.5 * x
    return h + h * jnp.tanh(h)


def _softplus(x):
    return jnp.maximum(x, 0.0) + jnp.log1p(jnp.exp(-jnp.abs(x)))


def _lane_lo(shape):
    return (lax.broadcasted_iota(jnp.int32, shape, len(shape) - 1) % LANES) < HALF


def _rope(t, tab):
    half = ROPE_DIM // 2
    return (t * tab[:, 0:LANES]
            + pltpu.roll(t, LANES - half, 1) * tab[:, LANES:2 * LANES]
            + pltpu.roll(t, half, 1) * tab[:, 2 * LANES:3 * LANES])


def _dot(a, b):
    return jnp.dot(a, b, preferred_element_type=F32)


def _dot_nt(a, b):
    return lax.dot_general(a, b, (((1,), (1,)), ((), ())), preferred_element_type=F32)


def _conv_classes(src_ref, j, rows, w_ref, bias, cs):
    n = CONV_STRIDE
    hist = CONV_W - 1
    taps = {s_: src_ref[j, pl.ds(SUBLANES + s_, rows // n, stride=n), :] for s_ in range(-hist, n)}
    outs = []
    for k in range(n):
        acc = bias + w_ref[hist:hist + 1, cs] * taps[k]
        for t in range(hist):
            acc = acc + w_ref[t:t + 1, cs] * taps[k - hist + t]
        outs.append(acc)
    return outs


def _layer_norm(v, g, b):
    mu = jnp.mean(v, -1, keepdims=True)
    d = v - mu
    var = jnp.mean(d * d, -1, keepdims=True)
    return d * lax.rsqrt(var + NORM_EPS) * g + b


def _lru_coeffs(xl, gates, lvec_ref):
    r = _sigmoid(gates[:, 0:LRU_WIDTH] + lvec_ref[1:2, :])
    ig = _sigmoid(gates[:, LRU_WIDTH:2 * LRU_WIDTH] + lvec_ref[2:3, :])
    log_a = (-LRU_C * _softplus(-lvec_ref[3:4, :])) * r
    a = jnp.exp(log_a)
    bt = jnp.sqrt(-jnp.tanh(log_a) * (1.0 + a * a)) * (ig * xl)
    return a, bt


def _project(x_tile, win_ref, aux_ref, dst):
    xb = x_tile.astype(BF16)
    kv0 = ATT_WIDTH + 2 * ATT_KV_WIDTH
    dst[:, Q_OFF:K_OFF] = _dot(xb, win_ref[:, 0:ATT_WIDTH])
    dst[:, K_OFF:GA_OFF] = _dot(xb, aux_ref[:, 0:GA_OFF - K_OFF])
    dst[:, GA_OFF:Z_OFF] = _dot(xb, win_ref[:, kv0:kv0 + Z_OFF - GA_OFF])
    dst[:, Z_OFF:DT_OFF] = _dot(xb, win_ref[:, kv0 + Z_OFF - GA_OFF:kv0 + DT_OFF - GA_OFF])
    dst[:, DT_OFF:N_COLS] = _dot(xb, aux_ref[:, GA_OFF - K_OFF:GA_OFF - K_OFF + LANES])


def _prompt_kernel(sinks_ref, x_ref, rope_ref, win_ref, aux_ref, wg_ref, wout_ref, lcw_ref, lvec_ref,
                   scw_ref, scb_ref, hcol_ref, svec_ref, ln_ref,
                   y_ref, ko_ref, vo_ref, lco_ref, lho_ref, sco_ref, sho_ref,
                   p_scr, kd_scr, vd_scr, lx_scr, sx_scr, sa_scr, sb_scr, hl_scr, ht_scr, mix_scr, xlb_scr, xc_scr):
    T = PROMPT_TILE
    nsub = T // WINDOW
    c = pl.program_id(1)
    last = c == pl.num_programs(1) - 1

    @pl.when(c == 0)
    def _():
        kd_scr[:, 0:WINDOW, :] = jnp.zeros((2, WINDOW, LANES), BF16)
        vd_scr[:, 0:WINDOW, :] = jnp.zeros((2, WINDOW, LANES), BF16)
        lx_scr[:, 0:SUBLANES, :] = jnp.zeros((N_LRU_TILES, SUBLANES, LANES), F32)
        sx_scr[:, 0:SUBLANES, :] = jnp.zeros((N_CONV_TILES, SUBLANES, LANES), F32)
        hl_scr[...] = jnp.zeros_like(hl_scr)
        ht_scr[...] = jnp.zeros_like(ht_scr)

    _project(x_ref[...], win_ref, aux_ref, p_scr)

    lo_sq = _lane_lo((WINDOW, LANES))

    for g in range(ATT_KV_HEADS):
        kd = _rope(p_scr[:, K_OFF + g * LANES:K_OFF + (g + 1) * LANES], rope_ref[...])
        kd_scr[g, WINDOW:WINDOW + T, :] = kd.astype(BF16)
        vd_scr[g, WINDOW:WINDOW + T, :] = p_scr[:, V_OFF + g * LANES:V_OFF + (g + 1) * LANES].astype(BF16)

    row = lax.broadcasted_iota(jnp.int32, (WINDOW, 2 * WINDOW), 0)
    col = lax.broadcasted_iota(jnp.int32, (WINDOW, 2 * WINDOW), 1)
    band = (col >= row) & (col <= row + WINDOW)
    first_lo = jnp.where(c > 0, 0, WINDOW)
    for i in range(nsub):
        r0 = i * WINDOW
        msk = (band & (col >= first_lo)) if i == 0 else band
        tab = rope_ref[r0:r0 + WINDOW, :]
        for cc in range(ATT_HEADS // 2):
            g = cc // (ATT_HEADS // ATT_KV_HEADS // 2)
            kg = kd_scr[g, r0:r0 + 2 * WINDOW, :]
            vg = vd_scr[g, r0:r0 + 2 * WINDOW, :]
            qp = _rope(p_scr[r0:r0 + WINDOW, Q_OFF + cc * LANES:Q_OFF + (cc + 1) * LANES], tab) * ATT_SCALE
            outs = []
            for half in range(2):
                sink = sinks_ref[2 * cc + half]
                qm = jnp.where(lo_sq if half == 0 else jnp.logical_not(lo_sq), qp, 0.0).astype(BF16)
                s = jnp.where(msk, _dot_nt(qm, kg), NEG_BIG)
                m = jnp.maximum(jnp.max(s, -1, keepdims=True), sink)
                e = jnp.exp(s - m)
                den = jnp.sum(e, -1, keepdims=True) + jnp.exp(sink - m)
                outs.append(_dot(e.astype(BF16), vg) * (1.0 / den))
            att = jnp.where(lo_sq, outs[0], outs[1])
            gate = p_scr[r0:r0 + WINDOW, GA_OFF + cc * LANES:GA_OFF + (cc + 1) * LANES]
            mix_scr[r0:r0 + WINDOW, cc * LANES:(cc + 1) * LANES] = (att * _silu(gate)).astype(BF16)
    for g in range(ATT_KV_HEADS):
        kd_scr[g, 0:WINDOW, :] = kd_scr[g, T:T + WINDOW, :]
        vd_scr[g, 0:WINDOW, :] = vd_scr[g, T:T + WINDOW, :]

    P = SUBLANES
    H = CONV_W - 1
    for j in range(N_LRU_TILES):
        cs_ = slice(j * LANES, (j + 1) * LANES)
        lx_scr[j, P:P + T, :] = p_scr[:, XL_OFF + j * LANES:XL_OFF + (j + 1) * LANES]
        for k, xl in enumerate(_conv_classes(lx_scr, j, T, lcw_ref, lvec_ref[0:1, cs_], cs_)):
            sb_scr[j, pl.ds(k, T // CONV_STRIDE, stride=CONV_STRIDE), :] = xl
        xlb_scr[:, cs_] = sb_scr[j].astype(BF16)
    lx_scr[:, 0:P, :] = lx_scr[:, T:T + P, :]

    G = T // SUBLANES
    row_g = lax.broadcasted_iota(jnp.int32, (G, LANES), 0)
    cl = -LRU_C * _softplus(-lvec_ref[3:4, :])
    for jj in range(N_LRU_TILES // 2):
        c0 = jj * 2 * LANES
        k0 = (c0 // LRU_BLOCK) * LRU_BLOCK // LANES * LANES
        k1 = -(-(-(-(c0 + 2 * LANES) // LRU_BLOCK) * LRU_BLOCK) // LANES) * LANES
        xk = xlb_scr[:, k0:k1]
        gr = _dot(xk, wg_ref[k0:k1, c0:c0 + 2 * LANES])
        gi = _dot(xk, wg_ref[k0:k1, LRU_WIDTH + c0:LRU_WIDTH + c0 + 2 * LANES])
        for u in range(2):
            j = 2 * jj + u
            cs_ = slice(j * LANES, (j + 1) * LANES)
            us = slice(u * LANES, (u + 1) * LANES)
            r = _sigmoid(gr[:, us] + lvec_ref[1:2, cs_])
            ig = _sigmoid(gi[:, us] + lvec_ref[2:3, cs_])
            log_a = cl[:, cs_] * r
            a = jnp.exp(log_a)
            sa_scr[j] = a
            sb_scr[j] = jnp.sqrt(-jnp.tanh(log_a) * (1.0 + a * a)) * (ig * sb_scr[j])
    for j in range(N_LRU_TILES):
        ca = sa_scr[j, pl.ds(0, G, stride=SUBLANES), :]
        cb = sb_scr[j, pl.ds(0, G, stride=SUBLANES), :]
        for k in range(1, SUBLANES):
            ak = sa_scr[j, pl.ds(k, G, stride=SUBLANES), :]
            cb = ak * cb + sb_scr[j, pl.ds(k, G, stride=SUBLANES), :]
            ca = ak * ca
            sa_scr[j, pl.ds(k, G, stride=SUBLANES), :] = ca
            sb_scr[j, pl.ds(k, G, stride=SUBLANES), :] = cb
        hin = hl_scr[0:1, j * LANES:(j + 1) * LANES]
        cb = cb + jnp.where(row_g == 0, ca * hin, 0.0)
        d = 1
        while d < G:
            ok = row_g >= d
            cb = cb + ca * jnp.where(ok, pltpu.roll(cb, d, 0), 0.0)
            ca = ca * jnp.where(ok, pltpu.roll(ca, d, 0), 1.0)
            d *= 2
        hprev = jnp.where(row_g == 0, hin, pltpu.roll(cb, 1, 0))
        for k in range(SUBLANES):
            sb_scr[j, pl.ds(k, G, stride=SUBLANES), :] = (
                sb_scr[j, pl.ds(k, G, stride=SUBLANES), :] + sa_scr[j, pl.ds(k, G, stride=SUBLANES), :] * hprev)
        hl_scr[0:1, j * LANES:(j + 1) * LANES] = cb[G - 1:G, :]
        gl = p_scr[:, GL_OFF + j * LANES:GL_OFF + (j + 1) * LANES]
        mix_scr[:, MIX_LRU + j * LANES:MIX_LRU + (j + 1) * LANES] = (sb_scr[j] * _silu(gl)).astype(BF16)

    for j in range(N_CONV_TILES):
        cs_ = slice(j * LANES, (j + 1) * LANES)
        sx_scr[j, P:P + T, :] = p_scr[:, XBC_OFF + j * LANES:XBC_OFF + (j + 1) * LANES]
        for k, xc in enumerate(_conv_classes(sx_scr, j, T, scw_ref, scb_ref[0:1, cs_], cs_)):
            xc_scr[j, pl.ds(k, T // CONV_STRIDE, stride=CONV_STRIDE), :] = _silu(xc)
    sx_scr[:, 0:P, :] = sx_scr[:, T:T + P, :]

    qi = lax.broadcasted_iota(jnp.int32, (WINDOW, WINDOW), 0)
    si = lax.broadcasted_iota(jnp.int32, (WINDOW, WINDOW), 1)
    causal = si <= qi
    upper = (qi <= si).astype(F32)
    hpad = jnp.zeros((WINDOW - HEAD_ROWS, WINDOW), F32)
    a_col = -jnp.exp(hcol_ref[1])
    for i in range(nsub):
        r0 = i * WINDOW
        dt_t = _softplus(p_scr[r0:r0 + WINDOW, DT_OFF:DT_OFF + LANES].T[0:HEAD_ROWS, :] + hcol_ref[0])
        cst = jnp.dot(dt_t * a_col, upper, precision=lax.Precision.HIGHEST, preferred_element_type=F32)
        dt_c = jnp.concatenate([dt_t, hpad], axis=0).T
        cs = jnp.concatenate([cst, hpad], axis=0).T
        ys = [None] * N_SSD_TILES
        for g in range(SSD_GROUPS):
            bg_t = xc_scr[N_SSD_TILES + g, r0:r0 + WINDOW, :].T.astype(BF16)
            cg = xc_scr[N_SSD_TILES + SSD_GROUPS + g, r0:r0 + WINDOW, :].astype(BF16)
            cbm = _dot(cg, bg_t)
            yo = _dot(cg, ht_scr[g].astype(BF16))
            xdec, edec = [], []
            for pp in range(TILES_PER_GROUP):
                p = g * TILES_PER_GROUP + pp
                bcs, mm, dts = [], [], []
                for h in (2 * p, 2 * p + 1):
                    bc = jnp.broadcast_to(cs[:, h:h + 1], (WINDOW, WINDOW))
                    lmat = jnp.exp(jnp.where(causal, bc - cst[h:h + 1, :], NEG_BIG))
                    mm.append((cbm * lmat).astype(BF16))
                    bcs.append(bc)
                    dts.append(jnp.broadcast_to(dt_c[:, h:h + 1], (WINDOW, LANES)))
                csl = jnp.where(lo_sq, bcs[0], bcs[1])
                ecs = jnp.exp(csl)
                dec = jnp.exp(csl[WINDOW - 1:WINDOW, :] - csl)
                xs_p = xc_scr[p, r0:r0 + WINDOW, :]
                xdt = xs_p * jnp.where(lo_sq, dts[0], dts[1])
                xdt_b = xdt.astype(BF16)
                y = jnp.where(lo_sq, _dot(mm[0], xdt_b), _dot(mm[1], xdt_b))
                y = y + yo[:, pp * LANES:(pp + 1) * LANES] * ecs
                y = y + svec_ref[0:1, p * LANES:(p + 1) * LANES] * xs_p
                ys[p] = y * _silu(p_scr[r0:r0 + WINDOW, Z_OFF + p * LANES:Z_OFF + (p + 1) * LANES])
                xdec.append((xdt * dec).astype(BF16))
                edec.append(ecs[WINDOW - 1:WINDOW, :])
            ht_scr[g] = (ht_scr[g] * jnp.concatenate(edec, axis=1)
                         + _dot(bg_t, jnp.concatenate(xdec, axis=1)))
        ss = jnp.sum(ys[0] * ys[0], -1, keepdims=True)
        for p in range(1, N_SSD_TILES):
            ss = ss + jnp.sum(ys[p] * ys[p], -1, keepdims=True)
        rinv = lax.rsqrt(ss * (1.0 / SSD_WIDTH) + NORM_EPS)
        for p in range(N_SSD_TILES):
            mix_scr[r0:r0 + WINDOW, MIX_SSD + p * LANES:MIX_SSD + (p + 1) * LANES] = (
                ys[p] * rinv * svec_ref[1:2, p * LANES:(p + 1) * LANES]).astype(BF16)

    out = _dot(mix_scr[...], wout_ref[...])
    y_ref[...] = _layer_norm(DEEPNORM_ALPHA * x_ref[...] + out, ln_ref[0:1, :], ln_ref[1:2, :])

    @pl.when(last)
    def _():
        tab = rope_ref[T - WINDOW:T, :]
        k0 = _rope(p_scr[T - WINDOW:T, K_OFF:K_OFF + LANES], tab)
        k1 = _rope(p_scr[T - WINDOW:T, K_OFF + LANES:K_OFF + 2 * LANES], tab)
        ko_ref[...] = jnp.where(lo_sq, k0, k1)
        vo_ref[...] = jnp.where(lo_sq, p_scr[T - WINDOW:T, V_OFF:V_OFF + LANES],
                                p_scr[T - WINDOW:T, V_OFF + LANES:V_OFF + 2 * LANES])
        for j in range(N_LRU_TILES):
            lco_ref[:, j * LANES:(j + 1) * LANES] = lx_scr[j, P + T - (CONV_W - 1):P + T, :]
        lho_ref[...] = hl_scr[...]
        for j in range(N_CONV_TILES):
            sco_ref[:, j * LANES:(j + 1) * LANES] = sx_scr[j, P + T - (CONV_W - 1):P + T, :]
        for g in range(SSD_GROUPS):
            for pp in range(TILES_PER_GROUP):
                tile = ht_scr[g, :, pp * LANES:(pp + 1) * LANES].T
                h = 2 * (g * TILES_PER_GROUP + pp)
                sho_ref[h] = tile[0:SSD_HEAD_DIM, :]
                sho_ref[h + 1] = tile[SSD_HEAD_DIM:2 * SSD_HEAD_DIM, :]


def _const_spec(shape, layer):
    nd = len(shape)
    return pl.BlockSpec((None,) + tuple(shape), lambda *_: (layer,) + (0,) * nd,
                        pipeline_mode=pl.Buffered(1))


def _prompt_layer(layer, x, rope_tab, sinks, prm):
    B, L, _ = x.shape
    T = PROMPT_TILE
    nc = L // T
    win, aux, wg, wout, lcw, lvec, scw, scb, hvec, svec, ln, hcol = prm
    tile = lambda b, c: (b, c, 0)
    per_b = lambda b, c: (b, 0, 0)

    in_specs = [
        pl.BlockSpec(memory_space=pltpu.SMEM),
        pl.BlockSpec((None, T, D_MODEL), tile),
        pl.BlockSpec((T, 3 * LANES), lambda b, c: (c, 0)),
        _const_spec((D_MODEL, W_IN_COLS), layer),
        _const_spec((D_MODEL, AUX_COLS), layer),
        _const_spec((LRU_WIDTH, 2 * LRU_WIDTH), layer),
        _const_spec((D_MIX, D_MODEL), layer),
        _const_spec((CONV_W, LRU_WIDTH), layer),
        _const_spec((4, LRU_WIDTH), layer),
        _const_spec((CONV_W, SSD_CONV_CH), layer),
        _const_spec((1, SSD_CONV_CH), layer),
        _const_spec((2, HEAD_ROWS, LANES), layer),
        _const_spec((2, SSD_WIDTH), layer),
        _const_spec((2, D_MODEL), layer),
    ]
    out_shape = (
        jax.ShapeDtypeStruct((B, L, D_MODEL), F32),
        jax.ShapeDtypeStruct((B, WINDOW, ATT_KV_WIDTH), F32),
        jax.ShapeDtypeStruct((B, WINDOW, ATT_KV_WIDTH), F32),
        jax.ShapeDtypeStruct((B, CONV_W - 1, LRU_WIDTH), F32),
        jax.ShapeDtypeStruct((B, 1, LRU_WIDTH), F32),
        jax.ShapeDtypeStruct((B, CONV_W - 1, SSD_CONV_CH), F32),
        jax.ShapeDtypeStruct((B, SSD_HEADS, SSD_HEAD_DIM, SSD_STATE), F32),
    )
    out_specs = (
        pl.BlockSpec((None, T, D_MODEL), tile),
        pl.BlockSpec((None, WINDOW, ATT_KV_WIDTH), per_b),
        pl.BlockSpec((None, WINDOW, ATT_KV_WIDTH), per_b),
        pl.BlockSpec((None, CONV_W - 1, LRU_WIDTH), per_b),
        pl.BlockSpec((None, 1, LRU_WIDTH), per_b),
        pl.BlockSpec((None, CONV_W - 1, SSD_CONV_CH), per_b),
        pl.BlockSpec((None, SSD_HEADS, SSD_HEAD_DIM, SSD_STATE), lambda b, c: (b, 0, 0, 0)),
    )
    scratch = [
        pltpu.VMEM((T, N_COLS), F32),
        pltpu.VMEM((ATT_KV_HEADS, WINDOW + T, LANES), BF16),
        pltpu.VMEM((ATT_KV_HEADS, WINDOW + T, LANES), BF16),
        pltpu.VMEM((N_LRU_TILES, SUBLANES + T, LANES), F32),
        pltpu.VMEM((N_CONV_TILES, SUBLANES + T, LANES), F32),
        pltpu.VMEM((N_LRU_TILES, T, LANES), F32),
        pltpu.VMEM((N_LRU_TILES, T, LANES), F32),
        pltpu.VMEM((1, LRU_WIDTH), F32),
        pltpu.VMEM((SSD_GROUPS, SSD_STATE, GROUP_W), F32),
        pltpu.VMEM((T, D_MIX), BF16),
        pltpu.VMEM((T, LRU_WIDTH), BF16),
        pltpu.VMEM((N_CONV_TILES, T, LANES), F32),
    ]
    return pl.pallas_call(
        _prompt_kernel,
        grid=(B, nc),
        in_specs=in_specs,
        out_specs=out_specs,
        out_shape=out_shape,
        scratch_shapes=scratch,
        compiler_params=pltpu.CompilerParams(
            dimension_semantics=("arbitrary", "arbitrary"),
            vmem_limit_bytes=VMEM_LIMIT_BYTES),
        name=f"prompt_layer{layer}",
    )(sinks, x, rope_tab, win, aux, wg, wout, lcw, lvec, scw, scb, hcol, svec, ln)


def _conv_step(x_new, st_ref, sto_ref, w_ref, bias):
    y = bias + w_ref[CONV_W - 1:CONV_W, :] * x_new
    for t in range(CONV_W - 1):
        y = y + w_ref[t:t + 1, :] * st_ref[t]
    for t in range(CONV_W - 2):
        sto_ref[t] = st_ref[t + 1]
    sto_ref[CONV_W - 2] = x_new
    return y


def _sample_kernel(n_aliased, sinks_ref, x_ref, rope_ref, win_ref, aux_ref, wg_ref, wout_ref, lcw_ref, lvec_ref,
                   scw_ref, scb_ref, hvec_ref, svec_ref, ln_ref,
                   ck_ref, cv_ref, lc_ref, lh_ref, sc_ref, sh_ref, *refs):
    (y_ref, cko_ref, cvo_ref, lco_ref, lho_ref, sco_ref, sho_ref,
     p_scr, q_scr, o_scr, kn_scr, vn_scr, xdt_scr, da_scr, y_scr, mix_scr) = refs[n_aliased:]
    NB = x_ref.shape[0]
    BT = SAMPLE_BT
    i = pl.program_id(0)
    lo_f = _lane_lo((NB, LANES))

    @pl.when(i == 0)
    def _():
        _project(x_ref[...], win_ref, aux_ref, p_scr)
        tab = rope_ref[...]
        for cc in range(ATT_HEADS // 2):
            g = cc // (ATT_HEADS // ATT_KV_HEADS // 2)
            qp = _rope(p_scr[:, Q_OFF + cc * LANES:Q_OFF + (cc + 1) * LANES], tab) * ATT_SCALE
            for half in range(2):
                t = jnp.where(lo_f if half == 0 else jnp.logical_not(lo_f), qp, 0.0)
                if half != g:
                    t = pltpu.roll(t, HALF, 1)
                q_scr[pl.ds(2 * cc + half, NB, stride=ATT_HEADS), :] = t
        k0 = _rope(p_scr[:, K_OFF:K_OFF + LANES], tab)
        k1 = _rope(p_scr[:, K_OFF + LANES:K_OFF + 2 * LANES], tab)
        kn_scr[...] = jnp.where(lo_f, k0, k1)
        vn_scr[...] = jnp.where(lo_f, p_scr[:, V_OFF:V_OFF + LANES], p_scr[:, V_OFF + LANES:V_OFF + 2 * LANES])
        xin = p_scr[:, XL_OFF:XL_OFF + LRU_WIDTH]
        xl = _conv_step(xin, lc_ref, lco_ref, lcw_ref, lvec_ref[0:1, :])
        gates = _dot(xl.astype(BF16), wg_ref[...])
        a, bt = _lru_coeffs(xl, gates, lvec_ref)
        h1 = a * lh_ref[...] + bt
        lho_ref[...] = h1
        mix_scr[:, MIX_LRU:MIX_LRU + LRU_WIDTH] = (h1 * _silu(p_scr[:, GL_OFF:GL_OFF + LRU_WIDTH])).astype(BF16)
        xin2 = p_scr[:, XBC_OFF:XBC_OFF + SSD_CONV_CH]
        xc = _conv_step(xin2, sc_ref, sco_ref, scw_ref, scb_ref[0:1, :])
        p_scr[:, XBC_OFF:XBC_OFF + SSD_CONV_CH] = _silu(xc)
        dt = _softplus(p_scr[:, DT_OFF:DT_OFF + LANES] + hvec_ref[0:1, :])
        da_scr[...] = jnp.exp(dt * (-jnp.exp(hvec_ref[1:2, :])))
        for p in range(N_SSD_TILES):
            dte = jnp.where(lo_f, jnp.broadcast_to(dt[:, 2 * p:2 * p + 1], (NB, LANES)),
                            jnp.broadcast_to(dt[:, 2 * p + 1:2 * p + 2], (NB, LANES)))
            xdt_scr[:, p * LANES:(p + 1) * LANES] = p_scr[:, XBC_OFF + p * LANES:XBC_OFF + (p + 1) * LANES] * dte

    r8 = pl.multiple_of(i * BT, BT)
    kn_blk = kn_scr[pl.ds(r8, BT), :]
    vn_blk = vn_scr[pl.ds(r8, BT), :]
    zpad = jnp.zeros((LANES - BT, LANES), F32)
    kn_t = jnp.concatenate([kn_blk, zpad], axis=0).T
    vn_t = jnp.concatenate([vn_blk, zpad], axis=0).T
    newest = lax.broadcasted_iota(jnp.int32, (ATT_KV_WIDTH, WINDOW), 1) == WINDOW - 1
    da_t = jnp.concatenate([da_scr[pl.ds(r8, BT), :], zpad], axis=0).T
    x_t = [jnp.concatenate([xdt_scr[pl.ds(r8, BT), p * LANES:(p + 1) * LANES], zpad], axis=0).T
           for p in range(N_SSD_TILES)]
    b_blk = p_scr[pl.ds(r8, BT), B_OFF:B_OFF + SSD_GROUPS * SSD_STATE]
    c_blk = p_scr[pl.ds(r8, BT), C_OFF:C_OFF + SSD_GROUPS * SSD_STATE].astype(BF16)
    row8 = lax.broadcasted_iota(jnp.int32, (ATT_HEADS, 1), 0)
    sink = jnp.zeros((ATT_HEADS, 1), F32)
    for h in range(ATT_HEADS):
        sink = jnp.where(row8 == h, sinks_ref[h], sink)
    rowb = lax.broadcasted_iota(jnp.int32, (BT, GROUP_W), 0)
    y_acc = [jnp.zeros((BT, GROUP_W), F32) for _ in range(SSD_GROUPS)]
    rowk = lax.broadcasted_iota(jnp.int32, (BT, SSD_STATE), 0)
    outer = []
    for g in range(SSD_GROUPS):
        bg = b_blk[:, g * SSD_STATE:(g + 1) * SSD_STATE]
        diag = jnp.concatenate([jnp.where(rowk == bb, bg, 0.0) for bb in range(BT)], axis=1)
        rhs = jnp.concatenate([diag, jnp.zeros((LANES - BT, BT * SSD_STATE), F32)], axis=0).astype(BF16)
        outer.append([_dot(x_t[g * TILES_PER_GROUP + pp].astype(BF16), rhs) for pp in range(TILES_PER_GROUP)])
    for bb in range(BT):
        qr = q_scr[pl.ds(pl.multiple_of((r8 + bb) * ATT_HEADS, ATT_HEADS), ATT_HEADS), :]
        kt = ck_ref[bb]
        vt = cv_ref[bb]
        s = _dot(qr.astype(BF16), kt.astype(BF16))
        s_new = jnp.sum(qr * kn_blk[bb:bb + 1, :], -1, keepdims=True)
        m = jnp.maximum(jnp.maximum(jnp.max(s, -1, keepdims=True), s_new), sink)
        e = jnp.exp(s - m)
        e_new = jnp.exp(s_new - m)
        den = jnp.sum(e, -1, keepdims=True) + e_new + jnp.exp(sink - m)
        o = _dot_nt(e.astype(BF16), vt.astype(BF16)) + e_new * vn_blk[bb:bb + 1, :]
        o_scr[pl.ds(pl.multiple_of((r8 + bb) * ATT_HEADS, ATT_HEADS), ATT_HEADS), :] = o * (1.0 / den)
        cko_ref[bb] = jnp.where(newest, kn_t[:, bb:bb + 1], pltpu.roll(kt, WINDOW - 1, 1))
        cvo_ref[bb] = jnp.where(newest, vn_t[:, bb:bb + 1], pltpu.roll(vt, WINDOW - 1, 1))
        for g in range(SSD_GROUPS):
            tiles = []
            for pp in range(TILES_PER_GROUP):
                p = g * TILES_PER_GROUP + pp
                for hh in range(2):
                    h = 2 * p + hh
                    dab = jnp.broadcast_to(da_t[h:h + 1, bb:bb + 1], (SSD_HEAD_DIM, SSD_STATE))
                    h1 = sh_ref[bb, h] * dab + outer[g][pp][hh * SSD_HEAD_DIM:(hh + 1) * SSD_HEAD_DIM,
                                                               bb * SSD_STATE:(bb + 1) * SSD_STATE]
                    sho_ref[bb, h] = h1
                    tiles.append(h1.astype(BF16))
            res = _dot_nt(c_blk[:, g * SSD_STATE:(g + 1) * SSD_STATE], jnp.concatenate(tiles, axis=0))
            y_acc[g] = jnp.where(rowb == bb, res, y_acc[g])
    for g in range(SSD_GROUPS):
        y_scr[pl.ds(r8, BT), g * GROUP_W:(g + 1) * GROUP_W] = y_acc[g]

    @pl.when(i == pl.num_programs(0) - 1)
    def _():
        for cc in range(ATT_HEADS // 2):
            g = cc // (ATT_HEADS // ATT_KV_HEADS // 2)
            oe = o_scr[pl.ds(2 * cc, NB, stride=ATT_HEADS), :]
            oo = o_scr[pl.ds(2 * cc + 1, NB, stride=ATT_HEADS), :]
            if g == 0:
                oo = pltpu.roll(oo, HALF, 1)
            else:
                oe = pltpu.roll(oe, HALF, 1)
            gate = p_scr[:, GA_OFF + cc * LANES:GA_OFF + (cc + 1) * LANES]
            mix_scr[:, cc * LANES:(cc + 1) * LANES] = (jnp.where(lo_f, oe, oo) * _silu(gate)).astype(BF16)
        y = (y_scr[...] + svec_ref[0:1, :] * p_scr[:, XBC_OFF:XBC_OFF + SSD_WIDTH]) * _silu(p_scr[:, Z_OFF:Z_OFF + SSD_WIDTH])
        y = y * lax.rsqrt(jnp.mean(y * y, -1, keepdims=True) + NORM_EPS) * svec_ref[1:2, :]
        mix_scr[:, MIX_SSD:MIX_SSD + SSD_WIDTH] = y.astype(BF16)
        out = _dot(mix_scr[...], wout_ref[...])
        y_ref[...] = _layer_norm(DEEPNORM_ALPHA * x_ref[...] + out, ln_ref[0:1, :], ln_ref[1:2, :])


def _sample_layer(layer, x, rope_tab, sinks, prm, ck, cv, lc, lh, sc, sh, prev):
    NB = x.shape[0]
    BT = SAMPLE_BT
    H = CONV_W - 1
    win, aux, wg, wout, lcw, lvec, scw, scb, hvec, svec, ln, _ = prm
    whole = lambda shape: pl.BlockSpec(shape, lambda i: (0,) * len(shape), pipeline_mode=pl.Buffered(1))
    lwhole = lambda shape: pl.BlockSpec((None,) + shape, lambda i: (layer,) + (0,) * len(shape),
                                        pipeline_mode=pl.Buffered(1))
    rows = lambda shape: pl.BlockSpec((None, BT) + shape, lambda i: (layer, i) + (0,) * len(shape))
    state_specs = [
        rows((ATT_KV_WIDTH, WINDOW)),
        rows((ATT_KV_WIDTH, WINDOW)),
        lwhole((H, NB, LRU_WIDTH)),
        lwhole((NB, LRU_WIDTH)),
        lwhole((H, NB, SSD_CONV_CH)),
        rows((SSD_HEADS, SSD_HEAD_DIM, SSD_STATE)),
    ]
    in_specs = [
        pl.BlockSpec(memory_space=pltpu.SMEM),
        whole((NB, D_MODEL)),
        whole((1, 3 * LANES)),
        _const_spec((D_MODEL, W_IN_COLS), layer),
        _const_spec((D_MODEL, AUX_COLS), layer),
        _const_spec((LRU_WIDTH, 2 * LRU_WIDTH), layer),
        _const_spec((D_MIX, D_MODEL), layer),
        _const_spec((CONV_W, LRU_WIDTH), layer),
        _const_spec((4, LRU_WIDTH), layer),
        _const_spec((CONV_W, SSD_CONV_CH), layer),
        _const_spec((1, SSD_CONV_CH), layer),
        _const_spec((2, LANES), layer),
        _const_spec((2, SSD_WIDTH), layer),
        _const_spec((2, D_MODEL), layer),
    ] + state_specs
    operands = [sinks, x, rope_tab, win, aux, wg, wout, lcw, lvec, scw, scb, hvec, svec, ln, ck, cv, lc, lh, sc, sh]
    aliases = {}
    if prev is not None:
        for k, arr in enumerate(prev):
            aliases[len(operands)] = 1 + k
            operands.append(arr)
            in_specs.append(pl.BlockSpec(memory_space=pl.ANY))
    out_shape = (jax.ShapeDtypeStruct((NB, D_MODEL), F32),) + tuple(
        jax.ShapeDtypeStruct(a.shape, F32) for a in (ck, cv, lc, lh, sc, sh))
    full = lambda shape: pl.BlockSpec(shape, lambda i: (0,) * len(shape))
    lfull = lambda shape: pl.BlockSpec((None,) + shape, lambda i: (layer,) + (0,) * len(shape))
    out_specs = (
        full((NB, D_MODEL)),
        rows((ATT_KV_WIDTH, WINDOW)),
        rows((ATT_KV_WIDTH, WINDOW)),
        lfull((H, NB, LRU_WIDTH)),
        lfull((NB, LRU_WIDTH)),
        lfull((H, NB, SSD_CONV_CH)),
        rows((SSD_HEADS, SSD_HEAD_DIM, SSD_STATE)),
    )
    scratch = [
        pltpu.VMEM((NB, N_COLS), F32),
        pltpu.VMEM((NB * ATT_HEADS, LANES), F32),
        pltpu.VMEM((NB * ATT_HEADS, LANES), F32),
        pltpu.VMEM((NB, ATT_KV_WIDTH), F32),
        pltpu.VMEM((NB, ATT_KV_WIDTH), F32),
        pltpu.VMEM((NB, SSD_WIDTH), F32),
        pltpu.VMEM((NB, LANES), F32),
        pltpu.VMEM((NB, SSD_WIDTH), F32),
        pltpu.VMEM((NB, D_MIX), BF16),
    ]
    return pl.pallas_call(
        functools.partial(_sample_kernel, len(aliases)),
        grid=(NB // BT,),
        in_specs=in_specs,
        out_specs=out_specs,
        out_shape=out_shape,
        scratch_shapes=scratch,
        input_output_aliases=aliases,
        compiler_params=pltpu.CompilerParams(
            dimension_semantics=("arbitrary",),
            vmem_limit_bytes=VMEM_LIMIT_BYTES),
        name=f"sample_layer{layer}",
    )(*operands)


def _rope_table(start, length):
    half = ROPE_DIM // 2
    inv = (np.float32(ROPE_THETA) ** (-np.arange(half, dtype=np.float32) / np.float32(half))).astype(np.float32)
    ang = (np.arange(start, start + length).astype(np.float32)[:, None] * inv[None, :]).astype(np.float32)
    cos, sin = np.cos(ang), np.sin(ang)
    d = np.arange(LANES) % ATT_HEAD_DIM
    fi = d % half
    c = np.where(d < ROPE_DIM, cos[:, fi], np.float32(1.0))
    s1 = np.where(d < half, -sin[:, fi], np.float32(0.0))
    s2 = np.where((d >= half) & (d < ROPE_DIM), sin[:, fi], np.float32(0.0))
    return jnp.asarray(np.concatenate([c, s1, s2], axis=1).astype(np.float32))


def _prep_params(w_in, w_out, lru_conv_w, lru_conv_b, lru_wa, lru_ba, lru_wx, lru_bx, lru_lambda,
                 ssd_conv_w, ssd_conv_b, ssd_dt_bias, ssd_a_log, ssd_d, ssd_norm_g, ln_g, ln_b):
    hd = ATT_HEAD_DIM
    win = w_in.astype(BF16)
    k = w_in[..., ATT_WIDTH:ATT_WIDTH + ATT_KV_WIDTH]
    v = w_in[..., ATT_WIDTH + ATT_KV_WIDTH:ATT_WIDTH + 2 * ATT_KV_WIDTH]
    dup = lambda t: jnp.concatenate([t[..., :hd], t[..., :hd], t[..., hd:], t[..., hd:]], -1)
    dtw = jnp.pad(w_in[..., W_IN_COLS - SSD_HEADS:], ((0, 0), (0, 0), (0, LANES - SSD_HEADS)))
    aux = jnp.concatenate([dup(k), dup(v), dtw], -1).astype(BF16)

    def dense(w):
        rows = [jnp.pad(w[:, n], ((0, 0), (0, 0), (n * LRU_BLOCK, LRU_WIDTH - (n + 1) * LRU_BLOCK)))
                for n in range(LRU_BLOCKS)]
        return jnp.concatenate(rows, 1)

    wg = jnp.concatenate([dense(lru_wa.astype(BF16)), dense(lru_wx.astype(BF16))], -1)
    wout = w_out.astype(BF16)
    lvec = jnp.stack([lru_conv_b, lru_ba, lru_bx, lru_lambda], 1)
    hpad = lambda t: jnp.pad(t, ((0, 0), (0, LANES - SSD_HEADS)))
    hvec = jnp.stack([hpad(ssd_dt_bias), hpad(ssd_a_log)], 1)
    rpad = lambda t: jnp.broadcast_to(jnp.pad(t, ((0, 0), (0, HEAD_ROWS - SSD_HEADS)))[:, :, None],
                                      (DEPTH, HEAD_ROWS, LANES))
    hcol = jnp.stack([rpad(ssd_dt_bias), rpad(ssd_a_log)], 1)
    svec = jnp.stack([jnp.repeat(ssd_d, SSD_HEAD_DIM, axis=1), ssd_norm_g], 1)
    ln = jnp.stack([ln_g, ln_b], 1)
    return (win, aux, wg, wout, lru_conv_w, lvec, ssd_conv_w, ssd_conv_b[:, None, :], hvec, svec, ln, hcol)


def kernel(x_prompt, x_sample, cache_swa_k, cache_swa_v, state_lru_conv, state_lru_h, state_ssd_conv, state_ssd_h, w_in, w_out, att_sinks, lru_conv_w, lru_conv_b, lru_wa, lru_ba, lru_wx, lru_bx, lru_lambda, ssd_conv_w, ssd_conv_b, ssd_dt_bias, ssd_a_log, ssd_d, ssd_norm_g, ln_g, ln_b):
    prm = _prep_params(w_in, w_out, lru_conv_w, lru_conv_b, lru_wa, lru_ba, lru_wx, lru_bx, lru_lambda,
                       ssd_conv_w, ssd_conv_b, ssd_dt_bias, ssd_a_log, ssd_d, ssd_norm_g, ln_g, ln_b)
    bp, lp, _ = x_prompt.shape
    rope_p = _rope_table(0, lp)
    xp = x_prompt
    new_p = [[] for _ in range(6)]
    for l in range(DEPTH):
        xp, ko, vo, lco, lho, sco, sho = _prompt_layer(l, xp, rope_p, att_sinks[l], prm)
        st = (ko.reshape(bp, WINDOW, ATT_KV_HEADS, ATT_HEAD_DIM), vo.reshape(bp, WINDOW, ATT_KV_HEADS, ATT_HEAD_DIM),
              lco, lho.reshape(bp, LRU_WIDTH), sco, sho)
        for lst, t in zip(new_p, st):
            lst.append(t)
    outs_p = [jnp.stack(t) for t in new_p]

    nb = x_sample.shape[0]
    rope_s = _rope_table(PAST_LEN, x_sample.shape[1])
    to_kt = lambda t: jnp.transpose(t, (0, 1, 3, 4, 2)).reshape(DEPTH, nb, ATT_KV_WIDTH, WINDOW)
    from_kt = lambda t: jnp.transpose(t.reshape(DEPTH, nb, ATT_KV_HEADS, ATT_HEAD_DIM, WINDOW), (0, 1, 4, 2, 3))
    swap = lambda t: jnp.transpose(t, (0, 2, 1, 3))
    state = (to_kt(cache_swa_k), to_kt(cache_swa_v), swap(state_lru_conv), state_lru_h, swap(state_ssd_conv),
             state_ssd_h)
    xs = x_sample.reshape(nb, D_MODEL)
    new = None
    for l in range(DEPTH):
        xs, *new = _sample_layer(l, xs, rope_s, att_sinks[l], prm, *state, new)
    cko, cvo, lco, lho, sco, sho = new
    outs_s = (from_kt(cko), from_kt(cvo), swap(lco), lho, swap(sco), sho)
    return (xp, xs.reshape(x_sample.shape)) + tuple(outs_p) + tuple(outs_s)
```

```python
import functools

import jax
import jax.numpy as jnp
import numpy as np
from jax import lax
from jax.experimental import pallas as pl
from jax.experimental.pallas import tpu as pltpu

F32 = jnp.float32
BF16 = jnp.bfloat16

D_MODEL = 1024
DEPTH = 4
PAST_LEN = 8192
D_MIX = 2 * D_MODEL
ATT_HEADS = 8
ATT_KV_HEADS = 2
ATT_HEAD_DIM = 64
ATT_WIDTH = ATT_HEADS * ATT_HEAD_DIM
ATT_KV_WIDTH = ATT_KV_HEADS * ATT_HEAD_DIM
ATT_SCALE = ATT_HEAD_DIM ** -0.5
WINDOW = 128
ROPE_THETA = 500000.0
ROPE_DIM = ATT_HEAD_DIM // 4
LRU_WIDTH = 3 * D_MIX // 8
LRU_BLOCKS = 8
LRU_BLOCK = LRU_WIDTH // LRU_BLOCKS
LRU_C = 8.0
CONV_W = 4
SSD_WIDTH = D_MIX - ATT_WIDTH - LRU_WIDTH
SSD_HEAD_DIM = 64
SSD_HEADS = SSD_WIDTH // SSD_HEAD_DIM
SSD_GROUPS = 2
SSD_STATE = 128
SSD_CONV_CH = SSD_WIDTH + 2 * SSD_GROUPS * SSD_STATE
DEEPNORM_ALPHA = (2.0 * DEPTH) ** 0.25
NORM_EPS = 1e-5

LANES = 128
SUBLANES = 8
HALF = LANES // 2
VMEM_LIMIT_BYTES = 60 * 1024 * 1024

Q_OFF = 0
K_OFF = Q_OFF + ATT_WIDTH
V_OFF = K_OFF + ATT_KV_WIDTH
GA_OFF = V_OFF + ATT_KV_WIDTH
XL_OFF = GA_OFF + ATT_WIDTH
GL_OFF = XL_OFF + LRU_WIDTH
Z_OFF = GL_OFF + LRU_WIDTH
XBC_OFF = Z_OFF + SSD_WIDTH
DT_OFF = XBC_OFF + SSD_CONV_CH
N_COLS = DT_OFF + LANES
W_IN_COLS = 2 * ATT_WIDTH + 2 * ATT_KV_WIDTH + 2 * LRU_WIDTH + SSD_WIDTH + SSD_CONV_CH + SSD_HEADS
B_OFF = XBC_OFF + SSD_WIDTH
C_OFF = B_OFF + SSD_GROUPS * SSD_STATE
MIX_LRU = ATT_WIDTH
MIX_SSD = ATT_WIDTH + LRU_WIDTH

N_LRU_TILES = LRU_WIDTH // LANES
N_SSD_TILES = SSD_WIDTH // LANES
N_CONV_TILES = SSD_CONV_CH // LANES
CONV_STRIDE = 4
TILES_PER_GROUP = N_SSD_TILES // SSD_GROUPS
GROUP_W = SSD_WIDTH // SSD_GROUPS
HEAD_ROWS = 16

NEG_BIG = -1e30

PROMPT_TILE = 256
SAMPLE_BT = 8


def _sigmoid(x):
    return 0.5 * jnp.tanh(0.5 * x) + 0.5


def _silu(x):
    h = 0.5 * x
    return h + h * jnp.tanh(h)


def _softplus(x):
    return jnp.maximum(x, 0.0) + jnp.log1p(jnp.exp(-jnp.abs(x)))


def _lane_lo(shape):
    return (lax.broadcasted_iota(jnp.int32, shape, len(shape) - 1) % LANES) < HALF


def _rope(t, tab):
    half = ROPE_DIM // 2
    return (t * tab[:, 0:LANES]
            + pltpu.roll(t, LANES - half, 1) * tab[:, LANES:2 * LANES]
            + pltpu.roll(t, half, 1) * tab[:, 2 * LANES:3 * LANES])


def _dot(a, b):
    return jnp.dot(a, b, preferred_element_type=F32)


def _dot_nt(a, b):
    return lax.dot_general(a, b, (((1,), (1,)), ((), ())), preferred_element_type=F32)


def _conv_classes(src_ref, j, rows, w_ref, bias, cs):
    n = CONV_STRIDE
    hist = CONV_W - 1
    taps = {s_: src_ref[j, pl.ds(SUBLANES + s_, rows // n, stride=n), :] for s_ in range(-hist, n)}
    outs = []
    for k in range(n):
        acc = bias + w_ref[hist:hist + 1, cs] * taps[k]
        for t in range(hist):
            acc = acc + w_ref[t:t + 1, cs] * taps[k - hist + t]
        outs.append(acc)
    return outs


def _layer_norm(v, g, b):
    mu = jnp.mean(v, -1, keepdims=True)
    d = v - mu
    var = jnp.mean(d * d, -1, keepdims=True)
    return d * lax.rsqrt(var + NORM_EPS) * g + b


def _lru_coeffs(xl, gates, lvec_ref):
    r = _sigmoid(gates[:, 0:LRU_WIDTH] + lvec_ref[1:2, :])
    ig = _sigmoid(gates[:, LRU_WIDTH:2 * LRU_WIDTH] + lvec_ref[2:3, :])
    log_a = (-LRU_C * _softplus(-lvec_ref[3:4, :])) * r
    a = jnp.exp(log_a)
    bt = jnp.sqrt(-jnp.tanh(log_a) * (1.0 + a * a)) * (ig * xl)
    return a, bt


def _project(x_tile, win_ref, dtw_ref, dst):
    xb = x_tile.astype(BF16)
    for lo, hi in ((Q_OFF, XL_OFF), (XL_OFF, Z_OFF), (Z_OFF, DT_OFF)):
        dst[:, lo:hi] = _dot(xb, win_ref[:, lo:hi])
    dst[:, DT_OFF:N_COLS] = _dot(xb, dtw_ref[...])


def _both_halves(t, lo):
    r = pltpu.roll(t, HALF, 1)
    return jnp.where(lo, t, r), jnp.where(lo, r, t)


def _prompt_kernel(sinks_ref, x_ref, rope_ref, win_ref, aux_ref, wg_ref, wout_ref, lcw_ref, lvec_ref,
                   scw_ref, scb_ref, hcol_ref, svec_ref, ln_ref,
                   y_ref, ko_ref, vo_ref, lco_ref, lho_ref, sco_ref, sho_ref,
                   p_scr, kd_scr, vd_scr, lx_scr, sx_scr, sa_scr, sb_scr, hl_scr, ht_scr, mix_scr, xlb_scr, xc_scr):
    T = PROMPT_TILE
    nsub = T // WINDOW
    c = pl.program_id(1)
    last = c == pl.num_programs(1) - 1

    @pl.when(c == 0)
    def _():
        kd_scr[:, 0:WINDOW, :] = jnp.zeros((2, WINDOW, LANES), BF16)
        vd_scr[:, 0:WINDOW, :] = jnp.zeros((2, WINDOW, LANES), BF16)
        lx_scr[:, 0:SUBLANES, :] = jnp.zeros((N_LRU_TILES, SUBLANES, LANES), F32)
        sx_scr[:, 0:SUBLANES, :] = jnp.zeros((N_CONV_TILES, SUBLANES, LANES), F32)
        hl_scr[...] = jnp.zeros_like(hl_scr)
        ht_scr[...] = jnp.zeros_like(ht_scr)

    _project(x_ref[...], win_ref, aux_ref, p_scr)

    lo_sq = _lane_lo((WINDOW, LANES))

    lo_t = _lane_lo((T, LANES))
    kds = _both_halves(_rope(p_scr[:, K_OFF:K_OFF + LANES], rope_ref[...]), lo_t)
    vds = _both_halves(p_scr[:, V_OFF:V_OFF + LANES], lo_t)
    for g in range(ATT_KV_HEADS):
        kd_scr[g, WINDOW:WINDOW + T, :] = kds[g].astype(BF16)
        vd_scr[g, WINDOW:WINDOW + T, :] = vds[g].astype(BF16)

    row = lax.broadcasted_iota(jnp.int32, (WINDOW, 2 * WINDOW), 0)
    col = lax.broadcasted_iota(jnp.int32, (WINDOW, 2 * WINDOW), 1)
    band = (col >= row) & (col <= row + WINDOW)
    first_lo = jnp.where(c > 0, 0, WINDOW)
    for i in range(nsub):
        r0 = i * WINDOW
        msk = (band & (col >= first_lo)) if i == 0 else band
        tab = rope_ref[r0:r0 + WINDOW, :]
        for cc in range(ATT_HEADS // 2):
            g = cc // (ATT_HEADS // ATT_KV_HEADS // 2)
            kg = kd_scr[g, r0:r0 + 2 * WINDOW, :]
            vg = vd_scr[g, r0:r0 + 2 * WINDOW, :]
            qp = _rope(p_scr[r0:r0 + WINDOW, Q_OFF + cc * LANES:Q_OFF + (cc + 1) * LANES], tab) * ATT_SCALE
            outs = []
            for half in range(2):
                sink = sinks_ref[2 * cc + half]
                qm = jnp.where(lo_sq if half == 0 else jnp.logical_not(lo_sq), qp, 0.0).astype(BF16)
                s = jnp.where(msk, _dot_nt(qm, kg), NEG_BIG)
                m = jnp.maximum(jnp.max(s, -1, keepdims=True), sink)
                e = jnp.exp(s - m)
                den = jnp.sum(e, -1, keepdims=True) + jnp.exp(sink - m)
                outs.append(_dot(e.astype(BF16), vg) * (1.0 / den))
            att = jnp.where(lo_sq, outs[0], outs[1])
            gate = p_scr[r0:r0 + WINDOW, GA_OFF + cc * LANES:GA_OFF + (cc + 1) * LANES]
            mix_scr[r0:r0 + WINDOW, cc * LANES:(cc + 1) * LANES] = (att * _silu(gate)).astype(BF16)
    for g in range(ATT_KV_HEADS):
        kd_scr[g, 0:WINDOW, :] = kd_scr[g, T:T + WINDOW, :]
        vd_scr[g, 0:WINDOW, :] = vd_scr[g, T:T + WINDOW, :]

    P = SUBLANES
    H = CONV_W - 1
    for j in range(N_LRU_TILES):
        cs_ = slice(j * LANES, (j + 1) * LANES)
        lx_scr[j, P:P + T, :] = p_scr[:, XL_OFF + j * LANES:XL_OFF + (j + 1) * LANES]
        for k, xl in enumerate(_conv_classes(lx_scr, j, T, lcw_ref, lvec_ref[0:1, cs_], cs_)):
            sb_scr[j, pl.ds(k, T // CONV_STRIDE, stride=CONV_STRIDE), :] = xl
        xlb_scr[:, cs_] = sb_scr[j].astype(BF16)
    lx_scr[:, 0:P, :] = lx_scr[:, T:T + P, :]

    G = T // SUBLANES
    row_g = lax.broadcasted_iota(jnp.int32, (G, LANES), 0)
    cl = -LRU_C * _softplus(-lvec_ref[3:4, :])
    for jj in range(N_LRU_TILES // 2):
        c0 = jj * 2 * LANES
        k0 = (c0 // LRU_BLOCK) * LRU_BLOCK // LANES * LANES
        k1 = -(-(-(-(c0 + 2 * LANES) // LRU_BLOCK) * LRU_BLOCK) // LANES) * LANES
        xk = xlb_scr[:, k0:k1]
        gr = _dot(xk, wg_ref[k0:k1, c0:c0 + 2 * LANES])
        gi = _dot(xk, wg_ref[k0:k1, LRU_WIDTH + c0:LRU_WIDTH + c0 + 2 * LANES])
        for u in range(2):
            j = 2 * jj + u
            cs_ = slice(j * LANES, (j + 1) * LANES)
            us = slice(u * LANES, (u + 1) * LANES)
            r = _sigmoid(gr[:, us] + lvec_ref[1:2, cs_])
            ig = _sigmoid(gi[:, us] + lvec_ref[2:3, cs_])
            log_a = cl[:, cs_] * r
            a = jnp.exp(log_a)
            sa_scr[j] = a
            sb_scr[j] = jnp.sqrt(-jnp.tanh(log_a) * (1.0 + a * a)) * (ig * sb_scr[j])
    for j in range(N_LRU_TILES):
        ca = sa_scr[j, pl.ds(0, G, stride=SUBLANES), :]
        cb = sb_scr[j, pl.ds(0, G, stride=SUBLANES), :]
        for k in range(1, SUBLANES):
            ak = sa_scr[j, pl.ds(k, G, stride=SUBLANES), :]
            cb = ak * cb + sb_scr[j, pl.ds(k, G, stride=SUBLANES), :]
            ca = ak * ca
            sa_scr[j, pl.ds(k, G, stride=SUBLANES), :] = ca
            sb_scr[j, pl.ds(k, G, stride=SUBLANES), :] = cb
        hin = hl_scr[0:1, j * LANES:(j + 1) * LANES]
        cb = cb + jnp.where(row_g == 0, ca * hin, 0.0)
        d = 1
        while d < G:
            ok = row_g >= d
            cb = cb + ca * jnp.where(ok, pltpu.roll(cb, d, 0), 0.0)
            ca = ca * jnp.where(ok, pltpu.roll(ca, d, 0), 1.0)
            d *= 2
        hprev = jnp.where(row_g == 0, hin, pltpu.roll(cb, 1, 0))
        for k in range(SUBLANES):
            sb_scr[j, pl.ds(k, G, stride=SUBLANES), :] = (
                sb_scr[j, pl.ds(k, G, stride=SUBLANES), :] + sa_scr[j, pl.ds(k, G, stride=SUBLANES), :] * hprev)
        hl_scr[0:1, j * LANES:(j + 1) * LANES] = cb[G - 1:G, :]
        gl = p_scr[:, GL_OFF + j * LANES:GL_OFF + (j + 1) * LANES]
        mix_scr[:, MIX_LRU + j * LANES:MIX_LRU + (j + 1) * LANES] = (sb_scr[j] * _silu(gl)).astype(BF16)

    for j in range(N_CONV_TILES):
        cs_ = slice(j * LANES, (j + 1) * LANES)
        sx_scr[j, P:P + T, :] = p_scr[:, XBC_OFF + j * LANES:XBC_OFF + (j + 1) * LANES]
        for k, xc in enumerate(_conv_classes(sx_scr, j, T, scw_ref, scb_ref[0:1, cs_], cs_)):
            xc_scr[j, pl.ds(k, T // CONV_STRIDE, stride=CONV_STRIDE), :] = _silu(xc)
    sx_scr[:, 0:P, :] = sx_scr[:, T:T + P, :]

    qi = lax.broadcasted_iota(jnp.int32, (WINDOW, WINDOW), 0)
    si = lax.broadcasted_iota(jnp.int32, (WINDOW, WINDOW), 1)
    causal = si <= qi
    upper = (qi <= si).astype(F32)
    hpad = jnp.zeros((WINDOW - HEAD_ROWS, WINDOW), F32)
    a_col = -jnp.exp(hcol_ref[1])
    for i in range(nsub):
        r0 = i * WINDOW
        dt_t = _softplus(p_scr[r0:r0 + WINDOW, DT_OFF:DT_OFF + LANES].T[0:HEAD_ROWS, :] + hcol_ref[0])
        cst = jnp.dot(dt_t * a_col, upper, precision=lax.Precision.HIGHEST, preferred_element_type=F32)
        dt_c = jnp.concatenate([dt_t, hpad], axis=0).T
        cs = jnp.concatenate([cst, hpad], axis=0).T
        ys = [None] * N_SSD_TILES
        for g in range(SSD_GROUPS):
            bg_t = xc_scr[N_SSD_TILES + g, r0:r0 + WINDOW, :].T.astype(BF16)
            cg = xc_scr[N_SSD_TILES + SSD_GROUPS + g, r0:r0 + WINDOW, :].astype(BF16)
            cbm = _dot(cg, bg_t)
            yo = _dot(cg, ht_scr[g].astype(BF16))
            xdec, edec = [], []
            for pp in range(TILES_PER_GROUP):
                p = g * TILES_PER_GROUP + pp
                bcs, mm, dts = [], [], []
                for h in (2 * p, 2 * p + 1):
                    bc = jnp.broadcast_to(cs[:, h:h + 1], (WINDOW, WINDOW))
                    lmat = jnp.exp(jnp.where(causal, bc - cst[h:h + 1, :], NEG_BIG))
                    mm.append((cbm * lmat).astype(BF16))
                    bcs.append(bc)
                    dts.append(jnp.broadcast_to(dt_c[:, h:h + 1], (WINDOW, LANES)))
                csl = jnp.where(lo_sq, bcs[0], bcs[1])
                ecs = jnp.exp(csl)
                dec = jnp.exp(csl[WINDOW - 1:WINDOW, :] - csl)
                xs_p = xc_scr[p, r0:r0 + WINDOW, :]
                xdt = xs_p * jnp.where(lo_sq, dts[0], dts[1])
                xdt_b = xdt.astype(BF16)
                y = jnp.where(lo_sq, _dot(mm[0], xdt_b), _dot(mm[1], xdt_b))
                y = y + yo[:, pp * LANES:(pp + 1) * LANES] * ecs
                y = y + svec_ref[0:1, p * LANES:(p + 1) * LANES] * xs_p
                ys[p] = y * _silu(p_scr[r0:r0 + WINDOW, Z_OFF + p * LANES:Z_OFF + (p + 1) * LANES])
                xdec.append((xdt * dec).astype(BF16))
                edec.append(ecs[WINDOW - 1:WINDOW, :])
            ht_scr[g] = (ht_scr[g] * jnp.concatenate(edec, axis=1)
                         + _dot(bg_t, jnp.concatenate(xdec, axis=1)))
        ss = jnp.sum(ys[0] * ys[0], -1, keepdims=True)
        for p in range(1, N_SSD_TILES):
            ss = ss + jnp.sum(ys[p] * ys[p], -1, keepdims=True)
        rinv = lax.rsqrt(ss * (1.0 / SSD_WIDTH) + NORM_EPS)
        for p in range(N_SSD_TILES):
            mix_scr[r0:r0 + WINDOW, MIX_SSD + p * LANES:MIX_SSD + (p + 1) * LANES] = (
                ys[p] * rinv * svec_ref[1:2, p * LANES:(p + 1) * LANES]).astype(BF16)

    out = _dot(mix_scr[...], wout_ref[...])
    y_ref[...] = _layer_norm(DEEPNORM_ALPHA * x_ref[...] + out, ln_ref[0:1, :], ln_ref[1:2, :])

    @pl.when(last)
    def _():
        ko_ref[...] = _rope(p_scr[T - WINDOW:T, K_OFF:K_OFF + LANES], rope_ref[T - WINDOW:T, :])
        vo_ref[...] = p_scr[T - WINDOW:T, V_OFF:V_OFF + LANES]
        for j in range(N_LRU_TILES):
            lco_ref[:, j * LANES:(j + 1) * LANES] = lx_scr[j, P + T - (CONV_W - 1):P + T, :]
        lho_ref[...] = hl_scr[...]
        for j in range(N_CONV_TILES):
            sco_ref[:, j * LANES:(j + 1) * LANES] = sx_scr[j, P + T - (CONV_W - 1):P + T, :]
        for g in range(SSD_GROUPS):
            for pp in range(TILES_PER_GROUP):
                tile = ht_scr[g, :, pp * LANES:(pp + 1) * LANES].T
                h = 2 * (g * TILES_PER_GROUP + pp)
                sho_ref[h] = tile[0:SSD_HEAD_DIM, :]
                sho_ref[h + 1] = tile[SSD_HEAD_DIM:2 * SSD_HEAD_DIM, :]


def _const_spec(shape, layer):
    nd = len(shape)
    return pl.BlockSpec((None,) + tuple(shape), lambda *_: (layer,) + (0,) * nd,
                        pipeline_mode=pl.Buffered(1))


def _prompt_layer(layer, x, rope_tab, sinks, prm):
    B, L, _ = x.shape
    T = PROMPT_TILE
    nc = L // T
    win, aux, wg, wout, lcw, lvec, scw, scb, hvec, svec, ln, hcol = prm
    tile = lambda b, c: (b, c, 0)
    per_b = lambda b, c: (b, 0, 0)

    in_specs = [
        pl.BlockSpec(memory_space=pltpu.SMEM),
        pl.BlockSpec((None, T, D_MODEL), tile),
        pl.BlockSpec((T, 3 * LANES), lambda b, c: (c, 0)),
        _const_spec((D_MODEL, W_IN_COLS), layer),
        _const_spec((D_MODEL, LANES), layer),
        _const_spec((LRU_WIDTH, 2 * LRU_WIDTH), layer),
        _const_spec((D_MIX, D_MODEL), layer),
        _const_spec((CONV_W, LRU_WIDTH), layer),
        _const_spec((4, LRU_WIDTH), layer),
        _const_spec((CONV_W, SSD_CONV_CH), layer),
        _const_spec((1, SSD_CONV_CH), layer),
        _const_spec((2, HEAD_ROWS, LANES), layer),
        _const_spec((2, SSD_WIDTH), layer),
        _const_spec((2, D_MODEL), layer),
    ]
    out_shape = (
        jax.ShapeDtypeStruct((B, L, D_MODEL), F32),
        jax.ShapeDtypeStruct((B, WINDOW, ATT_KV_WIDTH), F32),
        jax.ShapeDtypeStruct((B, WINDOW, ATT_KV_WIDTH), F32),
        jax.ShapeDtypeStruct((B, CONV_W - 1, LRU_WIDTH), F32),
        jax.ShapeDtypeStruct((B, 1, LRU_WIDTH), F32),
        jax.ShapeDtypeStruct((B, CONV_W - 1, SSD_CONV_CH), F32),
        jax.ShapeDtypeStruct((B, SSD_HEADS, SSD_HEAD_DIM, SSD_STATE), F32),
    )
    out_specs = (
        pl.BlockSpec((None, T, D_MODEL), tile),
        pl.BlockSpec((None, WINDOW, ATT_KV_WIDTH), per_b),
        pl.BlockSpec((None, WINDOW, ATT_KV_WIDTH), per_b),
        pl.BlockSpec((None, CONV_W - 1, LRU_WIDTH), per_b),
        pl.BlockSpec((None, 1, LRU_WIDTH), per_b),
        pl.BlockSpec((None, CONV_W - 1, SSD_CONV_CH), per_b),
        pl.BlockSpec((None, SSD_HEADS, SSD_HEAD_DIM, SSD_STATE), lambda b, c: (b, 0, 0, 0)),
    )
    scratch = [
        pltpu.VMEM((T, N_COLS), F32),
        pltpu.VMEM((ATT_KV_HEADS, WINDOW + T, LANES), BF16),
        pltpu.VMEM((ATT_KV_HEADS, WINDOW + T, LANES), BF16),
        pltpu.VMEM((N_LRU_TILES, SUBLANES + T, LANES), F32),
        pltpu.VMEM((N_CONV_TILES, SUBLANES + T, LANES), F32),
        pltpu.VMEM((N_LRU_TILES, T, LANES), F32),
        pltpu.VMEM((N_LRU_TILES, T, LANES), F32),
        pltpu.VMEM((1, LRU_WIDTH), F32),
        pltpu.VMEM((SSD_GROUPS, SSD_STATE, GROUP_W), F32),
        pltpu.VMEM((T, D_MIX), BF16),
        pltpu.VMEM((T, LRU_WIDTH), BF16),
        pltpu.VMEM((N_CONV_TILES, T, LANES), F32),
    ]
    return pl.pallas_call(
        _prompt_kernel,
        grid=(B, nc),
        in_specs=in_specs,
        out_specs=out_specs,
        out_shape=out_shape,
        scratch_shapes=scratch,
        compiler_params=pltpu.CompilerParams(
            dimension_semantics=("arbitrary", "arbitrary"),
            vmem_limit_bytes=VMEM_LIMIT_BYTES),
        name=f"prompt_layer{layer}",
    )(sinks, x, rope_tab, win, aux, wg, wout, lcw, lvec, scw, scb, hcol, svec, ln)


def _conv_step(x_new, st_ref, sto_ref, w_ref, bias):
    y = bias + w_ref[CONV_W - 1:CONV_W, :] * x_new
    for t in range(CONV_W - 1):
        y = y + w_ref[t:t + 1, :] * st_ref[t]
    for t in range(CONV_W - 2):
        sto_ref[t] = st_ref[t + 1]
    sto_ref[CONV_W - 2] = x_new
    return y


def _sample_kernel(n_aliased, sinks_ref, x_ref, rope_ref, win_ref, aux_ref, wg_ref, wout_ref, lcw_ref, lvec_ref,
                   scw_ref, scb_ref, hvec_ref, svec_ref, ln_ref,
                   ck_ref, cv_ref, lc_ref, lh_ref, sc_ref, sh_ref, *refs):
    (y_ref, cko_ref, cvo_ref, lco_ref, lho_ref, sco_ref, sho_ref,
     p_scr, q_scr, o_scr, kn_scr, vn_scr, xdt_scr, da_scr, y_scr, mix_scr) = refs[n_aliased:]
    NB = x_ref.shape[0]
    BT = SAMPLE_BT
    i = pl.program_id(0)
    lo_f = _lane_lo((NB, LANES))

    @pl.when(i == 0)
    def _():
        _project(x_ref[...], win_ref, aux_ref, p_scr)
        tab = rope_ref[...]
        for cc in range(ATT_HEADS // 2):
            g = cc // (ATT_HEADS // ATT_KV_HEADS // 2)
            qp = _rope(p_scr[:, Q_OFF + cc * LANES:Q_OFF + (cc + 1) * LANES], tab) * ATT_SCALE
            for half in range(2):
                t = jnp.where(lo_f if half == 0 else jnp.logical_not(lo_f), qp, 0.0)
                if half != g:
                    t = pltpu.roll(t, HALF, 1)
                q_scr[pl.ds(2 * cc + half, NB, stride=ATT_HEADS), :] = t
        kn_scr[...] = _rope(p_scr[:, K_OFF:K_OFF + LANES], tab)
        vn_scr[...] = p_scr[:, V_OFF:V_OFF + LANES]
        xin = p_scr[:, XL_OFF:XL_OFF + LRU_WIDTH]
        xl = _conv_step(xin, lc_ref, lco_ref, lcw_ref, lvec_ref[0:1, :])
        gates = _dot(xl.astype(BF16), wg_ref[...])
        a, bt = _lru_coeffs(xl, gates, lvec_ref)
        h1 = a * lh_ref[...] + bt
        lho_ref[...] = h1
        mix_scr[:, MIX_LRU:MIX_LRU + LRU_WIDTH] = (h1 * _silu(p_scr[:, GL_OFF:GL_OFF + LRU_WIDTH])).astype(BF16)
        xin2 = p_scr[:, XBC_OFF:XBC_OFF + SSD_CONV_CH]
        xc = _conv_step(xin2, sc_ref, sco_ref, scw_ref, scb_ref[0:1, :])
        p_scr[:, XBC_OFF:XBC_OFF + SSD_CONV_CH] = _silu(xc)
        dt = _softplus(p_scr[:, DT_OFF:DT_OFF + LANES] + hvec_ref[0:1, :])
        da_scr[...] = jnp.exp(dt * (-jnp.exp(hvec_ref[1:2, :])))
        for p in range(N_SSD_TILES):
            dte = jnp.where(lo_f, jnp.broadcast_to(dt[:, 2 * p:2 * p + 1], (NB, LANES)),
                            jnp.broadcast_to(dt[:, 2 * p + 1:2 * p + 2], (NB, LANES)))
            xdt_scr[:, p * LANES:(p + 1) * LANES] = p_scr[:, XBC_OFF + p * LANES:XBC_OFF + (p + 1) * LANES] * dte

    r8 = pl.multiple_of(i * BT, BT)
    kn_blk = kn_scr[pl.ds(r8, BT), :]
    vn_blk = vn_scr[pl.ds(r8, BT), :]
    zpad = jnp.zeros((LANES - BT, LANES), F32)
    kn_t = jnp.concatenate([kn_blk, zpad], axis=0).T
    vn_t = jnp.concatenate([vn_blk, zpad], axis=0).T
    newest = lax.broadcasted_iota(jnp.int32, (ATT_KV_WIDTH, WINDOW), 1) == WINDOW - 1
    da_t = jnp.concatenate([da_scr[pl.ds(r8, BT), :], zpad], axis=0).T
    x_t = [jnp.concatenate([xdt_scr[pl.ds(r8, BT), p * LANES:(p + 1) * LANES], zpad], axis=0).T
           for p in range(N_SSD_TILES)]
    b_blk = p_scr[pl.ds(r8, BT), B_OFF:B_OFF + SSD_GROUPS * SSD_STATE]
    c_blk = p_scr[pl.ds(r8, BT), C_OFF:C_OFF + SSD_GROUPS * SSD_STATE].astype(BF16)
    row8 = lax.broadcasted_iota(jnp.int32, (ATT_HEADS, 1), 0)
    sink = jnp.zeros((ATT_HEADS, 1), F32)
    for h in range(ATT_HEADS):
        sink = jnp.where(row8 == h, sinks_ref[h], sink)
    rowb = lax.broadcasted_iota(jnp.int32, (BT, GROUP_W), 0)
    y_acc = [jnp.zeros((BT, GROUP_W), F32) for _ in range(SSD_GROUPS)]
    rowk = lax.broadcasted_iota(jnp.int32, (BT, SSD_STATE), 0)
    outer = []
    for g in range(SSD_GROUPS):
        bg = b_blk[:, g * SSD_STATE:(g + 1) * SSD_STATE]
        diag = jnp.concatenate([jnp.where(rowk == bb, bg, 0.0) for bb in range(BT)], axis=1)
        rhs = jnp.concatenate([diag, jnp.zeros((LANES - BT, BT * SSD_STATE), F32)], axis=0).astype(BF16)
        outer.append([_dot(x_t[g * TILES_PER_GROUP + pp].astype(BF16), rhs) for pp in range(TILES_PER_GROUP)])
    for bb in range(BT):
        qr = q_scr[pl.ds(pl.multiple_of((r8 + bb) * ATT_HEADS, ATT_HEADS), ATT_HEADS), :]
        kt = ck_ref[bb]
        vt = cv_ref[bb]
        s = _dot(qr.astype(BF16), kt.astype(BF16))
        s_new = jnp.sum(qr * kn_blk[bb:bb + 1, :], -1, keepdims=True)
        m = jnp.maximum(jnp.maximum(jnp.max(s, -1, keepdims=True), s_new), sink)
        e = jnp.exp(s - m)
        e_new = jnp.exp(s_new - m)
        den = jnp.sum(e, -1, keepdims=True) + e_new + jnp.exp(sink - m)
        o = _dot_nt(e.astype(BF16), vt.astype(BF16)) + e_new * vn_blk[bb:bb + 1, :]
        o_scr[pl.ds(pl.multiple_of((r8 + bb) * ATT_HEADS, ATT_HEADS), ATT_HEADS), :] = o * (1.0 / den)
        cko_ref[bb] = jnp.where(newest, kn_t[:, bb:bb + 1], pltpu.roll(kt, WINDOW - 1, 1))
        cvo_ref[bb] = jnp.where(newest, vn_t[:, bb:bb + 1], pltpu.roll(vt, WINDOW - 1, 1))
        for g in range(SSD_GROUPS):
            tiles = []
            for pp in range(TILES_PER_GROUP):
                p = g * TILES_PER_GROUP + pp
                for hh in range(2):
                    h = 2 * p + hh
                    dab = jnp.broadcast_to(da_t[h:h + 1, bb:bb + 1], (SSD_HEAD_DIM, SSD_STATE))
                    h1 = sh_ref[bb, h] * dab + outer[g][pp][hh * SSD_HEAD_DIM:(hh + 1) * SSD_HEAD_DIM,
                                                               bb * SSD_STATE:(bb + 1) * SSD_STATE]
                    sho_ref[bb, h] = h1
                    tiles.append(h1.astype(BF16))
            res = _dot_nt(c_blk[:, g * SSD_STATE:(g + 1) * SSD_STATE], jnp.concatenate(tiles, axis=0))
            y_acc[g] = jnp.where(rowb == bb, res, y_acc[g])
    for g in range(SSD_GROUPS):
        y_scr[pl.ds(r8, BT), g * GROUP_W:(g + 1) * GROUP_W] = y_acc[g]

    @pl.when(i == pl.num_programs(0) - 1)
    def _():
        for cc in range(ATT_HEADS // 2):
            g = cc // (ATT_HEADS // ATT_KV_HEADS // 2)
            oe = o_scr[pl.ds(2 * cc, NB, stride=ATT_HEADS), :]
            oo = o_scr[pl.ds(2 * cc + 1, NB, stride=ATT_HEADS), :]
            if g == 0:
                oo = pltpu.roll(oo, HALF, 1)
            else:
                oe = pltpu.roll(oe, HALF, 1)
            gate = p_scr[:, GA_OFF + cc * LANES:GA_OFF + (cc + 1) * LANES]
            mix_scr[:, cc * LANES:(cc + 1) * LANES] = (jnp.where(lo_f, oe, oo) * _silu(gate)).astype(BF16)
        y = (y_scr[...] + svec_ref[0:1, :] * p_scr[:, XBC_OFF:XBC_OFF + SSD_WIDTH]) * _silu(p_scr[:, Z_OFF:Z_OFF + SSD_WIDTH])
        y = y * lax.rsqrt(jnp.mean(y * y, -1, keepdims=True) + NORM_EPS) * svec_ref[1:2, :]
        mix_scr[:, MIX_SSD:MIX_SSD + SSD_WIDTH] = y.astype(BF16)
        out = _dot(mix_scr[...], wout_ref[...])
        y_ref[...] = _layer_norm(DEEPNORM_ALPHA * x_ref[...] + out, ln_ref[0:1, :], ln_ref[1:2, :])


def _sample_layer(layer, x, rope_tab, sinks, prm, ck, cv, lc, lh, sc, sh, prev):
    NB = x.shape[0]
    BT = SAMPLE_BT
    H = CONV_W - 1
    win, aux, wg, wout, lcw, lvec, scw, scb, hvec, svec, ln, _ = prm
    whole = lambda shape: pl.BlockSpec(shape, lambda i: (0,) * len(shape), pipeline_mode=pl.Buffered(1))
    lwhole = lambda shape: pl.BlockSpec((None,) + shape, lambda i: (layer,) + (0,) * len(shape),
                                        pipeline_mode=pl.Buffered(1))
    rows = lambda shape: pl.BlockSpec((None, BT) + shape, lambda i: (layer, i) + (0,) * len(shape))
    state_specs = [
        rows((ATT_KV_WIDTH, WINDOW)),
        rows((ATT_KV_WIDTH, WINDOW)),
        lwhole((H, NB, LRU_WIDTH)),
        lwhole((NB, LRU_WIDTH)),
        lwhole((H, NB, SSD_CONV_CH)),
        rows((SSD_HEADS, SSD_HEAD_DIM, SSD_STATE)),
    ]
    in_specs = [
        pl.BlockSpec(memory_space=pltpu.SMEM),
        whole((NB, D_MODEL)),
        whole((1, 3 * LANES)),
        _const_spec((D_MODEL, W_IN_COLS), layer),
        _const_spec((D_MODEL, LANES), layer),
        _const_spec((LRU_WIDTH, 2 * LRU_WIDTH), layer),
        _const_spec((D_MIX, D_MODEL), layer),
        _const_spec((CONV_W, LRU_WIDTH), layer),
        _const_spec((4, LRU_WIDTH), layer),
        _const_spec((CONV_W, SSD_CONV_CH), layer),
        _const_spec((1, SSD_CONV_CH), layer),
        _const_spec((2, LANES), layer),
        _const_spec((2, SSD_WIDTH), layer),
        _const_spec((2, D_MODEL), layer),
    ] + state_specs
    operands = [sinks, x, rope_tab, win, aux, wg, wout, lcw, lvec, scw, scb, hvec, svec, ln, ck, cv, lc, lh, sc, sh]
    aliases = {}
    if prev is not None:
        for k, arr in enumerate(prev):
            aliases[len(operands)] = 1 + k
            operands.append(arr)
            in_specs.append(pl.BlockSpec(memory_space=pl.ANY))
    out_shape = (jax.ShapeDtypeStruct((NB, D_MODEL), F32),) + tuple(
        jax.ShapeDtypeStruct(a.shape, F32) for a in (ck, cv, lc, lh, sc, sh))
    full = lambda shape: pl.BlockSpec(shape, lambda i: (0,) * len(shape))
    lfull = lambda shape: pl.BlockSpec((None,) + shape, lambda i: (layer,) + (0,) * len(shape))
    out_specs = (
        full((NB, D_MODEL)),
        rows((ATT_KV_WIDTH, WINDOW)),
        rows((ATT_KV_WIDTH, WINDOW)),
        lfull((H, NB, LRU_WIDTH)),
        lfull((NB, LRU_WIDTH)),
        lfull((H, NB, SSD_CONV_CH)),
        rows((SSD_HEADS, SSD_HEAD_DIM, SSD_STATE)),
    )
    scratch = [
        pltpu.VMEM((NB, N_COLS), F32),
        pltpu.VMEM((NB * ATT_HEADS, LANES), F32),
        pltpu.VMEM((NB * ATT_HEADS, LANES), F32),
        pltpu.VMEM((NB, ATT_KV_WIDTH), F32),
        pltpu.VMEM((NB, ATT_KV_WIDTH), F32),
        pltpu.VMEM((NB, SSD_WIDTH), F32),
        pltpu.VMEM((NB, LANES), F32),
        pltpu.VMEM((NB, SSD_WIDTH), F32),
        pltpu.VMEM((NB, D_MIX), BF16),
    ]
    return pl.pallas_call(
        functools.partial(_sample_kernel, len(aliases)),
        grid=(NB // BT,),
        in_specs=in_specs,
        out_specs=out_specs,
        out_shape=out_shape,
        scratch_shapes=scratch,
        input_output_aliases=aliases,
        compiler_params=pltpu.CompilerParams(
            dimension_semantics=("arbitrary",),
            vmem_limit_bytes=VMEM_LIMIT_BYTES),
        name=f"sample_layer{layer}",
    )(*operands)


def _rope_table(start, length):
    half = ROPE_DIM // 2
    inv = (np.float32(ROPE_THETA) ** (-np.arange(half, dtype=np.float32) / np.float32(half))).astype(np.float32)
    ang = (np.arange(start, start + length).astype(np.float32)[:, None] * inv[None, :]).astype(np.float32)
    cos, sin = np.cos(ang), np.sin(ang)
    d = np.arange(LANES) % ATT_HEAD_DIM
    fi = d % half
    c = np.where(d < ROPE_DIM, cos[:, fi], np.float32(1.0))
    s1 = np.where(d < half, -sin[:, fi], np.float32(0.0))
    s2 = np.where((d >= half) & (d < ROPE_DIM), sin[:, fi], np.float32(0.0))
    return jnp.asarray(np.concatenate([c, s1, s2], axis=1).astype(np.float32))


def _prep_params(w_in, w_out, lru_conv_w, lru_conv_b, lru_wa, lru_ba, lru_wx, lru_bx, lru_lambda,
                 ssd_conv_w, ssd_conv_b, ssd_dt_bias, ssd_a_log, ssd_d, ssd_norm_g, ln_g, ln_b):
    win = w_in.astype(BF16)
    aux = jnp.pad(w_in[..., DT_OFF:], ((0, 0), (0, 0), (0, LANES - SSD_HEADS))).astype(BF16)

    def dense(w):
        rows = [jnp.pad(w[:, n], ((0, 0), (0, 0), (n * LRU_BLOCK, LRU_WIDTH - (n + 1) * LRU_BLOCK)))
                for n in range(LRU_BLOCKS)]
        return jnp.concatenate(rows, 1)

    wg = jnp.concatenate([dense(lru_wa.astype(BF16)), dense(lru_wx.astype(BF16))], -1)
    wout = w_out.astype(BF16)
    lvec = jnp.stack([lru_conv_b, lru_ba, lru_bx, lru_lambda], 1)
    hpad = lambda t: jnp.pad(t, ((0, 0), (0, LANES - SSD_HEADS)))
    hvec = jnp.stack([hpad(ssd_dt_bias), hpad(ssd_a_log)], 1)
    rpad = lambda t: jnp.broadcast_to(jnp.pad(t, ((0, 0), (0, HEAD_ROWS - SSD_HEADS)))[:, :, None],
                                      (DEPTH, HEAD_ROWS, LANES))
    hcol = jnp.stack([rpad(ssd_dt_bias), rpad(ssd_a_log)], 1)
    svec = jnp.stack([jnp.repeat(ssd_d, SSD_HEAD_DIM, axis=1), ssd_norm_g], 1)
    ln = jnp.stack([ln_g, ln_b], 1)
    return (win, aux, wg, wout, lru_conv_w, lvec, ssd_conv_w, ssd_conv_b[:, None, :], hvec, svec, ln, hcol)


def kernel(x_prompt, x_sample, cache_swa_k, cache_swa_v, state_lru_conv, state_lru_h, state_ssd_conv, state_ssd_h, w_in, w_out, att_sinks, lru_conv_w, lru_conv_b, lru_wa, lru_ba, lru_wx, lru_bx, lru_lambda, ssd_conv_w, ssd_conv_b, ssd_dt_bias, ssd_a_log, ssd_d, ssd_norm_g, ln_g, ln_b):
    prm = _prep_params(w_in, w_out, lru_conv_w, lru_conv_b, lru_wa, lru_ba, lru_wx, lru_bx, lru_lambda,
                       ssd_conv_w, ssd_conv_b, ssd_dt_bias, ssd_a_log, ssd_d, ssd_norm_g, ln_g, ln_b)
    bp, lp, _ = x_prompt.shape
    rope_p = _rope_table(0, lp)
    xp = x_prompt
    new_p = [[] for _ in range(6)]
    for l in range(DEPTH):
        xp, ko, vo, lco, lho, sco, sho = _prompt_layer(l, xp, rope_p, att_sinks[l], prm)
        st = (ko.reshape(bp, WINDOW, ATT_KV_HEADS, ATT_HEAD_DIM), vo.reshape(bp, WINDOW, ATT_KV_HEADS, ATT_HEAD_DIM),
              lco, lho.reshape(bp, LRU_WIDTH), sco, sho)
        for lst, t in zip(new_p, st):
            lst.append(t)
    outs_p = [jnp.stack(t) for t in new_p]

    nb = x_sample.shape[0]
    rope_s = _rope_table(PAST_LEN, x_sample.shape[1])
    to_kt = lambda t: jnp.transpose(t, (0, 1, 3, 4, 2)).reshape(DEPTH, nb, ATT_KV_WIDTH, WINDOW)
    from_kt = lambda t: jnp.transpose(t.reshape(DEPTH, nb, ATT_KV_HEADS, ATT_HEAD_DIM, WINDOW), (0, 1, 4, 2, 3))
    swap = lambda t: jnp.transpose(t, (0, 2, 1, 3))
    state = (to_kt(cache_swa_k), to_kt(cache_swa_v), swap(state_lru_conv), state_lru_h, swap(state_ssd_conv),
             state_ssd_h)
    xs = x_sample.reshape(nb, D_MODEL)
    new = None
    for l in range(DEPTH):
        xs, *new = _sample_layer(l, xs, rope_s, att_sinks[l], prm, *state, new)
    cko, cvo, lco, lho, sco, sho = new
    outs_s = (from_kt(cko), from_kt(cvo), swap(lco), lho, swap(sco), sho)
    return (xp, xs.reshape(x_sample.shape)) + tuple(outs_p) + tuple(outs_s)
```

```python
import functools

import jax
import jax.numpy as jnp
import numpy as np
from jax import lax
from jax.experimental import pallas as pl
from jax.experimental.pallas import tpu as pltpu

F32 = jnp.float32
BF16 = jnp.bfloat16

D_MODEL = 1024
DEPTH = 4
PAST_LEN = 8192
D_MIX = 2 * D_MODEL
ATT_HEADS = 8
ATT_KV_HEADS = 2
ATT_HEAD_DIM = 64
ATT_WIDTH = ATT_HEADS * ATT_HEAD_DIM
ATT_KV_WIDTH = ATT_KV_HEADS * ATT_HEAD_DIM
ATT_SCALE = ATT_HEAD_DIM ** -0.5
WINDOW = 128
ROPE_THETA = 500000.0
ROPE_DIM = ATT_HEAD_DIM // 4
LRU_WIDTH = 3 * D_MIX // 8
LRU_BLOCKS = 8
LRU_BLOCK = LRU_WIDTH // LRU_BLOCKS
LRU_C = 8.0
CONV_W = 4
SSD_WIDTH = D_MIX - ATT_WIDTH - LRU_WIDTH
SSD_HEAD_DIM = 64
SSD_HEADS = SSD_WIDTH // SSD_HEAD_DIM
SSD_GROUPS = 2
SSD_STATE = 128
SSD_CONV_CH = SSD_WIDTH + 2 * SSD_GROUPS * SSD_STATE
DEEPNORM_ALPHA = (2.0 * DEPTH) ** 0.25
NORM_EPS = 1e-5

LANES = 128
SUBLANES = 8
HALF = LANES // 2
VMEM_LIMIT_BYTES = 60 * 1024 * 1024

Q_OFF = 0
K_OFF = Q_OFF + ATT_WIDTH
V_OFF = K_OFF + 2 * ATT_KV_WIDTH
GA_OFF = V_OFF + 2 * ATT_KV_WIDTH
XL_OFF = GA_OFF + ATT_WIDTH
GL_OFF = XL_OFF + LRU_WIDTH
Z_OFF = GL_OFF + LRU_WIDTH
XBC_OFF = Z_OFF + SSD_WIDTH
DT_OFF = XBC_OFF + SSD_CONV_CH
N_COLS = DT_OFF + LANES
W_IN_COLS = 2 * ATT_WIDTH + 2 * ATT_KV_WIDTH + 2 * LRU_WIDTH + SSD_WIDTH + SSD_CONV_CH + SSD_HEADS
AUX_COLS = 4 * ATT_KV_WIDTH + LANES
B_OFF = XBC_OFF + SSD_WIDTH
C_OFF = B_OFF + SSD_GROUPS * SSD_STATE
MIX_LRU = ATT_WIDTH
MIX_SSD = ATT_WIDTH + LRU_WIDTH

N_LRU_TILES = LRU_WIDTH // LANES
N_SSD_TILES = SSD_WIDTH // LANES
N_CONV_TILES = SSD_CONV_CH // LANES
CONV_STRIDE = 4
TILES_PER_GROUP = N_SSD_TILES // SSD_GROUPS
GROUP_W = SSD_WIDTH // SSD_GROUPS
HEAD_ROWS = 16

NEG_BIG = -1e30

PROMPT_TILE = 256
SAMPLE_BT = 8
RING_SLOTS = 3


def _sigmoid(x):
    return 0.5 * jnp.tanh(0.5 * x) + 0.5


def _silu(x):
    h = 0.5 * x
    return h + h * jnp.tanh(h)


def _softplus(x):
    return jnp.maximum(x, 0.0) + jnp.log1p(jnp.exp(-jnp.abs(x)))


def _lane_lo(shape):
    return (lax.broadcasted_iota(jnp.int32, shape, len(shape) - 1) % LANES) < HALF


def _rope(t, tab):
    half = ROPE_DIM // 2
    return (t * tab[:, 0:LANES]
            + pltpu.roll(t, LANES - half, 1) * tab[:, LANES:2 * LANES]
            + pltpu.roll(t, half, 1) * tab[:, 2 * LANES:3 * LANES])


def _dot(a, b):
    return jnp.dot(a, b, preferred_element_type=F32)


def _dot_nt(a, b):
    return lax.dot_general(a, b, (((1,), (1,)), ((), ())), preferred_element_type=F32)


def _conv_classes(src_ref, j, rows, w_ref, bias, cs):
    n = CONV_STRIDE
    hist = CONV_W - 1
    taps = {s_: src_ref[j, pl.ds(SUBLANES + s_, rows // n, stride=n), :] for s_ in range(-hist, n)}
    outs = []
    for k in range(n):
        acc = bias + w_ref[hist:hist + 1, cs] * taps[k]
        for t in range(hist):
            acc = acc + w_ref[t:t + 1, cs] * taps[k - hist + t]
        outs.append(acc)
    return outs


def _layer_norm(v, g, b):
    mu = jnp.mean(v, -1, keepdims=True)
    d = v - mu
    var = jnp.mean(d * d, -1, keepdims=True)
    return d * lax.rsqrt(var + NORM_EPS) * g + b


def _lru_coeffs(xl, gates, lvec_ref):
    r = _sigmoid(gates[:, 0:LRU_WIDTH] + lvec_ref[1:2, :])
    ig = _sigmoid(gates[:, LRU_WIDTH:2 * LRU_WIDTH] + lvec_ref[2:3, :])
    log_a = (-LRU_C * _softplus(-lvec_ref[3:4, :])) * r
    a = jnp.exp(log_a)
    bt = jnp.sqrt(-jnp.tanh(log_a) * (1.0 + a * a)) * (ig * xl)
    return a, bt


def _project(x_tile, win_ref, aux_ref, dst):
    xb = x_tile.astype(BF16)
    kv0 = ATT_WIDTH + 2 * ATT_KV_WIDTH
    dst[:, Q_OFF:K_OFF] = _dot(xb, win_ref[:, 0:ATT_WIDTH])
    dst[:, K_OFF:GA_OFF] = _dot(xb, aux_ref[:, 0:GA_OFF - K_OFF])
    dst[:, GA_OFF:Z_OFF] = _dot(xb, win_ref[:, kv0:kv0 + Z_OFF - GA_OFF])
    dst[:, Z_OFF:DT_OFF] = _dot(xb, win_ref[:, kv0 + Z_OFF - GA_OFF:kv0 + DT_OFF - GA_OFF])
    dst[:, DT_OFF:N_COLS] = _dot(xb, aux_ref[:, GA_OFF - K_OFF:GA_OFF - K_OFF + LANES])


def _prompt_kernel(sinks_ref, x_ref, rope_ref, win_ref, aux_ref, wg_ref, wout_ref, lcw_ref, lvec_ref,
                   scw_ref, scb_ref, hcol_ref, svec_ref, ln_ref,
                   y_ref, ko_ref, vo_ref, lco_ref, lho_ref, sco_ref, sho_ref,
                   p_scr, kd_scr, vd_scr, lx_scr, sx_scr, sa_scr, sb_scr, hl_scr, ht_scr, mix_scr, xlb_scr, xc_scr):
    T = PROMPT_TILE
    nsub = T // WINDOW
    c = pl.program_id(1)
    last = c == pl.num_programs(1) - 1

    @pl.when(c == 0)
    def _():
        kd_scr[:, 0:WINDOW, :] = jnp.zeros((2, WINDOW, LANES), BF16)
        vd_scr[:, 0:WINDOW, :] = jnp.zeros((2, WINDOW, LANES), BF16)
        lx_scr[:, 0:SUBLANES, :] = jnp.zeros((N_LRU_TILES, SUBLANES, LANES), F32)
        sx_scr[:, 0:SUBLANES, :] = jnp.zeros((N_CONV_TILES, SUBLANES, LANES), F32)
        hl_scr[...] = jnp.zeros_like(hl_scr)
        ht_scr[...] = jnp.zeros_like(ht_scr)

    _project(x_ref[...], win_ref, aux_ref, p_scr)

    lo_sq = _lane_lo((WINDOW, LANES))

    for g in range(ATT_KV_HEADS):
        kd = _rope(p_scr[:, K_OFF + g * LANES:K_OFF + (g + 1) * LANES], rope_ref[...])
        kd_scr[g, WINDOW:WINDOW + T, :] = kd.astype(BF16)
        vd_scr[g, WINDOW:WINDOW + T, :] = p_scr[:, V_OFF + g * LANES:V_OFF + (g + 1) * LANES].astype(BF16)

    row = lax.broadcasted_iota(jnp.int32, (WINDOW, 2 * WINDOW), 0)
    col = lax.broadcasted_iota(jnp.int32, (WINDOW, 2 * WINDOW), 1)
    band = (col >= row) & (col <= row + WINDOW)
    first_lo = jnp.where(c > 0, 0, WINDOW)
    for i in range(nsub):
        r0 = i * WINDOW
        msk = (band & (col >= first_lo)) if i == 0 else band
        tab = rope_ref[r0:r0 + WINDOW, :]
        for cc in range(ATT_HEADS // 2):
            g = cc // (ATT_HEADS // ATT_KV_HEADS // 2)
            kg = kd_scr[g, r0:r0 + 2 * WINDOW, :]
            vg = vd_scr[g, r0:r0 + 2 * WINDOW, :]
            qp = _rope(p_scr[r0:r0 + WINDOW, Q_OFF + cc * LANES:Q_OFF + (cc + 1) * LANES], tab) * ATT_SCALE
            outs = []
            for half in range(2):
                sink = sinks_ref[2 * cc + half]
                qm = jnp.where(lo_sq if half == 0 else jnp.logical_not(lo_sq), qp, 0.0).astype(BF16)
                s = jnp.where(msk, _dot_nt(qm, kg), NEG_BIG)
                m = jnp.maximum(jnp.max(s, -1, keepdims=True), sink)
                e = jnp.exp(s - m)
                den = jnp.sum(e, -1, keepdims=True) + jnp.exp(sink - m)
                outs.append(_dot(e.astype(BF16), vg) * (1.0 / den))
            att = jnp.where(lo_sq, outs[0], outs[1])
            gate = p_scr[r0:r0 + WINDOW, GA_OFF + cc * LANES:GA_OFF + (cc + 1) * LANES]
            mix_scr[r0:r0 + WINDOW, cc * LANES:(cc + 1) * LANES] = (att * _silu(gate)).astype(BF16)
    for g in range(ATT_KV_HEADS):
        kd_scr[g, 0:WINDOW, :] = kd_scr[g, T:T + WINDOW, :]
        vd_scr[g, 0:WINDOW, :] = vd_scr[g, T:T + WINDOW, :]

    P = SUBLANES
    H = CONV_W - 1
    for j in range(N_LRU_TILES):
        cs_ = slice(j * LANES, (j + 1) * LANES)
        lx_scr[j, P:P + T, :] = p_scr[:, XL_OFF + j * LANES:XL_OFF + (j + 1) * LANES]
        for k, xl in enumerate(_conv_classes(lx_scr, j, T, lcw_ref, lvec_ref[0:1, cs_], cs_)):
            sb_scr[j, pl.ds(k, T // CONV_STRIDE, stride=CONV_STRIDE), :] = xl
        xlb_scr[:, cs_] = sb_scr[j].astype(BF16)
    lx_scr[:, 0:P, :] = lx_scr[:, T:T + P, :]

    G = T // SUBLANES
    row_g = lax.broadcasted_iota(jnp.int32, (G, LANES), 0)
    cl = -LRU_C * _softplus(-lvec_ref[3:4, :])
    for jj in range(N_LRU_TILES // 2):
        c0 = jj * 2 * LANES
        k0 = (c0 // LRU_BLOCK) * LRU_BLOCK // LANES * LANES
        k1 = -(-(-(-(c0 + 2 * LANES) // LRU_BLOCK) * LRU_BLOCK) // LANES) * LANES
        xk = xlb_scr[:, k0:k1]
        gr = _dot(xk, wg_ref[k0:k1, c0:c0 + 2 * LANES])
        gi = _dot(xk, wg_ref[k0:k1, LRU_WIDTH + c0:LRU_WIDTH + c0 + 2 * LANES])
        for u in range(2):
            j = 2 * jj + u
            cs_ = slice(j * LANES, (j + 1) * LANES)
            us = slice(u * LANES, (u + 1) * LANES)
            r = _sigmoid(gr[:, us] + lvec_ref[1:2, cs_])
            ig = _sigmoid(gi[:, us] + lvec_ref[2:3, cs_])
            log_a = cl[:, cs_] * r
            a = jnp.exp(log_a)
            sa_scr[j] = a
            sb_scr[j] = jnp.sqrt(-jnp.tanh(log_a) * (1.0 + a * a)) * (ig * sb_scr[j])
    for j in range(N_LRU_TILES):
        ca = sa_scr[j, pl.ds(0, G, stride=SUBLANES), :]
        cb = sb_scr[j, pl.ds(0, G, stride=SUBLANES), :]
        for k in range(1, SUBLANES):
            ak = sa_scr[j, pl.ds(k, G, stride=SUBLANES), :]
            cb = ak * cb + sb_scr[j, pl.ds(k, G, stride=SUBLANES), :]
            ca = ak * ca
            sa_scr[j, pl.ds(k, G, stride=SUBLANES), :] = ca
            sb_scr[j, pl.ds(k, G, stride=SUBLANES), :] = cb
        hin = hl_scr[0:1, j * LANES:(j + 1) * LANES]
        cb = cb + jnp.where(row_g == 0, ca * hin, 0.0)
        d = 1
        while d < G:
            ok = row_g >= d
            cb = cb + ca * jnp.where(ok, pltpu.roll(cb, d, 0), 0.0)
            ca = ca * jnp.where(ok, pltpu.roll(ca, d, 0), 1.0)
            d *= 2
        hprev = jnp.where(row_g == 0, hin, pltpu.roll(cb, 1, 0))
        for k in range(SUBLANES):
            sb_scr[j, pl.ds(k, G, stride=SUBLANES), :] = (
                sb_scr[j, pl.ds(k, G, stride=SUBLANES), :] + sa_scr[j, pl.ds(k, G, stride=SUBLANES), :] * hprev)
        hl_scr[0:1, j * LANES:(j + 1) * LANES] = cb[G - 1:G, :]
        gl = p_scr[:, GL_OFF + j * LANES:GL_OFF + (j + 1) * LANES]
        mix_scr[:, MIX_LRU + j * LANES:MIX_LRU + (j + 1) * LANES] = (sb_scr[j] * _silu(gl)).astype(BF16)

    for j in range(N_CONV_TILES):
        cs_ = slice(j * LANES, (j + 1) * LANES)
        sx_scr[j, P:P + T, :] = p_scr[:, XBC_OFF + j * LANES:XBC_OFF + (j + 1) * LANES]
        for k, xc in enumerate(_conv_classes(sx_scr, j, T, scw_ref, scb_ref[0:1, cs_], cs_)):
            xc_scr[j, pl.ds(k, T // CONV_STRIDE, stride=CONV_STRIDE), :] = _silu(xc)
    sx_scr[:, 0:P, :] = sx_scr[:, T:T + P, :]

    qi = lax.broadcasted_iota(jnp.int32, (WINDOW, WINDOW), 0)
    si = lax.broadcasted_iota(jnp.int32, (WINDOW, WINDOW), 1)
    causal = si <= qi
    upper = (qi <= si).astype(F32)
    hpad = jnp.zeros((WINDOW - HEAD_ROWS, WINDOW), F32)
    a_col = -jnp.exp(hcol_ref[1])
    for i in range(nsub):
        r0 = i * WINDOW
        dt_t = _softplus(p_scr[r0:r0 + WINDOW, DT_OFF:DT_OFF + LANES].T[0:HEAD_ROWS, :] + hcol_ref[0])
        cst = jnp.dot(dt_t * a_col, upper, precision=lax.Precision.HIGHEST, preferred_element_type=F32)
        dt_c = jnp.concatenate([dt_t, hpad], axis=0).T
        cs = jnp.concatenate([cst, hpad], axis=0).T
        ys = [None] * N_SSD_TILES
        for g in range(SSD_GROUPS):
            bg_t = xc_scr[N_SSD_TILES + g, r0:r0 + WINDOW, :].T.astype(BF16)
            cg = xc_scr[N_SSD_TILES + SSD_GROUPS + g, r0:r0 + WINDOW, :].astype(BF16)
            cbm = _dot(cg, bg_t)
            yo = _dot(cg, ht_scr[g].astype(BF16))
            xdec, edec = [], []
            for pp in range(TILES_PER_GROUP):
                p = g * TILES_PER_GROUP + pp
                bcs, mm, dts = [], [], []
                for h in (2 * p, 2 * p + 1):
                    bc = jnp.broadcast_to(cs[:, h:h + 1], (WINDOW, WINDOW))
                    lmat = jnp.exp(jnp.where(causal, bc - cst[h:h + 1, :], NEG_BIG))
                    mm.append((cbm * lmat).astype(BF16))
                    bcs.append(bc)
                    dts.append(jnp.broadcast_to(dt_c[:, h:h + 1], (WINDOW, LANES)))
                csl = jnp.where(lo_sq, bcs[0], bcs[1])
                ecs = jnp.exp(csl)
                dec = jnp.exp(csl[WINDOW - 1:WINDOW, :] - csl)
                xs_p = xc_scr[p, r0:r0 + WINDOW, :]
                xdt = xs_p * jnp.where(lo_sq, dts[0], dts[1])
                xdt_b = xdt.astype(BF16)
                y = jnp.where(lo_sq, _dot(mm[0], xdt_b), _dot(mm[1], xdt_b))
                y = y + yo[:, pp * LANES:(pp + 1) * LANES] * ecs
                y = y + svec_ref[0:1, p * LANES:(p + 1) * LANES] * xs_p
                ys[p] = y * _silu(p_scr[r0:r0 + WINDOW, Z_OFF + p * LANES:Z_OFF + (p + 1) * LANES])
                xdec.append((xdt * dec).astype(BF16))
                edec.append(ecs[WINDOW - 1:WINDOW, :])
            ht_scr[g] = (ht_scr[g] * jnp.concatenate(edec, axis=1)
                         + _dot(bg_t, jnp.concatenate(xdec, axis=1)))
        ss = jnp.sum(ys[0] * ys[0], -1, keepdims=True)
        for p in range(1, N_SSD_TILES):
            ss = ss + jnp.sum(ys[p] * ys[p], -1, keepdims=True)
        rinv = lax.rsqrt(ss * (1.0 / SSD_WIDTH) + NORM_EPS)
        for p in range(N_SSD_TILES):
            mix_scr[r0:r0 + WINDOW, MIX_SSD + p * LANES:MIX_SSD + (p + 1) * LANES] = (
                ys[p] * rinv * svec_ref[1:2, p * LANES:(p + 1) * LANES]).astype(BF16)

    out = _dot(mix_scr[...], wout_ref[...])
    y_ref[...] = _layer_norm(DEEPNORM_ALPHA * x_ref[...] + out, ln_ref[0:1, :], ln_ref[1:2, :])

    @pl.when(last)
    def _():
        tab = rope_ref[T - WINDOW:T, :]
        k0 = _rope(p_scr[T - WINDOW:T, K_OFF:K_OFF + LANES], tab)
        k1 = _rope(p_scr[T - WINDOW:T, K_OFF + LANES:K_OFF + 2 * LANES], tab)
        ko_ref[...] = jnp.where(lo_sq, k0, k1)
        vo_ref[...] = jnp.where(lo_sq, p_scr[T - WINDOW:T, V_OFF:V_OFF + LANES],
                                p_scr[T - WINDOW:T, V_OFF + LANES:V_OFF + 2 * LANES])
        for j in range(N_LRU_TILES):
            lco_ref[:, j * LANES:(j + 1) * LANES] = lx_scr[j, P + T - (CONV_W - 1):P + T, :]
        lho_ref[...] = hl_scr[...]
        for j in range(N_CONV_TILES):
            sco_ref[:, j * LANES:(j + 1) * LANES] = sx_scr[j, P + T - (CONV_W - 1):P + T, :]
        for g in range(SSD_GROUPS):
            for pp in range(TILES_PER_GROUP):
                tile = ht_scr[g, :, pp * LANES:(pp + 1) * LANES].T
                h = 2 * (g * TILES_PER_GROUP + pp)
                sho_ref[h] = tile[0:SSD_HEAD_DIM, :]
                sho_ref[h + 1] = tile[SSD_HEAD_DIM:2 * SSD_HEAD_DIM, :]


def _const_spec(shape, layer):
    nd = len(shape)
    return pl.BlockSpec((None,) + tuple(shape), lambda *_: (layer,) + (0,) * nd,
                        pipeline_mode=pl.Buffered(1))


def _prompt_layer(layer, x, rope_tab, sinks, prm):
    B, L, _ = x.shape
    T = PROMPT_TILE
    nc = L // T
    win, aux, wg, wout, lcw, lvec, scw, scb, hvec, svec, ln, hcol = prm
    tile = lambda b, c: (b, c, 0)
    per_b = lambda b, c: (b, 0, 0)

    in_specs = [
        pl.BlockSpec(memory_space=pltpu.SMEM),
        pl.BlockSpec((None, T, D_MODEL), tile),
        pl.BlockSpec((T, 3 * LANES), lambda b, c: (c, 0)),
        _const_spec((D_MODEL, W_IN_COLS), layer),
        _const_spec((D_MODEL, AUX_COLS), layer),
        _const_spec((LRU_WIDTH, 2 * LRU_WIDTH), layer),
        _const_spec((D_MIX, D_MODEL), layer),
        _const_spec((CONV_W, LRU_WIDTH), layer),
        _const_spec((4, LRU_WIDTH), layer),
        _const_spec((CONV_W, SSD_CONV_CH), layer),
        _const_spec((1, SSD_CONV_CH), layer),
        _const_spec((2, HEAD_ROWS, LANES), layer),
        _const_spec((2, SSD_WIDTH), layer),
        _const_spec((2, D_MODEL), layer),
    ]
    out_shape = (
        jax.ShapeDtypeStruct((B, L, D_MODEL), F32),
        jax.ShapeDtypeStruct((B, WINDOW, ATT_KV_WIDTH), F32),
        jax.ShapeDtypeStruct((B, WINDOW, ATT_KV_WIDTH), F32),
        jax.ShapeDtypeStruct((B, CONV_W - 1, LRU_WIDTH), F32),
        jax.ShapeDtypeStruct((B, 1, LRU_WIDTH), F32),
        jax.ShapeDtypeStruct((B, CONV_W - 1, SSD_CONV_CH), F32),
        jax.ShapeDtypeStruct((B, SSD_HEADS, SSD_HEAD_DIM, SSD_STATE), F32),
    )
    out_specs = (
        pl.BlockSpec((None, T, D_MODEL), tile),
        pl.BlockSpec((None, WINDOW, ATT_KV_WIDTH), per_b),
        pl.BlockSpec((None, WINDOW, ATT_KV_WIDTH), per_b),
        pl.BlockSpec((None, CONV_W - 1, LRU_WIDTH), per_b),
        pl.BlockSpec((None, 1, LRU_WIDTH), per_b),
        pl.BlockSpec((None, CONV_W - 1, SSD_CONV_CH), per_b),
        pl.BlockSpec((None, SSD_HEADS, SSD_HEAD_DIM, SSD_STATE), lambda b, c: (b, 0, 0, 0)),
    )
    scratch = [
        pltpu.VMEM((T, N_COLS), F32),
        pltpu.VMEM((ATT_KV_HEADS, WINDOW + T, LANES), BF16),
        pltpu.VMEM((ATT_KV_HEADS, WINDOW + T, LANES), BF16),
        pltpu.VMEM((N_LRU_TILES, SUBLANES + T, LANES), F32),
        pltpu.VMEM((N_CONV_TILES, SUBLANES + T, LANES), F32),
        pltpu.VMEM((N_LRU_TILES, T, LANES), F32),
        pltpu.VMEM((N_LRU_TILES, T, LANES), F32),
        pltpu.VMEM((1, LRU_WIDTH), F32),
        pltpu.VMEM((SSD_GROUPS, SSD_STATE, GROUP_W), F32),
        pltpu.VMEM((T, D_MIX), BF16),
        pltpu.VMEM((T, LRU_WIDTH), BF16),
        pltpu.VMEM((N_CONV_TILES, T, LANES), F32),
    ]
    return pl.pallas_call(
        _prompt_kernel,
        grid=(B, nc),
        in_specs=in_specs,
        out_specs=out_specs,
        out_shape=out_shape,
        scratch_shapes=scratch,
        compiler_params=pltpu.CompilerParams(
            dimension_semantics=("arbitrary", "arbitrary"),
            vmem_limit_bytes=VMEM_LIMIT_BYTES),
        name=f"prompt_layer{layer}",
    )(sinks, x, rope_tab, win, aux, wg, wout, lcw, lvec, scw, scb, hcol, svec, ln)


def _conv_step(x_new, st_ref, sto_ref, w_ref, bias):
    y = bias + w_ref[CONV_W - 1:CONV_W, :] * x_new
    for t in range(CONV_W - 1):
        y = y + w_ref[t:t + 1, :] * st_ref[t]
    for t in range(CONV_W - 2):
        sto_ref[t] = st_ref[t + 1]
    sto_ref[CONV_W - 2] = x_new
    return y


def _sample_kernel(layer, n_aliased, sinks_ref, x_ref, rope_ref, win_ref, aux_ref, wg_ref, wout_ref, lcw_ref,
                   lvec_ref, scw_ref, scb_ref, hvec_ref, svec_ref, ln_ref,
                   ck_hbm, cv_hbm, lc_ref, lh_ref, sc_ref, sh_hbm, *refs):
    (y_ref, cko_ref, cvo_ref, lco_ref, lho_ref, sco_ref, sho_ref,
     p_scr, q_scr, o_scr, kn_scr, vn_scr, xdt_scr, da_scr, y_scr, mix_scr,
     ck_ring, cv_ring, sh_ring, ring_sem) = refs[n_aliased:]
    NB = x_ref.shape[0]
    BT = SAMPLE_BT
    i = pl.program_id(0)
    nsteps = pl.num_programs(0)
    lo_f = _lane_lo((NB, LANES))

    def fetch(j):
        slot = lax.rem(j, RING_SLOTS)
        rows = pl.ds(pl.multiple_of(j * BT, BT), BT)
        return (pltpu.make_async_copy(ck_hbm.at[layer, rows], ck_ring.at[slot], ring_sem.at[0, slot]),
                pltpu.make_async_copy(cv_hbm.at[layer, rows], cv_ring.at[slot], ring_sem.at[1, slot]),
                pltpu.make_async_copy(sh_hbm.at[layer, rows], sh_ring.at[slot], ring_sem.at[2, slot]))

    @pl.when(i == 0)
    def _():
        for j in range(RING_SLOTS - 1):
            for cp in fetch(j):
                cp.start()

    @pl.when(i + RING_SLOTS - 1 < nsteps)
    def _():
        for cp in fetch(i + RING_SLOTS - 1):
            cp.start()

    @pl.when(i == 0)
    def _():
        _project(x_ref[...], win_ref, aux_ref, p_scr)
        tab = rope_ref[...]
        for cc in range(ATT_HEADS // 2):
            g = cc // (ATT_HEADS // ATT_KV_HEADS // 2)
            qp = _rope(p_scr[:, Q_OFF + cc * LANES:Q_OFF + (cc + 1) * LANES], tab) * ATT_SCALE
            for half in range(2):
                t = jnp.where(lo_f if half == 0 else jnp.logical_not(lo_f), qp, 0.0)
                if half != g:
                    t = pltpu.roll(t, HALF, 1)
                q_scr[pl.ds(2 * cc + half, NB, stride=ATT_HEADS), :] = t
        k0 = _rope(p_scr[:, K_OFF:K_OFF + LANES], tab)
        k1 = _rope(p_scr[:, K_OFF + LANES:K_OFF + 2 * LANES], tab)
        kn_scr[...] = jnp.where(lo_f, k0, k1)
        vn_scr[...] = jnp.where(lo_f, p_scr[:, V_OFF:V_OFF + LANES], p_scr[:, V_OFF + LANES:V_OFF + 2 * LANES])
        xin = p_scr[:, XL_OFF:XL_OFF + LRU_WIDTH]
        xl = _conv_step(xin, lc_ref, lco_ref, lcw_ref, lvec_ref[0:1, :])
        gates = _dot(xl.astype(BF16), wg_ref[...])
        a, bt = _lru_coeffs(xl, gates, lvec_ref)
        h1 = a * lh_ref[...] + bt
        lho_ref[...] = h1
        mix_scr[:, MIX_LRU:MIX_LRU + LRU_WIDTH] = (h1 * _silu(p_scr[:, GL_OFF:GL_OFF + LRU_WIDTH])).astype(BF16)
        xin2 = p_scr[:, XBC_OFF:XBC_OFF + SSD_CONV_CH]
        xc = _conv_step(xin2, sc_ref, sco_ref, scw_ref, scb_ref[0:1, :])
        p_scr[:, XBC_OFF:XBC_OFF + SSD_CONV_CH] = _silu(xc)
        dt = _softplus(p_scr[:, DT_OFF:DT_OFF + LANES] + hvec_ref[0:1, :])
        da_scr[...] = jnp.exp(dt * (-jnp.exp(hvec_ref[1:2, :])))
        for p in range(N_SSD_TILES):
            dte = jnp.where(lo_f, jnp.broadcast_to(dt[:, 2 * p:2 * p + 1], (NB, LANES)),
                            jnp.broadcast_to(dt[:, 2 * p + 1:2 * p + 2], (NB, LANES)))
            xdt_scr[:, p * LANES:(p + 1) * LANES] = p_scr[:, XBC_OFF + p * LANES:XBC_OFF + (p + 1) * LANES] * dte

    for cp in fetch(i):
        cp.wait()
    slot = lax.rem(i, RING_SLOTS)
    ck_ref, cv_ref, sh_ref = ck_ring.at[slot], cv_ring.at[slot], sh_ring.at[slot]
    r8 = pl.multiple_of(i * BT, BT)
    kn_blk = kn_scr[pl.ds(r8, BT), :]
    vn_blk = vn_scr[pl.ds(r8, BT), :]
    zpad = jnp.zeros((LANES - BT, LANES), F32)
    kn_t = jnp.concatenate([kn_blk, zpad], axis=0).T
    vn_t = jnp.concatenate([vn_blk, zpad], axis=0).T
    newest = lax.broadcasted_iota(jnp.int32, (ATT_KV_WIDTH, WINDOW), 1) == WINDOW - 1
    da_t = jnp.concatenate([da_scr[pl.ds(r8, BT), :], zpad], axis=0).T
    x_t = [jnp.concatenate([xdt_scr[pl.ds(r8, BT), p * LANES:(p + 1) * LANES], zpad], axis=0).T
           for p in range(N_SSD_TILES)]
    b_blk = p_scr[pl.ds(r8, BT), B_OFF:B_OFF + SSD_GROUPS * SSD_STATE]
    c_blk = p_scr[pl.ds(r8, BT), C_OFF:C_OFF + SSD_GROUPS * SSD_STATE].astype(BF16)
    row8 = lax.broadcasted_iota(jnp.int32, (ATT_HEADS, 1), 0)
    sink = jnp.zeros((ATT_HEADS, 1), F32)
    for h in range(ATT_HEADS):
        sink = jnp.where(row8 == h, sinks_ref[h], sink)
    rowb = lax.broadcasted_iota(jnp.int32, (BT, GROUP_W), 0)
    y_acc = [jnp.zeros((BT, GROUP_W), F32) for _ in range(SSD_GROUPS)]
    rowk = lax.broadcasted_iota(jnp.int32, (BT, SSD_STATE), 0)
    outer = []
    for g in range(SSD_GROUPS):
        bg = b_blk[:, g * SSD_STATE:(g + 1) * SSD_STATE]
        diag = jnp.concatenate([jnp.where(rowk == bb, bg, 0.0) for bb in range(BT)], axis=1)
        rhs = jnp.concatenate([diag, jnp.zeros((LANES - BT, BT * SSD_STATE), F32)], axis=0).astype(BF16)
        outer.append([_dot(x_t[g * TILES_PER_GROUP + pp].astype(BF16), rhs) for pp in range(TILES_PER_GROUP)])
    for bb in range(BT):
        qr = q_scr[pl.ds(pl.multiple_of((r8 + bb) * ATT_HEADS, ATT_HEADS), ATT_HEADS), :]
        kt = ck_ref[bb]
        vt = cv_ref[bb]
        s = _dot(qr.astype(BF16), kt.astype(BF16))
        s_new = jnp.sum(qr * kn_blk[bb:bb + 1, :], -1, keepdims=True)
        m = jnp.maximum(jnp.maximum(jnp.max(s, -1, keepdims=True), s_new), sink)
        e = jnp.exp(s - m)
        e_new = jnp.exp(s_new - m)
        den = jnp.sum(e, -1, keepdims=True) + e_new + jnp.exp(sink - m)
        o = _dot_nt(e.astype(BF16), vt.astype(BF16)) + e_new * vn_blk[bb:bb + 1, :]
        o_scr[pl.ds(pl.multiple_of((r8 + bb) * ATT_HEADS, ATT_HEADS), ATT_HEADS), :] = o * (1.0 / den)
        cko_ref[bb] = jnp.where(newest, kn_t[:, bb:bb + 1], pltpu.roll(kt, WINDOW - 1, 1))
        cvo_ref[bb] = jnp.where(newest, vn_t[:, bb:bb + 1], pltpu.roll(vt, WINDOW - 1, 1))
        for g in range(SSD_GROUPS):
            tiles = []
            for pp in range(TILES_PER_GROUP):
                p = g * TILES_PER_GROUP + pp
                for hh in range(2):
                    h = 2 * p + hh
                    dab = jnp.broadcast_to(da_t[h:h + 1, bb:bb + 1], (SSD_HEAD_DIM, SSD_STATE))
                    h1 = sh_ref[bb, h] * dab + outer[g][pp][hh * SSD_HEAD_DIM:(hh + 1) * SSD_HEAD_DIM,
                                                               bb * SSD_STATE:(bb + 1) * SSD_STATE]
                    sho_ref[bb, h] = h1
                    tiles.append(h1.astype(BF16))
            res = _dot_nt(c_blk[:, g * SSD_STATE:(g + 1) * SSD_STATE], jnp.concatenate(tiles, axis=0))
            y_acc[g] = jnp.where(rowb == bb, res, y_acc[g])
    for g in range(SSD_GROUPS):
        y_scr[pl.ds(r8, BT), g * GROUP_W:(g + 1) * GROUP_W] = y_acc[g]

    @pl.when(i == pl.num_programs(0) - 1)
    def _():
        for cc in range(ATT_HEADS // 2):
            g = cc // (ATT_HEADS // ATT_KV_HEADS // 2)
            oe = o_scr[pl.ds(2 * cc, NB, stride=ATT_HEADS), :]
            oo = o_scr[pl.ds(2 * cc + 1, NB, stride=ATT_HEADS), :]
            if g == 0:
                oo = pltpu.roll(oo, HALF, 1)
            else:
                oe = pltpu.roll(oe, HALF, 1)
            gate = p_scr[:, GA_OFF + cc * LANES:GA_OFF + (cc + 1) * LANES]
            mix_scr[:, cc * LANES:(cc + 1) * LANES] = (jnp.where(lo_f, oe, oo) * _silu(gate)).astype(BF16)
        y = (y_scr[...] + svec_ref[0:1, :] * p_scr[:, XBC_OFF:XBC_OFF + SSD_WIDTH]) * _silu(p_scr[:, Z_OFF:Z_OFF + SSD_WIDTH])
        y = y * lax.rsqrt(jnp.mean(y * y, -1, keepdims=True) + NORM_EPS) * svec_ref[1:2, :]
        mix_scr[:, MIX_SSD:MIX_SSD + SSD_WIDTH] = y.astype(BF16)
        out = _dot(mix_scr[...], wout_ref[...])
        y_ref[...] = _layer_norm(DEEPNORM_ALPHA * x_ref[...] + out, ln_ref[0:1, :], ln_ref[1:2, :])


def _sample_layer(layer, x, rope_tab, sinks, prm, ck, cv, lc, lh, sc, sh, prev):
    NB = x.shape[0]
    BT = SAMPLE_BT
    H = CONV_W - 1
    win, aux, wg, wout, lcw, lvec, scw, scb, hvec, svec, ln, _ = prm
    whole = lambda shape: pl.BlockSpec(shape, lambda i: (0,) * len(shape), pipeline_mode=pl.Buffered(1))
    lwhole = lambda shape: pl.BlockSpec((None,) + shape, lambda i: (layer,) + (0,) * len(shape),
                                        pipeline_mode=pl.Buffered(1))
    rows = lambda shape: pl.BlockSpec((None, BT) + shape, lambda i: (layer, i) + (0,) * len(shape))
    hbm = pl.BlockSpec(memory_space=pl.ANY)
    state_specs = [
        hbm,
        hbm,
        lwhole((H, NB, LRU_WIDTH)),
        lwhole((NB, LRU_WIDTH)),
        lwhole((H, NB, SSD_CONV_CH)),
        hbm,
    ]
    in_specs = [
        pl.BlockSpec(memory_space=pltpu.SMEM),
        whole((NB, D_MODEL)),
        whole((1, 3 * LANES)),
        _const_spec((D_MODEL, W_IN_COLS), layer),
        _const_spec((D_MODEL, AUX_COLS), layer),
        _const_spec((LRU_WIDTH, 2 * LRU_WIDTH), layer),
        _const_spec((D_MIX, D_MODEL), layer),
        _const_spec((CONV_W, LRU_WIDTH), layer),
        _const_spec((4, LRU_WIDTH), layer),
        _const_spec((CONV_W, SSD_CONV_CH), layer),
        _const_spec((1, SSD_CONV_CH), layer),
        _const_spec((2, LANES), layer),
        _const_spec((2, SSD_WIDTH), layer),
        _const_spec((2, D_MODEL), layer),
    ] + state_specs
    operands = [sinks, x, rope_tab, win, aux, wg, wout, lcw, lvec, scw, scb, hvec, svec, ln, ck, cv, lc, lh, sc, sh]
    aliases = {}
    if prev is not None:
        for k, arr in enumerate(prev):
            aliases[len(operands)] = 1 + k
            operands.append(arr)
            in_specs.append(pl.BlockSpec(memory_space=pl.ANY))
    out_shape = (jax.ShapeDtypeStruct((NB, D_MODEL), F32),) + tuple(
        jax.ShapeDtypeStruct(a.shape, F32) for a in (ck, cv, lc, lh, sc, sh))
    full = lambda shape: pl.BlockSpec(shape, lambda i: (0,) * len(shape))
    lfull = lambda shape: pl.BlockSpec((None,) + shape, lambda i: (layer,) + (0,) * len(shape))
    out_specs = (
        full((NB, D_MODEL)),
        rows((ATT_KV_WIDTH, WINDOW)),
        rows((ATT_KV_WIDTH, WINDOW)),
        lfull((H, NB, LRU_WIDTH)),
        lfull((NB, LRU_WIDTH)),
        lfull((H, NB, SSD_CONV_CH)),
        rows((SSD_HEADS, SSD_HEAD_DIM, SSD_STATE)),
    )
    scratch = [
        pltpu.VMEM((NB, N_COLS), F32),
        pltpu.VMEM((NB * ATT_HEADS, LANES), F32),
        pltpu.VMEM((NB * ATT_HEADS, LANES), F32),
        pltpu.VMEM((NB, ATT_KV_WIDTH), F32),
        pltpu.VMEM((NB, ATT_KV_WIDTH), F32),
        pltpu.VMEM((NB, SSD_WIDTH), F32),
        pltpu.VMEM((NB, LANES), F32),
        pltpu.VMEM((NB, SSD_WIDTH), F32),
        pltpu.VMEM((NB, D_MIX), BF16),
        pltpu.VMEM((RING_SLOTS, BT, ATT_KV_WIDTH, WINDOW), F32),
        pltpu.VMEM((RING_SLOTS, BT, ATT_KV_WIDTH, WINDOW), F32),
        pltpu.VMEM((RING_SLOTS, BT, SSD_HEADS, SSD_HEAD_DIM, SSD_STATE), F32),
        pltpu.SemaphoreType.DMA((3, RING_SLOTS)),
    ]
    assert NB // BT >= RING_SLOTS - 1
    return pl.pallas_call(
        functools.partial(_sample_kernel, layer, len(aliases)),
        grid=(NB // BT,),
        in_specs=in_specs,
        out_specs=out_specs,
        out_shape=out_shape,
        scratch_shapes=scratch,
        input_output_aliases=aliases,
        compiler_params=pltpu.CompilerParams(
            dimension_semantics=("arbitrary",),
            vmem_limit_bytes=VMEM_LIMIT_BYTES),
        name=f"sample_layer{layer}",
    )(*operands)


def _rope_table(start, length):
    half = ROPE_DIM // 2
    inv = (np.float32(ROPE_THETA) ** (-np.arange(half, dtype=np.float32) / np.float32(half))).astype(np.float32)
    ang = (np.arange(start, start + length).astype(np.float32)[:, None] * inv[None, :]).astype(np.float32)
    cos, sin = np.cos(ang), np.sin(ang)
    d = np.arange(LANES) % ATT_HEAD_DIM
    fi = d % half
    c = np.where(d < ROPE_DIM, cos[:, fi], np.float32(1.0))
    s1 = np.where(d < half, -sin[:, fi], np.float32(0.0))
    s2 = np.where((d >= half) & (d < ROPE_DIM), sin[:, fi], np.float32(0.0))
    return jnp.asarray(np.concatenate([c, s1, s2], axis=1).astype(np.float32))


def _prep_params(w_in, w_out, lru_conv_w, lru_conv_b, lru_wa, lru_ba, lru_wx, lru_bx, lru_lambda,
                 ssd_conv_w, ssd_conv_b, ssd_dt_bias, ssd_a_log, ssd_d, ssd_norm_g, ln_g, ln_b):
    hd = ATT_HEAD_DIM
    win = w_in.astype(BF16)
    k = w_in[..., ATT_WIDTH:ATT_WIDTH + ATT_KV_WIDTH]
    v = w_in[..., ATT_WIDTH + ATT_KV_WIDTH:ATT_WIDTH + 2 * ATT_KV_WIDTH]
    dup = lambda t: jnp.concatenate([t[..., :hd], t[..., :hd], t[..., hd:], t[..., hd:]], -1)
    dtw = jnp.pad(w_in[..., W_IN_COLS - SSD_HEADS:], ((0, 0), (0, 0), (0, LANES - SSD_HEADS)))
    aux = jnp.concatenate([dup(k), dup(v), dtw], -1).astype(BF16)

    def dense(w):
        rows = [jnp.pad(w[:, n], ((0, 0), (0, 0), (n * LRU_BLOCK, LRU_WIDTH - (n + 1) * LRU_BLOCK)))
                for n in range(LRU_BLOCKS)]
        return jnp.concatenate(rows, 1)

    wg = jnp.concatenate([dense(lru_wa.astype(BF16)), dense(lru_wx.astype(BF16))], -1)
    wout = w_out.astype(BF16)
    lvec = jnp.stack([lru_conv_b, lru_ba, lru_bx, lru_lambda], 1)
    hpad = lambda t: jnp.pad(t, ((0, 0), (0, LANES - SSD_HEADS)))
    hvec = jnp.stack([hpad(ssd_dt_bias), hpad(ssd_a_log)], 1)
    rpad = lambda t: jnp.broadcast_to(jnp.pad(t, ((0, 0), (0, HEAD_ROWS - SSD_HEADS)))[:, :, None],
                                      (DEPTH, HEAD_ROWS, LANES))
    hcol = jnp.stack([rpad(ssd_dt_bias), rpad(ssd_a_log)], 1)
    svec = jnp.stack([jnp.repeat(ssd_d, SSD_HEAD_DIM, axis=1), ssd_norm_g], 1)
    ln = jnp.stack([ln_g, ln_b], 1)
    return (win, aux, wg, wout, lru_conv_w, lvec, ssd_conv_w, ssd_conv_b[:, None, :], hvec, svec, ln, hcol)


def kernel(x_prompt, x_sample, cache_swa_k, cache_swa_v, state_lru_conv, state_lru_h, state_ssd_conv, state_ssd_h, w_in, w_out, att_sinks, lru_conv_w, lru_conv_b, lru_wa, lru_ba, lru_wx, lru_bx, lru_lambda, ssd_conv_w, ssd_conv_b, ssd_dt_bias, ssd_a_log, ssd_d, ssd_norm_g, ln_g, ln_b):
    prm = _prep_params(w_in, w_out, lru_conv_w, lru_conv_b, lru_wa, lru_ba, lru_wx, lru_bx, lru_lambda,
                       ssd_conv_w, ssd_conv_b, ssd_dt_bias, ssd_a_log, ssd_d, ssd_norm_g, ln_g, ln_b)
    bp, lp, _ = x_prompt.shape
    rope_p = _rope_table(0, lp)
    xp = x_prompt
    new_p = [[] for _ in range(6)]
    for l in range(DEPTH):
        xp, ko, vo, lco, lho, sco, sho = _prompt_layer(l, xp, rope_p, att_sinks[l], prm)
        st = (ko.reshape(bp, WINDOW, ATT_KV_HEADS, ATT_HEAD_DIM), vo.reshape(bp, WINDOW, ATT_KV_HEADS, ATT_HEAD_DIM),
              lco, lho.reshape(bp, LRU_WIDTH), sco, sho)
        for lst, t in zip(new_p, st):
            lst.append(t)
    outs_p = [jnp.stack(t) for t in new_p]

    nb = x_sample.shape[0]
    rope_s = _rope_table(PAST_LEN, x_sample.shape[1])
    to_kt = lambda t: jnp.transpose(t, (0, 1, 3, 4, 2)).reshape(DEPTH, nb, ATT_KV_WIDTH, WINDOW)
    from_kt = lambda t: jnp.transpose(t.reshape(DEPTH, nb, ATT_KV_HEADS, ATT_HEAD_DIM, WINDOW), (0, 1, 4, 2, 3))
    swap = lambda t: jnp.transpose(t, (0, 2, 1, 3))
    state = (to_kt(cache_swa_k), to_kt(cache_swa_v), swap(state_lru_conv), state_lru_h, swap(state_ssd_conv),
             state_ssd_h)
    xs = x_sample.reshape(nb, D_MODEL)
    new = None
    for l in range(DEPTH):
        xs, *new = _sample_layer(l, xs, rope_s, att_sinks[l], prm, *state, new)
    cko, cvo, lco, lho, sco, sho = new
    outs_s = (from_kt(cko), from_kt(cvo), swap(lco), lho, swap(sco), sho)
    return (xp, xs.reshape(x_sample.shape)) + tuple(outs_p) + tuple(outs_s)
```

```python
import functools

import jax
import jax.numpy as jnp
import numpy as np
from jax import lax
from jax.experimental import pallas as pl
from jax.experimental.pallas import tpu as pltpu

F32 = jnp.float32
BF16 = jnp.bfloat16

D_MODEL = 1024
DEPTH = 4
PAST_LEN = 8192
D_MIX = 2 * D_MODEL
ATT_HEADS = 8
ATT_KV_HEADS = 2
ATT_HEAD_DIM = 64
ATT_WIDTH = ATT_HEADS * ATT_HEAD_DIM
ATT_KV_WIDTH = ATT_KV_HEADS * ATT_HEAD_DIM
ATT_SCALE = ATT_HEAD_DIM ** -0.5
WINDOW = 128
ROPE_THETA = 500000.0
ROPE_DIM = ATT_HEAD_DIM // 4
LRU_WIDTH = 3 * D_MIX // 8
LRU_BLOCKS = 8
LRU_BLOCK = LRU_WIDTH // LRU_BLOCKS
LRU_C = 8.0
CONV_W = 4
SSD_WIDTH = D_MIX - ATT_WIDTH - LRU_WIDTH
SSD_HEAD_DIM = 64
SSD_HEADS = SSD_WIDTH // SSD_HEAD_DIM
SSD_GROUPS = 2
SSD_STATE = 128
SSD_CONV_CH = SSD_WIDTH + 2 * SSD_GROUPS * SSD_STATE
DEEPNORM_ALPHA = (2.0 * DEPTH) ** 0.25
NORM_EPS = 1e-5

LANES = 128
SUBLANES = 8
HALF = LANES // 2
VMEM_LIMIT_BYTES = 60 * 1024 * 1024

Q_OFF = 0
K_OFF = Q_OFF + ATT_WIDTH
V_OFF = K_OFF + 2 * ATT_KV_WIDTH
GA_OFF = V_OFF + 2 * ATT_KV_WIDTH
XL_OFF = GA_OFF + ATT_WIDTH
GL_OFF = XL_OFF + LRU_WIDTH
Z_OFF = GL_OFF + LRU_WIDTH
XBC_OFF = Z_OFF + SSD_WIDTH
DT_OFF = XBC_OFF + SSD_CONV_CH
N_COLS = DT_OFF + LANES
W_IN_COLS = 2 * ATT_WIDTH + 2 * ATT_KV_WIDTH + 2 * LRU_WIDTH + SSD_WIDTH + SSD_CONV_CH + SSD_HEADS
KV_END = ATT_WIDTH + 2 * ATT_KV_WIDTH
W_PIECES = ((0, KV_END), (KV_END, KV_END + Z_OFF - GA_OFF), (KV_END + Z_OFF - GA_OFF, W_IN_COLS))
AUX_COLS = 4 * ATT_KV_WIDTH + LANES
B_OFF = XBC_OFF + SSD_WIDTH
C_OFF = B_OFF + SSD_GROUPS * SSD_STATE
MIX_LRU = ATT_WIDTH
MIX_SSD = ATT_WIDTH + LRU_WIDTH

N_LRU_TILES = LRU_WIDTH // LANES
N_SSD_TILES = SSD_WIDTH // LANES
N_CONV_TILES = SSD_CONV_CH // LANES
CONV_STRIDE = 4
TILES_PER_GROUP = N_SSD_TILES // SSD_GROUPS
GROUP_W = SSD_WIDTH // SSD_GROUPS
HEAD_ROWS = 16

NEG_BIG = -1e30

PROMPT_TILE = 256
SAMPLE_BT = 8
RING_SLOTS = 3


def _sigmoid(x):
    return 0.5 * jnp.tanh(0.5 * x) + 0.5


def _silu(x):
    h = 0.5 * x
    return h + h * jnp.tanh(h)


def _softplus(x):
    return jnp.maximum(x, 0.0) + jnp.log1p(jnp.exp(-jnp.abs(x)))


def _lane_lo(shape):
    return (lax.broadcasted_iota(jnp.int32, shape, len(shape) - 1) % LANES) < HALF


def _rope(t, tab):
    half = ROPE_DIM // 2
    return (t * tab[:, 0:LANES]
            + pltpu.roll(t, LANES - half, 1) * tab[:, LANES:2 * LANES]
            + pltpu.roll(t, half, 1) * tab[:, 2 * LANES:3 * LANES])


def _dot(a, b):
    return jnp.dot(a, b, preferred_element_type=F32)


def _dot_nt(a, b):
    return lax.dot_general(a, b, (((1,), (1,)), ((), ())), preferred_element_type=F32)


def _conv_classes(src_ref, j, rows, w_ref, bias, cs):
    n = CONV_STRIDE
    hist = CONV_W - 1
    taps = {s_: src_ref[j, pl.ds(SUBLANES + s_, rows // n, stride=n), :] for s_ in range(-hist, n)}
    outs = []
    for k in range(n):
        acc = bias + w_ref[hist:hist + 1, cs] * taps[k]
        for t in range(hist):
            acc = acc + w_ref[t:t + 1, cs] * taps[k - hist + t]
        outs.append(acc)
    return outs


def _layer_norm(v, g, b):
    mu = jnp.mean(v, -1, keepdims=True)
    d = v - mu
    var = jnp.mean(d * d, -1, keepdims=True)
    return d * lax.rsqrt(var + NORM_EPS) * g + b


def _lru_coeffs(xl, gates, lvec_ref):
    r = _sigmoid(gates[:, 0:LRU_WIDTH] + lvec_ref[1:2, :])
    ig = _sigmoid(gates[:, LRU_WIDTH:2 * LRU_WIDTH] + lvec_ref[2:3, :])
    log_a = (-LRU_C * _softplus(-lvec_ref[3:4, :])) * r
    a = jnp.exp(log_a)
    bt = jnp.sqrt(-jnp.tanh(log_a) * (1.0 + a * a)) * (ig * xl)
    return a, bt


def _project(x_tile, win_ref, aux_ref, dst, before=None):
    before = before or (lambda: None,) * 5
    xb = x_tile.astype(BF16)
    kv0 = KV_END
    before[0]()
    dst[:, Q_OFF:K_OFF] = _dot(xb, win_ref[:, 0:ATT_WIDTH])
    before[1]()
    dst[:, K_OFF:GA_OFF] = _dot(xb, aux_ref[:, 0:GA_OFF - K_OFF])
    before[2]()
    dst[:, GA_OFF:Z_OFF] = _dot(xb, win_ref[:, kv0:kv0 + Z_OFF - GA_OFF])
    before[3]()
    dst[:, Z_OFF:DT_OFF] = _dot(xb, win_ref[:, kv0 + Z_OFF - GA_OFF:kv0 + DT_OFF - GA_OFF])
    before[4]()
    dst[:, DT_OFF:N_COLS] = _dot(xb, aux_ref[:, GA_OFF - K_OFF:GA_OFF - K_OFF + LANES])


def _prompt_kernel(sinks_ref, x_ref, rope_ref, win_ref, aux_ref, wg_ref, wout_ref, lcw_ref, lvec_ref,
                   scw_ref, scb_ref, hcol_ref, svec_ref, ln_ref,
                   y_ref, ko_ref, vo_ref, lco_ref, lho_ref, sco_ref, sho_ref,
                   p_scr, kd_scr, vd_scr, lx_scr, sx_scr, sa_scr, sb_scr, hl_scr, ht_scr, mix_scr, xlb_scr, xc_scr):
    T = PROMPT_TILE
    nsub = T // WINDOW
    c = pl.program_id(1)
    last = c == pl.num_programs(1) - 1

    @pl.when(c == 0)
    def _():
        kd_scr[:, 0:WINDOW, :] = jnp.zeros((2, WINDOW, LANES), BF16)
        vd_scr[:, 0:WINDOW, :] = jnp.zeros((2, WINDOW, LANES), BF16)
        lx_scr[:, 0:SUBLANES, :] = jnp.zeros((N_LRU_TILES, SUBLANES, LANES), F32)
        sx_scr[:, 0:SUBLANES, :] = jnp.zeros((N_CONV_TILES, SUBLANES, LANES), F32)
        hl_scr[...] = jnp.zeros_like(hl_scr)
        ht_scr[...] = jnp.zeros_like(ht_scr)

    _project(x_ref[...], win_ref, aux_ref, p_scr)

    lo_sq = _lane_lo((WINDOW, LANES))

    for g in range(ATT_KV_HEADS):
        kd = _rope(p_scr[:, K_OFF + g * LANES:K_OFF + (g + 1) * LANES], rope_ref[...])
        kd_scr[g, WINDOW:WINDOW + T, :] = kd.astype(BF16)
        vd_scr[g, WINDOW:WINDOW + T, :] = p_scr[:, V_OFF + g * LANES:V_OFF + (g + 1) * LANES].astype(BF16)

    row = lax.broadcasted_iota(jnp.int32, (WINDOW, 2 * WINDOW), 0)
    col = lax.broadcasted_iota(jnp.int32, (WINDOW, 2 * WINDOW), 1)
    band = (col >= row) & (col <= row + WINDOW)
    first_lo = jnp.where(c > 0, 0, WINDOW)
    for i in range(nsub):
        r0 = i * WINDOW
        msk = (band & (col >= first_lo)) if i == 0 else band
        tab = rope_ref[r0:r0 + WINDOW, :]
        for cc in range(ATT_HEADS // 2):
            g = cc // (ATT_HEADS // ATT_KV_HEADS // 2)
            kg = kd_scr[g, r0:r0 + 2 * WINDOW, :]
            vg = vd_scr[g, r0:r0 + 2 * WINDOW, :]
            qp = _rope(p_scr[r0:r0 + WINDOW, Q_OFF + cc * LANES:Q_OFF + (cc + 1) * LANES], tab) * ATT_SCALE
            outs = []
            for half in range(2):
                sink = sinks_ref[2 * cc + half]
                qm = jnp.where(lo_sq if half == 0 else jnp.logical_not(lo_sq), qp, 0.0).astype(BF16)
                s = jnp.where(msk, _dot_nt(qm, kg), NEG_BIG)
                m = jnp.maximum(jnp.max(s, -1, keepdims=True), sink)
                e = jnp.exp(s - m)
                den = jnp.sum(e, -1, keepdims=True) + jnp.exp(sink - m)
                outs.append(_dot(e.astype(BF16), vg) * (1.0 / den))
            att = jnp.where(lo_sq, outs[0], outs[1])
            gate = p_scr[r0:r0 + WINDOW, GA_OFF + cc * LANES:GA_OFF + (cc + 1) * LANES]
            mix_scr[r0:r0 + WINDOW, cc * LANES:(cc + 1) * LANES] = (att * _silu(gate)).astype(BF16)
    for g in range(ATT_KV_HEADS):
        kd_scr[g, 0:WINDOW, :] = kd_scr[g, T:T + WINDOW, :]
        vd_scr[g, 0:WINDOW, :] = vd_scr[g, T:T + WINDOW, :]

    P = SUBLANES
    H = CONV_W - 1
    for j in range(N_LRU_TILES):
        cs_ = slice(j * LANES, (j + 1) * LANES)
        lx_scr[j, P:P + T, :] = p_scr[:, XL_OFF + j * LANES:XL_OFF + (j + 1) * LANES]
        for k, xl in enumerate(_conv_classes(lx_scr, j, T, lcw_ref, lvec_ref[0:1, cs_], cs_)):
            sb_scr[j, pl.ds(k, T // CONV_STRIDE, stride=CONV_STRIDE), :] = xl
        xlb_scr[:, cs_] = sb_scr[j].astype(BF16)
    lx_scr[:, 0:P, :] = lx_scr[:, T:T + P, :]

    G = T // SUBLANES
    row_g = lax.broadcasted_iota(jnp.int32, (G, LANES), 0)
    cl = -LRU_C * _softplus(-lvec_ref[3:4, :])
    for jj in range(N_LRU_TILES // 2):
        c0 = jj * 2 * LANES
        k0 = (c0 // LRU_BLOCK) * LRU_BLOCK // LANES * LANES
        k1 = -(-(-(-(c0 + 2 * LANES) // LRU_BLOCK) * LRU_BLOCK) // LANES) * LANES
        xk = xlb_scr[:, k0:k1]
        gr = _dot(xk, wg_ref[k0:k1, c0:c0 + 2 * LANES])
        gi = _dot(xk, wg_ref[k0:k1, LRU_WIDTH + c0:LRU_WIDTH + c0 + 2 * LANES])
        for u in range(2):
            j = 2 * jj + u
            cs_ = slice(j * LANES, (j + 1) * LANES)
            us = slice(u * LANES, (u + 1) * LANES)
            r = _sigmoid(gr[:, us] + lvec_ref[1:2, cs_])
            ig = _sigmoid(gi[:, us] + lvec_ref[2:3, cs_])
            log_a = cl[:, cs_] * r
            a = jnp.exp(log_a)
            sa_scr[j] = a
            sb_scr[j] = jnp.sqrt(-jnp.tanh(log_a) * (1.0 + a * a)) * (ig * sb_scr[j])
    for j in range(N_LRU_TILES):
        ca = sa_scr[j, pl.ds(0, G, stride=SUBLANES), :]
        cb = sb_scr[j, pl.ds(0, G, stride=SUBLANES), :]
        for k in range(1, SUBLANES):
            ak = sa_scr[j, pl.ds(k, G, stride=SUBLANES), :]
            cb = ak * cb + sb_scr[j, pl.ds(k, G, stride=SUBLANES), :]
            ca = ak * ca
            sa_scr[j, pl.ds(k, G, stride=SUBLANES), :] = ca
            sb_scr[j, pl.ds(k, G, stride=SUBLANES), :] = cb
        hin = hl_scr[0:1, j * LANES:(j + 1) * LANES]
        cb = cb + jnp.where(row_g == 0, ca * hin, 0.0)
        d = 1
        while d < G:
            ok = row_g >= d
            cb = cb + ca * jnp.where(ok, pltpu.roll(cb, d, 0), 0.0)
            ca = ca * jnp.where(ok, pltpu.roll(ca, d, 0), 1.0)
            d *= 2
        hprev = jnp.where(row_g == 0, hin, pltpu.roll(cb, 1, 0))
        for k in range(SUBLANES):
            sb_scr[j, pl.ds(k, G, stride=SUBLANES), :] = (
                sb_scr[j, pl.ds(k, G, stride=SUBLANES), :] + sa_scr[j, pl.ds(k, G, stride=SUBLANES), :] * hprev)
        hl_scr[0:1, j * LANES:(j + 1) * LANES] = cb[G - 1:G, :]
        gl = p_scr[:, GL_OFF + j * LANES:GL_OFF + (j + 1) * LANES]
        mix_scr[:, MIX_LRU + j * LANES:MIX_LRU + (j + 1) * LANES] = (sb_scr[j] * _silu(gl)).astype(BF16)

    for j in range(N_CONV_TILES):
        cs_ = slice(j * LANES, (j + 1) * LANES)
        sx_scr[j, P:P + T, :] = p_scr[:, XBC_OFF + j * LANES:XBC_OFF + (j + 1) * LANES]
        for k, xc in enumerate(_conv_classes(sx_scr, j, T, scw_ref, scb_ref[0:1, cs_], cs_)):
            xc_scr[j, pl.ds(k, T // CONV_STRIDE, stride=CONV_STRIDE), :] = _silu(xc)
    sx_scr[:, 0:P, :] = sx_scr[:, T:T + P, :]

    qi = lax.broadcasted_iota(jnp.int32, (WINDOW, WINDOW), 0)
    si = lax.broadcasted_iota(jnp.int32, (WINDOW, WINDOW), 1)
    causal = si <= qi
    upper = (qi <= si).astype(F32)
    hpad = jnp.zeros((WINDOW - HEAD_ROWS, WINDOW), F32)
    a_col = -jnp.exp(hcol_ref[1])
    for i in range(nsub):
        r0 = i * WINDOW
        dt_t = _softplus(p_scr[r0:r0 + WINDOW, DT_OFF:DT_OFF + LANES].T[0:HEAD_ROWS, :] + hcol_ref[0])
        cst = jnp.dot(dt_t * a_col, upper, precision=lax.Precision.HIGHEST, preferred_element_type=F32)
        dt_c = jnp.concatenate([dt_t, hpad], axis=0).T
        cs = jnp.concatenate([cst, hpad], axis=0).T
        ys = [None] * N_SSD_TILES
        for g in range(SSD_GROUPS):
            bg_t = xc_scr[N_SSD_TILES + g, r0:r0 + WINDOW, :].T.astype(BF16)
            cg = xc_scr[N_SSD_TILES + SSD_GROUPS + g, r0:r0 + WINDOW, :].astype(BF16)
            cbm = _dot(cg, bg_t)
            yo = _dot(cg, ht_scr[g].astype(BF16))
            xdec, edec = [], []
            for pp in range(TILES_PER_GROUP):
                p = g * TILES_PER_GROUP + pp
                bcs, mm, dts = [], [], []
                for h in (2 * p, 2 * p + 1):
                    bc = jnp.broadcast_to(cs[:, h:h + 1], (WINDOW, WINDOW))
                    lmat = jnp.exp(jnp.where(causal, bc - cst[h:h + 1, :], NEG_BIG))
                    mm.append((cbm * lmat).astype(BF16))
                    bcs.append(bc)
                    dts.append(jnp.broadcast_to(dt_c[:, h:h + 1], (WINDOW, LANES)))
                csl = jnp.where(lo_sq, bcs[0], bcs[1])
                ecs = jnp.exp(csl)
                dec = jnp.exp(csl[WINDOW - 1:WINDOW, :] - csl)
                xs_p = xc_scr[p, r0:r0 + WINDOW, :]
                xdt = xs_p * jnp.where(lo_sq, dts[0], dts[1])
                xdt_b = xdt.astype(BF16)
                y = jnp.where(lo_sq, _dot(mm[0], xdt_b), _dot(mm[1], xdt_b))
                y = y + yo[:, pp * LANES:(pp + 1) * LANES] * ecs
                y = y + svec_ref[0:1, p * LANES:(p + 1) * LANES] * xs_p
                ys[p] = y * _silu(p_scr[r0:r0 + WINDOW, Z_OFF + p * LANES:Z_OFF + (p + 1) * LANES])
                xdec.append((xdt * dec).astype(BF16))
                edec.append(ecs[WINDOW - 1:WINDOW, :])
            ht_scr[g] = (ht_scr[g] * jnp.concatenate(edec, axis=1)
                         + _dot(bg_t, jnp.concatenate(xdec, axis=1)))
        ss = jnp.sum(ys[0] * ys[0], -1, keepdims=True)
        for p in range(1, N_SSD_TILES):
            ss = ss + jnp.sum(ys[p] * ys[p], -1, keepdims=True)
        rinv = lax.rsqrt(ss * (1.0 / SSD_WIDTH) + NORM_EPS)
        for p in range(N_SSD_TILES):
            mix_scr[r0:r0 + WINDOW, MIX_SSD + p * LANES:MIX_SSD + (p + 1) * LANES] = (
                ys[p] * rinv * svec_ref[1:2, p * LANES:(p + 1) * LANES]).astype(BF16)

    out = _dot(mix_scr[...], wout_ref[...])
    y_ref[...] = _layer_norm(DEEPNORM_ALPHA * x_ref[...] + out, ln_ref[0:1, :], ln_ref[1:2, :])

    @pl.when(last)
    def _():
        tab = rope_ref[T - WINDOW:T, :]
        k0 = _rope(p_scr[T - WINDOW:T, K_OFF:K_OFF + LANES], tab)
        k1 = _rope(p_scr[T - WINDOW:T, K_OFF + LANES:K_OFF + 2 * LANES], tab)
        ko_ref[...] = jnp.where(lo_sq, k0, k1)
        vo_ref[...] = jnp.where(lo_sq, p_scr[T - WINDOW:T, V_OFF:V_OFF + LANES],
                                p_scr[T - WINDOW:T, V_OFF + LANES:V_OFF + 2 * LANES])
        for j in range(N_LRU_TILES):
            lco_ref[:, j * LANES:(j + 1) * LANES] = lx_scr[j, P + T - (CONV_W - 1):P + T, :]
        lho_ref[...] = hl_scr[...]
        for j in range(N_CONV_TILES):
            sco_ref[:, j * LANES:(j + 1) * LANES] = sx_scr[j, P + T - (CONV_W - 1):P + T, :]
        for g in range(SSD_GROUPS):
            for pp in range(TILES_PER_GROUP):
                tile = ht_scr[g, :, pp * LANES:(pp + 1) * LANES].T
                h = 2 * (g * TILES_PER_GROUP + pp)
                sho_ref[h] = tile[0:SSD_HEAD_DIM, :]
                sho_ref[h + 1] = tile[SSD_HEAD_DIM:2 * SSD_HEAD_DIM, :]


def _const_spec(shape, layer):
    nd = len(shape)
    return pl.BlockSpec((None,) + tuple(shape), lambda *_: (layer,) + (0,) * nd,
                        pipeline_mode=pl.Buffered(1))


def _prompt_layer(layer, x, rope_tab, sinks, prm):
    B, L, _ = x.shape
    T = PROMPT_TILE
    nc = L // T
    win, aux, wg, wout, lcw, lvec, scw, scb, hvec, svec, ln, hcol = prm
    tile = lambda b, c: (b, c, 0)
    per_b = lambda b, c: (b, 0, 0)

    in_specs = [
        pl.BlockSpec(memory_space=pltpu.SMEM),
        pl.BlockSpec((None, T, D_MODEL), tile),
        pl.BlockSpec((T, 3 * LANES), lambda b, c: (c, 0)),
        _const_spec((D_MODEL, W_IN_COLS), layer),
        _const_spec((D_MODEL, AUX_COLS), layer),
        _const_spec((LRU_WIDTH, 2 * LRU_WIDTH), layer),
        _const_spec((D_MIX, D_MODEL), layer),
        _const_spec((CONV_W, LRU_WIDTH), layer),
        _const_spec((4, LRU_WIDTH), layer),
        _const_spec((CONV_W, SSD_CONV_CH), layer),
        _const_spec((1, SSD_CONV_CH), layer),
        _const_spec((2, HEAD_ROWS, LANES), layer),
        _const_spec((2, SSD_WIDTH), layer),
        _const_spec((2, D_MODEL), layer),
    ]
    out_shape = (
        jax.ShapeDtypeStruct((B, L, D_MODEL), F32),
        jax.ShapeDtypeStruct((B, WINDOW, ATT_KV_WIDTH), F32),
        jax.ShapeDtypeStruct((B, WINDOW, ATT_KV_WIDTH), F32),
        jax.ShapeDtypeStruct((B, CONV_W - 1, LRU_WIDTH), F32),
        jax.ShapeDtypeStruct((B, 1, LRU_WIDTH), F32),
        jax.ShapeDtypeStruct((B, CONV_W - 1, SSD_CONV_CH), F32),
        jax.ShapeDtypeStruct((B, SSD_HEADS, SSD_HEAD_DIM, SSD_STATE), F32),
    )
    out_specs = (
        pl.BlockSpec((None, T, D_MODEL), tile),
        pl.BlockSpec((None, WINDOW, ATT_KV_WIDTH), per_b),
        pl.BlockSpec((None, WINDOW, ATT_KV_WIDTH), per_b),
        pl.BlockSpec((None, CONV_W - 1, LRU_WIDTH), per_b),
        pl.BlockSpec((None, 1, LRU_WIDTH), per_b),
        pl.BlockSpec((None, CONV_W - 1, SSD_CONV_CH), per_b),
        pl.BlockSpec((None, SSD_HEADS, SSD_HEAD_DIM, SSD_STATE), lambda b, c: (b, 0, 0, 0)),
    )
    scratch = [
        pltpu.VMEM((T, N_COLS), F32),
        pltpu.VMEM((ATT_KV_HEADS, WINDOW + T, LANES), BF16),
        pltpu.VMEM((ATT_KV_HEADS, WINDOW + T, LANES), BF16),
        pltpu.VMEM((N_LRU_TILES, SUBLANES + T, LANES), F32),
        pltpu.VMEM((N_CONV_TILES, SUBLANES + T, LANES), F32),
        pltpu.VMEM((N_LRU_TILES, T, LANES), F32),
        pltpu.VMEM((N_LRU_TILES, T, LANES), F32),
        pltpu.VMEM((1, LRU_WIDTH), F32),
        pltpu.VMEM((SSD_GROUPS, SSD_STATE, GROUP_W), F32),
        pltpu.VMEM((T, D_MIX), BF16),
        pltpu.VMEM((T, LRU_WIDTH), BF16),
        pltpu.VMEM((N_CONV_TILES, T, LANES), F32),
    ]
    return pl.pallas_call(
        _prompt_kernel,
        grid=(B, nc),
        in_specs=in_specs,
        out_specs=out_specs,
        out_shape=out_shape,
        scratch_shapes=scratch,
        compiler_params=pltpu.CompilerParams(
            dimension_semantics=("arbitrary", "arbitrary"),
            vmem_limit_bytes=VMEM_LIMIT_BYTES),
        name=f"prompt_layer{layer}",
    )(sinks, x, rope_tab, win, aux, wg, wout, lcw, lvec, scw, scb, hcol, svec, ln)


def _conv_step(x_new, st_ref, sto_ref, w_ref, bias):
    y = bias + w_ref[CONV_W - 1:CONV_W, :] * x_new
    for t in range(CONV_W - 1):
        y = y + w_ref[t:t + 1, :] * st_ref[t]
    for t in range(CONV_W - 2):
        sto_ref[t] = st_ref[t + 1]
    sto_ref[CONV_W - 2] = x_new
    return y


def _sample_kernel(layer, n_aliased, sinks_ref, x_ref, rope_ref, win_hbm, aux_hbm, wg_hbm, wout_hbm, lcw_ref,
                   lvec_ref, scw_ref, scb_ref, hvec_ref, svec_ref, ln_ref,
                   ck_hbm, cv_hbm, lc_ref, lh_ref, sc_ref, sh_hbm, *refs):
    (y_ref, cko_ref, cvo_ref, lco_ref, lho_ref, sco_ref, sho_ref,
     p_scr, q_scr, o_scr, kn_scr, vn_scr, xdt_scr, da_scr, y_scr, mix_scr,
     ck_ring, cv_ring, sh_ring, ring_sem, win_ref, aux_ref, wg_ref, wout_ref, w_sem) = refs[n_aliased:]

    def w_copy(k):
        lo, hi = W_PIECES[k] if k < len(W_PIECES) else (None, None)
        if k < len(W_PIECES):
            return pltpu.make_async_copy(win_hbm.at[layer, :, lo:hi], win_ref.at[:, lo:hi], w_sem.at[k])
        src, dst = ((aux_hbm, aux_ref), (wg_hbm, wg_ref), (wout_hbm, wout_ref))[k - len(W_PIECES)]
        return pltpu.make_async_copy(src.at[layer], dst, w_sem.at[k])

    n_w = len(W_PIECES) + 3
    NB = x_ref.shape[0]
    BT = SAMPLE_BT
    i = pl.program_id(0)
    nsteps = pl.num_programs(0)
    lo_f = _lane_lo((NB, LANES))

    def fetch(j):
        slot = lax.rem(j, RING_SLOTS)
        rows = pl.ds(pl.multiple_of(j * BT, BT), BT)
        return (pltpu.make_async_copy(ck_hbm.at[layer, rows], ck_ring.at[slot], ring_sem.at[0, slot]),
                pltpu.make_async_copy(cv_hbm.at[layer, rows], cv_ring.at[slot], ring_sem.at[1, slot]),
                pltpu.make_async_copy(sh_hbm.at[layer, rows], sh_ring.at[slot], ring_sem.at[2, slot]))

    @pl.when(i == 0)
    def _():
        for j in range(RING_SLOTS - 1):
            for cp in fetch(j):
                cp.start()

    @pl.when(i + RING_SLOTS - 1 < nsteps)
    def _():
        for cp in fetch(i + RING_SLOTS - 1):
            cp.start()

    @pl.when(i == 0)
    def _():
        for k in (0, len(W_PIECES), 1, 2, len(W_PIECES) + 1, len(W_PIECES) + 2):
            w_copy(k).start()
        _project(x_ref[...], win_ref, aux_ref, p_scr,
                 before=(lambda: w_copy(0).wait(), lambda: w_copy(len(W_PIECES)).wait(),
                         lambda: w_copy(1).wait(), lambda: w_copy(2).wait(), lambda: None))
        tab = rope_ref[...]
        for cc in range(ATT_HEADS // 2):
            g = cc // (ATT_HEADS // ATT_KV_HEADS // 2)
            qp = _rope(p_scr[:, Q_OFF + cc * LANES:Q_OFF + (cc + 1) * LANES], tab) * ATT_SCALE
            for half in range(2):
                t = jnp.where(lo_f if half == 0 else jnp.logical_not(lo_f), qp, 0.0)
                if half != g:
                    t = pltpu.roll(t, HALF, 1)
                q_scr[pl.ds(2 * cc + half, NB, stride=ATT_HEADS), :] = t
        k0 = _rope(p_scr[:, K_OFF:K_OFF + LANES], tab)
        k1 = _rope(p_scr[:, K_OFF + LANES:K_OFF + 2 * LANES], tab)
        kn_scr[...] = jnp.where(lo_f, k0, k1)
        vn_scr[...] = jnp.where(lo_f, p_scr[:, V_OFF:V_OFF + LANES], p_scr[:, V_OFF + LANES:V_OFF + 2 * LANES])
        xin = p_scr[:, XL_OFF:XL_OFF + LRU_WIDTH]
        xl = _conv_step(xin, lc_ref, lco_ref, lcw_ref, lvec_ref[0:1, :])
        w_copy(len(W_PIECES) + 1).wait()
        gates = _dot(xl.astype(BF16), wg_ref[...])
        a, bt = _lru_coeffs(xl, gates, lvec_ref)
        h1 = a * lh_ref[...] + bt
        lho_ref[...] = h1
        mix_scr[:, MIX_LRU:MIX_LRU + LRU_WIDTH] = (h1 * _silu(p_scr[:, GL_OFF:GL_OFF + LRU_WIDTH])).astype(BF16)
        xin2 = p_scr[:, XBC_OFF:XBC_OFF + SSD_CONV_CH]
        xc = _conv_step(xin2, sc_ref, sco_ref, scw_ref, scb_ref[0:1, :])
        p_scr[:, XBC_OFF:XBC_OFF + SSD_CONV_CH] = _silu(xc)
        dt = _softplus(p_scr[:, DT_OFF:DT_OFF + LANES] + hvec_ref[0:1, :])
        da_scr[...] = jnp.exp(dt * (-jnp.exp(hvec_ref[1:2, :])))
        for p in range(N_SSD_TILES):
            dte = jnp.where(lo_f, jnp.broadcast_to(dt[:, 2 * p:2 * p + 1], (NB, LANES)),
                            jnp.broadcast_to(dt[:, 2 * p + 1:2 * p + 2], (NB, LANES)))
            xdt_scr[:, p * LANES:(p + 1) * LANES] = p_scr[:, XBC_OFF + p * LANES:XBC_OFF + (p + 1) * LANES] * dte

    for cp in fetch(i):
        cp.wait()
    slot = lax.rem(i, RING_SLOTS)
    ck_ref, cv_ref, sh_ref = ck_ring.at[slot], cv_ring.at[slot], sh_ring.at[slot]
    r8 = pl.multiple_of(i * BT, BT)
    kn_blk = kn_scr[pl.ds(r8, BT), :]
    vn_blk = vn_scr[pl.ds(r8, BT), :]
    zpad = jnp.zeros((LANES - BT, LANES), F32)
    kn_t = jnp.concatenate([kn_blk, zpad], axis=0).T
    vn_t = jnp.concatenate([vn_blk, zpad], axis=0).T
    newest = lax.broadcasted_iota(jnp.int32, (ATT_KV_WIDTH, WINDOW), 1) == WINDOW - 1
    da_t = jnp.concatenate([da_scr[pl.ds(r8, BT), :], zpad], axis=0).T
    x_t = [jnp.concatenate([xdt_scr[pl.ds(r8, BT), p * LANES:(p + 1) * LANES], zpad], axis=0).T
           for p in range(N_SSD_TILES)]
    b_blk = p_scr[pl.ds(r8, BT), B_OFF:B_OFF + SSD_GROUPS * SSD_STATE]
    c_blk = p_scr[pl.ds(r8, BT), C_OFF:C_OFF + SSD_GROUPS * SSD_STATE].astype(BF16)
    row8 = lax.broadcasted_iota(jnp.int32, (ATT_HEADS, 1), 0)
    sink = jnp.zeros((ATT_HEADS, 1), F32)
    for h in range(ATT_HEADS):
        sink = jnp.where(row8 == h, sinks_ref[h], sink)
    rowb = lax.broadcasted_iota(jnp.int32, (BT, GROUP_W), 0)
    y_acc = [jnp.zeros((BT, GROUP_W), F32) for _ in range(SSD_GROUPS)]
    rowk = lax.broadcasted_iota(jnp.int32, (BT, SSD_STATE), 0)
    outer = []
    for g in range(SSD_GROUPS):
        bg = b_blk[:, g * SSD_STATE:(g + 1) * SSD_STATE]
        diag = jnp.concatenate([jnp.where(rowk == bb, bg, 0.0) for bb in range(BT)], axis=1)
        rhs = jnp.concatenate([diag, jnp.zeros((LANES - BT, BT * SSD_STATE), F32)], axis=0).astype(BF16)
        outer.append([_dot(x_t[g * TILES_PER_GROUP + pp].astype(BF16), rhs) for pp in range(TILES_PER_GROUP)])
    for bb in range(BT):
        qr = q_scr[pl.ds(pl.multiple_of((r8 + bb) * ATT_HEADS, ATT_HEADS), ATT_HEADS), :]
        kt = ck_ref[bb]
        vt = cv_ref[bb]
        s = _dot(qr.astype(BF16), kt.astype(BF16))
        s_new = jnp.sum(qr * kn_blk[bb:bb + 1, :], -1, keepdims=True)
        m = jnp.maximum(jnp.maximum(jnp.max(s, -1, keepdims=True), s_new), sink)
        e = jnp.exp(s - m)
        e_new = jnp.exp(s_new - m)
        den = jnp.sum(e, -1, keepdims=True) + e_new + jnp.exp(sink - m)
        o = _dot_nt(e.astype(BF16), vt.astype(BF16)) + e_new * vn_blk[bb:bb + 1, :]
        o_scr[pl.ds(pl.multiple_of((r8 + bb) * ATT_HEADS, ATT_HEADS), ATT_HEADS), :] = o * (1.0 / den)
        cko_ref[bb] = jnp.where(newest, kn_t[:, bb:bb + 1], pltpu.roll(kt, WINDOW - 1, 1))
        cvo_ref[bb] = jnp.where(newest, vn_t[:, bb:bb + 1], pltpu.roll(vt, WINDOW - 1, 1))
        for g in range(SSD_GROUPS):
            tiles = []
            for pp in range(TILES_PER_GROUP):
                p = g * TILES_PER_GROUP + pp
                for hh in range(2):
                    h = 2 * p + hh
                    dab = jnp.broadcast_to(da_t[h:h + 1, bb:bb + 1], (SSD_HEAD_DIM, SSD_STATE))
                    h1 = sh_ref[bb, h] * dab + outer[g][pp][hh * SSD_HEAD_DIM:(hh + 1) * SSD_HEAD_DIM,
                                                               bb * SSD_STATE:(bb + 1) * SSD_STATE]
                    sho_ref[bb, h] = h1
                    tiles.append(h1.astype(BF16))
            res = _dot_nt(c_blk[:, g * SSD_STATE:(g + 1) * SSD_STATE], jnp.concatenate(tiles, axis=0))
            y_acc[g] = jnp.where(rowb == bb, res, y_acc[g])
    for g in range(SSD_GROUPS):
        y_scr[pl.ds(r8, BT), g * GROUP_W:(g + 1) * GROUP_W] = y_acc[g]

    @pl.when(i == pl.num_programs(0) - 1)
    def _():
        for cc in range(ATT_HEADS // 2):
            g = cc // (ATT_HEADS // ATT_KV_HEADS // 2)
            oe = o_scr[pl.ds(2 * cc, NB, stride=ATT_HEADS), :]
            oo = o_scr[pl.ds(2 * cc + 1, NB, stride=ATT_HEADS), :]
            if g == 0:
                oo = pltpu.roll(oo, HALF, 1)
            else:
                oe = pltpu.roll(oe, HALF, 1)
            gate = p_scr[:, GA_OFF + cc * LANES:GA_OFF + (cc + 1) * LANES]
            mix_scr[:, cc * LANES:(cc + 1) * LANES] = (jnp.where(lo_f, oe, oo) * _silu(gate)).astype(BF16)
        y = (y_scr[...] + svec_ref[0:1, :] * p_scr[:, XBC_OFF:XBC_OFF + SSD_WIDTH]) * _silu(p_scr[:, Z_OFF:Z_OFF + SSD_WIDTH])
        y = y * lax.rsqrt(jnp.mean(y * y, -1, keepdims=True) + NORM_EPS) * svec_ref[1:2, :]
        mix_scr[:, MIX_SSD:MIX_SSD + SSD_WIDTH] = y.astype(BF16)
        w_copy(len(W_PIECES) + 2).wait()
        out = _dot(mix_scr[...], wout_ref[...])
        y_ref[...] = _layer_norm(DEEPNORM_ALPHA * x_ref[...] + out, ln_ref[0:1, :], ln_ref[1:2, :])


def _sample_layer(layer, x, rope_tab, sinks, prm, ck, cv, lc, lh, sc, sh, prev):
    NB = x.shape[0]
    BT = SAMPLE_BT
    H = CONV_W - 1
    win, aux, wg, wout, lcw, lvec, scw, scb, hvec, svec, ln, _ = prm
    whole = lambda shape: pl.BlockSpec(shape, lambda i: (0,) * len(shape), pipeline_mode=pl.Buffered(1))
    lwhole = lambda shape: pl.BlockSpec((None,) + shape, lambda i: (layer,) + (0,) * len(shape),
                                        pipeline_mode=pl.Buffered(1))
    rows = lambda shape: pl.BlockSpec((None, BT) + shape, lambda i: (layer, i) + (0,) * len(shape))
    hbm = pl.BlockSpec(memory_space=pl.ANY)
    state_specs = [
        hbm,
        hbm,
        lwhole((H, NB, LRU_WIDTH)),
        lwhole((NB, LRU_WIDTH)),
        lwhole((H, NB, SSD_CONV_CH)),
        hbm,
    ]
    in_specs = [
        pl.BlockSpec(memory_space=pltpu.SMEM),
        whole((NB, D_MODEL)),
        whole((1, 3 * LANES)),
        pl.BlockSpec(memory_space=pl.ANY),
        pl.BlockSpec(memory_space=pl.ANY),
        pl.BlockSpec(memory_space=pl.ANY),
        pl.BlockSpec(memory_space=pl.ANY),
        _const_spec((CONV_W, LRU_WIDTH), layer),
        _const_spec((4, LRU_WIDTH), layer),
        _const_spec((CONV_W, SSD_CONV_CH), layer),
        _const_spec((1, SSD_CONV_CH), layer),
        _const_spec((2, LANES), layer),
        _const_spec((2, SSD_WIDTH), layer),
        _const_spec((2, D_MODEL), layer),
    ] + state_specs
    operands = [sinks, x, rope_tab, win, aux, wg, wout, lcw, lvec, scw, scb, hvec, svec, ln, ck, cv, lc, lh, sc, sh]
    aliases = {}
    if prev is not None:
        for k, arr in enumerate(prev):
            aliases[len(operands)] = 1 + k
            operands.append(arr)
            in_specs.append(pl.BlockSpec(memory_space=pl.ANY))
    out_shape = (jax.ShapeDtypeStruct((NB, D_MODEL), F32),) + tuple(
        jax.ShapeDtypeStruct(a.shape, F32) for a in (ck, cv, lc, lh, sc, sh))
    full = lambda shape: pl.BlockSpec(shape, lambda i: (0,) * len(shape))
    lfull = lambda shape: pl.BlockSpec((None,) + shape, lambda i: (layer,) + (0,) * len(shape))
    out_specs = (
        full((NB, D_MODEL)),
        rows((ATT_KV_WIDTH, WINDOW)),
        rows((ATT_KV_WIDTH, WINDOW)),
        lfull((H, NB, LRU_WIDTH)),
        lfull((NB, LRU_WIDTH)),
        lfull((H, NB, SSD_CONV_CH)),
        rows((SSD_HEADS, SSD_HEAD_DIM, SSD_STATE)),
    )
    scratch = [
        pltpu.VMEM((NB, N_COLS), F32),
        pltpu.VMEM((NB * ATT_HEADS, LANES), F32),
        pltpu.VMEM((NB * ATT_HEADS, LANES), F32),
        pltpu.VMEM((NB, ATT_KV_WIDTH), F32),
        pltpu.VMEM((NB, ATT_KV_WIDTH), F32),
        pltpu.VMEM((NB, SSD_WIDTH), F32),
        pltpu.VMEM((NB, LANES), F32),
        pltpu.VMEM((NB, SSD_WIDTH), F32),
        pltpu.VMEM((NB, D_MIX), BF16),
        pltpu.VMEM((RING_SLOTS, BT, ATT_KV_WIDTH, WINDOW), F32),
        pltpu.VMEM((RING_SLOTS, BT, ATT_KV_WIDTH, WINDOW), F32),
        pltpu.VMEM((RING_SLOTS, BT, SSD_HEADS, SSD_HEAD_DIM, SSD_STATE), F32),
        pltpu.SemaphoreType.DMA((3, RING_SLOTS)),
        pltpu.VMEM((D_MODEL, W_IN_COLS), BF16),
        pltpu.VMEM((D_MODEL, AUX_COLS), BF16),
        pltpu.VMEM((LRU_WIDTH, 2 * LRU_WIDTH), BF16),
        pltpu.VMEM((D_MIX, D_MODEL), BF16),
        pltpu.SemaphoreType.DMA((len(W_PIECES) + 3,)),
    ]
    assert NB // BT >= RING_SLOTS - 1
    return pl.pallas_call(
        functools.partial(_sample_kernel, layer, len(aliases)),
        grid=(NB // BT,),
        in_specs=in_specs,
        out_specs=out_specs,
        out_shape=out_shape,
        scratch_shapes=scratch,
        input_output_aliases=aliases,
        compiler_params=pltpu.CompilerParams(
            dimension_semantics=("arbitrary",),
            vmem_limit_bytes=VMEM_LIMIT_BYTES),
        name=f"sample_layer{layer}",
    )(*operands)


def _rope_table(start, length):
    half = ROPE_DIM // 2
    inv = (np.float32(ROPE_THETA) ** (-np.arange(half, dtype=np.float32) / np.float32(half))).astype(np.float32)
    ang = (np.arange(start, start + length).astype(np.float32)[:, None] * inv[None, :]).astype(np.float32)
    cos, sin = np.cos(ang), np.sin(ang)
    d = np.arange(LANES) % ATT_HEAD_DIM
    fi = d % half
    c = np.where(d < ROPE_DIM, cos[:, fi], np.float32(1.0))
    s1 = np.where(d < half, -sin[:, fi], np.float32(0.0))
    s2 = np.where((d >= half) & (d < ROPE_DIM), sin[:, fi], np.float32(0.0))
    return jnp.asarray(np.concatenate([c, s1, s2], axis=1).astype(np.float32))


def _prep_params(w_in, w_out, lru_conv_w, lru_conv_b, lru_wa, lru_ba, lru_wx, lru_bx, lru_lambda,
                 ssd_conv_w, ssd_conv_b, ssd_dt_bias, ssd_a_log, ssd_d, ssd_norm_g, ln_g, ln_b):
    hd = ATT_HEAD_DIM
    win = w_in.astype(BF16)
    k = w_in[..., ATT_WIDTH:ATT_WIDTH + ATT_KV_WIDTH]
    v = w_in[..., ATT_WIDTH + ATT_KV_WIDTH:ATT_WIDTH + 2 * ATT_KV_WIDTH]
    dup = lambda t: jnp.concatenate([t[..., :hd], t[..., :hd], t[..., hd:], t[..., hd:]], -1)
    dtw = jnp.pad(w_in[..., W_IN_COLS - SSD_HEADS:], ((0, 0), (0, 0), (0, LANES - SSD_HEADS)))
    aux = jnp.concatenate([dup(k), dup(v), dtw], -1).astype(BF16)

    def dense(w):
        rows = [jnp.pad(w[:, n], ((0, 0), (0, 0), (n * LRU_BLOCK, LRU_WIDTH - (n + 1) * LRU_BLOCK)))
                for n in range(LRU_BLOCKS)]
        return jnp.concatenate(rows, 1)

    wg = jnp.concatenate([dense(lru_wa.astype(BF16)), dense(lru_wx.astype(BF16))], -1)
    wout = w_out.astype(BF16)
    lvec = jnp.stack([lru_conv_b, lru_ba, lru_bx, lru_lambda], 1)
    hpad = lambda t: jnp.pad(t, ((0, 0), (0, LANES - SSD_HEADS)))
    hvec = jnp.stack([hpad(ssd_dt_bias), hpad(ssd_a_log)], 1)
    rpad = lambda t: jnp.broadcast_to(jnp.pad(t, ((0, 0), (0, HEAD_ROWS - SSD_HEADS)))[:, :, None],
                                      (DEPTH, HEAD_ROWS, LANES))
    hcol = jnp.stack([rpad(ssd_dt_bias), rpad(ssd_a_log)], 1)
    svec = jnp.stack([jnp.repeat(ssd_d, SSD_HEAD_DIM, axis=1), ssd_norm_g], 1)
    ln = jnp.stack([ln_g, ln_b], 1)
    return (win, aux, wg, wout, lru_conv_w, lvec, ssd_conv_w, ssd_conv_b[:, None, :], hvec, svec, ln, hcol)


def kernel(x_prompt, x_sample, cache_swa_k, cache_swa_v, state_lru_conv, state_lru_h, state_ssd_conv, state_ssd_h, w_in, w_out, att_sinks, lru_conv_w, lru_conv_b, lru_wa, lru_ba, lru_wx, lru_bx, lru_lambda, ssd_conv_w, ssd_conv_b, ssd_dt_bias, ssd_a_log, ssd_d, ssd_norm_g, ln_g, ln_b):
    prm = _prep_params(w_in, w_out, lru_conv_w, lru_conv_b, lru_wa, lru_ba, lru_wx, lru_bx, lru_lambda,
                       ssd_conv_w, ssd_conv_b, ssd_dt_bias, ssd_a_log, ssd_d, ssd_norm_g, ln_g, ln_b)
    bp, lp, _ = x_prompt.shape
    rope_p = _rope_table(0, lp)
    xp = x_prompt
    new_p = [[] for _ in range(6)]
    for l in range(DEPTH):
        xp, ko, vo, lco, lho, sco, sho = _prompt_layer(l, xp, rope_p, att_sinks[l], prm)
        st = (ko.reshape(bp, WINDOW, ATT_KV_HEADS, ATT_HEAD_DIM), vo.reshape(bp, WINDOW, ATT_KV_HEADS, ATT_HEAD_DIM),
              lco, lho.reshape(bp, LRU_WIDTH), sco, sho)
        for lst, t in zip(new_p, st):
            lst.append(t)
    outs_p = [jnp.stack(t) for t in new_p]

    nb = x_sample.shape[0]
    rope_s = _rope_table(PAST_LEN, x_sample.shape[1])
    to_kt = lambda t: jnp.transpose(t, (0, 1, 3, 4, 2)).reshape(DEPTH, nb, ATT_KV_WIDTH, WINDOW)
    from_kt = lambda t: jnp.transpose(t.reshape(DEPTH, nb, ATT_KV_HEADS, ATT_HEAD_DIM, WINDOW), (0, 1, 4, 2, 3))
    swap = lambda t: jnp.transpose(t, (0, 2, 1, 3))
    state = (to_kt(cache_swa_k), to_kt(cache_swa_v), swap(state_lru_conv), state_lru_h, swap(state_ssd_conv),
             state_ssd_h)
    xs = x_sample.reshape(nb, D_MODEL)
    new = None
    for l in range(DEPTH):
        xs, *new = _sample_layer(l, xs, rope_s, att_sinks[l], prm, *state, new)
    cko, cvo, lco, lho, sco, sho = new
    outs_s = (from_kt(cko), from_kt(cvo), swap(lco), lho, swap(sco), sho)
    return (xp, xs.reshape(x_sample.shape)) + tuple(outs_p) + tuple(outs_s)
```

```python
import functools

import jax
import jax.numpy as jnp
import numpy as np
from jax import lax
from jax.experimental import pallas as pl
from jax.experimental.pallas import tpu as pltpu

F32 = jnp.float32
BF16 = jnp.bfloat16

D_MODEL = 1024
DEPTH = 4
PAST_LEN = 8192
D_MIX = 2 * D_MODEL
ATT_HEADS = 8
ATT_KV_HEADS = 2
ATT_HEAD_DIM = 64
ATT_WIDTH = ATT_HEADS * ATT_HEAD_DIM
ATT_KV_WIDTH = ATT_KV_HEADS * ATT_HEAD_DIM
ATT_SCALE = ATT_HEAD_DIM ** -0.5
WINDOW = 128
ROPE_THETA = 500000.0
ROPE_DIM = ATT_HEAD_DIM // 4
LRU_WIDTH = 3 * D_MIX // 8
LRU_BLOCKS = 8
LRU_BLOCK = LRU_WIDTH // LRU_BLOCKS
LRU_C = 8.0
CONV_W = 4
SSD_WIDTH = D_MIX - ATT_WIDTH - LRU_WIDTH
SSD_HEAD_DIM = 64
SSD_HEADS = SSD_WIDTH // SSD_HEAD_DIM
SSD_GROUPS = 2
SSD_STATE = 128
SSD_CONV_CH = SSD_WIDTH + 2 * SSD_GROUPS * SSD_STATE
DEEPNORM_ALPHA = (2.0 * DEPTH) ** 0.25
NORM_EPS = 1e-5

LANES = 128
SUBLANES = 8
HALF = LANES // 2
VMEM_LIMIT_BYTES = 60 * 1024 * 1024

Q_OFF = 0
K_OFF = Q_OFF + ATT_WIDTH
V_OFF = K_OFF + 2 * ATT_KV_WIDTH
GA_OFF = V_OFF + 2 * ATT_KV_WIDTH
XL_OFF = GA_OFF + ATT_WIDTH
GL_OFF = XL_OFF + LRU_WIDTH
Z_OFF = GL_OFF + LRU_WIDTH
XBC_OFF = Z_OFF + SSD_WIDTH
DT_OFF = XBC_OFF + SSD_CONV_CH
N_COLS = DT_OFF + LANES
W_IN_COLS = 2 * ATT_WIDTH + 2 * ATT_KV_WIDTH + 2 * LRU_WIDTH + SSD_WIDTH + SSD_CONV_CH + SSD_HEADS
AUX_COLS = 4 * ATT_KV_WIDTH + LANES
B_OFF = XBC_OFF + SSD_WIDTH
C_OFF = B_OFF + SSD_GROUPS * SSD_STATE
MIX_LRU = ATT_WIDTH
MIX_SSD = ATT_WIDTH + LRU_WIDTH

N_LRU_TILES = LRU_WIDTH // LANES
N_SSD_TILES = SSD_WIDTH // LANES
N_CONV_TILES = SSD_CONV_CH // LANES
CONV_STRIDE = 4
TILES_PER_GROUP = N_SSD_TILES // SSD_GROUPS
GROUP_W = SSD_WIDTH // SSD_GROUPS
HEAD_ROWS = 16

NEG_BIG = -1e30

PROMPT_TILE = 256
SAMPLE_BT = 8
RING_SLOTS = 3


def _sigmoid(x):
    return 0.5 * jnp.tanh(0.5 * x) + 0.5


def _silu(x):
    h = 0.5 * x
    return h + h * jnp.tanh(h)


def _softplus(x):
    return jnp.maximum(x, 0.0) + jnp.log1p(jnp.exp(-jnp.abs(x)))


def _lane_lo(shape):
    return (lax.broadcasted_iota(jnp.int32, shape, len(shape) - 1) % LANES) < HALF


def _rope(t, tab):
    half = ROPE_DIM // 2
    return (t * tab[:, 0:LANES]
            + pltpu.roll(t, LANES - half, 1) * tab[:, LANES:2 * LANES]
            + pltpu.roll(t, half, 1) * tab[:, 2 * LANES:3 * LANES])


def _dot(a, b):
    return jnp.dot(a, b, preferred_element_type=F32)


def _dot_nt(a, b):
    return lax.dot_general(a, b, (((1,), (1,)), ((), ())), preferred_element_type=F32)


def _conv_classes(src_ref, j, rows, w_ref, bias, cs):
    n = CONV_STRIDE
    hist = CONV_W - 1
    taps = {s_: src_ref[j, pl.ds(SUBLANES + s_, rows // n, stride=n), :] for s_ in range(-hist, n)}
    outs = []
    for k in range(n):
        acc = bias + w_ref[hist:hist + 1, cs] * taps[k]
        for t in range(hist):
            acc = acc + w_ref[t:t + 1, cs] * taps[k - hist + t]
        outs.append(acc)
    return outs


def _layer_norm(v, g, b):
    mu = jnp.mean(v, -1, keepdims=True)
    d = v - mu
    var = jnp.mean(d * d, -1, keepdims=True)
    return d * lax.rsqrt(var + NORM_EPS) * g + b


def _lru_coeffs(xl, gates, lvec_ref):
    r = _sigmoid(gates[:, 0:LRU_WIDTH] + lvec_ref[1:2, :])
    ig = _sigmoid(gates[:, LRU_WIDTH:2 * LRU_WIDTH] + lvec_ref[2:3, :])
    log_a = (-LRU_C * _softplus(-lvec_ref[3:4, :])) * r
    a = jnp.exp(log_a)
    bt = jnp.sqrt(-jnp.tanh(log_a) * (1.0 + a * a)) * (ig * xl)
    return a, bt


def _project(x_tile, win_ref, aux_ref, dst):
    xb = x_tile.astype(BF16)
    kv0 = ATT_WIDTH + 2 * ATT_KV_WIDTH
    dst[:, Q_OFF:K_OFF] = _dot(xb, win_ref[:, 0:ATT_WIDTH])
    dst[:, K_OFF:GA_OFF] = _dot(xb, aux_ref[:, 0:GA_OFF - K_OFF])
    dst[:, GA_OFF:Z_OFF] = _dot(xb, win_ref[:, kv0:kv0 + Z_OFF - GA_OFF])
    dst[:, Z_OFF:DT_OFF] = _dot(xb, win_ref[:, kv0 + Z_OFF - GA_OFF:kv0 + DT_OFF - GA_OFF])
    dst[:, DT_OFF:N_COLS] = _dot(xb, aux_ref[:, GA_OFF - K_OFF:GA_OFF - K_OFF + LANES])


def _prompt_kernel(sinks_ref, x_ref, rope_ref, win_ref, aux_ref, wg_ref, wout_ref, lcw_ref, lvec_ref,
                   scw_ref, scb_ref, hcol_ref, svec_ref, ln_ref,
                   y_ref, ko_ref, vo_ref, lco_ref, lho_ref, sco_ref, sho_ref,
                   p_scr, kd_scr, vd_scr, lx_scr, sx_scr, sa_scr, sb_scr, hl_scr, ht_scr, mix_scr, xlb_scr, xc_scr):
    T = PROMPT_TILE
    nsub = T // WINDOW
    c = pl.program_id(1)
    last = c == pl.num_programs(1) - 1

    @pl.when(c == 0)
    def _():
        kd_scr[:, 0:WINDOW, :] = jnp.zeros((2, WINDOW, LANES), BF16)
        vd_scr[:, 0:WINDOW, :] = jnp.zeros((2, WINDOW, LANES), BF16)
        lx_scr[:, 0:SUBLANES, :] = jnp.zeros((N_LRU_TILES, SUBLANES, LANES), F32)
        sx_scr[:, 0:SUBLANES, :] = jnp.zeros((N_CONV_TILES, SUBLANES, LANES), F32)
        hl_scr[...] = jnp.zeros_like(hl_scr)
        ht_scr[...] = jnp.zeros_like(ht_scr)

    _project(x_ref[...], win_ref, aux_ref, p_scr)

    lo_sq = _lane_lo((WINDOW, LANES))

    for g in range(ATT_KV_HEADS):
        kd = _rope(p_scr[:, K_OFF + g * LANES:K_OFF + (g + 1) * LANES], rope_ref[...])
        kd_scr[g, WINDOW:WINDOW + T, :] = kd.astype(BF16)
        vd_scr[g, WINDOW:WINDOW + T, :] = p_scr[:, V_OFF + g * LANES:V_OFF + (g + 1) * LANES].astype(BF16)

    row = lax.broadcasted_iota(jnp.int32, (WINDOW, 2 * WINDOW), 0)
    col = lax.broadcasted_iota(jnp.int32, (WINDOW, 2 * WINDOW), 1)
    band = (col >= row) & (col <= row + WINDOW)
    first_lo = jnp.where(c > 0, 0, WINDOW)
    for i in range(nsub):
        r0 = i * WINDOW
        msk = (band & (col >= first_lo)) if i == 0 else band
        tab = rope_ref[r0:r0 + WINDOW, :]
        for cc in range(ATT_HEADS // 2):
            g = cc // (ATT_HEADS // ATT_KV_HEADS // 2)
            kg = kd_scr[g, r0:r0 + 2 * WINDOW, :]
            vg = vd_scr[g, r0:r0 + 2 * WINDOW, :]
            qp = _rope(p_scr[r0:r0 + WINDOW, Q_OFF + cc * LANES:Q_OFF + (cc + 1) * LANES], tab) * ATT_SCALE
            outs = []
            for half in range(2):
                sink = sinks_ref[2 * cc + half]
                qm = jnp.where(lo_sq if half == 0 else jnp.logical_not(lo_sq), qp, 0.0).astype(BF16)
                s = jnp.where(msk, _dot_nt(qm, kg), NEG_BIG)
                m = jnp.maximum(jnp.max(s, -1, keepdims=True), sink)
                e = jnp.exp(s - m)
                den = jnp.sum(e, -1, keepdims=True) + jnp.exp(sink - m)
                outs.append(_dot(e.astype(BF16), vg) * (1.0 / den))
            att = jnp.where(lo_sq, outs[0], outs[1])
            gate = p_scr[r0:r0 + WINDOW, GA_OFF + cc * LANES:GA_OFF + (cc + 1) * LANES]
            mix_scr[r0:r0 + WINDOW, cc * LANES:(cc + 1) * LANES] = (att * _silu(gate)).astype(BF16)
    for g in range(ATT_KV_HEADS):
        kd_scr[g, 0:WINDOW, :] = kd_scr[g, T:T + WINDOW, :]
        vd_scr[g, 0:WINDOW, :] = vd_scr[g, T:T + WINDOW, :]

    P = SUBLANES
    H = CONV_W - 1
    for j in range(N_LRU_TILES):
        cs_ = slice(j * LANES, (j + 1) * LANES)
        lx_scr[j, P:P + T, :] = p_scr[:, XL_OFF + j * LANES:XL_OFF + (j + 1) * LANES]
        for k, xl in enumerate(_conv_classes(lx_scr, j, T, lcw_ref, lvec_ref[0:1, cs_], cs_)):
            sb_scr[j, pl.ds(k, T // CONV_STRIDE, stride=CONV_STRIDE), :] = xl
        xlb_scr[:, cs_] = sb_scr[j].astype(BF16)
    lx_scr[:, 0:P, :] = lx_scr[:, T:T + P, :]

    G = T // SUBLANES
    row_g = lax.broadcasted_iota(jnp.int32, (G, LANES), 0)
    cl = -LRU_C * _softplus(-lvec_ref[3:4, :])
    for jj in range(N_LRU_TILES // 2):
        c0 = jj * 2 * LANES
        k0 = (c0 // LRU_BLOCK) * LRU_BLOCK // LANES * LANES
        k1 = -(-(-(-(c0 + 2 * LANES) // LRU_BLOCK) * LRU_BLOCK) // LANES) * LANES
        xk = xlb_scr[:, k0:k1]
        gr = _dot(xk, wg_ref[k0:k1, c0:c0 + 2 * LANES])
        gi = _dot(xk, wg_ref[k0:k1, LRU_WIDTH + c0:LRU_WIDTH + c0 + 2 * LANES])
        for u in range(2):
            j = 2 * jj + u
            cs_ = slice(j * LANES, (j + 1) * LANES)
            us = slice(u * LANES, (u + 1) * LANES)
            r = _sigmoid(gr[:, us] + lvec_ref[1:2, cs_])
            ig = _sigmoid(gi[:, us] + lvec_ref[2:3, cs_])
            log_a = cl[:, cs_] * r
            a = jnp.exp(log_a)
            sa_scr[j] = a
            sb_scr[j] = jnp.sqrt(-jnp.tanh(log_a) * (1.0 + a * a)) * (ig * sb_scr[j])
    for j in range(N_LRU_TILES):
        ca = sa_scr[j, pl.ds(0, G, stride=SUBLANES), :]
        cb = sb_scr[j, pl.ds(0, G, stride=SUBLANES), :]
        for k in range(1, SUBLANES):
            ak = sa_scr[j, pl.ds(k, G, stride=SUBLANES), :]
            cb = ak * cb + sb_scr[j, pl.ds(k, G, stride=SUBLANES), :]
            ca = ak * ca
            sa_scr[j, pl.ds(k, G, stride=SUBLANES), :] = ca
            sb_scr[j, pl.ds(k, G, stride=SUBLANES), :] = cb
        hin = hl_scr[0:1, j * LANES:(j + 1) * LANES]
        cb = cb + jnp.where(row_g == 0, ca * hin, 0.0)
        d = 1
        while d < G:
            ok = row_g >= d
            cb = cb + ca * jnp.where(ok, pltpu.roll(cb, d, 0), 0.0)
            ca = ca * jnp.where(ok, pltpu.roll(ca, d, 0), 1.0)
            d *= 2
        hprev = jnp.where(row_g == 0, hin, pltpu.roll(cb, 1, 0))
        for k in range(SUBLANES):
            sb_scr[j, pl.ds(k, G, stride=SUBLANES), :] = (
                sb_scr[j, pl.ds(k, G, stride=SUBLANES), :] + sa_scr[j, pl.ds(k, G, stride=SUBLANES), :] * hprev)
        hl_scr[0:1, j * LANES:(j + 1) * LANES] = cb[G - 1:G, :]
        gl = p_scr[:, GL_OFF + j * LANES:GL_OFF + (j + 1) * LANES]
        mix_scr[:, MIX_LRU + j * LANES:MIX_LRU + (j + 1) * LANES] = (sb_scr[j] * _silu(gl)).astype(BF16)

    for j in range(N_CONV_TILES):
        cs_ = slice(j * LANES, (j + 1) * LANES)
        sx_scr[j, P:P + T, :] = p_scr[:, XBC_OFF + j * LANES:XBC_OFF + (j + 1) * LANES]
        for k, xc in enumerate(_conv_classes(sx_scr, j, T, scw_ref, scb_ref[0:1, cs_], cs_)):
            xc_scr[j, pl.ds(k, T // CONV_STRIDE, stride=CONV_STRIDE), :] = _silu(xc)
    sx_scr[:, 0:P, :] = sx_scr[:, T:T + P, :]

    qi = lax.broadcasted_iota(jnp.int32, (WINDOW, WINDOW), 0)
    si = lax.broadcasted_iota(jnp.int32, (WINDOW, WINDOW), 1)
    causal = si <= qi
    upper = (qi <= si).astype(F32)
    hpad = jnp.zeros((WINDOW - HEAD_ROWS, WINDOW), F32)
    a_col = -jnp.exp(hcol_ref[1])
    for i in range(nsub):
        r0 = i * WINDOW
        dt_t = _softplus(p_scr[r0:r0 + WINDOW, DT_OFF:DT_OFF + LANES].T[0:HEAD_ROWS, :] + hcol_ref[0])
        cst = jnp.dot(dt_t * a_col, upper, precision=lax.Precision.HIGHEST, preferred_element_type=F32)
        dt_c = jnp.concatenate([dt_t, hpad], axis=0).T
        cs = jnp.concatenate([cst, hpad], axis=0).T
        ys = [None] * N_SSD_TILES
        for g in range(SSD_GROUPS):
            bg_t = xc_scr[N_SSD_TILES + g, r0:r0 + WINDOW, :].T.astype(BF16)
            cg = xc_scr[N_SSD_TILES + SSD_GROUPS + g, r0:r0 + WINDOW, :].astype(BF16)
            cbm = _dot(cg, bg_t)
            yo = _dot(cg, ht_scr[g].astype(BF16))
            xdec, edec = [], []
            for pp in range(TILES_PER_GROUP):
                p = g * TILES_PER_GROUP + pp
                bcs, mm, dts = [], [], []
                for h in (2 * p, 2 * p + 1):
                    bc = jnp.broadcast_to(cs[:, h:h + 1], (WINDOW, WINDOW))
                    lmat = jnp.exp(jnp.where(causal, bc - cst[h:h + 1, :], NEG_BIG))
                    mm.append((cbm * lmat).astype(BF16))
                    bcs.append(bc)
                    dts.append(jnp.broadcast_to(dt_c[:, h:h + 1], (WINDOW, LANES)))
                csl = jnp.where(lo_sq, bcs[0], bcs[1])
                ecs = jnp.exp(csl)
                dec = jnp.exp(csl[WINDOW - 1:WINDOW, :] - csl)
                xs_p = xc_scr[p, r0:r0 + WINDOW, :]
                xdt = xs_p * jnp.where(lo_sq, dts[0], dts[1])
                xdt_b = xdt.astype(BF16)
                y = jnp.where(lo_sq, _dot(mm[0], xdt_b), _dot(mm[1], xdt_b))
                y = y + yo[:, pp * LANES:(pp + 1) * LANES] * ecs
                y = y + svec_ref[0:1, p * LANES:(p + 1) * LANES] * xs_p
                ys[p] = y * _silu(p_scr[r0:r0 + WINDOW, Z_OFF + p * LANES:Z_OFF + (p + 1) * LANES])
                xdec.append((xdt * dec).astype(BF16))
                edec.append(ecs[WINDOW - 1:WINDOW, :])
            ht_scr[g] = (ht_scr[g] * jnp.concatenate(edec, axis=1)
                         + _dot(bg_t, jnp.concatenate(xdec, axis=1)))
        ss = jnp.sum(ys[0] * ys[0], -1, keepdims=True)
        for p in range(1, N_SSD_TILES):
            ss = ss + jnp.sum(ys[p] * ys[p], -1, keepdims=True)
        rinv = lax.rsqrt(ss * (1.0 / SSD_WIDTH) + NORM_EPS)
        for p in range(N_SSD_TILES):
            mix_scr[r0:r0 + WINDOW, MIX_SSD + p * LANES:MIX_SSD + (p + 1) * LANES] = (
                ys[p] * rinv * svec_ref[1:2, p * LANES:(p + 1) * LANES]).astype(BF16)

    out = _dot(mix_scr[...], wout_ref[...])
    y_ref[...] = _layer_norm(DEEPNORM_ALPHA * x_ref[...] + out, ln_ref[0:1, :], ln_ref[1:2, :])

    @pl.when(last)
    def _():
        tab = rope_ref[T - WINDOW:T, :]
        k0 = _rope(p_scr[T - WINDOW:T, K_OFF:K_OFF + LANES], tab)
        k1 = _rope(p_scr[T - WINDOW:T, K_OFF + LANES:K_OFF + 2 * LANES], tab)
        ko_ref[...] = jnp.where(lo_sq, k0, k1)
        vo_ref[...] = jnp.where(lo_sq, p_scr[T - WINDOW:T, V_OFF:V_OFF + LANES],
                                p_scr[T - WINDOW:T, V_OFF + LANES:V_OFF + 2 * LANES])
        for j in range(N_LRU_TILES):
            lco_ref[:, j * LANES:(j + 1) * LANES] = lx_scr[j, P + T - (CONV_W - 1):P + T, :]
        lho_ref[...] = hl_scr[...]
        for j in range(N_CONV_TILES):
            sco_ref[:, j * LANES:(j + 1) * LANES] = sx_scr[j, P + T - (CONV_W - 1):P + T, :]
        for g in range(SSD_GROUPS):
            for pp in range(TILES_PER_GROUP):
                tile = ht_scr[g, :, pp * LANES:(pp + 1) * LANES].T
                h = 2 * (g * TILES_PER_GROUP + pp)
                sho_ref[h] = tile[0:SSD_HEAD_DIM, :]
                sho_ref[h + 1] = tile[SSD_HEAD_DIM:2 * SSD_HEAD_DIM, :]


def _const_spec(shape, layer):
    nd = len(shape)
    return pl.BlockSpec((None,) + tuple(shape), lambda *_: (layer,) + (0,) * nd,
                        pipeline_mode=pl.Buffered(1))


def _prompt_layer(layer, x, rope_tab, sinks, prm, prev):
    B, L, _ = x.shape
    T = PROMPT_TILE
    nc = L // T
    win, aux, wg, wout, lcw, lvec, scw, scb, hvec, svec, ln, hcol = prm
    tile = lambda b, c: (b, c, 0)

    in_specs = [
        pl.BlockSpec(memory_space=pltpu.SMEM),
        pl.BlockSpec((None, T, D_MODEL), tile),
        pl.BlockSpec((T, 3 * LANES), lambda b, c: (c, 0)),
        _const_spec((D_MODEL, W_IN_COLS), layer),
        _const_spec((D_MODEL, AUX_COLS), layer),
        _const_spec((LRU_WIDTH, 2 * LRU_WIDTH), layer),
        _const_spec((D_MIX, D_MODEL), layer),
        _const_spec((CONV_W, LRU_WIDTH), layer),
        _const_spec((4, LRU_WIDTH), layer),
        _const_spec((CONV_W, SSD_CONV_CH), layer),
        _const_spec((1, SSD_CONV_CH), layer),
        _const_spec((2, HEAD_ROWS, LANES), layer),
        _const_spec((2, SSD_WIDTH), layer),
        _const_spec((2, D_MODEL), layer),
    ]
    state_shapes = (
        (WINDOW, ATT_KV_WIDTH),
        (WINDOW, ATT_KV_WIDTH),
        (CONV_W - 1, LRU_WIDTH),
        (1, LRU_WIDTH),
        (CONV_W - 1, SSD_CONV_CH),
        (SSD_HEADS, SSD_HEAD_DIM, SSD_STATE),
    )
    out_shape = (jax.ShapeDtypeStruct((B, L, D_MODEL), F32),) + tuple(
        jax.ShapeDtypeStruct((DEPTH, B) + s, F32) for s in state_shapes)
    per_lb = lambda s: pl.BlockSpec((None, None) + s, lambda b, c: (layer, b) + (0,) * len(s))
    out_specs = (pl.BlockSpec((None, T, D_MODEL), tile),) + tuple(per_lb(s) for s in state_shapes)
    operands = [sinks, x, rope_tab, win, aux, wg, wout, lcw, lvec, scw, scb, hcol, svec, ln]
    n_in = len(operands)
    aliases = {}
    if prev is not None:
        for k, arr in enumerate(prev):
            aliases[len(operands)] = 1 + k
            operands.append(arr)
            in_specs.append(pl.BlockSpec(memory_space=pl.ANY))

    def body(*refs):
        _prompt_kernel(*refs[:n_in], *refs[n_in + len(aliases):])

    scratch = [
        pltpu.VMEM((T, N_COLS), F32),
        pltpu.VMEM((ATT_KV_HEADS, WINDOW + T, LANES), BF16),
        pltpu.VMEM((ATT_KV_HEADS, WINDOW + T, LANES), BF16),
        pltpu.VMEM((N_LRU_TILES, SUBLANES + T, LANES), F32),
        pltpu.VMEM((N_CONV_TILES, SUBLANES + T, LANES), F32),
        pltpu.VMEM((N_LRU_TILES, T, LANES), F32),
        pltpu.VMEM((N_LRU_TILES, T, LANES), F32),
        pltpu.VMEM((1, LRU_WIDTH), F32),
        pltpu.VMEM((SSD_GROUPS, SSD_STATE, GROUP_W), F32),
        pltpu.VMEM((T, D_MIX), BF16),
        pltpu.VMEM((T, LRU_WIDTH), BF16),
        pltpu.VMEM((N_CONV_TILES, T, LANES), F32),
    ]
    return pl.pallas_call(
        body,
        grid=(B, nc),
        in_specs=in_specs,
        out_specs=out_specs,
        out_shape=out_shape,
        scratch_shapes=scratch,
        input_output_aliases=aliases,
        compiler_params=pltpu.CompilerParams(
            dimension_semantics=("arbitrary", "arbitrary"),
            vmem_limit_bytes=VMEM_LIMIT_BYTES),
        name=f"prompt_layer{layer}",
    )(*operands)


def _conv_step(x_new, st_ref, sto_ref, w_ref, bias):
    y = bias + w_ref[CONV_W - 1:CONV_W, :] * x_new
    for t in range(CONV_W - 1):
        y = y + w_ref[t:t + 1, :] * st_ref[t]
    for t in range(CONV_W - 2):
        sto_ref[t] = st_ref[t + 1]
    sto_ref[CONV_W - 2] = x_new
    return y


def _sample_kernel(layer, n_aliased, sinks_ref, x_ref, rope_ref, win_ref, aux_ref, wg_ref, wout_ref, lcw_ref,
                   lvec_ref, scw_ref, scb_ref, hvec_ref, svec_ref, ln_ref,
                   ck_hbm, cv_hbm, lc_ref, lh_ref, sc_ref, sh_hbm, *refs):
    (y_ref, cko_ref, cvo_ref, lco_ref, lho_ref, sco_ref, sho_ref,
     p_scr, q_scr, o_scr, kn_scr, vn_scr, xdt_scr, da_scr, y_scr, mix_scr,
     ck_ring, cv_ring, sh_ring, ring_sem) = refs[n_aliased:]
    NB = x_ref.shape[0]
    BT = SAMPLE_BT
    i = pl.program_id(0)
    nsteps = pl.num_programs(0)
    lo_f = _lane_lo((NB, LANES))

    def fetch(j):
        slot = lax.rem(j, RING_SLOTS)
        rows = pl.ds(pl.multiple_of(j * BT, BT), BT)
        return (pltpu.make_async_copy(ck_hbm.at[layer, rows], ck_ring.at[slot], ring_sem.at[0, slot]),
                pltpu.make_async_copy(cv_hbm.at[layer, rows], cv_ring.at[slot], ring_sem.at[1, slot]),
                pltpu.make_async_copy(sh_hbm.at[layer, rows], sh_ring.at[slot], ring_sem.at[2, slot]))

    @pl.when(i == 0)
    def _():
        for j in range(RING_SLOTS - 1):
            for cp in fetch(j):
                cp.start()

    @pl.when(i + RING_SLOTS - 1 < nsteps)
    def _():
        for cp in fetch(i + RING_SLOTS - 1):
            cp.start()

    @pl.when(i == 0)
    def _():
        _project(x_ref[...], win_ref, aux_ref, p_scr)
        tab = rope_ref[...]
        for cc in range(ATT_HEADS // 2):
            g = cc // (ATT_HEADS // ATT_KV_HEADS // 2)
            qp = _rope(p_scr[:, Q_OFF + cc * LANES:Q_OFF + (cc + 1) * LANES], tab) * ATT_SCALE
            for half in range(2):
                t = jnp.where(lo_f if half == 0 else jnp.logical_not(lo_f), qp, 0.0)
                if half != g:
                    t = pltpu.roll(t, HALF, 1)
                q_scr[pl.ds(2 * cc + half, NB, stride=ATT_HEADS), :] = t
        k0 = _rope(p_scr[:, K_OFF:K_OFF + LANES], tab)
        k1 = _rope(p_scr[:, K_OFF + LANES:K_OFF + 2 * LANES], tab)
        kn_scr[...] = jnp.where(lo_f, k0, k1)
        vn_scr[...] = jnp.where(lo_f, p_scr[:, V_OFF:V_OFF + LANES], p_scr[:, V_OFF + LANES:V_OFF + 2 * LANES])
        xin = p_scr[:, XL_OFF:XL_OFF + LRU_WIDTH]
        xl = _conv_step(xin, lc_ref, lco_ref, lcw_ref, lvec_ref[0:1, :])
        gates = _dot(xl.astype(BF16), wg_ref[...])
        a, bt = _lru_coeffs(xl, gates, lvec_ref)
        h1 = a * lh_ref[...] + bt
        lho_ref[...] = h1
        mix_scr[:, MIX_LRU:MIX_LRU + LRU_WIDTH] = (h1 * _silu(p_scr[:, GL_OFF:GL_OFF + LRU_WIDTH])).astype(BF16)
        xin2 = p_scr[:, XBC_OFF:XBC_OFF + SSD_CONV_CH]
        xc = _conv_step(xin2, sc_ref, sco_ref, scw_ref, scb_ref[0:1, :])
        p_scr[:, XBC_OFF:XBC_OFF + SSD_CONV_CH] = _silu(xc)
        dt = _softplus(p_scr[:, DT_OFF:DT_OFF + LANES] + hvec_ref[0:1, :])
        da_scr[...] = jnp.exp(dt * (-jnp.exp(hvec_ref[1:2, :])))
        for p in range(N_SSD_TILES):
            dte = jnp.where(lo_f, jnp.broadcast_to(dt[:, 2 * p:2 * p + 1], (NB, LANES)),
                            jnp.broadcast_to(dt[:, 2 * p + 1:2 * p + 2], (NB, LANES)))
            xdt_scr[:, p * LANES:(p + 1) * LANES] = p_scr[:, XBC_OFF + p * LANES:XBC_OFF + (p + 1) * LANES] * dte

    for cp in fetch(i):
        cp.wait()
    slot = lax.rem(i, RING_SLOTS)
    ck_ref, cv_ref, sh_ref = ck_ring.at[slot], cv_ring.at[slot], sh_ring.at[slot]
    r8 = pl.multiple_of(i * BT, BT)
    kn_blk = kn_scr[pl.ds(r8, BT), :]
    vn_blk = vn_scr[pl.ds(r8, BT), :]
    zpad = jnp.zeros((LANES - BT, LANES), F32)
    kn_t = jnp.concatenate([kn_blk, zpad], axis=0).T
    vn_t = jnp.concatenate([vn_blk, zpad], axis=0).T
    newest = lax.broadcasted_iota(jnp.int32, (ATT_KV_WIDTH, WINDOW), 1) == WINDOW - 1
    da_t = jnp.concatenate([da_scr[pl.ds(r8, BT), :], zpad], axis=0).T
    x_t = [jnp.concatenate([xdt_scr[pl.ds(r8, BT), p * LANES:(p + 1) * LANES], zpad], axis=0).T
           for p in range(N_SSD_TILES)]
    b_blk = p_scr[pl.ds(r8, BT), B_OFF:B_OFF + SSD_GROUPS * SSD_STATE]
    c_blk = p_scr[pl.ds(r8, BT), C_OFF:C_OFF + SSD_GROUPS * SSD_STATE].astype(BF16)
    row8 = lax.broadcasted_iota(jnp.int32, (ATT_HEADS, 1), 0)
    sink = jnp.zeros((ATT_HEADS, 1), F32)
    for h in range(ATT_HEADS):
        sink = jnp.where(row8 == h, sinks_ref[h], sink)
    rowb = lax.broadcasted_iota(jnp.int32, (BT, GROUP_W), 0)
    y_acc = [jnp.zeros((BT, GROUP_W), F32) for _ in range(SSD_GROUPS)]
    rowk = lax.broadcasted_iota(jnp.int32, (BT, SSD_STATE), 0)
    outer = []
    for g in range(SSD_GROUPS):
        bg = b_blk[:, g * SSD_STATE:(g + 1) * SSD_STATE]
        diag = jnp.concatenate([jnp.where(rowk == bb, bg, 0.0) for bb in range(BT)], axis=1)
        rhs = jnp.concatenate([diag, jnp.zeros((LANES - BT, BT * SSD_STATE), F32)], axis=0).astype(BF16)
        outer.append([_dot(x_t[g * TILES_PER_GROUP + pp].astype(BF16), rhs) for pp in range(TILES_PER_GROUP)])
    for bb in range(BT):
        qr = q_scr[pl.ds(pl.multiple_of((r8 + bb) * ATT_HEADS, ATT_HEADS), ATT_HEADS), :]
        kt = ck_ref[bb]
        vt = cv_ref[bb]
        s = _dot(qr.astype(BF16), kt.astype(BF16))
        s_new = jnp.sum(qr * kn_blk[bb:bb + 1, :], -1, keepdims=True)
        m = jnp.maximum(jnp.maximum(jnp.max(s, -1, keepdims=True), s_new), sink)
        e = jnp.exp(s - m)
        e_new = jnp.exp(s_new - m)
        den = jnp.sum(e, -1, keepdims=True) + e_new + jnp.exp(sink - m)
        o = _dot_nt(e.astype(BF16), vt.astype(BF16)) + e_new * vn_blk[bb:bb + 1, :]
        o_scr[pl.ds(pl.multiple_of((r8 + bb) * ATT_HEADS, ATT_HEADS), ATT_HEADS), :] = o * (1.0 / den)
        cko_ref[bb] = jnp.where(newest, kn_t[:, bb:bb + 1], pltpu.roll(kt, WINDOW - 1, 1))
        cvo_ref[bb] = jnp.where(newest, vn_t[:, bb:bb + 1], pltpu.roll(vt, WINDOW - 1, 1))
        for g in range(SSD_GROUPS):
            tiles = []
            for pp in range(TILES_PER_GROUP):
                p = g * TILES_PER_GROUP + pp
                for hh in range(2):
                    h = 2 * p + hh
                    dab = jnp.broadcast_to(da_t[h:h + 1, bb:bb + 1], (SSD_HEAD_DIM, SSD_STATE))
                    h1 = sh_ref[bb, h] * dab + outer[g][pp][hh * SSD_HEAD_DIM:(hh + 1) * SSD_HEAD_DIM,
                                                               bb * SSD_STATE:(bb + 1) * SSD_STATE]
                    sho_ref[bb, h] = h1
                    tiles.append(h1.astype(BF16))
            res = _dot_nt(c_blk[:, g * SSD_STATE:(g + 1) * SSD_STATE], jnp.concatenate(tiles, axis=0))
            y_acc[g] = jnp.where(rowb == bb, res, y_acc[g])
    for g in range(SSD_GROUPS):
        y_scr[pl.ds(r8, BT), g * GROUP_W:(g + 1) * GROUP_W] = y_acc[g]

    @pl.when(i == pl.num_programs(0) - 1)
    def _():
        for cc in range(ATT_HEADS // 2):
            g = cc // (ATT_HEADS // ATT_KV_HEADS // 2)
            oe = o_scr[pl.ds(2 * cc, NB, stride=ATT_HEADS), :]
            oo = o_scr[pl.ds(2 * cc + 1, NB, stride=ATT_HEADS), :]
            if g == 0:
                oo = pltpu.roll(oo, HALF, 1)
            else:
                oe = pltpu.roll(oe, HALF, 1)
            gate = p_scr[:, GA_OFF + cc * LANES:GA_OFF + (cc + 1) * LANES]
            mix_scr[:, cc * LANES:(cc + 1) * LANES] = (jnp.where(lo_f, oe, oo) * _silu(gate)).astype(BF16)
        y = (y_scr[...] + svec_ref[0:1, :] * p_scr[:, XBC_OFF:XBC_OFF + SSD_WIDTH]) * _silu(p_scr[:, Z_OFF:Z_OFF + SSD_WIDTH])
        y = y * lax.rsqrt(jnp.mean(y * y, -1, keepdims=True) + NORM_EPS) * svec_ref[1:2, :]
        mix_scr[:, MIX_SSD:MIX_SSD + SSD_WIDTH] = y.astype(BF16)
        out = _dot(mix_scr[...], wout_ref[...])
        y_ref[...] = _layer_norm(DEEPNORM_ALPHA * x_ref[...] + out, ln_ref[0:1, :], ln_ref[1:2, :])


def _sample_layer(layer, x, rope_tab, sinks, prm, ck, cv, lc, lh, sc, sh, prev):
    NB = x.shape[0]
    BT = SAMPLE_BT
    H = CONV_W - 1
    win, aux, wg, wout, lcw, lvec, scw, scb, hvec, svec, ln, _ = prm
    whole = lambda shape: pl.BlockSpec(shape, lambda i: (0,) * len(shape), pipeline_mode=pl.Buffered(1))
    lwhole = lambda shape: pl.BlockSpec((None,) + shape, lambda i: (layer,) + (0,) * len(shape),
                                        pipeline_mode=pl.Buffered(1))
    rows = lambda shape: pl.BlockSpec((None, BT) + shape, lambda i: (layer, i) + (0,) * len(shape))
    hbm = pl.BlockSpec(memory_space=pl.ANY)
    state_specs = [
        hbm,
        hbm,
        lwhole((H, NB, LRU_WIDTH)),
        lwhole((NB, LRU_WIDTH)),
        lwhole((H, NB, SSD_CONV_CH)),
        hbm,
    ]
    in_specs = [
        pl.BlockSpec(memory_space=pltpu.SMEM),
        whole((NB, D_MODEL)),
        whole((1, 3 * LANES)),
        _const_spec((D_MODEL, W_IN_COLS), layer),
        _const_spec((D_MODEL, AUX_COLS), layer),
        _const_spec((LRU_WIDTH, 2 * LRU_WIDTH), layer),
        _const_spec((D_MIX, D_MODEL), layer),
        _const_spec((CONV_W, LRU_WIDTH), layer),
        _const_spec((4, LRU_WIDTH), layer),
        _const_spec((CONV_W, SSD_CONV_CH), layer),
        _const_spec((1, SSD_CONV_CH), layer),
        _const_spec((2, LANES), layer),
        _const_spec((2, SSD_WIDTH), layer),
        _const_spec((2, D_MODEL), layer),
    ] + state_specs
    operands = [sinks, x, rope_tab, win, aux, wg, wout, lcw, lvec, scw, scb, hvec, svec, ln, ck, cv, lc, lh, sc, sh]
    aliases = {}
    if prev is not None:
        for k, arr in enumerate(prev):
            aliases[len(operands)] = 1 + k
            operands.append(arr)
            in_specs.append(pl.BlockSpec(memory_space=pl.ANY))
    out_shape = (jax.ShapeDtypeStruct((NB, D_MODEL), F32),) + tuple(
        jax.ShapeDtypeStruct(a.shape, F32) for a in (ck, cv, lc, lh, sc, sh))
    full = lambda shape: pl.BlockSpec(shape, lambda i: (0,) * len(shape))
    lfull = lambda shape: pl.BlockSpec((None,) + shape, lambda i: (layer,) + (0,) * len(shape))
    out_specs = (
        full((NB, D_MODEL)),
        rows((ATT_KV_WIDTH, WINDOW)),
        rows((ATT_KV_WIDTH, WINDOW)),
        lfull((H, NB, LRU_WIDTH)),
        lfull((NB, LRU_WIDTH)),
        lfull((H, NB, SSD_CONV_CH)),
        rows((SSD_HEADS, SSD_HEAD_DIM, SSD_STATE)),
    )
    scratch = [
        pltpu.VMEM((NB, N_COLS), F32),
        pltpu.VMEM((NB * ATT_HEADS, LANES), F32),
        pltpu.VMEM((NB * ATT_HEADS, LANES), F32),
        pltpu.VMEM((NB, ATT_KV_WIDTH), F32),
        pltpu.VMEM((NB, ATT_KV_WIDTH), F32),
        pltpu.VMEM((NB, SSD_WIDTH), F32),
        pltpu.VMEM((NB, LANES), F32),
        pltpu.VMEM((NB, SSD_WIDTH), F32),
        pltpu.VMEM((NB, D_MIX), BF16),
        pltpu.VMEM((RING_SLOTS, BT, ATT_KV_WIDTH, WINDOW), F32),
        pltpu.VMEM((RING_SLOTS, BT, ATT_KV_WIDTH, WINDOW), F32),
        pltpu.VMEM((RING_SLOTS, BT, SSD_HEADS, SSD_HEAD_DIM, SSD_STATE), F32),
        pltpu.SemaphoreType.DMA((3, RING_SLOTS)),
    ]
    assert NB // BT >= RING_SLOTS - 1
    return pl.pallas_call(
        functools.partial(_sample_kernel, layer, len(aliases)),
        grid=(NB // BT,),
        in_specs=in_specs,
        out_specs=out_specs,
        out_shape=out_shape,
        scratch_shapes=scratch,
        input_output_aliases=aliases,
        compiler_params=pltpu.CompilerParams(
            dimension_semantics=("arbitrary",),
            vmem_limit_bytes=VMEM_LIMIT_BYTES),
        name=f"sample_layer{layer}",
    )(*operands)


def _rope_table(start, length):
    half = ROPE_DIM // 2
    inv = (np.float32(ROPE_THETA) ** (-np.arange(half, dtype=np.float32) / np.float32(half))).astype(np.float32)
    ang = (np.arange(start, start + length).astype(np.float32)[:, None] * inv[None, :]).astype(np.float32)
    cos, sin = np.cos(ang), np.sin(ang)
    d = np.arange(LANES) % ATT_HEAD_DIM
    fi = d % half
    c = np.where(d < ROPE_DIM, cos[:, fi], np.float32(1.0))
    s1 = np.where(d < half, -sin[:, fi], np.float32(0.0))
    s2 = np.where((d >= half) & (d < ROPE_DIM), sin[:, fi], np.float32(0.0))
    return jnp.asarray(np.concatenate([c, s1, s2], axis=1).astype(np.float32))


def _prep_params(w_in, w_out, lru_conv_w, lru_conv_b, lru_wa, lru_ba, lru_wx, lru_bx, lru_lambda,
                 ssd_conv_w, ssd_conv_b, ssd_dt_bias, ssd_a_log, ssd_d, ssd_norm_g, ln_g, ln_b):
    hd = ATT_HEAD_DIM
    win = w_in.astype(BF16)
    k = w_in[..., ATT_WIDTH:ATT_WIDTH + ATT_KV_WIDTH]
    v = w_in[..., ATT_WIDTH + ATT_KV_WIDTH:ATT_WIDTH + 2 * ATT_KV_WIDTH]
    dup = lambda t: jnp.concatenate([t[..., :hd], t[..., :hd], t[..., hd:], t[..., hd:]], -1)
    dtw = jnp.pad(w_in[..., W_IN_COLS - SSD_HEADS:], ((0, 0), (0, 0), (0, LANES - SSD_HEADS)))
    aux = jnp.concatenate([dup(k), dup(v), dtw], -1).astype(BF16)

    def dense(w):
        rows = [jnp.pad(w[:, n], ((0, 0), (0, 0), (n * LRU_BLOCK, LRU_WIDTH - (n + 1) * LRU_BLOCK)))
                for n in range(LRU_BLOCKS)]
        return jnp.concatenate(rows, 1)

    wg = jnp.concatenate([dense(lru_wa.astype(BF16)), dense(lru_wx.astype(BF16))], -1)
    wout = w_out.astype(BF16)
    lvec = jnp.stack([lru_conv_b, lru_ba, lru_bx, lru_lambda], 1)
    hpad = lambda t: jnp.pad(t, ((0, 0), (0, LANES - SSD_HEADS)))
    hvec = jnp.stack([hpad(ssd_dt_bias), hpad(ssd_a_log)], 1)
    rpad = lambda t: jnp.broadcast_to(jnp.pad(t, ((0, 0), (0, HEAD_ROWS - SSD_HEADS)))[:, :, None],
                                      (DEPTH, HEAD_ROWS, LANES))
    hcol = jnp.stack([rpad(ssd_dt_bias), rpad(ssd_a_log)], 1)
    svec = jnp.stack([jnp.repeat(ssd_d, SSD_HEAD_DIM, axis=1), ssd_norm_g], 1)
    ln = jnp.stack([ln_g, ln_b], 1)
    return (win, aux, wg, wout, lru_conv_w, lvec, ssd_conv_w, ssd_conv_b[:, None, :], hvec, svec, ln, hcol)


def kernel(x_prompt, x_sample, cache_swa_k, cache_swa_v, state_lru_conv, state_lru_h, state_ssd_conv, state_ssd_h, w_in, w_out, att_sinks, lru_conv_w, lru_conv_b, lru_wa, lru_ba, lru_wx, lru_bx, lru_lambda, ssd_conv_w, ssd_conv_b, ssd_dt_bias, ssd_a_log, ssd_d, ssd_norm_g, ln_g, ln_b):
    prm = _prep_params(w_in, w_out, lru_conv_w, lru_conv_b, lru_wa, lru_ba, lru_wx, lru_bx, lru_lambda,
                       ssd_conv_w, ssd_conv_b, ssd_dt_bias, ssd_a_log, ssd_d, ssd_norm_g, ln_g, ln_b)
    bp, lp, _ = x_prompt.shape
    rope_p = _rope_table(0, lp)
    xp = x_prompt
    new_p = None
    for l in range(DEPTH):
        xp, *new_p = _prompt_layer(l, xp, rope_p, att_sinks[l], prm, new_p)
    ko, vo, lco, lho, sco, sho = new_p
    outs_p = (ko.reshape(DEPTH, bp, WINDOW, ATT_KV_HEADS, ATT_HEAD_DIM),
              vo.reshape(DEPTH, bp, WINDOW, ATT_KV_HEADS, ATT_HEAD_DIM),
              lco, lho.reshape(DEPTH, bp, LRU_WIDTH), sco, sho)

    nb = x_sample.shape[0]
    rope_s = _rope_table(PAST_LEN, x_sample.shape[1])
    to_kt = lambda t: jnp.transpose(t, (0, 1, 3, 4, 2)).reshape(DEPTH, nb, ATT_KV_WIDTH, WINDOW)
    from_kt = lambda t: jnp.transpose(t.reshape(DEPTH, nb, ATT_KV_HEADS, ATT_HEAD_DIM, WINDOW), (0, 1, 4, 2, 3))
    swap = lambda t: jnp.transpose(t, (0, 2, 1, 3))
    state = (to_kt(cache_swa_k), to_kt(cache_swa_v), swap(state_lru_conv), state_lru_h, swap(state_ssd_conv),
             state_ssd_h)
    xs = x_sample.reshape(nb, D_MODEL)
    new = None
    for l in range(DEPTH):
        xs, *new = _sample_layer(l, xs, rope_s, att_sinks[l], prm, *state, new)
    cko, cvo, lco, lho, sco, sho = new
    outs_s = (from_kt(cko), from_kt(cvo), swap(lco), lho, swap(sco), sho)
    return (xp, xs.reshape(x_sample.shape)) + tuple(outs_p) + tuple(outs_s)
```

```python
import functools

import jax
import jax.numpy as jnp
import numpy as np
from jax import lax
from jax.experimental import pallas as pl
from jax.experimental.pallas import tpu as pltpu

F32 = jnp.float32
BF16 = jnp.bfloat16

D_MODEL = 1024
DEPTH = 4
PAST_LEN = 8192
D_MIX = 2 * D_MODEL
ATT_HEADS = 8
ATT_KV_HEADS = 2
ATT_HEAD_DIM = 64
ATT_WIDTH = ATT_HEADS * ATT_HEAD_DIM
ATT_KV_WIDTH = ATT_KV_HEADS * ATT_HEAD_DIM
ATT_SCALE = ATT_HEAD_DIM ** -0.5
WINDOW = 128
ROPE_THETA = 500000.0
ROPE_DIM = ATT_HEAD_DIM // 4
LRU_WIDTH = 3 * D_MIX // 8
LRU_BLOCKS = 8
LRU_BLOCK = LRU_WIDTH // LRU_BLOCKS
LRU_C = 8.0
CONV_W = 4
SSD_WIDTH = D_MIX - ATT_WIDTH - LRU_WIDTH
SSD_HEAD_DIM = 64
SSD_HEADS = SSD_WIDTH // SSD_HEAD_DIM
SSD_GROUPS = 2
SSD_STATE = 128
SSD_CONV_CH = SSD_WIDTH + 2 * SSD_GROUPS * SSD_STATE
DEEPNORM_ALPHA = (2.0 * DEPTH) ** 0.25
NORM_EPS = 1e-5

LANES = 128
SUBLANES = 8
HALF = LANES // 2
VMEM_LIMIT_BYTES = 60 * 1024 * 1024

Q_OFF = 0
K_OFF = Q_OFF + ATT_WIDTH
V_OFF = K_OFF + ATT_KV_WIDTH
GA_OFF = V_OFF + ATT_KV_WIDTH
XL_OFF = GA_OFF + ATT_WIDTH
GL_OFF = XL_OFF + LRU_WIDTH
Z_OFF = GL_OFF + LRU_WIDTH
XBC_OFF = Z_OFF + SSD_WIDTH
DT_OFF = XBC_OFF + SSD_CONV_CH
N_COLS = DT_OFF + LANES
W_IN_COLS = 2 * ATT_WIDTH + 2 * ATT_KV_WIDTH + 2 * LRU_WIDTH + SSD_WIDTH + SSD_CONV_CH + SSD_HEADS
B_OFF = XBC_OFF + SSD_WIDTH
C_OFF = B_OFF + SSD_GROUPS * SSD_STATE
MIX_LRU = ATT_WIDTH
MIX_SSD = ATT_WIDTH + LRU_WIDTH

N_LRU_TILES = LRU_WIDTH // LANES
N_SSD_TILES = SSD_WIDTH // LANES
N_CONV_TILES = SSD_CONV_CH // LANES
CONV_STRIDE = 4
TILES_PER_GROUP = N_SSD_TILES // SSD_GROUPS
GROUP_W = SSD_WIDTH // SSD_GROUPS
HEAD_ROWS = 16

NEG_BIG = -1e30

PROMPT_TILE = 256
SAMPLE_BT = 8
RING_SLOTS = 3


def _sigmoid(x):
    return 0.5 * jnp.tanh(0.5 * x) + 0.5


def _silu(x):
    h = 0.5 * x
    return h + h * jnp.tanh(h)


def _softplus(x):
    return jnp.maximum(x, 0.0) + jnp.log1p(jnp.exp(-jnp.abs(x)))


def _lane_lo(shape):
    return (lax.broadcasted_iota(jnp.int32, shape, len(shape) - 1) % LANES) < HALF


def _rope(t, tab):
    half = ROPE_DIM // 2
    return (t * tab[:, 0:LANES]
            + pltpu.roll(t, LANES - half, 1) * tab[:, LANES:2 * LANES]
            + pltpu.roll(t, half, 1) * tab[:, 2 * LANES:3 * LANES])


def _dot(a, b):
    return jnp.dot(a, b, preferred_element_type=F32)


def _dot_nt(a, b):
    return lax.dot_general(a, b, (((1,), (1,)), ((), ())), preferred_element_type=F32)


def _conv_classes(src_ref, j, rows, w_ref, bias, cs):
    n = CONV_STRIDE
    hist = CONV_W - 1
    taps = {s_: src_ref[j, pl.ds(SUBLANES + s_, rows // n, stride=n), :] for s_ in range(-hist, n)}
    outs = []
    for k in range(n):
        acc = bias + w_ref[hist:hist + 1, cs] * taps[k]
        for t in range(hist):
            acc = acc + w_ref[t:t + 1, cs] * taps[k - hist + t]
        outs.append(acc)
    return outs


def _layer_norm(v, g, b):
    mu = jnp.mean(v, -1, keepdims=True)
    d = v - mu
    var = jnp.mean(d * d, -1, keepdims=True)
    return d * lax.rsqrt(var + NORM_EPS) * g + b


def _lru_coeffs(xl, gates, lvec_ref):
    r = _sigmoid(gates[:, 0:LRU_WIDTH] + lvec_ref[1:2, :])
    ig = _sigmoid(gates[:, LRU_WIDTH:2 * LRU_WIDTH] + lvec_ref[2:3, :])
    log_a = (-LRU_C * _softplus(-lvec_ref[3:4, :])) * r
    a = jnp.exp(log_a)
    bt = jnp.sqrt(-jnp.tanh(log_a) * (1.0 + a * a)) * (ig * xl)
    return a, bt


def _project(x_tile, win_ref, dtw_ref, dst):
    xb = x_tile.astype(BF16)
    for lo, hi in ((Q_OFF, XL_OFF), (XL_OFF, Z_OFF), (Z_OFF, DT_OFF)):
        dst[:, lo:hi] = _dot(xb, win_ref[:, lo:hi])
    dst[:, DT_OFF:N_COLS] = _dot(xb, dtw_ref[...])


def _both_halves(t, lo):
    r = pltpu.roll(t, HALF, 1)
    return jnp.where(lo, t, r), jnp.where(lo, r, t)


def _prompt_kernel(sinks_ref, x_ref, rope_ref, win_ref, aux_ref, wg_ref, wout_ref, lcw_ref, lvec_ref,
                   scw_ref, scb_ref, hcol_ref, svec_ref, ln_ref,
                   y_ref, ko_ref, vo_ref, lco_ref, lho_ref, sco_ref, sho_ref,
                   p_scr, kd_scr, vd_scr, lx_scr, sx_scr, sa_scr, sb_scr, hl_scr, ht_scr, mix_scr, xlb_scr, xc_scr):
    T = PROMPT_TILE
    nsub = T // WINDOW
    c = pl.program_id(1)
    last = c == pl.num_programs(1) - 1

    @pl.when(c == 0)
    def _():
        kd_scr[:, 0:WINDOW, :] = jnp.zeros((2, WINDOW, LANES), BF16)
        vd_scr[:, 0:WINDOW, :] = jnp.zeros((2, WINDOW, LANES), BF16)
        lx_scr[:, 0:SUBLANES, :] = jnp.zeros((N_LRU_TILES, SUBLANES, LANES), F32)
        sx_scr[:, 0:SUBLANES, :] = jnp.zeros((N_CONV_TILES, SUBLANES, LANES), F32)
        hl_scr[...] = jnp.zeros_like(hl_scr)
        ht_scr[...] = jnp.zeros_like(ht_scr)

    _project(x_ref[...], win_ref, aux_ref, p_scr)

    lo_sq = _lane_lo((WINDOW, LANES))

    lo_t = _lane_lo((T, LANES))
    kds = _both_halves(_rope(p_scr[:, K_OFF:K_OFF + LANES], rope_ref[...]), lo_t)
    vds = _both_halves(p_scr[:, V_OFF:V_OFF + LANES], lo_t)
    for g in range(ATT_KV_HEADS):
        kd_scr[g, WINDOW:WINDOW + T, :] = kds[g].astype(BF16)
        vd_scr[g, WINDOW:WINDOW + T, :] = vds[g].astype(BF16)

    row = lax.broadcasted_iota(jnp.int32, (WINDOW, 2 * WINDOW), 0)
    col = lax.broadcasted_iota(jnp.int32, (WINDOW, 2 * WINDOW), 1)
    band = (col >= row) & (col <= row + WINDOW)
    first_lo = jnp.where(c > 0, 0, WINDOW)
    for i in range(nsub):
        r0 = i * WINDOW
        msk = (band & (col >= first_lo)) if i == 0 else band
        tab = rope_ref[r0:r0 + WINDOW, :]
        for cc in range(ATT_HEADS // 2):
            g = cc // (ATT_HEADS // ATT_KV_HEADS // 2)
            kg = kd_scr[g, r0:r0 + 2 * WINDOW, :]
            vg = vd_scr[g, r0:r0 + 2 * WINDOW, :]
            qp = _rope(p_scr[r0:r0 + WINDOW, Q_OFF + cc * LANES:Q_OFF + (cc + 1) * LANES], tab) * ATT_SCALE
            outs = []
            for half in range(2):
                sink = sinks_ref[2 * cc + half]
                qm = jnp.where(lo_sq if half == 0 else jnp.logical_not(lo_sq), qp, 0.0).astype(BF16)
                s = jnp.where(msk, _dot_nt(qm, kg), NEG_BIG)
                m = jnp.maximum(jnp.max(s, -1, keepdims=True), sink)
                e = jnp.exp(s - m)
                den = jnp.sum(e, -1, keepdims=True) + jnp.exp(sink - m)
                outs.append(_dot(e.astype(BF16), vg) * (1.0 / den))
            att = jnp.where(lo_sq, outs[0], outs[1])
            gate = p_scr[r0:r0 + WINDOW, GA_OFF + cc * LANES:GA_OFF + (cc + 1) * LANES]
            mix_scr[r0:r0 + WINDOW, cc * LANES:(cc + 1) * LANES] = (att * _silu(gate)).astype(BF16)
    for g in range(ATT_KV_HEADS):
        kd_scr[g, 0:WINDOW, :] = kd_scr[g, T:T + WINDOW, :]
        vd_scr[g, 0:WINDOW, :] = vd_scr[g, T:T + WINDOW, :]

    P = SUBLANES
    H = CONV_W - 1
    for j in range(N_LRU_TILES):
        cs_ = slice(j * LANES, (j + 1) * LANES)
        lx_scr[j, P:P + T, :] = p_scr[:, XL_OFF + j * LANES:XL_OFF + (j + 1) * LANES]
        for k, xl in enumerate(_conv_classes(lx_scr, j, T, lcw_ref, lvec_ref[0:1, cs_], cs_)):
            sb_scr[j, pl.ds(k, T // CONV_STRIDE, stride=CONV_STRIDE), :] = xl
        xlb_scr[:, cs_] = sb_scr[j].astype(BF16)
    lx_scr[:, 0:P, :] = lx_scr[:, T:T + P, :]

    G = T // SUBLANES
    row_g = lax.broadcasted_iota(jnp.int32, (G, LANES), 0)
    cl = -LRU_C * _softplus(-lvec_ref[3:4, :])
    for jj in range(N_LRU_TILES // 2):
        c0 = jj * 2 * LANES
        k0 = (c0 // LRU_BLOCK) * LRU_BLOCK // LANES * LANES
        k1 = -(-(-(-(c0 + 2 * LANES) // LRU_BLOCK) * LRU_BLOCK) // LANES) * LANES
        xk = xlb_scr[:, k0:k1]
        gr = _dot(xk, wg_ref[k0:k1, c0:c0 + 2 * LANES])
        gi = _dot(xk, wg_ref[k0:k1, LRU_WIDTH + c0:LRU_WIDTH + c0 + 2 * LANES])
        for u in range(2):
            j = 2 * jj + u
            cs_ = slice(j * LANES, (j + 1) * LANES)
            us = slice(u * LANES, (u + 1) * LANES)
            r = _sigmoid(gr[:, us] + lvec_ref[1:2, cs_])
            ig = _sigmoid(gi[:, us] + lvec_ref[2:3, cs_])
            log_a = cl[:, cs_] * r
            a = jnp.exp(log_a)
            sa_scr[j] = a
            sb_scr[j] = jnp.sqrt(-jnp.tanh(log_a) * (1.0 + a * a)) * (ig * sb_scr[j])
    for j in range(N_LRU_TILES):
        ca = sa_scr[j, pl.ds(0, G, stride=SUBLANES), :]
        cb = sb_scr[j, pl.ds(0, G, stride=SUBLANES), :]
        for k in range(1, SUBLANES):
            ak = sa_scr[j, pl.ds(k, G, stride=SUBLANES), :]
            cb = ak * cb + sb_scr[j, pl.ds(k, G, stride=SUBLANES), :]
            ca = ak * ca
            sa_scr[j, pl.ds(k, G, stride=SUBLANES), :] = ca
            sb_scr[j, pl.ds(k, G, stride=SUBLANES), :] = cb
        hin = hl_scr[0:1, j * LANES:(j + 1) * LANES]
        cb = cb + jnp.where(row_g == 0, ca * hin, 0.0)
        d = 1
        while d < G:
            ok = row_g >= d
            cb = cb + ca * jnp.where(ok, pltpu.roll(cb, d, 0), 0.0)
            ca = ca * jnp.where(ok, pltpu.roll(ca, d, 0), 1.0)
            d *= 2
        hprev = jnp.where(row_g == 0, hin, pltpu.roll(cb, 1, 0))
        for k in range(SUBLANES):
            sb_scr[j, pl.ds(k, G, stride=SUBLANES), :] = (
                sb_scr[j, pl.ds(k, G, stride=SUBLANES), :] + sa_scr[j, pl.ds(k, G, stride=SUBLANES), :] * hprev)
        hl_scr[0:1, j * LANES:(j + 1) * LANES] = cb[G - 1:G, :]
        gl = p_scr[:, GL_OFF + j * LANES:GL_OFF + (j + 1) * LANES]
        mix_scr[:, MIX_LRU + j * LANES:MIX_LRU + (j + 1) * LANES] = (sb_scr[j] * _silu(gl)).astype(BF16)

    for j in range(N_CONV_TILES):
        cs_ = slice(j * LANES, (j + 1) * LANES)
        sx_scr[j, P:P + T, :] = p_scr[:, XBC_OFF + j * LANES:XBC_OFF + (j + 1) * LANES]
        for k, xc in enumerate(_conv_classes(sx_scr, j, T, scw_ref, scb_ref[0:1, cs_], cs_)):
            xc_scr[j, pl.ds(k, T // CONV_STRIDE, stride=CONV_STRIDE), :] = _silu(xc)
    sx_scr[:, 0:P, :] = sx_scr[:, T:T + P, :]

    qi = lax.broadcasted_iota(jnp.int32, (WINDOW, WINDOW), 0)
    si = lax.broadcasted_iota(jnp.int32, (WINDOW, WINDOW), 1)
    causal = si <= qi
    upper = (qi <= si).astype(F32)
    hpad = jnp.zeros((WINDOW - HEAD_ROWS, WINDOW), F32)
    a_col = -jnp.exp(hcol_ref[1])
    for i in range(nsub):
        r0 = i * WINDOW
        dt_t = _softplus(p_scr[r0:r0 + WINDOW, DT_OFF:DT_OFF + LANES].T[0:HEAD_ROWS, :] + hcol_ref[0])
        cst = jnp.dot(dt_t * a_col, upper, precision=lax.Precision.HIGHEST, preferred_element_type=F32)
        dt_c = jnp.concatenate([dt_t, hpad], axis=0).T
        cs = jnp.concatenate([cst, hpad], axis=0).T
        ys = [None] * N_SSD_TILES
        for g in range(SSD_GROUPS):
            bg_t = xc_scr[N_SSD_TILES + g, r0:r0 + WINDOW, :].T.astype(BF16)
            cg = xc_scr[N_SSD_TILES + SSD_GROUPS + g, r0:r0 + WINDOW, :].astype(BF16)
            cbm = _dot(cg, bg_t)
            yo = _dot(cg, ht_scr[g].astype(BF16))
            xdec, edec = [], []
            for pp in range(TILES_PER_GROUP):
                p = g * TILES_PER_GROUP + pp
                bcs, mm, dts = [], [], []
                for h in (2 * p, 2 * p + 1):
                    bc = jnp.broadcast_to(cs[:, h:h + 1], (WINDOW, WINDOW))
                    lmat = jnp.exp(jnp.where(causal, bc - cst[h:h + 1, :], NEG_BIG))
                    mm.append((cbm * lmat).astype(BF16))
                    bcs.append(bc)
                    dts.append(jnp.broadcast_to(dt_c[:, h:h + 1], (WINDOW, LANES)))
                csl = jnp.where(lo_sq, bcs[0], bcs[1])
                ecs = jnp.exp(csl)
                dec = jnp.exp(csl[WINDOW - 1:WINDOW, :] - csl)
                xs_p = xc_scr[p, r0:r0 + WINDOW, :]
                xdt = xs_p * jnp.where(lo_sq, dts[0], dts[1])
                xdt_b = xdt.astype(BF16)
                y = jnp.where(lo_sq, _dot(mm[0], xdt_b), _dot(mm[1], xdt_b))
                y = y + yo[:, pp * LANES:(pp + 1) * LANES] * ecs
                y = y + svec_ref[0:1, p * LANES:(p + 1) * LANES] * xs_p
                ys[p] = y * _silu(p_scr[r0:r0 + WINDOW, Z_OFF + p * LANES:Z_OFF + (p + 1) * LANES])
                xdec.append((xdt * dec).astype(BF16))
                edec.append(ecs[WINDOW - 1:WINDOW, :])
            ht_scr[g] = (ht_scr[g] * jnp.concatenate(edec, axis=1)
                         + _dot(bg_t, jnp.concatenate(xdec, axis=1)))
        ss = jnp.sum(ys[0] * ys[0], -1, keepdims=True)
        for p in range(1, N_SSD_TILES):
            ss = ss + jnp.sum(ys[p] * ys[p], -1, keepdims=True)
        rinv = lax.rsqrt(ss * (1.0 / SSD_WIDTH) + NORM_EPS)
        for p in range(N_SSD_TILES):
            mix_scr[r0:r0 + WINDOW, MIX_SSD + p * LANES:MIX_SSD + (p + 1) * LANES] = (
                ys[p] * rinv * svec_ref[1:2, p * LANES:(p + 1) * LANES]).astype(BF16)

    out = _dot(mix_scr[...], wout_ref[...])
    y_ref[...] = _layer_norm(DEEPNORM_ALPHA * x_ref[...] + out, ln_ref[0:1, :], ln_ref[1:2, :])

    @pl.when(last)
    def _():
        ko_ref[...] = _rope(p_scr[T - WINDOW:T, K_OFF:K_OFF + LANES], rope_ref[T - WINDOW:T, :])
        vo_ref[...] = p_scr[T - WINDOW:T, V_OFF:V_OFF + LANES]
        for j in range(N_LRU_TILES):
            lco_ref[:, j * LANES:(j + 1) * LANES] = lx_scr[j, P + T - (CONV_W - 1):P + T, :]
        lho_ref[...] = hl_scr[...]
        for j in range(N_CONV_TILES):
            sco_ref[:, j * LANES:(j + 1) * LANES] = sx_scr[j, P + T - (CONV_W - 1):P + T, :]
        for g in range(SSD_GROUPS):
            for pp in range(TILES_PER_GROUP):
                tile = ht_scr[g, :, pp * LANES:(pp + 1) * LANES].T
                h = 2 * (g * TILES_PER_GROUP + pp)
                sho_ref[h] = tile[0:SSD_HEAD_DIM, :]
                sho_ref[h + 1] = tile[SSD_HEAD_DIM:2 * SSD_HEAD_DIM, :]


def _const_spec(shape, layer):
    nd = len(shape)
    return pl.BlockSpec((None,) + tuple(shape), lambda *_: (layer,) + (0,) * nd,
                        pipeline_mode=pl.Buffered(1))


def _prompt_layer(layer, x, rope_tab, sinks, prm, prev):
    B, L, _ = x.shape
    T = PROMPT_TILE
    nc = L // T
    win, aux, wg, wout, lcw, lvec, scw, scb, hvec, svec, ln, hcol = prm
    tile = lambda b, c: (b, c, 0)

    in_specs = [
        pl.BlockSpec(memory_space=pltpu.SMEM),
        pl.BlockSpec((None, T, D_MODEL), tile),
        pl.BlockSpec((T, 3 * LANES), lambda b, c: (c, 0)),
        _const_spec((D_MODEL, W_IN_COLS), layer),
        _const_spec((D_MODEL, LANES), layer),
        _const_spec((LRU_WIDTH, 2 * LRU_WIDTH), layer),
        _const_spec((D_MIX, D_MODEL), layer),
        _const_spec((CONV_W, LRU_WIDTH), layer),
        _const_spec((4, LRU_WIDTH), layer),
        _const_spec((CONV_W, SSD_CONV_CH), layer),
        _const_spec((1, SSD_CONV_CH), layer),
        _const_spec((2, HEAD_ROWS, LANES), layer),
        _const_spec((2, SSD_WIDTH), layer),
        _const_spec((2, D_MODEL), layer),
    ]
    state_shapes = (
        (WINDOW, ATT_KV_WIDTH),
        (WINDOW, ATT_KV_WIDTH),
        (CONV_W - 1, LRU_WIDTH),
        (1, LRU_WIDTH),
        (CONV_W - 1, SSD_CONV_CH),
        (SSD_HEADS, SSD_HEAD_DIM, SSD_STATE),
    )
    out_shape = (jax.ShapeDtypeStruct((B, L, D_MODEL), F32),) + tuple(
        jax.ShapeDtypeStruct((DEPTH, B) + s, F32) for s in state_shapes)
    per_lb = lambda s: pl.BlockSpec((None, None) + s, lambda b, c: (layer, b) + (0,) * len(s))
    out_specs = (pl.BlockSpec((None, T, D_MODEL), tile),) + tuple(per_lb(s) for s in state_shapes)
    operands = [sinks, x, rope_tab, win, aux, wg, wout, lcw, lvec, scw, scb, hcol, svec, ln]
    n_in = len(operands)
    aliases = {}
    if prev is not None:
        for k, arr in enumerate(prev):
            aliases[len(operands)] = 1 + k
            operands.append(arr)
            in_specs.append(pl.BlockSpec(memory_space=pl.ANY))

    def body(*refs):
        _prompt_kernel(*refs[:n_in], *refs[n_in + len(aliases):])

    scratch = [
        pltpu.VMEM((T, N_COLS), F32),
        pltpu.VMEM((ATT_KV_HEADS, WINDOW + T, LANES), BF16),
        pltpu.VMEM((ATT_KV_HEADS, WINDOW + T, LANES), BF16),
        pltpu.VMEM((N_LRU_TILES, SUBLANES + T, LANES), F32),
        pltpu.VMEM((N_CONV_TILES, SUBLANES + T, LANES), F32),
        pltpu.VMEM((N_LRU_TILES, T, LANES), F32),
        pltpu.VMEM((N_LRU_TILES, T, LANES), F32),
        pltpu.VMEM((1, LRU_WIDTH), F32),
        pltpu.VMEM((SSD_GROUPS, SSD_STATE, GROUP_W), F32),
        pltpu.VMEM((T, D_MIX), BF16),
        pltpu.VMEM((T, LRU_WIDTH), BF16),
        pltpu.VMEM((N_CONV_TILES, T, LANES), F32),
    ]
    return pl.pallas_call(
        body,
        grid=(B, nc),
        in_specs=in_specs,
        out_specs=out_specs,
        out_shape=out_shape,
        scratch_shapes=scratch,
        input_output_aliases=aliases,
        compiler_params=pltpu.CompilerParams(
            dimension_semantics=("arbitrary", "arbitrary"),
            vmem_limit_bytes=VMEM_LIMIT_BYTES),
        name=f"prompt_layer{layer}",
    )(*operands)


def _conv_step(x_new, st_ref, sto_ref, w_ref, bias):
    y = bias + w_ref[CONV_W - 1:CONV_W, :] * x_new
    for t in range(CONV_W - 1):
        y = y + w_ref[t:t + 1, :] * st_ref[t]
    for t in range(CONV_W - 2):
        sto_ref[t] = st_ref[t + 1]
    sto_ref[CONV_W - 2] = x_new
    return y


def _sample_kernel(layer, n_aliased, sinks_ref, x_ref, rope_ref, win_ref, aux_ref, wg_ref, wout_ref, lcw_ref,
                   lvec_ref, scw_ref, scb_ref, hvec_ref, svec_ref, ln_ref,
                   ck_hbm, cv_hbm, lc_ref, lh_ref, sc_ref, sh_hbm, *refs):
    (y_ref, cko_ref, cvo_ref, lco_ref, lho_ref, sco_ref, sho_ref,
     p_scr, q_scr, o_scr, kn_scr, vn_scr, xdt_scr, da_scr, y_scr, mix_scr,
     ck_ring, cv_ring, sh_ring, ring_sem) = refs[n_aliased:]
    NB = x_ref.shape[0]
    BT = SAMPLE_BT
    i = pl.program_id(0)
    nsteps = pl.num_programs(0)
    lo_f = _lane_lo((NB, LANES))

    def fetch(j):
        slot = lax.rem(j, RING_SLOTS)
        rows = pl.ds(pl.multiple_of(j * BT, BT), BT)
        return (pltpu.make_async_copy(ck_hbm.at[layer, rows], ck_ring.at[slot], ring_sem.at[0, slot]),
                pltpu.make_async_copy(cv_hbm.at[layer, rows], cv_ring.at[slot], ring_sem.at[1, slot]),
                pltpu.make_async_copy(sh_hbm.at[layer, rows], sh_ring.at[slot], ring_sem.at[2, slot]))

    @pl.when(i == 0)
    def _():
        for j in range(RING_SLOTS - 1):
            for cp in fetch(j):
                cp.start()

    @pl.when(i + RING_SLOTS - 1 < nsteps)
    def _():
        for cp in fetch(i + RING_SLOTS - 1):
            cp.start()

    @pl.when(i == 0)
    def _():
        _project(x_ref[...], win_ref, aux_ref, p_scr)
        tab = rope_ref[...]
        for cc in range(ATT_HEADS // 2):
            g = cc // (ATT_HEADS // ATT_KV_HEADS // 2)
            qp = _rope(p_scr[:, Q_OFF + cc * LANES:Q_OFF + (cc + 1) * LANES], tab) * ATT_SCALE
            for half in range(2):
                t = jnp.where(lo_f if half == 0 else jnp.logical_not(lo_f), qp, 0.0)
                if half != g:
                    t = pltpu.roll(t, HALF, 1)
                q_scr[pl.ds(2 * cc + half, NB, stride=ATT_HEADS), :] = t
        kn_scr[...] = _rope(p_scr[:, K_OFF:K_OFF + LANES], tab)
        vn_scr[...] = p_scr[:, V_OFF:V_OFF + LANES]
        xin = p_scr[:, XL_OFF:XL_OFF + LRU_WIDTH]
        xl = _conv_step(xin, lc_ref, lco_ref, lcw_ref, lvec_ref[0:1, :])
        gates = _dot(xl.astype(BF16), wg_ref[...])
        a, bt = _lru_coeffs(xl, gates, lvec_ref)
        h1 = a * lh_ref[...] + bt
        lho_ref[...] = h1
        mix_scr[:, MIX_LRU:MIX_LRU + LRU_WIDTH] = (h1 * _silu(p_scr[:, GL_OFF:GL_OFF + LRU_WIDTH])).astype(BF16)
        xin2 = p_scr[:, XBC_OFF:XBC_OFF + SSD_CONV_CH]
        xc = _conv_step(xin2, sc_ref, sco_ref, scw_ref, scb_ref[0:1, :])
        p_scr[:, XBC_OFF:XBC_OFF + SSD_CONV_CH] = _silu(xc)
        dt = _softplus(p_scr[:, DT_OFF:DT_OFF + LANES] + hvec_ref[0:1, :])
        da_scr[...] = jnp.exp(dt * (-jnp.exp(hvec_ref[1:2, :])))
        for p in range(N_SSD_TILES):
            dte = jnp.where(lo_f, jnp.broadcast_to(dt[:, 2 * p:2 * p + 1], (NB, LANES)),
                            jnp.broadcast_to(dt[:, 2 * p + 1:2 * p + 2], (NB, LANES)))
            xdt_scr[:, p * LANES:(p + 1) * LANES] = p_scr[:, XBC_OFF + p * LANES:XBC_OFF + (p + 1) * LANES] * dte

    for cp in fetch(i):
        cp.wait()
    slot = lax.rem(i, RING_SLOTS)
    ck_ref, cv_ref, sh_ref = ck_ring.at[slot], cv_ring.at[slot], sh_ring.at[slot]
    r8 = pl.multiple_of(i * BT, BT)
    kn_blk = kn_scr[pl.ds(r8, BT), :]
    vn_blk = vn_scr[pl.ds(r8, BT), :]
    zpad = jnp.zeros((LANES - BT, LANES), F32)
    kn_t = jnp.concatenate([kn_blk, zpad], axis=0).T
    vn_t = jnp.concatenate([vn_blk, zpad], axis=0).T
    newest = lax.broadcasted_iota(jnp.int32, (ATT_KV_WIDTH, WINDOW), 1) == WINDOW - 1
    da_t = jnp.concatenate([da_scr[pl.ds(r8, BT), :], zpad], axis=0).T
    x_t = [jnp.concatenate([xdt_scr[pl.ds(r8, BT), p * LANES:(p + 1) * LANES], zpad], axis=0).T
           for p in range(N_SSD_TILES)]
    b_blk = p_scr[pl.ds(r8, BT), B_OFF:B_OFF + SSD_GROUPS * SSD_STATE]
    c_blk = p_scr[pl.ds(r8, BT), C_OFF:C_OFF + SSD_GROUPS * SSD_STATE].astype(BF16)
    row8 = lax.broadcasted_iota(jnp.int32, (ATT_HEADS, 1), 0)
    sink = jnp.zeros((ATT_HEADS, 1), F32)
    for h in range(ATT_HEADS):
        sink = jnp.where(row8 == h, sinks_ref[h], sink)
    rowb = lax.broadcasted_iota(jnp.int32, (BT, GROUP_W), 0)
    y_acc = [jnp.zeros((BT, GROUP_W), F32) for _ in range(SSD_GROUPS)]
    rowk = lax.broadcasted_iota(jnp.int32, (BT, SSD_STATE), 0)
    outer = []
    for g in range(SSD_GROUPS):
        bg = b_blk[:, g * SSD_STATE:(g + 1) * SSD_STATE]
        diag = jnp.concatenate([jnp.where(rowk == bb, bg, 0.0) for bb in range(BT)], axis=1)
        rhs = jnp.concatenate([diag, jnp.zeros((LANES - BT, BT * SSD_STATE), F32)], axis=0).astype(BF16)
        outer.append([_dot(x_t[g * TILES_PER_GROUP + pp].astype(BF16), rhs) for pp in range(TILES_PER_GROUP)])
    for bb in range(BT):
        qr = q_scr[pl.ds(pl.multiple_of((r8 + bb) * ATT_HEADS, ATT_HEADS), ATT_HEADS), :]
        kt = ck_ref[bb]
        vt = cv_ref[bb]
        s = _dot(qr.astype(BF16), kt.astype(BF16))
        s_new = jnp.sum(qr * kn_blk[bb:bb + 1, :], -1, keepdims=True)
        m = jnp.maximum(jnp.maximum(jnp.max(s, -1, keepdims=True), s_new), sink)
        e = jnp.exp(s - m)
        e_new = jnp.exp(s_new - m)
        den = jnp.sum(e, -1, keepdims=True) + e_new + jnp.exp(sink - m)
        o = _dot_nt(e.astype(BF16), vt.astype(BF16)) + e_new * vn_blk[bb:bb + 1, :]
        o_scr[pl.ds(pl.multiple_of((r8 + bb) * ATT_HEADS, ATT_HEADS), ATT_HEADS), :] = o * (1.0 / den)
        cko_ref[bb] = jnp.where(newest, kn_t[:, bb:bb + 1], pltpu.roll(kt, WINDOW - 1, 1))
        cvo_ref[bb] = jnp.where(newest, vn_t[:, bb:bb + 1], pltpu.roll(vt, WINDOW - 1, 1))
        for g in range(SSD_GROUPS):
            tiles = []
            for pp in range(TILES_PER_GROUP):
                p = g * TILES_PER_GROUP + pp
                for hh in range(2):
                    h = 2 * p + hh
                    dab = jnp.broadcast_to(da_t[h:h + 1, bb:bb + 1], (SSD_HEAD_DIM, SSD_STATE))
                    h1 = sh_ref[bb, h] * dab + outer[g][pp][hh * SSD_HEAD_DIM:(hh + 1) * SSD_HEAD_DIM,
                                                               bb * SSD_STATE:(bb + 1) * SSD_STATE]
                    sho_ref[bb, h] = h1
                    tiles.append(h1.astype(BF16))
            res = _dot_nt(c_blk[:, g * SSD_STATE:(g + 1) * SSD_STATE], jnp.concatenate(tiles, axis=0))
            y_acc[g] = jnp.where(rowb == bb, res, y_acc[g])
    for g in range(SSD_GROUPS):
        y_scr[pl.ds(r8, BT), g * GROUP_W:(g + 1) * GROUP_W] = y_acc[g]

    @pl.when(i == pl.num_programs(0) - 1)
    def _():
        for cc in range(ATT_HEADS // 2):
            g = cc // (ATT_HEADS // ATT_KV_HEADS // 2)
            oe = o_scr[pl.ds(2 * cc, NB, stride=ATT_HEADS), :]
            oo = o_scr[pl.ds(2 * cc + 1, NB, stride=ATT_HEADS), :]
            if g == 0:
                oo = pltpu.roll(oo, HALF, 1)
            else:
                oe = pltpu.roll(oe, HALF, 1)
            gate = p_scr[:, GA_OFF + cc * LANES:GA_OFF + (cc + 1) * LANES]
            mix_scr[:, cc * LANES:(cc + 1) * LANES] = (jnp.where(lo_f, oe, oo) * _silu(gate)).astype(BF16)
        y = (y_scr[...] + svec_ref[0:1, :] * p_scr[:, XBC_OFF:XBC_OFF + SSD_WIDTH]) * _silu(p_scr[:, Z_OFF:Z_OFF + SSD_WIDTH])
        y = y * lax.rsqrt(jnp.mean(y * y, -1, keepdims=True) + NORM_EPS) * svec_ref[1:2, :]
        mix_scr[:, MIX_SSD:MIX_SSD + SSD_WIDTH] = y.astype(BF16)
        out = _dot(mix_scr[...], wout_ref[...])
        y_ref[...] = _layer_norm(DEEPNORM_ALPHA * x_ref[...] + out, ln_ref[0:1, :], ln_ref[1:2, :])


def _sample_layer(layer, x, rope_tab, sinks, prm, ck, cv, lc, lh, sc, sh, prev):
    NB = x.shape[0]
    BT = SAMPLE_BT
    H = CONV_W - 1
    win, aux, wg, wout, lcw, lvec, scw, scb, hvec, svec, ln, _ = prm
    whole = lambda shape: pl.BlockSpec(shape, lambda i: (0,) * len(shape), pipeline_mode=pl.Buffered(1))
    lwhole = lambda shape: pl.BlockSpec((None,) + shape, lambda i: (layer,) + (0,) * len(shape),
                                        pipeline_mode=pl.Buffered(1))
    rows = lambda shape: pl.BlockSpec((None, BT) + shape, lambda i: (layer, i) + (0,) * len(shape))
    hbm = pl.BlockSpec(memory_space=pl.ANY)
    state_specs = [
        hbm,
        hbm,
        lwhole((H, NB, LRU_WIDTH)),
        lwhole((NB, LRU_WIDTH)),
        lwhole((H, NB, SSD_CONV_CH)),
        hbm,
    ]
    in_specs = [
        pl.BlockSpec(memory_space=pltpu.SMEM),
        whole((NB, D_MODEL)),
        whole((1, 3 * LANES)),
        _const_spec((D_MODEL, W_IN_COLS), layer),
        _const_spec((D_MODEL, LANES), layer),
        _const_spec((LRU_WIDTH, 2 * LRU_WIDTH), layer),
        _const_spec((D_MIX, D_MODEL), layer),
        _const_spec((CONV_W, LRU_WIDTH), layer),
        _const_spec((4, LRU_WIDTH), layer),
        _const_spec((CONV_W, SSD_CONV_CH), layer),
        _const_spec((1, SSD_CONV_CH), layer),
        _const_spec((2, LANES), layer),
        _const_spec((2, SSD_WIDTH), layer),
        _const_spec((2, D_MODEL), layer),
    ] + state_specs
    operands = [sinks, x, rope_tab, win, aux, wg, wout, lcw, lvec, scw, scb, hvec, svec, ln, ck, cv, lc, lh, sc, sh]
    aliases = {}
    if prev is not None:
        for k, arr in enumerate(prev):
            aliases[len(operands)] = 1 + k
            operands.append(arr)
            in_specs.append(pl.BlockSpec(memory_space=pl.ANY))
    out_shape = (jax.ShapeDtypeStruct((NB, D_MODEL), F32),) + tuple(
        jax.ShapeDtypeStruct(a.shape, F32) for a in (ck, cv, lc, lh, sc, sh))
    full = lambda shape: pl.BlockSpec(shape, lambda i: (0,) * len(shape))
    lfull = lambda shape: pl.BlockSpec((None,) + shape, lambda i: (layer,) + (0,) * len(shape))
    out_specs = (
        full((NB, D_MODEL)),
        rows((ATT_KV_WIDTH, WINDOW)),
        rows((ATT_KV_WIDTH, WINDOW)),
        lfull((H, NB, LRU_WIDTH)),
        lfull((NB, LRU_WIDTH)),
        lfull((H, NB, SSD_CONV_CH)),
        rows((SSD_HEADS, SSD_HEAD_DIM, SSD_STATE)),
    )
    scratch = [
        pltpu.VMEM((NB, N_COLS), F32),
        pltpu.VMEM((NB * ATT_HEADS, LANES), F32),
        pltpu.VMEM((NB * ATT_HEADS, LANES), F32),
        pltpu.VMEM((NB, ATT_KV_WIDTH), F32),
        pltpu.VMEM((NB, ATT_KV_WIDTH), F32),
        pltpu.VMEM((NB, SSD_WIDTH), F32),
        pltpu.VMEM((NB, LANES), F32),
        pltpu.VMEM((NB, SSD_WIDTH), F32),
        pltpu.VMEM((NB, D_MIX), BF16),
        pltpu.VMEM((RING_SLOTS, BT, ATT_KV_WIDTH, WINDOW), F32),
        pltpu.VMEM((RING_SLOTS, BT, ATT_KV_WIDTH, WINDOW), F32),
        pltpu.VMEM((RING_SLOTS, BT, SSD_HEADS, SSD_HEAD_DIM, SSD_STATE), F32),
        pltpu.SemaphoreType.DMA((3, RING_SLOTS)),
    ]
    assert NB // BT >= RING_SLOTS - 1
    return pl.pallas_call(
        functools.partial(_sample_kernel, layer, len(aliases)),
        grid=(NB // BT,),
        in_specs=in_specs,
        out_specs=out_specs,
        out_shape=out_shape,
        scratch_shapes=scratch,
        input_output_aliases=aliases,
        compiler_params=pltpu.CompilerParams(
            dimension_semantics=("arbitrary",),
            vmem_limit_bytes=VMEM_LIMIT_BYTES),
        name=f"sample_layer{layer}",
    )(*operands)


def _rope_table(start, length):
    half = ROPE_DIM // 2
    inv = (np.float32(ROPE_THETA) ** (-np.arange(half, dtype=np.float32) / np.float32(half))).astype(np.float32)
    ang = (np.arange(start, start + length).astype(np.float32)[:, None] * inv[None, :]).astype(np.float32)
    cos, sin = np.cos(ang), np.sin(ang)
    d = np.arange(LANES) % ATT_HEAD_DIM
    fi = d % half
    c = np.where(d < ROPE_DIM, cos[:, fi], np.float32(1.0))
    s1 = np.where(d < half, -sin[:, fi], np.float32(0.0))
    s2 = np.where((d >= half) & (d < ROPE_DIM), sin[:, fi], np.float32(0.0))
    return jnp.asarray(np.concatenate([c, s1, s2], axis=1).astype(np.float32))


def _prep_params(w_in, w_out, lru_conv_w, lru_conv_b, lru_wa, lru_ba, lru_wx, lru_bx, lru_lambda,
                 ssd_conv_w, ssd_conv_b, ssd_dt_bias, ssd_a_log, ssd_d, ssd_norm_g, ln_g, ln_b):
    win = w_in.astype(BF16)
    aux = jnp.pad(w_in[..., DT_OFF:], ((0, 0), (0, 0), (0, LANES - SSD_HEADS))).astype(BF16)

    def dense(w):
        rows = [jnp.pad(w[:, n], ((0, 0), (0, 0), (n * LRU_BLOCK, LRU_WIDTH - (n + 1) * LRU_BLOCK)))
                for n in range(LRU_BLOCKS)]
        return jnp.concatenate(rows, 1)

    wg = jnp.concatenate([dense(lru_wa.astype(BF16)), dense(lru_wx.astype(BF16))], -1)
    wout = w_out.astype(BF16)
    lvec = jnp.stack([lru_conv_b, lru_ba, lru_bx, lru_lambda], 1)
    hpad = lambda t: jnp.pad(t, ((0, 0), (0, LANES - SSD_HEADS)))
    hvec = jnp.stack([hpad(ssd_dt_bias), hpad(ssd_a_log)], 1)
    rpad = lambda t: jnp.broadcast_to(jnp.pad(t, ((0, 0), (0, HEAD_ROWS - SSD_HEADS)))[:, :, None],
                                      (DEPTH, HEAD_ROWS, LANES))
    hcol = jnp.stack([rpad(ssd_dt_bias), rpad(ssd_a_log)], 1)
    svec = jnp.stack([jnp.repeat(ssd_d, SSD_HEAD_DIM, axis=1), ssd_norm_g], 1)
    ln = jnp.stack([ln_g, ln_b], 1)
    return (win, aux, wg, wout, lru_conv_w, lvec, ssd_conv_w, ssd_conv_b[:, None, :], hvec, svec, ln, hcol)


def kernel(x_prompt, x_sample, cache_swa_k, cache_swa_v, state_lru_conv, state_lru_h, state_ssd_conv, state_ssd_h, w_in, w_out, att_sinks, lru_conv_w, lru_conv_b, lru_wa, lru_ba, lru_wx, lru_bx, lru_lambda, ssd_conv_w, ssd_conv_b, ssd_dt_bias, ssd_a_log, ssd_d, ssd_norm_g, ln_g, ln_b):
    prm = _prep_params(w_in, w_out, lru_conv_w, lru_conv_b, lru_wa, lru_ba, lru_wx, lru_bx, lru_lambda,
                       ssd_conv_w, ssd_conv_b, ssd_dt_bias, ssd_a_log, ssd_d, ssd_norm_g, ln_g, ln_b)
    bp, lp, _ = x_prompt.shape
    rope_p = _rope_table(0, lp)
    xp = x_prompt
    new_p = None
    for l in range(DEPTH):
        xp, *new_p = _prompt_layer(l, xp, rope_p, att_sinks[l], prm, new_p)
    ko, vo, lco, lho, sco, sho = new_p
    outs_p = (ko.reshape(DEPTH, bp, WINDOW, ATT_KV_HEADS, ATT_HEAD_DIM),
              vo.reshape(DEPTH, bp, WINDOW, ATT_KV_HEADS, ATT_HEAD_DIM),
              lco, lho.reshape(DEPTH, bp, LRU_WIDTH), sco, sho)

    nb = x_sample.shape[0]
    rope_s = _rope_table(PAST_LEN, x_sample.shape[1])
    to_kt = lambda t: jnp.transpose(t, (0, 1, 3, 4, 2)).reshape(DEPTH, nb, ATT_KV_WIDTH, WINDOW)
    from_kt = lambda t: jnp.transpose(t.reshape(DEPTH, nb, ATT_KV_HEADS, ATT_HEAD_DIM, WINDOW), (0, 1, 4, 2, 3))
    swap = lambda t: jnp.transpose(t, (0, 2, 1, 3))
    state = (to_kt(cache_swa_k), to_kt(cache_swa_v), swap(state_lru_conv), state_lru_h, swap(state_ssd_conv),
             state_ssd_h)
    xs = x_sample.reshape(nb, D_MODEL)
    new = None
    for l in range(DEPTH):
        xs, *new = _sample_layer(l, xs, rope_s, att_sinks[l], prm, *state, new)
    cko, cvo, lco, lho, sco, sho = new
    outs_s = (from_kt(cko), from_kt(cvo), swap(lco), lho, swap(sco), sho)
    return (xp, xs.reshape(x_sample.shape)) + tuple(outs_p) + tuple(outs_s)
```

```python
import functools

import jax
import jax.numpy as jnp
import numpy as np
from jax import lax
from jax.experimental import pallas as pl
from jax.experimental.pallas import tpu as pltpu

F32 = jnp.float32
BF16 = jnp.bfloat16

D_MODEL = 1024
DEPTH = 4
PAST_LEN = 8192
D_MIX = 2 * D_MODEL
ATT_HEADS = 8
ATT_KV_HEADS = 2
ATT_HEAD_DIM = 64
ATT_WIDTH = ATT_HEADS * ATT_HEAD_DIM
ATT_KV_WIDTH = ATT_KV_HEADS * ATT_HEAD_DIM
ATT_SCALE = ATT_HEAD_DIM ** -0.5
WINDOW = 128
ROPE_THETA = 500000.0
ROPE_DIM = ATT_HEAD_DIM // 4
LRU_WIDTH = 3 * D_MIX // 8
LRU_BLOCKS = 8
LRU_BLOCK = LRU_WIDTH // LRU_BLOCKS
LRU_C = 8.0
CONV_W = 4
SSD_WIDTH = D_MIX - ATT_WIDTH - LRU_WIDTH
SSD_HEAD_DIM = 64
SSD_HEADS = SSD_WIDTH // SSD_HEAD_DIM
SSD_GROUPS = 2
SSD_STATE = 128
SSD_CONV_CH = SSD_WIDTH + 2 * SSD_GROUPS * SSD_STATE
DEEPNORM_ALPHA = (2.0 * DEPTH) ** 0.25
NORM_EPS = 1e-5

LANES = 128
SUBLANES = 8
HALF = LANES // 2
VMEM_LIMIT_BYTES = 60 * 1024 * 1024

Q_OFF = 0
K_OFF = Q_OFF + ATT_WIDTH
V_OFF = K_OFF + ATT_KV_WIDTH
GA_OFF = V_OFF + ATT_KV_WIDTH
XL_OFF = GA_OFF + ATT_WIDTH
GL_OFF = XL_OFF + LRU_WIDTH
Z_OFF = GL_OFF + LRU_WIDTH
XBC_OFF = Z_OFF + SSD_WIDTH
DT_OFF = XBC_OFF + SSD_CONV_CH
N_COLS = DT_OFF + LANES
W_IN_COLS = 2 * ATT_WIDTH + 2 * ATT_KV_WIDTH + 2 * LRU_WIDTH + SSD_WIDTH + SSD_CONV_CH + SSD_HEADS
B_OFF = XBC_OFF + SSD_WIDTH
C_OFF = B_OFF + SSD_GROUPS * SSD_STATE
MIX_LRU = ATT_WIDTH
MIX_SSD = ATT_WIDTH + LRU_WIDTH

N_LRU_TILES = LRU_WIDTH // LANES
N_SSD_TILES = SSD_WIDTH // LANES
N_CONV_TILES = SSD_CONV_CH // LANES
CONV_STRIDE = 4
TILES_PER_GROUP = N_SSD_TILES // SSD_GROUPS
GROUP_W = SSD_WIDTH // SSD_GROUPS
HEAD_ROWS = 16

NEG_BIG = -1e30

PROMPT_TILE = 256
SAMPLE_BT = 8
RING_SLOTS = 3
RING_PRIORITY = 1


def _sigmoid(x):
    return 0.5 * jnp.tanh(0.5 * x) + 0.5


def _silu(x):
    h = 0.5 * x
    return h + h * jnp.tanh(h)


def _softplus(x):
    return jnp.maximum(x, 0.0) + jnp.log1p(jnp.exp(-jnp.abs(x)))


def _lane_lo(shape):
    return (lax.broadcasted_iota(jnp.int32, shape, len(shape) - 1) % LANES) < HALF


def _rope(t, tab):
    half = ROPE_DIM // 2
    return (t * tab[:, 0:LANES]
            + pltpu.roll(t, LANES - half, 1) * tab[:, LANES:2 * LANES]
            + pltpu.roll(t, half, 1) * tab[:, 2 * LANES:3 * LANES])


def _dot(a, b):
    return jnp.dot(a, b, preferred_element_type=F32)


def _dot_nt(a, b):
    return lax.dot_general(a, b, (((1,), (1,)), ((), ())), preferred_element_type=F32)


def _conv_classes(src_ref, j, rows, w_ref, bias, cs):
    n = CONV_STRIDE
    hist = CONV_W - 1
    taps = {s_: src_ref[j, pl.ds(SUBLANES + s_, rows // n, stride=n), :] for s_ in range(-hist, n)}
    outs = []
    for k in range(n):
        acc = bias + w_ref[hist:hist + 1, cs] * taps[k]
        for t in range(hist):
            acc = acc + w_ref[t:t + 1, cs] * taps[k - hist + t]
        outs.append(acc)
    return outs


def _layer_norm(v, g, b):
    mu = jnp.mean(v, -1, keepdims=True)
    d = v - mu
    var = jnp.mean(d * d, -1, keepdims=True)
    return d * lax.rsqrt(var + NORM_EPS) * g + b


def _lru_coeffs(xl, gates, lvec_ref):
    r = _sigmoid(gates[:, 0:LRU_WIDTH] + lvec_ref[1:2, :])
    ig = _sigmoid(gates[:, LRU_WIDTH:2 * LRU_WIDTH] + lvec_ref[2:3, :])
    log_a = (-LRU_C * _softplus(-lvec_ref[3:4, :])) * r
    a = jnp.exp(log_a)
    bt = jnp.sqrt(-jnp.tanh(log_a) * (1.0 + a * a)) * (ig * xl)
    return a, bt


def _project(x_tile, win_ref, dtw_ref, dst):
    xb = x_tile.astype(BF16)
    for lo, hi in ((Q_OFF, XL_OFF), (XL_OFF, Z_OFF), (Z_OFF, DT_OFF)):
        dst[:, lo:hi] = _dot(xb, win_ref[:, lo:hi])
    dst[:, DT_OFF:N_COLS] = _dot(xb, dtw_ref[...])


def _both_halves(t, lo):
    r = pltpu.roll(t, HALF, 1)
    return jnp.where(lo, t, r), jnp.where(lo, r, t)


def _prompt_kernel(sinks_ref, x_ref, rope_ref, win_ref, aux_ref, wg_ref, wout_ref, lcw_ref, lvec_ref,
                   scw_ref, scb_ref, hcol_ref, svec_ref, ln_ref,
                   y_ref, ko_ref, vo_ref, lco_ref, lho_ref, sco_ref, sho_ref,
                   p_scr, kd_scr, vd_scr, lx_scr, sx_scr, sa_scr, sb_scr, hl_scr, ht_scr, mix_scr, xlb_scr, xc_scr):
    T = PROMPT_TILE
    nsub = T // WINDOW
    c = pl.program_id(1)
    last = c == pl.num_programs(1) - 1

    @pl.when(c == 0)
    def _():
        kd_scr[:, 0:WINDOW, :] = jnp.zeros((2, WINDOW, LANES), BF16)
        vd_scr[:, 0:WINDOW, :] = jnp.zeros((2, WINDOW, LANES), BF16)
        lx_scr[:, 0:SUBLANES, :] = jnp.zeros((N_LRU_TILES, SUBLANES, LANES), F32)
        sx_scr[:, 0:SUBLANES, :] = jnp.zeros((N_CONV_TILES, SUBLANES, LANES), F32)
        hl_scr[...] = jnp.zeros_like(hl_scr)
        ht_scr[...] = jnp.zeros_like(ht_scr)

    _project(x_ref[...], win_ref, aux_ref, p_scr)

    lo_sq = _lane_lo((WINDOW, LANES))

    lo_t = _lane_lo((T, LANES))
    kds = _both_halves(_rope(p_scr[:, K_OFF:K_OFF + LANES], rope_ref[...]), lo_t)
    vds = _both_halves(p_scr[:, V_OFF:V_OFF + LANES], lo_t)
    for g in range(ATT_KV_HEADS):
        kd_scr[g, WINDOW:WINDOW + T, :] = kds[g].astype(BF16)
        vd_scr[g, WINDOW:WINDOW + T, :] = vds[g].astype(BF16)

    row = lax.broadcasted_iota(jnp.int32, (WINDOW, 2 * WINDOW), 0)
    col = lax.broadcasted_iota(jnp.int32, (WINDOW, 2 * WINDOW), 1)
    band = (col >= row) & (col <= row + WINDOW)
    first_lo = jnp.where(c > 0, 0, WINDOW)
    for i in range(nsub):
        r0 = i * WINDOW
        msk = (band & (col >= first_lo)) if i == 0 else band
        tab = rope_ref[r0:r0 + WINDOW, :]
        for cc in range(ATT_HEADS // 2):
            g = cc // (ATT_HEADS // ATT_KV_HEADS // 2)
            kg = kd_scr[g, r0:r0 + 2 * WINDOW, :]
            vg = vd_scr[g, r0:r0 + 2 * WINDOW, :]
            qp = _rope(p_scr[r0:r0 + WINDOW, Q_OFF + cc * LANES:Q_OFF + (cc + 1) * LANES], tab) * ATT_SCALE
            outs = []
            for half in range(2):
                sink = sinks_ref[2 * cc + half]
                qm = jnp.where(lo_sq if half == 0 else jnp.logical_not(lo_sq), qp, 0.0).astype(BF16)
                s = jnp.where(msk, _dot_nt(qm, kg), NEG_BIG)
                m = jnp.maximum(jnp.max(s, -1, keepdims=True), sink)
                e = jnp.exp(s - m)
                den = jnp.sum(e, -1, keepdims=True) + jnp.exp(sink - m)
                outs.append(_dot(e.astype(BF16), vg) * (1.0 / den))
            att = jnp.where(lo_sq, outs[0], outs[1])
            gate = p_scr[r0:r0 + WINDOW, GA_OFF + cc * LANES:GA_OFF + (cc + 1) * LANES]
            mix_scr[r0:r0 + WINDOW, cc * LANES:(cc + 1) * LANES] = (att * _silu(gate)).astype(BF16)
    for g in range(ATT_KV_HEADS):
        kd_scr[g, 0:WINDOW, :] = kd_scr[g, T:T + WINDOW, :]
        vd_scr[g, 0:WINDOW, :] = vd_scr[g, T:T + WINDOW, :]

    P = SUBLANES
    H = CONV_W - 1
    for j in range(N_LRU_TILES):
        cs_ = slice(j * LANES, (j + 1) * LANES)
        lx_scr[j, P:P + T, :] = p_scr[:, XL_OFF + j * LANES:XL_OFF + (j + 1) * LANES]
        for k, xl in enumerate(_conv_classes(lx_scr, j, T, lcw_ref, lvec_ref[0:1, cs_], cs_)):
            sb_scr[j, pl.ds(k, T // CONV_STRIDE, stride=CONV_STRIDE), :] = xl
        xlb_scr[:, cs_] = sb_scr[j].astype(BF16)
    lx_scr[:, 0:P, :] = lx_scr[:, T:T + P, :]

    G = T // SUBLANES
    row_g = lax.broadcasted_iota(jnp.int32, (G, LANES), 0)
    cl = -LRU_C * _softplus(-lvec_ref[3:4, :])
    for jj in range(N_LRU_TILES // 2):
        c0 = jj * 2 * LANES
        k0 = (c0 // LRU_BLOCK) * LRU_BLOCK // LANES * LANES
        k1 = -(-(-(-(c0 + 2 * LANES) // LRU_BLOCK) * LRU_BLOCK) // LANES) * LANES
        xk = xlb_scr[:, k0:k1]
        gr = _dot(xk, wg_ref[k0:k1, c0:c0 + 2 * LANES])
        gi = _dot(xk, wg_ref[k0:k1, LRU_WIDTH + c0:LRU_WIDTH + c0 + 2 * LANES])
        for u in range(2):
            j = 2 * jj + u
            cs_ = slice(j * LANES, (j + 1) * LANES)
            us = slice(u * LANES, (u + 1) * LANES)
            r = _sigmoid(gr[:, us] + lvec_ref[1:2, cs_])
            ig = _sigmoid(gi[:, us] + lvec_ref[2:3, cs_])
            log_a = cl[:, cs_] * r
            a = jnp.exp(log_a)
            sa_scr[j] = a
            sb_scr[j] = jnp.sqrt(-jnp.tanh(log_a) * (1.0 + a * a)) * (ig * sb_scr[j])
    for j in range(N_LRU_TILES):
        ca = sa_scr[j, pl.ds(0, G, stride=SUBLANES), :]
        cb = sb_scr[j, pl.ds(0, G, stride=SUBLANES), :]
        for k in range(1, SUBLANES):
            ak = sa_scr[j, pl.ds(k, G, stride=SUBLANES), :]
            cb = ak * cb + sb_scr[j, pl.ds(k, G, stride=SUBLANES), :]
            ca = ak * ca
            sa_scr[j, pl.ds(k, G, stride=SUBLANES), :] = ca
            sb_scr[j, pl.ds(k, G, stride=SUBLANES), :] = cb
        hin = hl_scr[0:1, j * LANES:(j + 1) * LANES]
        cb = cb + jnp.where(row_g == 0, ca * hin, 0.0)
        d = 1
        while d < G:
            ok = row_g >= d
            cb = cb + ca * jnp.where(ok, pltpu.roll(cb, d, 0), 0.0)
            ca = ca * jnp.where(ok, pltpu.roll(ca, d, 0), 1.0)
            d *= 2
        hprev = jnp.where(row_g == 0, hin, pltpu.roll(cb, 1, 0))
        for k in range(SUBLANES):
            sb_scr[j, pl.ds(k, G, stride=SUBLANES), :] = (
                sb_scr[j, pl.ds(k, G, stride=SUBLANES), :] + sa_scr[j, pl.ds(k, G, stride=SUBLANES), :] * hprev)
        hl_scr[0:1, j * LANES:(j + 1) * LANES] = cb[G - 1:G, :]
        gl = p_scr[:, GL_OFF + j * LANES:GL_OFF + (j + 1) * LANES]
        mix_scr[:, MIX_LRU + j * LANES:MIX_LRU + (j + 1) * LANES] = (sb_scr[j] * _silu(gl)).astype(BF16)

    for j in range(N_CONV_TILES):
        cs_ = slice(j * LANES, (j + 1) * LANES)
        sx_scr[j, P:P + T, :] = p_scr[:, XBC_OFF + j * LANES:XBC_OFF + (j + 1) * LANES]
        for k, xc in enumerate(_conv_classes(sx_scr, j, T, scw_ref, scb_ref[0:1, cs_], cs_)):
            xc_scr[j, pl.ds(k, T // CONV_STRIDE, stride=CONV_STRIDE), :] = _silu(xc)
    sx_scr[:, 0:P, :] = sx_scr[:, T:T + P, :]

    qi = lax.broadcasted_iota(jnp.int32, (WINDOW, WINDOW), 0)
    si = lax.broadcasted_iota(jnp.int32, (WINDOW, WINDOW), 1)
    causal = si <= qi
    upper = (qi <= si).astype(F32)
    hpad = jnp.zeros((WINDOW - HEAD_ROWS, WINDOW), F32)
    a_col = -jnp.exp(hcol_ref[1])
    for i in range(nsub):
        r0 = i * WINDOW
        dt_t = _softplus(p_scr[r0:r0 + WINDOW, DT_OFF:DT_OFF + LANES].T[0:HEAD_ROWS, :] + hcol_ref[0])
        cst = jnp.dot(dt_t * a_col, upper, precision=lax.Precision.HIGHEST, preferred_element_type=F32)
        dt_c = jnp.concatenate([dt_t, hpad], axis=0).T
        cs = jnp.concatenate([cst, hpad], axis=0).T
        ys = [None] * N_SSD_TILES
        for g in range(SSD_GROUPS):
            bg_t = xc_scr[N_SSD_TILES + g, r0:r0 + WINDOW, :].T.astype(BF16)
            cg = xc_scr[N_SSD_TILES + SSD_GROUPS + g, r0:r0 + WINDOW, :].astype(BF16)
            cbm = _dot(cg, bg_t)
            yo = _dot(cg, ht_scr[g].astype(BF16))
            xdec, edec = [], []
            for pp in range(TILES_PER_GROUP):
                p = g * TILES_PER_GROUP + pp
                bcs, mm, dts = [], [], []
                for h in (2 * p, 2 * p + 1):
                    bc = jnp.broadcast_to(cs[:, h:h + 1], (WINDOW, WINDOW))
                    lmat = jnp.exp(jnp.where(causal, bc - cst[h:h + 1, :], NEG_BIG))
                    mm.append((cbm * lmat).astype(BF16))
                    bcs.append(bc)
                    dts.append(jnp.broadcast_to(dt_c[:, h:h + 1], (WINDOW, LANES)))
                csl = jnp.where(lo_sq, bcs[0], bcs[1])
                ecs = jnp.exp(csl)
                dec = jnp.exp(csl[WINDOW - 1:WINDOW, :] - csl)
                xs_p = xc_scr[p, r0:r0 + WINDOW, :]
                xdt = xs_p * jnp.where(lo_sq, dts[0], dts[1])
                xdt_b = xdt.astype(BF16)
                y = jnp.where(lo_sq, _dot(mm[0], xdt_b), _dot(mm[1], xdt_b))
                y = y + yo[:, pp * LANES:(pp + 1) * LANES] * ecs
                y = y + svec_ref[0:1, p * LANES:(p + 1) * LANES] * xs_p
                ys[p] = y * _silu(p_scr[r0:r0 + WINDOW, Z_OFF + p * LANES:Z_OFF + (p + 1) * LANES])
                xdec.append((xdt * dec).astype(BF16))
                edec.append(ecs[WINDOW - 1:WINDOW, :])
            ht_scr[g] = (ht_scr[g] * jnp.concatenate(edec, axis=1)
                         + _dot(bg_t, jnp.concatenate(xdec, axis=1)))
        ss = jnp.sum(ys[0] * ys[0], -1, keepdims=True)
        for p in range(1, N_SSD_TILES):
            ss = ss + jnp.sum(ys[p] * ys[p], -1, keepdims=True)
        rinv = lax.rsqrt(ss * (1.0 / SSD_WIDTH) + NORM_EPS)
        for p in range(N_SSD_TILES):
            mix_scr[r0:r0 + WINDOW, MIX_SSD + p * LANES:MIX_SSD + (p + 1) * LANES] = (
                ys[p] * rinv * svec_ref[1:2, p * LANES:(p + 1) * LANES]).astype(BF16)

    out = _dot(mix_scr[...], wout_ref[...])
    y_ref[...] = _layer_norm(DEEPNORM_ALPHA * x_ref[...] + out, ln_ref[0:1, :], ln_ref[1:2, :])

    @pl.when(last)
    def _():
        ko_ref[...] = _rope(p_scr[T - WINDOW:T, K_OFF:K_OFF + LANES], rope_ref[T - WINDOW:T, :])
        vo_ref[...] = p_scr[T - WINDOW:T, V_OFF:V_OFF + LANES]
        for j in range(N_LRU_TILES):
            lco_ref[:, j * LANES:(j + 1) * LANES] = lx_scr[j, P + T - (CONV_W - 1):P + T, :]
        lho_ref[...] = hl_scr[...]
        for j in range(N_CONV_TILES):
            sco_ref[:, j * LANES:(j + 1) * LANES] = sx_scr[j, P + T - (CONV_W - 1):P + T, :]
        for g in range(SSD_GROUPS):
            for pp in range(TILES_PER_GROUP):
                tile = ht_scr[g, :, pp * LANES:(pp + 1) * LANES].T
                h = 2 * (g * TILES_PER_GROUP + pp)
                sho_ref[h] = tile[0:SSD_HEAD_DIM, :]
                sho_ref[h + 1] = tile[SSD_HEAD_DIM:2 * SSD_HEAD_DIM, :]


def _const_spec(shape, layer):
    nd = len(shape)
    return pl.BlockSpec((None,) + tuple(shape), lambda *_: (layer,) + (0,) * nd,
                        pipeline_mode=pl.Buffered(1))


def _prompt_layer(layer, x, rope_tab, sinks, prm, prev):
    B, L, _ = x.shape
    T = PROMPT_TILE
    nc = L // T
    win, aux, wg, wout, lcw, lvec, scw, scb, hvec, svec, ln, hcol = prm
    tile = lambda b, c: (b, c, 0)

    in_specs = [
        pl.BlockSpec(memory_space=pltpu.SMEM),
        pl.BlockSpec((None, T, D_MODEL), tile),
        pl.BlockSpec((T, 3 * LANES), lambda b, c: (c, 0)),
        _const_spec((D_MODEL, W_IN_COLS), layer),
        _const_spec((D_MODEL, LANES), layer),
        _const_spec((LRU_WIDTH, 2 * LRU_WIDTH), layer),
        _const_spec((D_MIX, D_MODEL), layer),
        _const_spec((CONV_W, LRU_WIDTH), layer),
        _const_spec((4, LRU_WIDTH), layer),
        _const_spec((CONV_W, SSD_CONV_CH), layer),
        _const_spec((1, SSD_CONV_CH), layer),
        _const_spec((2, HEAD_ROWS, LANES), layer),
        _const_spec((2, SSD_WIDTH), layer),
        _const_spec((2, D_MODEL), layer),
    ]
    state_shapes = (
        (WINDOW, ATT_KV_WIDTH),
        (WINDOW, ATT_KV_WIDTH),
        (CONV_W - 1, LRU_WIDTH),
        (1, LRU_WIDTH),
        (CONV_W - 1, SSD_CONV_CH),
        (SSD_HEADS, SSD_HEAD_DIM, SSD_STATE),
    )
    out_shape = (jax.ShapeDtypeStruct((B, L, D_MODEL), F32),) + tuple(
        jax.ShapeDtypeStruct((DEPTH, B) + s, F32) for s in state_shapes)
    per_lb = lambda s: pl.BlockSpec((None, None) + s, lambda b, c: (layer, b) + (0,) * len(s))
    out_specs = (pl.BlockSpec((None, T, D_MODEL), tile),) + tuple(per_lb(s) for s in state_shapes)
    operands = [sinks, x, rope_tab, win, aux, wg, wout, lcw, lvec, scw, scb, hcol, svec, ln]
    n_in = len(operands)
    aliases = {}
    if prev is not None:
        for k, arr in enumerate(prev):
            aliases[len(operands)] = 1 + k
            operands.append(arr)
            in_specs.append(pl.BlockSpec(memory_space=pl.ANY))

    def body(*refs):
        _prompt_kernel(*refs[:n_in], *refs[n_in + len(aliases):])

    scratch = [
        pltpu.VMEM((T, N_COLS), F32),
        pltpu.VMEM((ATT_KV_HEADS, WINDOW + T, LANES), BF16),
        pltpu.VMEM((ATT_KV_HEADS, WINDOW + T, LANES), BF16),
        pltpu.VMEM((N_LRU_TILES, SUBLANES + T, LANES), F32),
        pltpu.VMEM((N_CONV_TILES, SUBLANES + T, LANES), F32),
        pltpu.VMEM((N_LRU_TILES, T, LANES), F32),
        pltpu.VMEM((N_LRU_TILES, T, LANES), F32),
        pltpu.VMEM((1, LRU_WIDTH), F32),
        pltpu.VMEM((SSD_GROUPS, SSD_STATE, GROUP_W), F32),
        pltpu.VMEM((T, D_MIX), BF16),
        pltpu.VMEM((T, LRU_WIDTH), BF16),
        pltpu.VMEM((N_CONV_TILES, T, LANES), F32),
    ]
    return pl.pallas_call(
        body,
        grid=(B, nc),
        in_specs=in_specs,
        out_specs=out_specs,
        out_shape=out_shape,
        scratch_shapes=scratch,
        input_output_aliases=aliases,
        compiler_params=pltpu.CompilerParams(
            dimension_semantics=("arbitrary", "arbitrary"),
            vmem_limit_bytes=VMEM_LIMIT_BYTES),
        name=f"prompt_layer{layer}",
    )(*operands)


def _conv_step(x_new, st_ref, sto_ref, w_ref, bias):
    y = bias + w_ref[CONV_W - 1:CONV_W, :] * x_new
    for t in range(CONV_W - 1):
        y = y + w_ref[t:t + 1, :] * st_ref[t]
    for t in range(CONV_W - 2):
        sto_ref[t] = st_ref[t + 1]
    sto_ref[CONV_W - 2] = x_new
    return y


def _sample_kernel(layer, n_aliased, sinks_ref, x_ref, rope_ref, win_ref, aux_ref, wg_ref, wout_ref, lcw_ref,
                   lvec_ref, scw_ref, scb_ref, hvec_ref, svec_ref, ln_ref,
                   ck_hbm, cv_hbm, lc_ref, lh_ref, sc_ref, sh_hbm, *refs):
    (y_ref, cko_ref, cvo_ref, lco_ref, lho_ref, sco_ref, sho_ref,
     p_scr, q_scr, o_scr, kn_scr, vn_scr, xdt_scr, da_scr, y_scr, mix_scr,
     ck_ring, cv_ring, sh_ring, ring_sem) = refs[n_aliased:]
    NB = x_ref.shape[0]
    BT = SAMPLE_BT
    i = pl.program_id(0)
    nsteps = pl.num_programs(0)
    lo_f = _lane_lo((NB, LANES))

    def fetch(j):
        slot = lax.rem(j, RING_SLOTS)
        rows = pl.ds(pl.multiple_of(j * BT, BT), BT)
        return (pltpu.make_async_copy(ck_hbm.at[layer, rows], ck_ring.at[slot], ring_sem.at[0, slot]),
                pltpu.make_async_copy(cv_hbm.at[layer, rows], cv_ring.at[slot], ring_sem.at[1, slot]),
                pltpu.make_async_copy(sh_hbm.at[layer, rows], sh_ring.at[slot], ring_sem.at[2, slot]))

    @pl.when(i == 0)
    def _():
        for j in range(RING_SLOTS - 1):
            for cp in fetch(j):
                cp.start(priority=RING_PRIORITY)

    @pl.when(i + RING_SLOTS - 1 < nsteps)
    def _():
        for cp in fetch(i + RING_SLOTS - 1):
            cp.start(priority=RING_PRIORITY)

    @pl.when(i == 0)
    def _():
        _project(x_ref[...], win_ref, aux_ref, p_scr)
        tab = rope_ref[...]
        for cc in range(ATT_HEADS // 2):
            g = cc // (ATT_HEADS // ATT_KV_HEADS // 2)
            qp = _rope(p_scr[:, Q_OFF + cc * LANES:Q_OFF + (cc + 1) * LANES], tab) * ATT_SCALE
            for half in range(2):
                t = jnp.where(lo_f if half == 0 else jnp.logical_not(lo_f), qp, 0.0)
                if half != g:
                    t = pltpu.roll(t, HALF, 1)
                q_scr[pl.ds(2 * cc + half, NB, stride=ATT_HEADS), :] = t
        kn_scr[...] = _rope(p_scr[:, K_OFF:K_OFF + LANES], tab)
        vn_scr[...] = p_scr[:, V_OFF:V_OFF + LANES]
        xin = p_scr[:, XL_OFF:XL_OFF + LRU_WIDTH]
        xl = _conv_step(xin, lc_ref, lco_ref, lcw_ref, lvec_ref[0:1, :])
        gates = _dot(xl.astype(BF16), wg_ref[...])
        a, bt = _lru_coeffs(xl, gates, lvec_ref)
        h1 = a * lh_ref[...] + bt
        lho_ref[...] = h1
        mix_scr[:, MIX_LRU:MIX_LRU + LRU_WIDTH] = (h1 * _silu(p_scr[:, GL_OFF:GL_OFF + LRU_WIDTH])).astype(BF16)
        xin2 = p_scr[:, XBC_OFF:XBC_OFF + SSD_CONV_CH]
        xc = _conv_step(xin2, sc_ref, sco_ref, scw_ref, scb_ref[0:1, :])
        p_scr[:, XBC_OFF:XBC_OFF + SSD_CONV_CH] = _silu(xc)
        dt = _softplus(p_scr[:, DT_OFF:DT_OFF + LANES] + hvec_ref[0:1, :])
        da_scr[...] = jnp.exp(dt * (-jnp.exp(hvec_ref[1:2, :])))
        for p in range(N_SSD_TILES):
            dte = jnp.where(lo_f, jnp.broadcast_to(dt[:, 2 * p:2 * p + 1], (NB, LANES)),
                            jnp.broadcast_to(dt[:, 2 * p + 1:2 * p + 2], (NB, LANES)))
            xdt_scr[:, p * LANES:(p + 1) * LANES] = p_scr[:, XBC_OFF + p * LANES:XBC_OFF + (p + 1) * LANES] * dte

    for cp in fetch(i):
        cp.wait()
    slot = lax.rem(i, RING_SLOTS)
    ck_ref, cv_ref, sh_ref = ck_ring.at[slot], cv_ring.at[slot], sh_ring.at[slot]
    r8 = pl.multiple_of(i * BT, BT)
    kn_blk = kn_scr[pl.ds(r8, BT), :]
    vn_blk = vn_scr[pl.ds(r8, BT), :]
    zpad = jnp.zeros((LANES - BT, LANES), F32)
    kn_t = jnp.concatenate([kn_blk, zpad], axis=0).T
    vn_t = jnp.concatenate([vn_blk, zpad], axis=0).T
    newest = lax.broadcasted_iota(jnp.int32, (ATT_KV_WIDTH, WINDOW), 1) == WINDOW - 1
    da_t = jnp.concatenate([da_scr[pl.ds(r8, BT), :], zpad], axis=0).T
    x_t = [jnp.concatenate([xdt_scr[pl.ds(r8, BT), p * LANES:(p + 1) * LANES], zpad], axis=0).T
           for p in range(N_SSD_TILES)]
    b_blk = p_scr[pl.ds(r8, BT), B_OFF:B_OFF + SSD_GROUPS * SSD_STATE]
    c_blk = p_scr[pl.ds(r8, BT), C_OFF:C_OFF + SSD_GROUPS * SSD_STATE].astype(BF16)
    row8 = lax.broadcasted_iota(jnp.int32, (ATT_HEADS, 1), 0)
    sink = jnp.zeros((ATT_HEADS, 1), F32)
    for h in range(ATT_HEADS):
        sink = jnp.where(row8 == h, sinks_ref[h], sink)
    rowb = lax.broadcasted_iota(jnp.int32, (BT, GROUP_W), 0)
    y_acc = [jnp.zeros((BT, GROUP_W), F32) for _ in range(SSD_GROUPS)]
    rowk = lax.broadcasted_iota(jnp.int32, (BT, SSD_STATE), 0)
    outer = []
    for g in range(SSD_GROUPS):
        bg = b_blk[:, g * SSD_STATE:(g + 1) * SSD_STATE]
        diag = jnp.concatenate([jnp.where(rowk == bb, bg, 0.0) for bb in range(BT)], axis=1)
        rhs = jnp.concatenate([diag, jnp.zeros((LANES - BT, BT * SSD_STATE), F32)], axis=0).astype(BF16)
        outer.append([_dot(x_t[g * TILES_PER_GROUP + pp].astype(BF16), rhs) for pp in range(TILES_PER_GROUP)])
    for bb in range(BT):
        qr = q_scr[pl.ds(pl.multiple_of((r8 + bb) * ATT_HEADS, ATT_HEADS), ATT_HEADS), :]
        kt = ck_ref[bb]
        vt = cv_ref[bb]
        s = _dot(qr.astype(BF16), kt.astype(BF16))
        s_new = jnp.sum(qr * kn_blk[bb:bb + 1, :], -1, keepdims=True)
        m = jnp.maximum(jnp.maximum(jnp.max(s, -1, keepdims=True), s_new), sink)
        e = jnp.exp(s - m)
        e_new = jnp.exp(s_new - m)
        den = jnp.sum(e, -1, keepdims=True) + e_new + jnp.exp(sink - m)
        o = _dot_nt(e.astype(BF16), vt.astype(BF16)) + e_new * vn_blk[bb:bb + 1, :]
        o_scr[pl.ds(pl.multiple_of((r8 + bb) * ATT_HEADS, ATT_HEADS), ATT_HEADS), :] = o * (1.0 / den)
        cko_ref[bb] = jnp.where(newest, kn_t[:, bb:bb + 1], pltpu.roll(kt, WINDOW - 1, 1))
        cvo_ref[bb] = jnp.where(newest, vn_t[:, bb:bb + 1], pltpu.roll(vt, WINDOW - 1, 1))
        for g in range(SSD_GROUPS):
            tiles = []
            for pp in range(TILES_PER_GROUP):
                p = g * TILES_PER_GROUP + pp
                for hh in range(2):
                    h = 2 * p + hh
                    dab = jnp.broadcast_to(da_t[h:h + 1, bb:bb + 1], (SSD_HEAD_DIM, SSD_STATE))
                    h1 = sh_ref[bb, h] * dab + outer[g][pp][hh * SSD_HEAD_DIM:(hh + 1) * SSD_HEAD_DIM,
                                                               bb * SSD_STATE:(bb + 1) * SSD_STATE]
                    sho_ref[bb, h] = h1
                    tiles.append(h1.astype(BF16))
            res = _dot_nt(c_blk[:, g * SSD_STATE:(g + 1) * SSD_STATE], jnp.concatenate(tiles, axis=0))
            y_acc[g] = jnp.where(rowb == bb, res, y_acc[g])
    for g in range(SSD_GROUPS):
        y_scr[pl.ds(r8, BT), g * GROUP_W:(g + 1) * GROUP_W] = y_acc[g]

    @pl.when(i == pl.num_programs(0) - 1)
    def _():
        for cc in range(ATT_HEADS // 2):
            g = cc // (ATT_HEADS // ATT_KV_HEADS // 2)
            oe = o_scr[pl.ds(2 * cc, NB, stride=ATT_HEADS), :]
            oo = o_scr[pl.ds(2 * cc + 1, NB, stride=ATT_HEADS), :]
            if g == 0:
                oo = pltpu.roll(oo, HALF, 1)
            else:
                oe = pltpu.roll(oe, HALF, 1)
            gate = p_scr[:, GA_OFF + cc * LANES:GA_OFF + (cc + 1) * LANES]
            mix_scr[:, cc * LANES:(cc + 1) * LANES] = (jnp.where(lo_f, oe, oo) * _silu(gate)).astype(BF16)
        y = (y_scr[...] + svec_ref[0:1, :] * p_scr[:, XBC_OFF:XBC_OFF + SSD_WIDTH]) * _silu(p_scr[:, Z_OFF:Z_OFF + SSD_WIDTH])
        y = y * lax.rsqrt(jnp.mean(y * y, -1, keepdims=True) + NORM_EPS) * svec_ref[1:2, :]
        mix_scr[:, MIX_SSD:MIX_SSD + SSD_WIDTH] = y.astype(BF16)
        out = _dot(mix_scr[...], wout_ref[...])
        y_ref[...] = _layer_norm(DEEPNORM_ALPHA * x_ref[...] + out, ln_ref[0:1, :], ln_ref[1:2, :])


def _sample_layer(layer, x, rope_tab, sinks, prm, ck, cv, lc, lh, sc, sh, prev):
    NB = x.shape[0]
    BT = SAMPLE_BT
    H = CONV_W - 1
    win, aux, wg, wout, lcw, lvec, scw, scb, hvec, svec, ln, _ = prm
    whole = lambda shape: pl.BlockSpec(shape, lambda i: (0,) * len(shape), pipeline_mode=pl.Buffered(1))
    lwhole = lambda shape: pl.BlockSpec((None,) + shape, lambda i: (layer,) + (0,) * len(shape),
                                        pipeline_mode=pl.Buffered(1))
    rows = lambda shape: pl.BlockSpec((None, BT) + shape, lambda i: (layer, i) + (0,) * len(shape))
    hbm = pl.BlockSpec(memory_space=pl.ANY)
    state_specs = [
        hbm,
        hbm,
        lwhole((H, NB, LRU_WIDTH)),
        lwhole((NB, LRU_WIDTH)),
        lwhole((H, NB, SSD_CONV_CH)),
        hbm,
    ]
    in_specs = [
        pl.BlockSpec(memory_space=pltpu.SMEM),
        whole((NB, D_MODEL)),
        whole((1, 3 * LANES)),
        _const_spec((D_MODEL, W_IN_COLS), layer),
        _const_spec((D_MODEL, LANES), layer),
        _const_spec((LRU_WIDTH, 2 * LRU_WIDTH), layer),
        _const_spec((D_MIX, D_MODEL), layer),
        _const_spec((CONV_W, LRU_WIDTH), layer),
        _const_spec((4, LRU_WIDTH), layer),
        _const_spec((CONV_W, SSD_CONV_CH), layer),
        _const_spec((1, SSD_CONV_CH), layer),
        _const_spec((2, LANES), layer),
        _const_spec((2, SSD_WIDTH), layer),
        _const_spec((2, D_MODEL), layer),
    ] + state_specs
    operands = [sinks, x, rope_tab, win, aux, wg, wout, lcw, lvec, scw, scb, hvec, svec, ln, ck, cv, lc, lh, sc, sh]
    aliases = {}
    if prev is not None:
        for k, arr in enumerate(prev):
            aliases[len(operands)] = 1 + k
            operands.append(arr)
            in_specs.append(pl.BlockSpec(memory_space=pl.ANY))
    out_shape = (jax.ShapeDtypeStruct((NB, D_MODEL), F32),) + tuple(
        jax.ShapeDtypeStruct(a.shape, F32) for a in (ck, cv, lc, lh, sc, sh))
    full = lambda shape: pl.BlockSpec(shape, lambda i: (0,) * len(shape))
    lfull = lambda shape: pl.BlockSpec((None,) + shape, lambda i: (layer,) + (0,) * len(shape))
    out_specs = (
        full((NB, D_MODEL)),
        rows((ATT_KV_WIDTH, WINDOW)),
        rows((ATT_KV_WIDTH, WINDOW)),
        lfull((H, NB, LRU_WIDTH)),
        lfull((NB, LRU_WIDTH)),
        lfull((H, NB, SSD_CONV_CH)),
        rows((SSD_HEADS, SSD_HEAD_DIM, SSD_STATE)),
    )
    scratch = [
        pltpu.VMEM((NB, N_COLS), F32),
        pltpu.VMEM((NB * ATT_HEADS, LANES), F32),
        pltpu.VMEM((NB * ATT_HEADS, LANES), F32),
        pltpu.VMEM((NB, ATT_KV_WIDTH), F32),
        pltpu.VMEM((NB, ATT_KV_WIDTH), F32),
        pltpu.VMEM((NB, SSD_WIDTH), F32),
        pltpu.VMEM((NB, LANES), F32),
        pltpu.VMEM((NB, SSD_WIDTH), F32),
        pltpu.VMEM((NB, D_MIX), BF16),
        pltpu.VMEM((RING_SLOTS, BT, ATT_KV_WIDTH, WINDOW), F32),
        pltpu.VMEM((RING_SLOTS, BT, ATT_KV_WIDTH, WINDOW), F32),
        pltpu.VMEM((RING_SLOTS, BT, SSD_HEADS, SSD_HEAD_DIM, SSD_STATE), F32),
        pltpu.SemaphoreType.DMA((3, RING_SLOTS)),
    ]
    assert NB // BT >= RING_SLOTS - 1
    return pl.pallas_call(
        functools.partial(_sample_kernel, layer, len(aliases)),
        grid=(NB // BT,),
        in_specs=in_specs,
        out_specs=out_specs,
        out_shape=out_shape,
        scratch_shapes=scratch,
        input_output_aliases=aliases,
        compiler_params=pltpu.CompilerParams(
            dimension_semantics=("arbitrary",),
            vmem_limit_bytes=VMEM_LIMIT_BYTES),
        name=f"sample_layer{layer}",
    )(*operands)


def _rope_table(start, length):
    half = ROPE_DIM // 2
    inv = (np.float32(ROPE_THETA) ** (-np.arange(half, dtype=np.float32) / np.float32(half))).astype(np.float32)
    ang = (np.arange(start, start + length).astype(np.float32)[:, None] * inv[None, :]).astype(np.float32)
    cos, sin = np.cos(ang), np.sin(ang)
    d = np.arange(LANES) % ATT_HEAD_DIM
    fi = d % half
    c = np.where(d < ROPE_DIM, cos[:, fi], np.float32(1.0))
    s1 = np.where(d < half, -sin[:, fi], np.float32(0.0))
    s2 = np.where((d >= half) & (d < ROPE_DIM), sin[:, fi], np.float32(0.0))
    return jnp.asarray(np.concatenate([c, s1, s2], axis=1).astype(np.float32))


def _prep_params(w_in, w_out, lru_conv_w, lru_conv_b, lru_wa, lru_ba, lru_wx, lru_bx, lru_lambda,
                 ssd_conv_w, ssd_conv_b, ssd_dt_bias, ssd_a_log, ssd_d, ssd_norm_g, ln_g, ln_b):
    win = w_in.astype(BF16)
    aux = jnp.pad(w_in[..., DT_OFF:], ((0, 0), (0, 0), (0, LANES - SSD_HEADS))).astype(BF16)

    def dense(w):
        rows = [jnp.pad(w[:, n], ((0, 0), (0, 0), (n * LRU_BLOCK, LRU_WIDTH - (n + 1) * LRU_BLOCK)))
                for n in range(LRU_BLOCKS)]
        return jnp.concatenate(rows, 1)

    wg = jnp.concatenate([dense(lru_wa.astype(BF16)), dense(lru_wx.astype(BF16))], -1)
    wout = w_out.astype(BF16)
    lvec = jnp.stack([lru_conv_b, lru_ba, lru_bx, lru_lambda], 1)
    hpad = lambda t: jnp.pad(t, ((0, 0), (0, LANES - SSD_HEADS)))
    hvec = jnp.stack([hpad(ssd_dt_bias), hpad(ssd_a_log)], 1)
    rpad = lambda t: jnp.broadcast_to(jnp.pad(t, ((0, 0), (0, HEAD_ROWS - SSD_HEADS)))[:, :, None],
                                      (DEPTH, HEAD_ROWS, LANES))
    hcol = jnp.stack([rpad(ssd_dt_bias), rpad(ssd_a_log)], 1)
    svec = jnp.stack([jnp.repeat(ssd_d, SSD_HEAD_DIM, axis=1), ssd_norm_g], 1)
    ln = jnp.stack([ln_g, ln_b], 1)
    return (win, aux, wg, wout, lru_conv_w, lvec, ssd_conv_w, ssd_conv_b[:, None, :], hvec, svec, ln, hcol)


def kernel(x_prompt, x_sample, cache_swa_k, cache_swa_v, state_lru_conv, state_lru_h, state_ssd_conv, state_ssd_h, w_in, w_out, att_sinks, lru_conv_w, lru_conv_b, lru_wa, lru_ba, lru_wx, lru_bx, lru_lambda, ssd_conv_w, ssd_conv_b, ssd_dt_bias, ssd_a_log, ssd_d, ssd_norm_g, ln_g, ln_b):
    prm = _prep_params(w_in, w_out, lru_conv_w, lru_conv_b, lru_wa, lru_ba, lru_wx, lru_bx, lru_lambda,
                       ssd_conv_w, ssd_conv_b, ssd_dt_bias, ssd_a_log, ssd_d, ssd_norm_g, ln_g, ln_b)
    bp, lp, _ = x_prompt.shape
    rope_p = _rope_table(0, lp)
    xp = x_prompt
    new_p = None
    for l in range(DEPTH):
        xp, *new_p = _prompt_layer(l, xp, rope_p, att_sinks[l], prm, new_p)
    ko, vo, lco, lho, sco, sho = new_p
    outs_p = (ko.reshape(DEPTH, bp, WINDOW, ATT_KV_HEADS, ATT_HEAD_DIM),
              vo.reshape(DEPTH, bp, WINDOW, ATT_KV_HEADS, ATT_HEAD_DIM),
              lco, lho.reshape(DEPTH, bp, LRU_WIDTH), sco, sho)

    nb = x_sample.shape[0]
    rope_s = _rope_table(PAST_LEN, x_sample.shape[1])
    to_kt = lambda t: jnp.transpose(t, (0, 1, 3, 4, 2)).reshape(DEPTH, nb, ATT_KV_WIDTH, WINDOW)
    from_kt = lambda t: jnp.transpose(t.reshape(DEPTH, nb, ATT_KV_HEADS, ATT_HEAD_DIM, WINDOW), (0, 1, 4, 2, 3))
    swap = lambda t: jnp.transpose(t, (0, 2, 1, 3))
    state = (to_kt(cache_swa_k), to_kt(cache_swa_v), swap(state_lru_conv), state_lru_h, swap(state_ssd_conv),
             state_ssd_h)
    xs = x_sample.reshape(nb, D_MODEL)
    new = None
    for l in range(DEPTH):
        xs, *new = _sample_layer(l, xs, rope_s, att_sinks[l], prm, *state, new)
    cko, cvo, lco, lho, sco, sho = new
    outs_s = (from_kt(cko), from_kt(cvo), swap(lco), lho, swap(sco), sho)
    return (xp, xs.reshape(x_sample.shape)) + tuple(outs_p) + tuple(outs_s)
```
